```python
import math
import jax, jax.numpy as jnp
from jax import lax
import numpy as np

D_MODEL = 1024
BATCH = 8
SEQ = 2048
DEPTH = 2
DEC_BATCH = 128
DEC_SEQ = 4
PAST_LEN = 16384
PAGE_SIZE = 128

N_MIXERS = 4
GROUP_W = D_MODEL // N_MIXERS
MIX_W = N_MIXERS * GROUP_W
HEAD_DIM = 64
N_HEADS = GROUP_W // HEAD_DIM
SSM_STATE = 64
SSM_GROUPS = 2
CONV_W = 4
D_FF = ((8 * D_MODEL // 3 + 127) // 128) * 128
CHUNK = 64
EPS = 1e-6
ROPE_BASE = 10000.0

RET_IN = 4 * GROUP_W
SSM_CONV_CH = GROUP_W + 2 * SSM_GROUPS * SSM_STATE
SSM_IN = GROUP_W + SSM_CONV_CH + N_HEADS
GDN_CONV_CH = 3 * GROUP_W
GDN_IN = GDN_CONV_CH + GROUP_W + 2 * N_HEADS
MLSTM_IN = 4 * GROUP_W + 2 * N_HEADS
IN_DIM = RET_IN + SSM_IN + GDN_IN + MLSTM_IN

kernel_name = "hymba_four_recurrent_group_decoder_step"


def _split(a, sizes):
    offs = [int(o) for o in np.cumsum(sizes)[:-1]]
    return jnp.split(a, offs, axis=-1)


def rmsnorm(x, g):
    xf = x.astype(jnp.float32)
    y = xf * lax.rsqrt(jnp.mean(xf * xf, axis=-1, keepdims=True) + EPS)
    return (y * g.astype(jnp.float32)).astype(x.dtype)


def head_rmsnorm(x, g):
    shp = x.shape
    xh = x.reshape(shp[:-1] + (N_HEADS, HEAD_DIM)).astype(jnp.float32)
    xh = xh * lax.rsqrt(jnp.mean(xh * xh, axis=-1, keepdims=True) + EPS)
    return xh.reshape(shp) * g.astype(jnp.float32)


def l2norm(x):
    return x * lax.rsqrt(jnp.sum(x * x, axis=-1, keepdims=True) + EPS)


def rotary(x, pos):
    half = HEAD_DIM // 2
    inv = ROPE_BASE ** (-jnp.arange(half, dtype=jnp.float32) / half)
    ang = pos.astype(jnp.float32)[:, None] * inv[None, :]
    cos = jnp.cos(ang)[None, :, None, :]
    sin = jnp.sin(ang)[None, :, None, :]
    x1, x2 = x[..., :half], x[..., half:]
    return jnp.concatenate([x1 * cos - x2 * sin, x1 * sin + x2 * cos], axis=-1)


def causal_conv(x, buf, w):
    xp = jnp.concatenate([buf.astype(x.dtype), x], axis=1)
    L = x.shape[1]
    out = xp[:, 0:L] * w[0]
    for j in range(1, CONV_W):
        out = out + xp[:, j:j + L] * w[j]
    return out, xp[:, -(CONV_W - 1):]


def to_chunks(a, c):
    b, L = a.shape[:2]
    return jnp.moveaxis(a.reshape((b, L // c, c) + a.shape[2:]), 1, 0)


def from_chunks(a):
    n, b, c = a.shape[:3]
    return jnp.moveaxis(a, 0, 1).reshape((b, n * c) + a.shape[3:])


def pick_chunk(L):
    return CHUNK if L % CHUNK == 0 else L


def retention_chunked(q, k, v, s0, chunk):
    lg = jnp.log(1.0 - jnp.exp2(-5.0 - jnp.arange(N_HEADS, dtype=jnp.float32)))
    idx = jnp.arange(chunk, dtype=jnp.float32)
    rel = idx[:, None] - idx[None, :]
    intra = jnp.exp(jnp.where((rel >= 0)[None], rel[None] * lg[:, None, None], -jnp.inf))
    inter = jnp.exp((idx[None, :] + 1.0) * lg[:, None])
    tail = jnp.exp((chunk - 1.0 - idx[None, :]) * lg[:, None])
    carry_decay = jnp.exp(chunk * lg)

    def body(s, xs):
        qc, kc, vc = xs
        att = jnp.einsum('bnhd,bmhd->bhnm', qc, kc) * intra
        o = (jnp.einsum('bhnm,bmhd->bnhd', att, vc)
             + jnp.einsum('bnhd,bhde,hn->bnhe', qc, s, inter))
        s = s * carry_decay[:, None, None] + jnp.einsum('bmhd,hm,bmhe->bhde', kc, tail, vc)
        return s, o

    s, o = lax.scan(body, s0, (to_chunks(q, chunk), to_chunks(k, chunk), to_chunks(v, chunk)))
    return from_chunks(o), s


def ssd_chunked(x, dt, A, bmat, cmat, h0, chunk):
    tri = jnp.tril(jnp.ones((chunk, chunk), dtype=bool))

    def body(h, xs):
        xc, dtc, bc, cc = xs
        cum = jnp.cumsum(dtc * A, axis=1)
        seg = cum[:, :, None, :] - cum[:, None, :, :]
        lmat = jnp.exp(jnp.where(tri[None, :, :, None], seg, -jnp.inf))
        scores = jnp.einsum('bthn,bshn->btsh', cc, bc) * lmat * dtc[:, None, :, :]
        y = (jnp.einsum('btsh,bshp->bthp', scores, xc)
             + jnp.einsum('bthn,bhpn->bthp', cc, h) * jnp.exp(cum)[..., None])
        w = jnp.exp(cum[:, -1:, :] - cum) * dtc
        h = h * jnp.exp(cum[:, -1, :])[:, :, None, None] + jnp.einsum('bsh,bshp,bshn->bhpn', w, xc, bc)
        return h, y

    h, y = lax.scan(body, h0, (to_chunks(x, chunk), to_chunks(dt, chunk),
                               to_chunks(bmat, chunk), to_chunks(cmat, chunk)))
    return from_chunks(y), h


def gated_delta_chunked(q, k, v, g, beta, s0, chunk):
    tri = jnp.tril(jnp.ones((chunk, chunk), dtype=bool))
    strict = jnp.tril(jnp.ones((chunk, chunk), dtype=bool), -1)

    def body(s, xs):
        qc, kc, vc, gc, bc = xs
        qh, kh, vh = (jnp.swapaxes(t, 1, 2) for t in (qc, kc, vc))
        gh, bh = jnp.swapaxes(gc, 1, 2), jnp.swapaxes(bc, 1, 2)
        cum = jnp.cumsum(gh, axis=-1)
        diff = cum[..., :, None] - cum[..., None, :]
        gam = jnp.exp(jnp.where(tri, diff, -jnp.inf))
        amat = jnp.einsum('bhtd,bhsd->bhts', kh, kh) * jnp.where(strict, gam, 0.0) * bh[..., :, None]
        rhs = jnp.concatenate([vh * bh[..., None], kh * (bh * jnp.exp(cum))[..., None]], axis=-1)
        sol = lax.linalg.triangular_solve(amat, rhs, left_side=True, lower=True, unit_diagonal=True)
        u, w = sol[..., :HEAD_DIM], sol[..., HEAD_DIM:]
        v_new = u - jnp.einsum('bhtd,bhde->bhte', w, s)
        qk = jnp.einsum('bhtd,bhsd->bhts', qh, kh) * gam
        o = (jnp.einsum('bhtd,bhde->bhte', qh * jnp.exp(cum)[..., None], s)
             + jnp.einsum('bhts,bhse->bhte', qk, v_new))
        s = (s * jnp.exp(cum[..., -1])[..., None, None]
             + jnp.einsum('bhsd,bhse->bhde', kh * jnp.exp(cum[..., -1:] - cum)[..., None], v_new))
        return s, jnp.swapaxes(o, 1, 2)

    s, o = lax.scan(body, s0, (to_chunks(q, chunk), to_chunks(k, chunk), to_chunks(v, chunk),
                               to_chunks(g, chunk), to_chunks(beta, chunk)))
    return from_chunks(o), s


def mlstm_chunked(q, k, v, logi, logf, c0, n0, m0, chunk):
    tri = jnp.tril(jnp.ones((chunk, chunk), dtype=bool))

    def body(carry, xs):
        cm, nm, mm = carry
        qc, kc, vc, ic, fc = xs
        qh, kh, vh = (jnp.swapaxes(t, 1, 2) for t in (qc, kc, vc))
        ih, fh = jnp.swapaxes(ic, 1, 2), jnp.swapaxes(fc, 1, 2)
        b = jnp.cumsum(fh, axis=-1)
        dmat = jnp.where(tri, b[..., :, None] - b[..., None, :] + ih[..., None, :], -jnp.inf)
        m_t = jnp.maximum(b + mm[..., None], jnp.max(dmat, axis=-1))
        wts = jnp.exp(dmat - m_t[..., None])
        inter = jnp.exp(b + mm[..., None] - m_t)
        qk = jnp.einsum('bhtd,bhsd->bhts', qh, kh) * wts
        num = (jnp.einsum('bhts,bhse->bhte', qk, vh)
               + inter[..., None] * jnp.einsum('bhtd,bhde->bhte', qh, cm))
        den = jnp.sum(qk, axis=-1) + inter * jnp.einsum('bhtd,bhd->bht', qh, nm)
        hout = num / jnp.maximum(jnp.abs(den), jnp.exp(-m_t))[..., None]
        m_new = m_t[..., -1]
        ws = jnp.exp(b[..., -1:] - b + ih - m_new[..., None])
        cs = jnp.exp(b[..., -1] + mm - m_new)
        cm = cm * cs[..., None, None] + jnp.einsum('bhs,bhsd,bhse->bhde', ws, kh, vh)
        nm = nm * cs[..., None] + jnp.einsum('bhs,bhsd->bhd', ws, kh)
        return (cm, nm, m_new), jnp.swapaxes(hout, 1, 2)

    (cm, nm, mm), h = lax.scan(body, (c0, n0, m0),
                               (to_chunks(q, chunk), to_chunks(k, chunk), to_chunks(v, chunk),
                                to_chunks(logi, chunk), to_chunks(logf, chunk)))
    return from_chunks(h), cm, nm, mm


def token_mixers(h, pos, p, st, chunk):
    f32 = jnp.float32
    bsz, L, _ = h.shape
    proj = (h @ p['w_in']).astype(f32)
    ret_p, ssm_p, gdn_p, ml_p = _split(proj, [RET_IN, SSM_IN, GDN_IN, MLSTM_IN])
    heads = lambda t: t.reshape(bsz, L, N_HEADS, -1)
    flat = lambda t: t.reshape(bsz, L, GROUP_W)

    rq, rk, rv, rg = _split(ret_p, [GROUP_W] * 4)
    rq = rotary(heads(rq), pos)
    rk = rotary(heads(rk), pos) * HEAD_DIM ** -0.5
    ro, ret_s = retention_chunked(rq, rk, heads(rv), st[0].astype(f32), chunk)
    out_ret = head_rmsnorm(flat(ro), p['ret_norm']) * jax.nn.silu(rg)

    z, xbc, dt_raw = _split(ssm_p, [GROUP_W, SSM_CONV_CH, N_HEADS])
    xbc, ssm_buf = causal_conv(xbc, st[2].astype(f32), p['ssm_conv_w'])
    xbc = jax.nn.silu(xbc + p['ssm_conv_b'])
    xs, bm, cm = _split(xbc, [GROUP_W, SSM_GROUPS * SSM_STATE, SSM_GROUPS * SSM_STATE])
    rep = N_HEADS // SSM_GROUPS
    bm = jnp.repeat(bm.reshape(bsz, L, SSM_GROUPS, SSM_STATE), rep, axis=2)
    cm = jnp.repeat(cm.reshape(bsz, L, SSM_GROUPS, SSM_STATE), rep, axis=2)
    dt = jax.nn.softplus(dt_raw + p['ssm_dt_bias'])
    A = -jnp.exp(p['ssm_A_log'].astype(f32))
    xh = heads(xs)
    y, ssm_h = ssd_chunked(xh, dt, A, bm, cm, st[1].astype(f32), chunk)
    y = y + p['ssm_D'][:, None] * xh
    out_ssm = rmsnorm(flat(y) * jax.nn.silu(z), p['ssm_norm'])

    qkv, gg, ga, gb = _split(gdn_p, [GDN_CONV_CH, GROUP_W, N_HEADS, N_HEADS])
    qkv, gdn_buf = causal_conv(qkv, st[4].astype(f32), p['gdn_conv_w'])
    qkv = jax.nn.silu(qkv)
    gq, gk, gv = _split(qkv, [GROUP_W] * 3)
    gq = l2norm(heads(gq)) * HEAD_DIM ** -0.5
    gk = l2norm(heads(gk))
    beta = jax.nn.sigmoid(gb)
    glog = -jnp.exp(p['gdn_A_log'].astype(f32)) * jax.nn.softplus(ga + p['gdn_dt_bias'])
    go, gdn_s = gated_delta_chunked(gq, gk, heads(gv), glog, beta, st[3].astype(f32), chunk)
    out_gdn = head_rmsnorm(flat(go), p['gdn_norm']) * jax.nn.silu(gg)

    mq, mk, mv, mo, mi, mf = _split(ml_p, [GROUP_W] * 4 + [N_HEADS] * 2)
    logi = mi + p['mlstm_i_bias']
    logf = jax.nn.log_sigmoid(mf + p['mlstm_f_bias'])
    mh, m_c, m_n, m_m = mlstm_chunked(heads(mq), heads(mk) * HEAD_DIM ** -0.5, heads(mv), logi, logf,
                                      st[5].astype(f32), st[6].astype(f32), st[7].astype(f32), chunk)
    out_ml = jax.nn.sigmoid(mo) * head_rmsnorm(flat(mh), p['mlstm_norm'])

    mixed = jnp.concatenate([out_ret, out_ssm, out_gdn, out_ml], axis=-1).astype(h.dtype) @ p['w_out']
    return mixed, (ret_s, ssm_h, ssm_buf, gdn_s, gdn_buf, m_c, m_n, m_m)


def swiglu(h, wg, wu, wd):
    return (jax.nn.silu(h @ wg) * (h @ wu)) @ wd


def run_trunk(x, pos, states, params, chunk):
    new = []
    for l in range(DEPTH):
        p = {name: arr[l] for name, arr in params.items()}
        st = tuple(s[l] for s in states)
        ns = p['norm_sandwich']
        hdn = rmsnorm(x, ns[0])
        x = x + 0.5 * rmsnorm(swiglu(hdn, p['ffn_w_gate'][0], p['ffn_w_up'][0], p['ffn_w_down'][0]), ns[1])
        hdn = rmsnorm(x, ns[2])
        mix, st_new = token_mixers(hdn, pos, p, st, chunk)
        x = x + rmsnorm(mix, ns[3])
        hdn = rmsnorm(x, ns[4])
        x = x + 0.5 * rmsnorm(swiglu(hdn, p['ffn_w_gate'][1], p['ffn_w_up'][1], p['ffn_w_down'][1]), ns[5])
        new.append(st_new)
    stacked = tuple(jnp.stack([n[i] for n in new], axis=0) for i in range(8))
    return x, stacked


def zero_states(b):
    f = jnp.float32
    return (jnp.zeros((DEPTH, b, N_HEADS, HEAD_DIM, HEAD_DIM), f),
            jnp.zeros((DEPTH, b, N_HEADS, HEAD_DIM, SSM_STATE), f),
            jnp.zeros((DEPTH, b, CONV_W - 1, SSM_CONV_CH), f),
            jnp.zeros((DEPTH, b, N_HEADS, HEAD_DIM, HEAD_DIM), f),
            jnp.zeros((DEPTH, b, CONV_W - 1, GDN_CONV_CH), f),
            jnp.zeros((DEPTH, b, N_HEADS, HEAD_DIM, HEAD_DIM), f),
            jnp.zeros((DEPTH, b, N_HEADS, HEAD_DIM), f),
            jnp.zeros((DEPTH, b, N_HEADS), f))


def setup_inputs(seed: int = 0) -> dict:
    key = jax.random.key(seed)
    ks = jax.random.split(key, 32)
    f = jnp.float32

    def nrm(i, shape, scale):
        return scale * jax.random.normal(ks[i], shape, f)

    def gain(i, shape):
        return 1.0 + 0.05 * jax.random.normal(ks[i], shape, f)

    def dt_bias(i):
        dt = jnp.exp(jax.random.uniform(ks[i], (DEPTH, N_HEADS), f, math.log(1e-3), math.log(1e-1)))
        return dt + jnp.log(-jnp.expm1(-dt))

    def a_log(i):
        return jnp.log(jax.random.uniform(ks[i], (DEPTH, N_HEADS), f, 1.0, 16.0))

    return {
        "x_prompt": nrm(0, (BATCH, SEQ, D_MODEL), 1.0),
        "x_sample": nrm(1, (DEC_BATCH, DEC_SEQ, D_MODEL), 1.0),
        "state_ret": nrm(2, (DEPTH, DEC_BATCH, N_HEADS, HEAD_DIM, HEAD_DIM), 0.5),
        "state_ssm": nrm(3, (DEPTH, DEC_BATCH, N_HEADS, HEAD_DIM, SSM_STATE), 0.5),
        "state_ssm_conv": nrm(4, (DEPTH, DEC_BATCH, CONV_W - 1, SSM_CONV_CH), 1.0),
        "state_gdn": nrm(5, (DEPTH, DEC_BATCH, N_HEADS, HEAD_DIM, HEAD_DIM), 0.2),
        "state_gdn_conv": nrm(6, (DEPTH, DEC_BATCH, CONV_W - 1, GDN_CONV_CH), 1.0),
        "state_mlstm_C": nrm(7, (DEPTH, DEC_BATCH, N_HEADS, HEAD_DIM, HEAD_DIM), 0.5),
        "state_mlstm_n": nrm(8, (DEPTH, DEC_BATCH, N_HEADS, HEAD_DIM), 0.5),
        "state_mlstm_m": nrm(9, (DEPTH, DEC_BATCH, N_HEADS), 1.0),
        "w_in": nrm(10, (DEPTH, D_MODEL, IN_DIM), D_MODEL ** -0.5),
        "w_out": nrm(11, (DEPTH, MIX_W, D_MODEL), MIX_W ** -0.5),
        "norm_sandwich": gain(12, (DEPTH, 6, D_MODEL)),
        "ffn_w_gate": nrm(13, (DEPTH, 2, D_MODEL, D_FF), D_MODEL ** -0.5),
        "ffn_w_up": nrm(14, (DEPTH, 2, D_MODEL, D_FF), D_MODEL ** -0.5),
        "ffn_w_down": nrm(15, (DEPTH, 2, D_FF, D_MODEL), D_FF ** -0.5),
        "ret_norm": gain(16, (DEPTH, GROUP_W)),
        "ssm_conv_w": nrm(17, (DEPTH, CONV_W, SSM_CONV_CH), CONV_W ** -0.5),
        "ssm_conv_b": nrm(18, (DEPTH, SSM_CONV_CH), 0.02),
        "ssm_dt_bias": dt_bias(19),
        "ssm_A_log": a_log(20),
        "ssm_D": 1.0 + 0.1 * jax.random.normal(ks[21], (DEPTH, N_HEADS), f),
        "ssm_norm": gain(22, (DEPTH, GROUP_W)),
        "gdn_conv_w": nrm(23, (DEPTH, CONV_W, GDN_CONV_CH), CONV_W ** -0.5),
        "gdn_dt_bias": dt_bias(24),
        "gdn_A_log": a_log(25),
        "gdn_norm": gain(26, (DEPTH, GROUP_W)),
        "mlstm_i_bias": nrm(27, (DEPTH, N_HEADS), 0.1),
        "mlstm_f_bias": jnp.linspace(3.0, 6.0, N_HEADS, dtype=f)[None, :] + nrm(28, (DEPTH, N_HEADS), 0.1),
        "mlstm_norm": gain(29, (DEPTH, GROUP_W)),
    }


def reference(x_prompt, x_sample, state_ret, state_ssm, state_ssm_conv, state_gdn, state_gdn_conv,
              state_mlstm_C, state_mlstm_n, state_mlstm_m, w_in, w_out, norm_sandwich,
              ffn_w_gate, ffn_w_up, ffn_w_down, ret_norm, ssm_conv_w, ssm_conv_b, ssm_dt_bias,
              ssm_A_log, ssm_D, ssm_norm, gdn_conv_w, gdn_dt_bias, gdn_A_log, gdn_norm,
              mlstm_i_bias, mlstm_f_bias, mlstm_norm):
    params = dict(w_in=w_in, w_out=w_out, norm_sandwich=norm_sandwich, ffn_w_gate=ffn_w_gate,
                  ffn_w_up=ffn_w_up, ffn_w_down=ffn_w_down, ret_norm=ret_norm,
                  ssm_conv_w=ssm_conv_w, ssm_conv_b=ssm_conv_b, ssm_dt_bias=ssm_dt_bias,
                  ssm_A_log=ssm_A_log, ssm_D=ssm_D, ssm_norm=ssm_norm, gdn_conv_w=gdn_conv_w,
                  gdn_dt_bias=gdn_dt_bias, gdn_A_log=gdn_A_log, gdn_norm=gdn_norm,
                  mlstm_i_bias=mlstm_i_bias, mlstm_f_bias=mlstm_f_bias, mlstm_norm=mlstm_norm)

    lp = x_prompt.shape[1]
    pos_p = jnp.arange(lp, dtype=jnp.int32)
    y_prompt, pst = run_trunk(x_prompt, pos_p, zero_states(x_prompt.shape[0]), params, pick_chunk(lp))
    p_ret, p_ssm, p_ssm_conv, p_gdn, p_gdn_conv, p_mC, p_mn, p_mm = pst

    ls = x_sample.shape[1]
    pos_s = PAST_LEN + jnp.arange(ls, dtype=jnp.int32)
    states_s = (state_ret, state_ssm, state_ssm_conv, state_gdn, state_gdn_conv,
                state_mlstm_C, state_mlstm_n, state_mlstm_m)
    y_sample, sst = run_trunk(x_sample, pos_s, states_s, params, pick_chunk(ls))
    s_ret, s_ssm, s_ssm_conv, s_gdn, s_gdn_conv, s_mC, s_mn, s_mm = sst

    return (y_prompt, y_sample, p_ret, p_ssm, p_ssm_conv, p_gdn, p_gdn_conv, p_mC, p_mn, p_mm,
            s_ret, s_ssm, s_ssm_conv, s_gdn, s_gdn_conv, s_mC, s_mn, s_mm)
```

```python
import functools
import math

import numpy as np
import jax
import jax.numpy as jnp
from jax import lax
from jax.experimental import pallas as pl
from jax.experimental.pallas import tpu as pltpu

F32 = jnp.float32
BF16 = jnp.bfloat16

D_MODEL = 1024
N_HEADS = 4
HEAD_DIM = 64
GROUP_W = N_HEADS * HEAD_DIM
D_FF = 2816
CONV_W = 4
EPS = 1e-6
ROPE_BASE = 10000.0
PAST_LEN = 16384
PROMPT_CHUNK = 64

COL_RET = 0
COL_SSM_Z = 1024
COL_CONV = 1280
CONV_CH = 1280
SSM_CONV_CH = 512
COL_GDN_GATE = 2560
COL_ML = 2816
COL_GATES = 3840
IN_PAD = 3968
G_DT, G_GA, G_GB, G_MI, G_MF, G_DTV = 0, 4, 8, 12, 16, 20

SUBLANES = 8
LANES = 128
VMEM_LIMIT = 56 * 1024 * 1024

FF_CHUNK = 256
TOKEN_TILE = 512
PROMPT_TILE = 256
SAMPLE_PAD = 8
SAMPLE_GROUP = 8
NEG_BIG = -1e30


def _sigmoid(x):
    return 1.0 / (1.0 + jnp.exp(-x))


def _silu(x):
    return x * _sigmoid(x)


def _softplus(x):
    return jnp.maximum(x, 0.0) + jnp.log1p(jnp.exp(-jnp.abs(x)))


def _rms(x, g):
    return x * lax.rsqrt(jnp.mean(x * x, axis=-1, keepdims=True) + EPS) * g


def _dot(a, b):
    return jnp.dot(a.astype(BF16), b.astype(BF16), preferred_element_type=F32)


def _dot_nt(a, b):
    return lax.dot_general(a.astype(BF16), b.astype(BF16), (((1,), (1,)), ((), ())),
                           preferred_element_type=F32)


def _dot_tn(a, b):
    return lax.dot_general(a.astype(BF16), b.astype(BF16), (((0,), (0,)), ((), ())),
                           preferred_element_type=F32)


def _split3(x):
    hi = x.astype(BF16)
    r = x - hi.astype(F32)
    mid = r.astype(BF16)
    lo = (r - mid.astype(F32)).astype(BF16)
    return hi, mid, lo


def _dot_sel(sel, x):
    hi, mid, lo = _split3(x)
    f = lambda p: jnp.dot(sel, p, preferred_element_type=F32)
    return (f(lo) + f(mid)) + f(hi)


def _dot_sel_r(x, sel):
    hi, mid, lo = _split3(x)
    f = lambda p: jnp.dot(p, sel, preferred_element_type=F32)
    return (f(lo) + f(mid)) + f(hi)


def _dot_sel_nt(sel, x):
    hi, mid, lo = _split3(x)
    f = lambda p: lax.dot_general(sel, p, (((1,), (1,)), ((), ())), preferred_element_type=F32)
    return (f(lo) + f(mid)) + f(hi)


def _ffn_body(x_ref, g_ref, wg_ref, wu_ref, wd_ref, o_ref, a_ref):
    x = x_ref[...]
    h = _rms(x, g_ref[0:1, :]).astype(BF16)
    for j in range(D_FF // FF_CHUNK):
        sl = slice(j * FF_CHUNK, (j + 1) * FF_CHUNK)
        gt = jnp.dot(h, wg_ref[:, sl], preferred_element_type=F32)
        up = jnp.dot(h, wu_ref[:, sl], preferred_element_type=F32)
        a_ref[:, sl] = (_silu(gt) * up).astype(BF16)
    y = jnp.dot(a_ref[...], wd_ref[...], preferred_element_type=F32)
    o_ref[...] = x + 0.5 * _rms(y, g_ref[1:2, :])


def _resident(shape):
    return pl.BlockSpec(shape, lambda i: (0,) * len(shape), pipeline_mode=pl.Buffered(1))


def _ffn(x, gains, wg, wu, wd):
    m = x.shape[0]
    tm = min(TOKEN_TILE, m)
    return pl.pallas_call(
        _ffn_body,
        out_shape=jax.ShapeDtypeStruct((m, D_MODEL), F32),
        grid=(m // tm,),
        in_specs=[pl.BlockSpec((tm, D_MODEL), lambda i: (i, 0)),
                  _resident((2, D_MODEL)),
                  _resident((D_MODEL, D_FF)),
                  _resident((D_MODEL, D_FF)),
                  _resident((D_FF, D_MODEL))],
        out_specs=pl.BlockSpec((tm, D_MODEL), lambda i: (i, 0)),
        scratch_shapes=[pltpu.VMEM((tm, D_FF), BF16)],
        compiler_params=pltpu.CompilerParams(dimension_semantics=("parallel",),
                                             vmem_limit_bytes=VMEM_LIMIT),
        name="ffn",
    )(x, gains, wg, wu, wd)


def _inproj_body(x_ref, g_ref, w_ref, o_ref):
    h = _rms(x_ref[...], g_ref[...]).astype(BF16)
    o_ref[...] = jnp.dot(h, w_ref[...], preferred_element_type=F32)


def _inproj(x, gain, w):
    m = x.shape[0]
    tm = min(TOKEN_TILE, m)
    return pl.pallas_call(
        _inproj_body,
        out_shape=jax.ShapeDtypeStruct((m, IN_PAD), F32),
        grid=(m // tm,),
        in_specs=[pl.BlockSpec((tm, D_MODEL), lambda i: (i, 0)),
                  _resident((1, D_MODEL)),
                  _resident((D_MODEL, IN_PAD))],
        out_specs=pl.BlockSpec((tm, IN_PAD), lambda i: (i, 0)),
        compiler_params=pltpu.CompilerParams(dimension_semantics=("parallel",),
                                             vmem_limit_bytes=VMEM_LIMIT),
        name="inproj",
    )(x, gain, w)


def _outproj_body(x_ref, m_ref, g_ref, w_ref, o_ref):
    y = jnp.dot(m_ref[...], w_ref[...], preferred_element_type=F32)
    o_ref[...] = x_ref[...] + _rms(y, g_ref[...])


def _outproj(x, mixed, gain, w):
    m = x.shape[0]
    tm = min(TOKEN_TILE, m)
    return pl.pallas_call(
        _outproj_body,
        out_shape=jax.ShapeDtypeStruct((m, D_MODEL), F32),
        grid=(m // tm,),
        in_specs=[pl.BlockSpec((tm, D_MODEL), lambda i: (i, 0)),
                  pl.BlockSpec((tm, D_MODEL), lambda i: (i, 0)),
                  _resident((1, D_MODEL)),
                  _resident((D_MODEL, D_MODEL))],
        out_specs=pl.BlockSpec((tm, D_MODEL), lambda i: (i, 0)),
        compiler_params=pltpu.CompilerParams(dimension_semantics=("parallel",),
                                             vmem_limit_bytes=VMEM_LIMIT),
        name="outproj",
    )(x, mixed, gain, w)


def _head(x, h):
    return x[:, h * HEAD_DIM:(h + 1) * HEAD_DIM]


def _neumann_inverse(amat, c):
    row = lax.broadcasted_iota(jnp.int32, (c, c), 0)
    col = lax.broadcasted_iota(jnp.int32, (c, c), 1)
    eye = (row == col).astype(F32)
    m = -amat
    p = eye + m
    for _ in range(max(int(math.ceil(math.log2(c))) - 1, 0)):
        m = _dot(m, m)
        p = p + _dot(p, m)
    return p


def _mix_body(*refs, sample, rows, chunk, ret_carry):
    nchunks = rows // chunk
    it = iter(refs)
    proj_ref = next(it)
    hist_ref = next(it) if sample else None
    cos_ref, sin_ref = next(it), next(it)
    rinter_ref, rtail_ref, rintra_ref = next(it), next(it), next(it)
    tri_ref, sel_ref, ones_ref = next(it), next(it), next(it)
    gvec_ref, convw_ref, convb_ref, dvec_ref, norms_ref = next(it), next(it), next(it), next(it), next(it)
    if sample:
        sret_in, sssm_in, sgdn_in, smc_in, smn_in, smm_in = (next(it) for _ in range(6))
    mixed_ref, convraw_ref = next(it), next(it)
    sret_ref, sssm_ref, sgdn_ref, smc_ref, smn_ref, smm_ref = (next(it) for _ in range(6))
    e_ref, g_ref, rq_ref, rk_ref, sx_ref, sbc_ref, gq_ref, gk_ref, gv_ref, o_ref = (next(it) for _ in range(10))
    if not sample:
        sret_in, sssm_in, sgdn_in, smc_in, smn_in, smm_in = (
            sret_ref, sssm_ref, sgdn_ref, smc_ref, smn_ref, smm_ref)

    if sample:
        rowid = lax.broadcasted_iota(jnp.int32, (rows, 1), 0)
        valid = (rowid % SAMPLE_PAD) >= (SAMPLE_PAD - 4)
        e_ref[0:SUBLANES, :] = jnp.zeros((SUBLANES, CONV_CH), F32)
    else:
        valid = None

        @pl.when(pl.program_id(1) == 0)
        def _():
            e_ref[0:SUBLANES, :] = jnp.zeros((SUBLANES, CONV_CH), F32)
            sret_ref[...] = jnp.zeros_like(sret_ref)
            sssm_ref[...] = jnp.zeros_like(sssm_ref)
            sgdn_ref[...] = jnp.zeros_like(sgdn_ref)
            smc_ref[...] = jnp.zeros_like(smc_ref)
            smn_ref[...] = jnp.zeros_like(smn_ref)
            smm_ref[...] = jnp.zeros_like(smm_ref)

    def keep(x, fill=0.0):
        return x if valid is None else jnp.where(valid, x, fill)

    lane = lax.broadcasted_iota(jnp.int32, (1, LANES), 1)
    in_lanes = lambda lo: (lane >= lo) & (lane < lo + N_HEADS)
    graw = proj_ref[:, COL_GATES:COL_GATES + LANES]
    xb = graw + gvec_ref[0:1, :]
    sp = _softplus(xb)
    neg_a = -jnp.exp(gvec_ref[1:2, :])
    logf = -_softplus(-xb)
    pre = jnp.where(lane < 2 * N_HEADS, neg_a * sp, jnp.where(in_lanes(G_MF), logf, 0.0))
    cum = _dot_sel(tri_ref[...], keep(pre))
    dtv = pltpu.roll(sp, G_DTV - G_DT, axis=1)
    aux = jnp.where(in_lanes(G_GB), _sigmoid(graw),
                    jnp.where(in_lanes(G_MI), xb, jnp.where(in_lanes(G_DTV), dtv, 0.0)))
    if sample:
        aux = jnp.where(valid, aux, jnp.where(in_lanes(G_MI), NEG_BIG, 0.0))
    g_ref[...] = jnp.where(in_lanes(G_GB) | in_lanes(G_MI) | in_lanes(G_DTV), aux, cum)

    lane_g = lax.broadcasted_iota(jnp.int32, (1, GROUP_W), 1)
    first_half = (lane_g % HEAD_DIM) < (HEAD_DIM // 2)

    def rotary(x):
        swapped = jnp.where(first_half, pltpu.roll(x, GROUP_W - HEAD_DIM // 2, axis=1),
                            pltpu.roll(x, HEAD_DIM // 2, axis=1))
        return x * cos_ref[...] + swapped * sin_ref[...]

    rq_ref[...] = rotary(proj_ref[:, COL_RET:COL_RET + GROUP_W])
    rk_ref[...] = keep(rotary(proj_ref[:, COL_RET + GROUP_W:COL_RET + 2 * GROUP_W]) * HEAD_DIM ** -0.5)

    raw = proj_ref[:, COL_CONV:COL_CONV + CONV_CH]
    if sample:
        raw = raw + hist_ref[...]
    e_ref[SUBLANES:SUBLANES + rows, :] = raw
    conv = raw * convw_ref[CONV_W - 1:CONV_W, :]
    for s in range(1, CONV_W):
        conv = conv + e_ref[SUBLANES - s:SUBLANES - s + rows, :] * convw_ref[CONV_W - 1 - s:CONV_W - s, :]
    conv = keep(_silu(conv + convb_ref[...]))
    if sample:
        convraw_ref[...] = raw
    else:
        convraw_ref[...] = e_ref[rows:rows + SUBLANES, :]
        e_ref[0:SUBLANES, :] = e_ref[rows:rows + SUBLANES, :]
    sx_ref[...] = conv[:, 0:GROUP_W]
    sbc_ref[...] = conv[:, GROUP_W:SSM_CONV_CH]
    ones_bd = ones_ref[...]

    def head_sumsq(x):
        return _dot_sel_r(x * x, ones_bd)

    gq = conv[:, SSM_CONV_CH:SSM_CONV_CH + GROUP_W]
    gk = conv[:, SSM_CONV_CH + GROUP_W:SSM_CONV_CH + 2 * GROUP_W]
    gq_ref[...] = gq * lax.rsqrt(head_sumsq(gq) + EPS) * HEAD_DIM ** -0.5
    gk_ref[...] = gk * lax.rsqrt(head_sumsq(gk) + EPS)
    gv_ref[...] = conv[:, SSM_CONV_CH + 2 * GROUP_W:CONV_CH]

    r_i = lax.broadcasted_iota(jnp.int32, (chunk, chunk), 0)
    c_i = lax.broadcasted_iota(jnp.int32, (chunk, chunk), 1)
    tri = r_i >= c_i
    strict = r_i > c_i
    lane_h = lax.broadcasted_iota(jnp.int32, (1, N_HEADS), 1)

    def chunk_step(i, carry):
        r0 = pl.multiple_of(i * chunk, chunk)
        rs = pl.ds(r0, chunk)
        sidx = i if sample else 0
        gc = g_ref[rs, :]
        gt = _dot_sel_nt(sel_ref[...], gc)
        col = lambda j: gc[:, j:j + 1]
        rowv = lambda j: gt[j:j + 1, :]
        last = lambda j: gc[chunk - 1:chunk, j:j + 1]

        q_all = rq_ref[rs, :]
        k_all = rk_ref[rs, :]
        v_all = proj_ref[rs, COL_RET + 2 * GROUP_W:COL_RET + 3 * GROUP_W]
        qi_all = q_all * rinter_ref[...]
        kt_all = k_all * rtail_ref[...]
        outs = []
        for h in range(N_HEADS):
            s_h = sret_in[sidx, h]
            att = _dot_nt(_head(q_all, h), _head(k_all, h)) * rintra_ref[h]
            outs.append(_dot(att, _head(v_all, h)) + _dot(_head(qi_all, h), s_h))
            sret_ref[sidx, h] = s_h * ret_carry[h] + _dot_tn(_head(kt_all, h), _head(v_all, h))
        o_ref[rs, 0:GROUP_W] = jnp.concatenate(outs, axis=1)

        x_all = sx_ref[rs, :]
        bc_all = sbc_ref[rs, :]
        outs = []
        for h in range(N_HEADS):
            grp = h // (N_HEADS // 2)
            b_g = bc_all[:, grp * HEAD_DIM:(grp + 1) * HEAD_DIM]
            c_g = bc_all[:, 2 * HEAD_DIM + grp * HEAD_DIM:2 * HEAD_DIM + (grp + 1) * HEAD_DIM]
            x_h = _head(x_all, h)
            cum_c, cum_r, cum_l = col(G_DT + h), rowv(G_DT + h), last(G_DT + h)
            lmat = jnp.exp(jnp.where(tri, cum_c - cum_r, -jnp.inf))
            scores = _dot_nt(c_g, b_g) * lmat * rowv(G_DTV + h)
            st = sssm_in[sidx, h]
            outs.append(_dot(scores, x_h) + _dot_nt(c_g, st) * jnp.exp(cum_c))
            w_c = jnp.exp(cum_l - cum_c) * col(G_DTV + h)
            sssm_ref[sidx, h] = st * jnp.exp(cum_l) + _dot_tn(x_h * w_c, b_g)
        o_ref[rs, GROUP_W:2 * GROUP_W] = jnp.concatenate(outs, axis=1)

        q_all = gq_ref[rs, :]
        k_all = gk_ref[rs, :]
        v_all = gv_ref[rs, :]
        outs = []
        for h in range(N_HEADS):
            q_h, k_h, v_h = _head(q_all, h), _head(k_all, h), _head(v_all, h)
            cum_c, cum_r, cum_l = col(G_GA + h), rowv(G_GA + h), last(G_GA + h)
            beta_c = col(G_GB + h)
            gam = jnp.exp(jnp.where(tri, cum_c - cum_r, -jnp.inf))
            amat = _dot_nt(k_h, k_h) * jnp.where(strict, gam, 0.0) * beta_c
            ecum = jnp.exp(cum_c)
            tinv = _neumann_inverse(amat, chunk)
            u = _dot(tinv, v_h * beta_c)
            w = _dot(tinv, k_h * (beta_c * ecum))
            st = sgdn_in[sidx, h]
            v_new = u - _dot(w, st)
            qk = _dot_nt(q_h, k_h) * gam
            outs.append(_dot(q_h * ecum, st) + _dot(qk, v_new))
            sgdn_ref[sidx, h] = st * jnp.exp(cum_l) + _dot_tn(k_h * jnp.exp(cum_l - cum_c), v_new)
        o_ref[rs, 2 * GROUP_W:3 * GROUP_W] = jnp.concatenate(outs, axis=1)

        q_all = proj_ref[rs, COL_ML:COL_ML + GROUP_W]
        k_all = proj_ref[rs, COL_ML + GROUP_W:COL_ML + 2 * GROUP_W] * HEAD_DIM ** -0.5
        v_all = proj_ref[rs, COL_ML + 2 * GROUP_W:COL_ML + 3 * GROUP_W]
        m_row = smm_in[sidx]
        m_new_row = jnp.zeros((1, N_HEADS), F32)
        outs = []
        for h in range(N_HEADS):
            q_h, k_h, v_h = _head(q_all, h), _head(k_all, h), _head(v_all, h)
            b_c, b_r, b_l = col(G_MF + h), rowv(G_MF + h), last(G_MF + h)
            i_c, i_r = col(G_MI + h), rowv(G_MI + h)
            m_old = m_row[:, h:h + 1]
            dmat = jnp.where(tri, b_c - b_r + i_r, -jnp.inf)
            m_t = jnp.maximum(b_c + m_old, jnp.max(dmat, axis=-1, keepdims=True))
            wts = jnp.exp(dmat - m_t)
            inter = jnp.exp(b_c + m_old - m_t)
            qk = _dot_nt(q_h, k_h) * wts
            c_h = smc_in[sidx, h]
            n_h = smn_in[sidx, pl.ds(h, 1), :]
            num = _dot(qk, v_h) + inter * _dot(q_h, c_h)
            den = jnp.sum(qk, axis=-1, keepdims=True) + inter * jnp.sum(q_h * n_h, axis=-1, keepdims=True)
            outs.append(num / jnp.maximum(jnp.abs(den), jnp.exp(-m_t)))
            m_new = m_t[chunk - 1:chunk, :]
            ws = jnp.exp(b_l - b_c + i_c - m_new)
            cs = jnp.exp(b_l + m_old - m_new)
            kw = k_h * ws
            smc_ref[sidx, h] = c_h * cs + _dot_tn(kw, v_h)
            smn_ref[sidx, pl.ds(h, 1), :] = n_h * cs + jnp.sum(kw, axis=0, keepdims=True)
            m_new_row = jnp.where(lane_h == h, m_new, m_new_row)
        smm_ref[sidx] = m_new_row
        o_ref[rs, 3 * GROUP_W:4 * GROUP_W] = jnp.concatenate(outs, axis=1)
        return carry

    lax.fori_loop(0, nchunks, chunk_step, 0)

    def head_norm(x, g):
        return x * lax.rsqrt(head_sumsq(x) * (1.0 / HEAD_DIM) + EPS) * g

    out_ret = head_norm(o_ref[:, 0:GROUP_W], norms_ref[0:1, :]) * _silu(
        proj_ref[:, COL_RET + 3 * GROUP_W:COL_RET + 4 * GROUP_W])
    mixed_ref[:, 0:GROUP_W] = keep(out_ret).astype(BF16)
    y = o_ref[:, GROUP_W:2 * GROUP_W] + dvec_ref[...] * sx_ref[...]
    out_ssm = _rms(y * _silu(proj_ref[:, COL_SSM_Z:COL_SSM_Z + GROUP_W]), norms_ref[1:2, :])
    mixed_ref[:, GROUP_W:2 * GROUP_W] = keep(out_ssm).astype(BF16)
    out_gdn = head_norm(o_ref[:, 2 * GROUP_W:3 * GROUP_W], norms_ref[2:3, :]) * _silu(
        proj_ref[:, COL_GDN_GATE:COL_GDN_GATE + GROUP_W])
    mixed_ref[:, 2 * GROUP_W:3 * GROUP_W] = keep(out_gdn).astype(BF16)
    out_ml = _sigmoid(proj_ref[:, COL_ML + 3 * GROUP_W:COL_ML + 4 * GROUP_W]) * head_norm(
        o_ref[:, 3 * GROUP_W:4 * GROUP_W], norms_ref[3:4, :])
    mixed_ref[:, 3 * GROUP_W:4 * GROUP_W] = keep(out_ml).astype(BF16)


def _ret_tables(chunk, pad):
    lg = np.log(1.0 - np.exp2(-5.0 - np.arange(N_HEADS, dtype=np.float64)))
    idx = np.arange(chunk, dtype=np.float64)
    rel = idx[:, None] - idx[None, :]
    intra = np.where(rel[None] >= 0, np.exp(np.maximum(rel[None], 0.0) * lg[:, None, None]), 0.0)
    inter = np.where(idx[None, :] >= pad, np.exp((idx[None, :] - pad + 1.0) * lg[:, None]), 0.0)
    tail = np.exp((chunk - 1.0 - idx[None, :]) * lg[:, None])
    carry = np.exp((chunk - pad) * lg)
    lanes = lambda t: np.repeat(t.T, HEAD_DIM, axis=1)
    return (jnp.asarray(lanes(inter), F32), jnp.asarray(lanes(tail), F32), jnp.asarray(intra, F32),
            tuple(float(c) for c in carry))


def _const_mats(rows, chunk):
    r = np.arange(rows)
    tri = ((r[:, None] >= r[None, :]) & (r[:, None] // chunk == r[None, :] // chunk))
    sel = np.zeros((4 * SUBLANES, LANES), np.float32)
    sel[np.arange(24), np.arange(24)] = 1.0
    g = np.arange(GROUP_W)
    ones_bd = (g[:, None] // HEAD_DIM == g[None, :] // HEAD_DIM)
    return (jnp.asarray(tri, BF16), jnp.asarray(sel, BF16), jnp.asarray(ones_bd, BF16))


def _mix(proj, hist, cos, sin, gvec, convw, convb, dvec, norms, states, *, sample, nbatch, seqlen):
    if sample:
        rows, chunk, pad = SAMPLE_GROUP * SAMPLE_PAD, SAMPLE_PAD, SAMPLE_PAD - 4
        grid = (nbatch // SAMPLE_GROUP,)
        nstate = SAMPLE_GROUP
        row_map = lambda i: (i, 0)
        st_map4 = lambda i: (i, 0, 0, 0)
        st_map3 = lambda i: (i, 0, 0)
        const_map2 = lambda i: (0, 0)
        const_map3 = lambda i: (0, 0, 0)
        rot_map = const_map2
        semantics = ("parallel",)
    else:
        rows, chunk, pad = min(PROMPT_TILE, seqlen), PROMPT_CHUNK, 0
        ntile = seqlen // rows
        grid = (nbatch, ntile)
        nstate = 1
        row_map = lambda b, t: (b * ntile + t, 0)
        st_map4 = lambda b, t: (b, 0, 0, 0)
        st_map3 = lambda b, t: (b, 0, 0)
        const_map2 = lambda b, t: (0, 0)
        const_map3 = lambda b, t: (0, 0, 0)
        rot_map = lambda b, t: (t, 0)
        semantics = ("parallel", "arbitrary")
    rinter, rtail, rintra, carry = _ret_tables(chunk, pad)
    tri, sel, ones_bd = _const_mats(rows, chunk)
    m = proj.shape[0]

    in_specs = [pl.BlockSpec((rows, IN_PAD), row_map)]
    args = [proj]
    if sample:
        in_specs.append(pl.BlockSpec((rows, CONV_CH), row_map))
        args.append(hist)
    in_specs += [pl.BlockSpec((rows, GROUP_W), rot_map), pl.BlockSpec((rows, GROUP_W), rot_map),
                 pl.BlockSpec((chunk, GROUP_W), const_map2), pl.BlockSpec((chunk, GROUP_W), const_map2),
                 pl.BlockSpec((N_HEADS, chunk, chunk), const_map3),
                 pl.BlockSpec((rows, rows), const_map2), pl.BlockSpec((4 * SUBLANES, LANES), const_map2),
                 pl.BlockSpec((GROUP_W, GROUP_W), const_map2),
                 pl.BlockSpec((2, LANES), const_map2), pl.BlockSpec((CONV_W, CONV_CH), const_map2),
                 pl.BlockSpec((1, CONV_CH), const_map2), pl.BlockSpec((1, GROUP_W), const_map2),
                 pl.BlockSpec((4, GROUP_W), const_map2)]
    args += [cos, sin, rinter, rtail, rintra, tri, sel, ones_bd, gvec, convw, convb, dvec, norms]
    st4 = pl.BlockSpec((nstate, N_HEADS, HEAD_DIM, HEAD_DIM), st_map4)
    stn = pl.BlockSpec((nstate, N_HEADS, HEAD_DIM), st_map3)
    stm = pl.BlockSpec((nstate, 1, N_HEADS), st_map3)
    state_specs = [st4, st4, st4, st4, stn, stm]
    if sample:
        in_specs += state_specs
        args += list(states)
    conv_rows = rows if sample else SUBLANES
    conv_total = m if sample else nbatch * SUBLANES
    conv_map = row_map if sample else (lambda b, t: (b, 0))
    out_shape = [jax.ShapeDtypeStruct((m, D_MODEL), BF16),
                 jax.ShapeDtypeStruct((conv_total, CONV_CH), F32),
                 jax.ShapeDtypeStruct((nbatch, N_HEADS, HEAD_DIM, HEAD_DIM), F32),
                 jax.ShapeDtypeStruct((nbatch, N_HEADS, HEAD_DIM, HEAD_DIM), F32),
                 jax.ShapeDtypeStruct((nbatch, N_HEADS, HEAD_DIM, HEAD_DIM), F32),
                 jax.ShapeDtypeStruct((nbatch, N_HEADS, HEAD_DIM, HEAD_DIM), F32),
                 jax.ShapeDtypeStruct((nbatch, N_HEADS, HEAD_DIM), F32),
                 jax.ShapeDtypeStruct((nbatch, 1, N_HEADS), F32)]
    out_specs = [pl.BlockSpec((rows, D_MODEL), row_map), pl.BlockSpec((conv_rows, CONV_CH), conv_map)] + state_specs
    scratch = [pltpu.VMEM((rows + SUBLANES, CONV_CH), F32), pltpu.VMEM((rows, LANES), F32)]
    scratch += [pltpu.VMEM((rows, GROUP_W), F32) for _ in range(7)]
    scratch += [pltpu.VMEM((rows, D_MODEL), F32)]
    body = functools.partial(_mix_body, sample=sample, rows=rows, chunk=chunk, ret_carry=carry)
    return pl.pallas_call(
        body, out_shape=out_shape, grid=grid, in_specs=in_specs, out_specs=out_specs,
        scratch_shapes=scratch,
        compiler_params=pltpu.CompilerParams(dimension_semantics=semantics, vmem_limit_bytes=VMEM_LIMIT),
        name="mix_sample" if sample else "mix_prompt",
    )(*args)


def _rot_tables(pos):
    half = HEAD_DIM // 2
    inv = ROPE_BASE ** (-jnp.arange(half, dtype=F32) / half)
    ang = pos.astype(F32)[:, None] * inv[None, :]
    cos, sin = jnp.cos(ang), jnp.sin(ang)
    cos_full = jnp.tile(jnp.concatenate([cos, cos], axis=-1), (1, N_HEADS))
    sin_signed = jnp.tile(jnp.concatenate([-sin, sin], axis=-1), (1, N_HEADS))
    return cos_full, sin_signed


def _layer_params(l, w_in, w_out, norm_sandwich, ffn_w_gate, ffn_w_up, ffn_w_down, ret_norm, ssm_conv_w,
                  ssm_conv_b, ssm_dt_bias, ssm_A_log, ssm_D, ssm_norm, gdn_conv_w, gdn_dt_bias, gdn_A_log,
                  gdn_norm, mlstm_i_bias, mlstm_f_bias, mlstm_norm):
    w = w_in[l]
    zeros4 = jnp.zeros((N_HEADS,), F32)
    w_in_p = jnp.concatenate(
        [w[:, 0:1792], w[:, 1796:2820], w[:, 2828:3852], w[:, 1792:1796], w[:, 2820:2828], w[:, 3852:3860],
         jnp.zeros((D_MODEL, IN_PAD - COL_GATES - 5 * N_HEADS), F32)], axis=1).astype(BF16)
    pad_lanes = jnp.zeros((LANES - 5 * N_HEADS,), F32)
    bias = jnp.concatenate([ssm_dt_bias[l], gdn_dt_bias[l], zeros4, mlstm_i_bias[l], mlstm_f_bias[l], pad_lanes])
    alog = jnp.concatenate([ssm_A_log[l], gdn_A_log[l], zeros4, zeros4, zeros4, pad_lanes])
    ns = norm_sandwich[l]
    return dict(
        w_in=w_in_p, w_out=w_out[l].astype(BF16),
        ffn=[(jnp.stack([ns[4 * k], ns[4 * k + 1]]),
              ffn_w_gate[l, k].astype(BF16), ffn_w_up[l, k].astype(BF16), ffn_w_down[l, k].astype(BF16))
             for k in range(2)],
        g_in=ns[2][None, :], g_out=ns[3][None, :],
        gvec=jnp.stack([bias, alog]),
        convw=jnp.concatenate([ssm_conv_w[l], gdn_conv_w[l]], axis=1),
        convb=jnp.concatenate([ssm_conv_b[l], jnp.zeros((CONV_CH - SSM_CONV_CH,), F32)])[None, :],
        dvec=jnp.repeat(ssm_D[l], HEAD_DIM)[None, :],
        norms=jnp.stack([ret_norm[l], ssm_norm[l], gdn_norm[l], mlstm_norm[l]]),
    )


def _trunk(x, p_layers, cos, sin, states, *, sample, nbatch, seqlen):
    new_states = []
    for l, p in enumerate(p_layers):
        x = _ffn(x, *p['ffn'][0])
        proj = _inproj(x, p['g_in'], p['w_in'])
        hist, st = (states[l][0], states[l][1:]) if sample else (None, None)
        outs = _mix(proj, hist, cos, sin, p['gvec'], p['convw'], p['convb'], p['dvec'], p['norms'], st,
                    sample=sample, nbatch=nbatch, seqlen=seqlen)
        x = _outproj(x, outs[0], p['g_out'], p['w_out'])
        x = _ffn(x, *p['ffn'][1])
        new_states.append(outs[1:])
    return x, new_states


def kernel(x_prompt, x_sample, state_ret, state_ssm, state_ssm_conv, state_gdn, state_gdn_conv,
           state_mlstm_C, state_mlstm_n, state_mlstm_m, w_in, w_out, norm_sandwich, ffn_w_gate, ffn_w_up,
           ffn_w_down, ret_norm, ssm_conv_w, ssm_conv_b, ssm_dt_bias, ssm_A_log, ssm_D, ssm_norm, gdn_conv_w,
           gdn_dt_bias, gdn_A_log, gdn_norm, mlstm_i_bias, mlstm_f_bias, mlstm_norm):
    depth = w_in.shape[0]
    bp, lp, _ = x_prompt.shape
    bs, ls, _ = x_sample.shape
    pad = SAMPLE_PAD - ls
    p_layers = [_layer_params(l, w_in, w_out, norm_sandwich, ffn_w_gate, ffn_w_up, ffn_w_down, ret_norm,
                              ssm_conv_w, ssm_conv_b, ssm_dt_bias, ssm_A_log, ssm_D, ssm_norm, gdn_conv_w,
                              gdn_dt_bias, gdn_A_log, gdn_norm, mlstm_i_bias, mlstm_f_bias, mlstm_norm)
                for l in range(depth)]

    cos_p, sin_p = _rot_tables(jnp.arange(lp, dtype=jnp.int32))
    y_p, st_p = _trunk(x_prompt.reshape(bp * lp, D_MODEL), p_layers, cos_p, sin_p, None,
                       sample=False, nbatch=bp, seqlen=lp)

    cos_s, sin_s = _rot_tables(PAST_LEN + jnp.arange(ls, dtype=jnp.int32))
    tile_rows = lambda t: jnp.tile(jnp.pad(t, ((pad, 0), (0, 0))), (SAMPLE_GROUP, 1))
    xs = jnp.pad(x_sample, ((0, 0), (pad, 0), (0, 0))).reshape(bs * SAMPLE_PAD, D_MODEL)
    hist = jnp.concatenate([state_ssm_conv, state_gdn_conv], axis=-1)
    hist = jnp.pad(hist, ((0, 0), (0, 0), (pad - (CONV_W - 1), ls), (0, 0)))
    states_s = [(hist[l].reshape(bs * SAMPLE_PAD, CONV_CH), state_ret[l], state_ssm[l], state_gdn[l],
                 state_mlstm_C[l], state_mlstm_n[l], state_mlstm_m[l][:, None, :]) for l in range(depth)]
    y_s, st_s = _trunk(xs, p_layers, tile_rows(cos_s), tile_rows(sin_s), states_s,
                       sample=True, nbatch=bs, seqlen=SAMPLE_PAD)

    def collect(sts, nb):
        conv = jnp.stack([s[0].reshape(nb, -1, CONV_CH)[:, -(CONV_W - 1):, :] for s in sts])
        stack = lambda k: jnp.stack([s[k] for s in sts])
        return (stack(1), stack(2), conv[..., :SSM_CONV_CH], stack(3), conv[..., SSM_CONV_CH:],
                stack(4), stack(5), stack(6)[:, :, 0, :])

    y_prompt = y_p.reshape(bp, lp, D_MODEL)
    y_sample = y_s.reshape(bs, SAMPLE_PAD, D_MODEL)[:, pad:, :]
    return (y_prompt, y_sample) + collect(st_p, bp) + collect(st_s, bs)
```

```python
import functools
import math

import numpy as np
import jax
import jax.numpy as jnp
from jax import lax
from jax.experimental import pallas as pl
from jax.experimental.pallas import tpu as pltpu

F32 = jnp.float32
BF16 = jnp.bfloat16

D_MODEL = 1024
N_HEADS = 4
HEAD_DIM = 64
GROUP_W = N_HEADS * HEAD_DIM
D_FF = 2816
CONV_W = 4
EPS = 1e-6
ROPE_BASE = 10000.0
PAST_LEN = 16384
PROMPT_CHUNK = 64

COL_RET = 0
COL_SSM_Z = 1024
COL_CONV = 1280
CONV_CH = 1280
SSM_CONV_CH = 512
COL_GDN_GATE = 2560
COL_ML = 2816
COL_GATES = 3840
IN_PAD = 3968
G_DT, G_GA, G_GB, G_MI, G_MF, G_DTV = 0, 4, 8, 12, 16, 20

SUBLANES = 8
LANES = 128
VMEM_LIMIT = 56 * 1024 * 1024

FF_CHUNK = 256
TOKEN_TILE = 512
PROMPT_TILE = 256
SAMPLE_PAD = 8
SAMPLE_GROUP = 8
NEG_BIG = -1e30


def _sigmoid(x):
    return 1.0 / (1.0 + jnp.exp(-x))


def _silu(x):
    return x * _sigmoid(x)


def _softplus(x):
    return jnp.maximum(x, 0.0) + jnp.log1p(jnp.exp(-jnp.abs(x)))


def _rms(x, g):
    return x * lax.rsqrt(jnp.mean(x * x, axis=-1, keepdims=True) + EPS) * g


def _dot(a, b):
    return jnp.dot(a.astype(BF16), b.astype(BF16), preferred_element_type=F32)


def _dot_nt(a, b):
    return lax.dot_general(a.astype(BF16), b.astype(BF16), (((1,), (1,)), ((), ())),
                           preferred_element_type=F32)


def _dot_tn(a, b):
    return lax.dot_general(a.astype(BF16), b.astype(BF16), (((0,), (0,)), ((), ())),
                           preferred_element_type=F32)


def _split3(x):
    hi = x.astype(BF16)
    r = x - hi.astype(F32)
    mid = r.astype(BF16)
    lo = (r - mid.astype(F32)).astype(BF16)
    return hi, mid, lo


def _dot_sel(sel, x):
    hi, mid, lo = _split3(x)
    f = lambda p: jnp.dot(sel, p, preferred_element_type=F32)
    return (f(lo) + f(mid)) + f(hi)


def _dot_sel_r(x, sel):
    hi, mid, lo = _split3(x)
    f = lambda p: jnp.dot(p, sel, preferred_element_type=F32)
    return (f(lo) + f(mid)) + f(hi)


def _dot_sel_nt(sel, x):
    hi, mid, lo = _split3(x)
    f = lambda p: lax.dot_general(sel, p, (((1,), (1,)), ((), ())), preferred_element_type=F32)
    return (f(lo) + f(mid)) + f(hi)


def _ffn_body(x_ref, g_ref, wg_ref, wu_ref, wd_ref, o_ref, a_ref):
    x = x_ref[...]
    h = _rms(x, g_ref[0:1, :]).astype(BF16)
    for j in range(D_FF // FF_CHUNK):
        sl = slice(j * FF_CHUNK, (j + 1) * FF_CHUNK)
        gt = jnp.dot(h, wg_ref[:, sl], preferred_element_type=F32)
        up = jnp.dot(h, wu_ref[:, sl], preferred_element_type=F32)
        a_ref[:, sl] = (_silu(gt) * up).astype(BF16)
    y = jnp.dot(a_ref[...], wd_ref[...], preferred_element_type=F32)
    o_ref[...] = x + 0.5 * _rms(y, g_ref[1:2, :])


def _resident(shape):
    return pl.BlockSpec(shape, lambda i: (0,) * len(shape), pipeline_mode=pl.Buffered(1))


def _ffn(x, gains, wg, wu, wd):
    m = x.shape[0]
    tm = min(TOKEN_TILE, m)
    return pl.pallas_call(
        _ffn_body,
        out_shape=jax.ShapeDtypeStruct((m, D_MODEL), F32),
        grid=(m // tm,),
        in_specs=[pl.BlockSpec((tm, D_MODEL), lambda i: (i, 0)),
                  _resident((2, D_MODEL)),
                  _resident((D_MODEL, D_FF)),
                  _resident((D_MODEL, D_FF)),
                  _resident((D_FF, D_MODEL))],
        out_specs=pl.BlockSpec((tm, D_MODEL), lambda i: (i, 0)),
        scratch_shapes=[pltpu.VMEM((tm, D_FF), BF16)],
        compiler_params=pltpu.CompilerParams(dimension_semantics=("parallel",),
                                             vmem_limit_bytes=VMEM_LIMIT),
        name="ffn",
    )(x, gains, wg, wu, wd)


def _inproj_body(x_ref, g_ref, w_ref, o_ref):
    h = _rms(x_ref[...], g_ref[...]).astype(BF16)
    o_ref[...] = jnp.dot(h, w_ref[...], preferred_element_type=F32)


def _inproj(x, gain, w):
    m = x.shape[0]
    tm = min(TOKEN_TILE, m)
    return pl.pallas_call(
        _inproj_body,
        out_shape=jax.ShapeDtypeStruct((m, IN_PAD), F32),
        grid=(m // tm,),
        in_specs=[pl.BlockSpec((tm, D_MODEL), lambda i: (i, 0)),
                  _resident((1, D_MODEL)),
                  _resident((D_MODEL, IN_PAD))],
        out_specs=pl.BlockSpec((tm, IN_PAD), lambda i: (i, 0)),
        compiler_params=pltpu.CompilerParams(dimension_semantics=("parallel",),
                                             vmem_limit_bytes=VMEM_LIMIT),
        name="inproj",
    )(x, gain, w)


def _outproj_body(x_ref, m_ref, g_ref, w_ref, o_ref):
    y = jnp.dot(m_ref[...], w_ref[...], preferred_element_type=F32)
    o_ref[...] = x_ref[...] + _rms(y, g_ref[...])


def _outproj(x, mixed, gain, w):
    m = x.shape[0]
    tm = min(TOKEN_TILE, m)
    return pl.pallas_call(
        _outproj_body,
        out_shape=jax.ShapeDtypeStruct((m, D_MODEL), F32),
        grid=(m // tm,),
        in_specs=[pl.BlockSpec((tm, D_MODEL), lambda i: (i, 0)),
                  pl.BlockSpec((tm, D_MODEL), lambda i: (i, 0)),
                  _resident((1, D_MODEL)),
                  _resident((D_MODEL, D_MODEL))],
        out_specs=pl.BlockSpec((tm, D_MODEL), lambda i: (i, 0)),
        compiler_params=pltpu.CompilerParams(dimension_semantics=("parallel",),
                                             vmem_limit_bytes=VMEM_LIMIT),
        name="outproj",
    )(x, mixed, gain, w)


def _head(x, h):
    return x[:, h * HEAD_DIM:(h + 1) * HEAD_DIM]


def _neumann_inverse(amat, c):
    row = lax.broadcasted_iota(jnp.int32, (c, c), 0)
    col = lax.broadcasted_iota(jnp.int32, (c, c), 1)
    eye = (row == col).astype(F32)
    m = -amat
    p = eye + m
    for _ in range(max(int(math.ceil(math.log2(c))) - 1, 0)):
        m = _dot(m, m)
        p = p + _dot(p, m)
    return p


def _mix_body(*refs, sample, rows, chunk, ret_carry):
    nchunks = rows // chunk
    it = iter(refs)
    proj_ref = next(it)
    hist_ref = next(it) if sample else None
    cos_ref, sin_ref = next(it), next(it)
    rinter_ref, rtail_ref, rintra_ref = next(it), next(it), next(it)
    tri_ref, sel_ref, ones_ref = next(it), next(it), next(it)
    gvec_ref, convw_ref, convb_ref, dvec_ref, norms_ref = next(it), next(it), next(it), next(it), next(it)
    if sample:
        sret_in, sssm_in, sgdn_in, smc_in, smn_in, smm_in = (next(it) for _ in range(6))
    mixed_ref, convraw_ref = next(it), next(it)
    sret_ref, sssm_ref, sgdn_ref, smc_ref, smn_ref, smm_ref = (next(it) for _ in range(6))
    e_ref, g_ref, rq_ref, rk_ref, sx_ref, sbc_ref, gq_ref, gk_ref, gv_ref, o_ref = (next(it) for _ in range(10))
    if not sample:
        sret_in, sssm_in, sgdn_in, smc_in, smn_in, smm_in = (
            sret_ref, sssm_ref, sgdn_ref, smc_ref, smn_ref, smm_ref)

    if sample:
        rowid = lax.broadcasted_iota(jnp.int32, (rows, 1), 0)
        valid = (rowid % SAMPLE_PAD) >= (SAMPLE_PAD - 4)
        e_ref[0:SUBLANES, :] = jnp.zeros((SUBLANES, CONV_CH), F32)
    else:
        valid = None

        @pl.when(pl.program_id(1) == 0)
        def _():
            e_ref[0:SUBLANES, :] = jnp.zeros((SUBLANES, CONV_CH), F32)
            sret_ref[...] = jnp.zeros_like(sret_ref)
            sssm_ref[...] = jnp.zeros_like(sssm_ref)
            sgdn_ref[...] = jnp.zeros_like(sgdn_ref)
            smc_ref[...] = jnp.zeros_like(smc_ref)
            smn_ref[...] = jnp.zeros_like(smn_ref)
            smm_ref[...] = jnp.zeros_like(smm_ref)

    def keep(x, fill=0.0):
        return x if valid is None else jnp.where(valid, x, fill)

    lane = lax.broadcasted_iota(jnp.int32, (1, LANES), 1)
    in_lanes = lambda lo: (lane >= lo) & (lane < lo + N_HEADS)
    graw = proj_ref[:, COL_GATES:COL_GATES + LANES]
    xb = graw + gvec_ref[0:1, :]
    sp = _softplus(xb)
    neg_a = -jnp.exp(gvec_ref[1:2, :])
    logf = -_softplus(-xb)
    pre = jnp.where(lane < 2 * N_HEADS, neg_a * sp, jnp.where(in_lanes(G_MF), logf, 0.0))
    cum = _dot_sel(tri_ref[...], keep(pre))
    dtv = pltpu.roll(sp, G_DTV - G_DT, axis=1)
    aux = jnp.where(in_lanes(G_GB), _sigmoid(graw),
                    jnp.where(in_lanes(G_MI), xb, jnp.where(in_lanes(G_DTV), dtv, 0.0)))
    if sample:
        aux = jnp.where(valid, aux, jnp.where(in_lanes(G_MI), NEG_BIG, 0.0))
    g_ref[...] = jnp.where(in_lanes(G_GB) | in_lanes(G_MI) | in_lanes(G_DTV), aux, cum)

    lane_g = lax.broadcasted_iota(jnp.int32, (1, GROUP_W), 1)
    first_half = (lane_g % HEAD_DIM) < (HEAD_DIM // 2)

    def rotary(x):
        swapped = jnp.where(first_half, pltpu.roll(x, GROUP_W - HEAD_DIM // 2, axis=1),
                            pltpu.roll(x, HEAD_DIM // 2, axis=1))
        return x * cos_ref[...] + swapped * sin_ref[...]

    rq_ref[...] = rotary(proj_ref[:, COL_RET:COL_RET + GROUP_W])
    rk_ref[...] = keep(rotary(proj_ref[:, COL_RET + GROUP_W:COL_RET + 2 * GROUP_W]) * HEAD_DIM ** -0.5)

    raw = proj_ref[:, COL_CONV:COL_CONV + CONV_CH]
    if sample:
        raw = raw + hist_ref[...]
    e_ref[SUBLANES:SUBLANES + rows, :] = raw
    conv = raw * convw_ref[CONV_W - 1:CONV_W, :]
    for s in range(1, CONV_W):
        conv = conv + e_ref[SUBLANES - s:SUBLANES - s + rows, :] * convw_ref[CONV_W - 1 - s:CONV_W - s, :]
    conv = keep(_silu(conv + convb_ref[...]))
    if sample:
        convraw_ref[...] = raw
    else:
        convraw_ref[...] = e_ref[rows:rows + SUBLANES, :]
        e_ref[0:SUBLANES, :] = e_ref[rows:rows + SUBLANES, :]
    sx_ref[...] = conv[:, 0:GROUP_W]
    sbc_ref[...] = conv[:, GROUP_W:SSM_CONV_CH]
    ones_bd = ones_ref[...]

    def head_sumsq(x):
        return _dot_sel_r(x * x, ones_bd)

    gq = conv[:, SSM_CONV_CH:SSM_CONV_CH + GROUP_W]
    gk = conv[:, SSM_CONV_CH + GROUP_W:SSM_CONV_CH + 2 * GROUP_W]
    gq_ref[...] = gq * lax.rsqrt(head_sumsq(gq) + EPS) * HEAD_DIM ** -0.5
    gk_ref[...] = gk * lax.rsqrt(head_sumsq(gk) + EPS)
    gv_ref[...] = conv[:, SSM_CONV_CH + 2 * GROUP_W:CONV_CH]

    r_i = lax.broadcasted_iota(jnp.int32, (chunk, chunk), 0)
    c_i = lax.broadcasted_iota(jnp.int32, (chunk, chunk), 1)
    tri = r_i >= c_i
    strict = r_i > c_i
    lane_h = lax.broadcasted_iota(jnp.int32, (1, N_HEADS), 1)

    def chunk_step(i, carry):
        r0 = pl.multiple_of(i * chunk, chunk)
        rs = pl.ds(r0, chunk)
        sidx = i if sample else 0
        gc = g_ref[rs, :]
        gt = _dot_sel_nt(sel_ref[...], gc)
        col = lambda j: gc[:, j:j + 1]
        rowv = lambda j: gt[j:j + 1, :]
        last = lambda j: gc[chunk - 1:chunk, j:j + 1]

        heads = range(N_HEADS)
        s_ret = [sret_in[sidx, h] for h in heads]
        s_ssm = [sssm_in[sidx, h] for h in heads]
        s_gdn = [sgdn_in[sidx, h] for h in heads]
        s_mc = [smc_in[sidx, h] for h in heads]
        n_all = smn_in[sidx]
        m_row = smm_in[sidx]
        rq_all, rk_all = rq_ref[rs, :], rk_ref[rs, :]
        rv_all = proj_ref[rs, COL_RET + 2 * GROUP_W:COL_RET + 3 * GROUP_W]
        x_all, bc_all = sx_ref[rs, :], sbc_ref[rs, :]
        gq_all, gk_all, gv_all = gq_ref[rs, :], gk_ref[rs, :], gv_ref[rs, :]
        mq_all = proj_ref[rs, COL_ML:COL_ML + GROUP_W]
        mk_all = proj_ref[rs, COL_ML + GROUP_W:COL_ML + 2 * GROUP_W] * HEAD_DIM ** -0.5
        mv_all = proj_ref[rs, COL_ML + 2 * GROUP_W:COL_ML + 3 * GROUP_W]

        qi_all = rq_all * rinter_ref[...]
        kt_all = rk_all * rtail_ref[...]
        o_ret, n_ret = [], []
        for h in heads:
            att = _dot_nt(_head(rq_all, h), _head(rk_all, h)) * rintra_ref[h]
            o_ret.append(_dot(att, _head(rv_all, h)) + _dot(_head(qi_all, h), s_ret[h]))
            n_ret.append(s_ret[h] * ret_carry[h] + _dot_tn(_head(kt_all, h), _head(rv_all, h)))

        o_ssm, n_ssm = [], []
        for h in heads:
            grp = h // (N_HEADS // 2)
            b_g = bc_all[:, grp * HEAD_DIM:(grp + 1) * HEAD_DIM]
            c_g = bc_all[:, 2 * HEAD_DIM + grp * HEAD_DIM:2 * HEAD_DIM + (grp + 1) * HEAD_DIM]
            x_h = _head(x_all, h)
            cum_c, cum_r, cum_l = col(G_DT + h), rowv(G_DT + h), last(G_DT + h)
            lmat = jnp.exp(jnp.where(tri, cum_c - cum_r, -jnp.inf))
            scores = _dot_nt(c_g, b_g) * lmat * rowv(G_DTV + h)
            o_ssm.append(_dot(scores, x_h) + _dot_nt(c_g, s_ssm[h]) * jnp.exp(cum_c))
            w_c = jnp.exp(cum_l - cum_c) * col(G_DTV + h)
            n_ssm.append(s_ssm[h] * jnp.exp(cum_l) + _dot_tn(x_h * w_c, b_g))

        o_gdn, n_gdn = [], []
        for h in heads:
            q_h, k_h, v_h = _head(gq_all, h), _head(gk_all, h), _head(gv_all, h)
            cum_c, cum_r, cum_l = col(G_GA + h), rowv(G_GA + h), last(G_GA + h)
            beta_c = col(G_GB + h)
            gam = jnp.exp(jnp.where(tri, cum_c - cum_r, -jnp.inf))
            amat = _dot_nt(k_h, k_h) * jnp.where(strict, gam, 0.0) * beta_c
            ecum = jnp.exp(cum_c)
            tinv = _neumann_inverse(amat, chunk)
            u = _dot(tinv, v_h * beta_c)
            w = _dot(tinv, k_h * (beta_c * ecum))
            v_new = u - _dot(w, s_gdn[h])
            qk = _dot_nt(q_h, k_h) * gam
            o_gdn.append(_dot(q_h * ecum, s_gdn[h]) + _dot(qk, v_new))
            n_gdn.append(s_gdn[h] * jnp.exp(cum_l) + _dot_tn(k_h * jnp.exp(cum_l - cum_c), v_new))

        m_new_row = jnp.zeros((1, N_HEADS), F32)
        o_ml, n_mc, n_mn = [], [], []
        for h in heads:
            q_h, k_h, v_h = _head(mq_all, h), _head(mk_all, h), _head(mv_all, h)
            b_c, b_r, b_l = col(G_MF + h), rowv(G_MF + h), last(G_MF + h)
            i_c, i_r = col(G_MI + h), rowv(G_MI + h)
            m_old = m_row[:, h:h + 1]
            n_h = n_all[h:h + 1, :]
            dmat = jnp.where(tri, b_c - b_r + i_r, -jnp.inf)
            m_t = jnp.maximum(b_c + m_old, jnp.max(dmat, axis=-1, keepdims=True))
            wts = jnp.exp(dmat - m_t)
            inter = jnp.exp(b_c + m_old - m_t)
            qk = _dot_nt(q_h, k_h) * wts
            num = _dot(qk, v_h) + inter * _dot(q_h, s_mc[h])
            den = jnp.sum(qk, axis=-1, keepdims=True) + inter * jnp.sum(q_h * n_h, axis=-1, keepdims=True)
            o_ml.append(num / jnp.maximum(jnp.abs(den), jnp.exp(-m_t)))
            m_new = m_t[chunk - 1:chunk, :]
            ws = jnp.exp(b_l - b_c + i_c - m_new)
            cs = jnp.exp(b_l + m_old - m_new)
            kw = k_h * ws
            n_mc.append(s_mc[h] * cs + _dot_tn(kw, v_h))
            n_mn.append(n_h * cs + jnp.sum(kw, axis=0, keepdims=True))
            m_new_row = jnp.where(lane_h == h, m_new, m_new_row)

        o_ref[rs, :] = jnp.concatenate(o_ret + o_ssm + o_gdn + o_ml, axis=1)
        for h in heads:
            sret_ref[sidx, h] = n_ret[h]
            sssm_ref[sidx, h] = n_ssm[h]
            sgdn_ref[sidx, h] = n_gdn[h]
            smc_ref[sidx, h] = n_mc[h]
        smn_ref[sidx] = jnp.concatenate(n_mn, axis=0)
        smm_ref[sidx] = m_new_row
        return carry

    lax.fori_loop(0, nchunks, chunk_step, 0)

    def head_norm(x, g):
        return x * lax.rsqrt(head_sumsq(x) * (1.0 / HEAD_DIM) + EPS) * g

    out_ret = head_norm(o_ref[:, 0:GROUP_W], norms_ref[0:1, :]) * _silu(
        proj_ref[:, COL_RET + 3 * GROUP_W:COL_RET + 4 * GROUP_W])
    mixed_ref[:, 0:GROUP_W] = keep(out_ret).astype(BF16)
    y = o_ref[:, GROUP_W:2 * GROUP_W] + dvec_ref[...] * sx_ref[...]
    out_ssm = _rms(y * _silu(proj_ref[:, COL_SSM_Z:COL_SSM_Z + GROUP_W]), norms_ref[1:2, :])
    mixed_ref[:, GROUP_W:2 * GROUP_W] = keep(out_ssm).astype(BF16)
    out_gdn = head_norm(o_ref[:, 2 * GROUP_W:3 * GROUP_W], norms_ref[2:3, :]) * _silu(
        proj_ref[:, COL_GDN_GATE:COL_GDN_GATE + GROUP_W])
    mixed_ref[:, 2 * GROUP_W:3 * GROUP_W] = keep(out_gdn).astype(BF16)
    out_ml = _sigmoid(proj_ref[:, COL_ML + 3 * GROUP_W:COL_ML + 4 * GROUP_W]) * head_norm(
        o_ref[:, 3 * GROUP_W:4 * GROUP_W], norms_ref[3:4, :])
    mixed_ref[:, 3 * GROUP_W:4 * GROUP_W] = keep(out_ml).astype(BF16)


NEXP = 6
X_CSSM, X_DT, X_CGDN, X_BETA, X_B, X_IMB = range(NEXP)


def _split2(x):
    hi = x.astype(BF16)
    return hi, (x - hi.astype(F32)).astype(BF16)


def _dot_sel_r2(x, sel):
    hi, lo = _split2(x)
    return jnp.dot(lo, sel, preferred_element_type=F32) + jnp.dot(hi, sel, preferred_element_type=F32)


def _mixp_body(proj_ref, cos_ref, sin_ref, rinter_ref, rtail_ref, rintra_ref, tri_ref, expand_ref,
               maskbd_ref, maskg_ref, maskht_ref, gvec_ref, convw_ref, convb_ref, dvec_ref, norms_ref,
               mixed_ref, convraw_ref, oret_ref, ossm_ref, ogdn_ref, omc_ref, omn_ref, omm_ref,
               e_ref, x_ref, rq_ref, rk_ref, sx_ref, sbc_ref, gq_ref, gk_ref, gv_ref, o_ref,
               sret_ref, sht_ref, sgdn_ref, smc_ref, vec_ref, *, rows, chunk, ret_carry):
    nchunks = rows // chunk
    t_id = pl.program_id(1)

    @pl.when(t_id == 0)
    def _():
        e_ref[0:SUBLANES, :] = jnp.zeros((SUBLANES, CONV_CH), F32)
        sret_ref[...] = jnp.zeros_like(sret_ref)
        sht_ref[...] = jnp.zeros_like(sht_ref)
        sgdn_ref[...] = jnp.zeros_like(sgdn_ref)
        smc_ref[...] = jnp.zeros_like(smc_ref)
        vec_ref[...] = jnp.zeros_like(vec_ref)

    maskbd = maskbd_ref[...]
    maskbd_b = maskbd.astype(BF16)
    maskg = maskg_ref[...]
    maskg_b = maskg.astype(BF16)

    def seg_sum(x):
        return _dot_sel_r2(x, maskbd_b)

    lane = lax.broadcasted_iota(jnp.int32, (1, LANES), 1)
    in_lanes = lambda lo: (lane >= lo) & (lane < lo + N_HEADS)
    graw = proj_ref[:, COL_GATES:COL_GATES + LANES]
    xb = graw + gvec_ref[0:1, :]
    sp = _softplus(xb)
    neg_a = -jnp.exp(gvec_ref[1:2, :])
    logf = -_softplus(-xb)
    pre = jnp.where(lane < 2 * N_HEADS, neg_a * sp, jnp.where(in_lanes(G_MF), logf, 0.0))
    cum = _dot_sel(tri_ref[...], pre)
    imb = xb - pltpu.roll(cum, LANES - (G_MF - G_MI), axis=1)
    dtv = pltpu.roll(sp, G_DTV - G_DT, axis=1)
    gates = jnp.where(in_lanes(G_GB), _sigmoid(graw),
                      jnp.where(in_lanes(G_MI), imb, jnp.where(in_lanes(G_DTV), dtv, cum)))
    for j, lo in enumerate((G_DT, G_DTV, G_GA, G_GB, G_MF, G_MI)):
        x_ref[j] = _dot_sel_r(gates, expand_ref[:, j * GROUP_W:(j + 1) * GROUP_W])

    lane_g = lax.broadcasted_iota(jnp.int32, (1, GROUP_W), 1)
    first_half = (lane_g % HEAD_DIM) < (HEAD_DIM // 2)

    def rotary(x):
        swapped = jnp.where(first_half, pltpu.roll(x, GROUP_W - HEAD_DIM // 2, axis=1),
                            pltpu.roll(x, HEAD_DIM // 2, axis=1))
        return x * cos_ref[...] + swapped * sin_ref[...]

    rq_ref[...] = rotary(proj_ref[:, COL_RET:COL_RET + GROUP_W])
    rk_ref[...] = rotary(proj_ref[:, COL_RET + GROUP_W:COL_RET + 2 * GROUP_W]) * HEAD_DIM ** -0.5

    raw = proj_ref[:, COL_CONV:COL_CONV + CONV_CH]
    e_ref[SUBLANES:SUBLANES + rows, :] = raw
    conv = raw * convw_ref[CONV_W - 1:CONV_W, :]
    for s in range(1, CONV_W):
        conv = conv + e_ref[SUBLANES - s:SUBLANES - s + rows, :] * convw_ref[CONV_W - 1 - s:CONV_W - s, :]
    conv = _silu(conv + convb_ref[...])
    convraw_ref[...] = e_ref[rows:rows + SUBLANES, :]
    e_ref[0:SUBLANES, :] = e_ref[rows:rows + SUBLANES, :]
    sx_ref[...] = conv[:, 0:GROUP_W]
    sbc_ref[...] = conv[:, GROUP_W:SSM_CONV_CH]
    gq = conv[:, SSM_CONV_CH:SSM_CONV_CH + GROUP_W]
    gk = conv[:, SSM_CONV_CH + GROUP_W:SSM_CONV_CH + 2 * GROUP_W]
    gq_ref[...] = gq * lax.rsqrt(seg_sum(gq * gq) + EPS) * HEAD_DIM ** -0.5
    gk_ref[...] = gk * lax.rsqrt(seg_sum(gk * gk) + EPS)
    gv_ref[...] = conv[:, SSM_CONV_CH + 2 * GROUP_W:CONV_CH]

    r_i = lax.broadcasted_iota(jnp.int32, (chunk, GROUP_W), 0)
    c_i = lax.broadcasted_iota(jnp.int32, (chunk, GROUP_W), 1) % HEAD_DIM
    trim = r_i >= c_i
    diag = r_i == c_i
    strict = r_i > c_i
    eye = diag.astype(F32)
    head_of_lane = lax.broadcasted_iota(jnp.int32, (1, GROUP_W), 1) // HEAD_DIM
    carry_row = jnp.zeros((1, GROUP_W), F32)
    for h in range(N_HEADS):
        carry_row = jnp.where(head_of_lane == h, ret_carry[h], carry_row)
    maskht = maskht_ref[...]

    def bd(x):
        xb16 = x.astype(BF16)
        return jnp.concatenate([xb16] * N_HEADS, axis=0) * maskbd_b

    def last_row(x):
        return x[chunk - 1:chunk, :]

    def row_form(x):
        return jnp.sum(jnp.where(diag, x, 0.0), axis=0, keepdims=True)

    def seg_max(x):
        out = None
        for h in range(N_HEADS):
            sel = head_of_lane == h
            m_h = jnp.max(jnp.where(sel, x, -jnp.inf), axis=-1, keepdims=True)
            out = jnp.where(sel, m_h, 0.0) if out is None else jnp.where(sel, m_h, out)
        return out

    chunks = [slice(c * chunk, (c + 1) * chunk) for c in range(nchunks)]

    g_gam, g_beta, g_ecum, g_k, g_n, g_p = [], [], [], [], [], []
    for rs in chunks:
        ce = x_ref[X_CGDN, rs, :]
        gam = jnp.exp(jnp.where(trim, ce - row_form(ce), -jnp.inf))
        k_c = gk_ref[rs, :]
        beta = x_ref[X_BETA, rs, :]
        n0 = -(_dot_nt(k_c, bd(k_c)) * jnp.where(strict, gam, 0.0) * beta)
        g_gam.append(gam), g_beta.append(beta), g_ecum.append(jnp.exp(ce)), g_k.append(k_c)
        g_n.append(n0), g_p.append(eye + n0)
    for _ in range(max(int(math.ceil(math.log2(chunk))) - 1, 0)):
        for c in range(nchunks):
            g_n[c] = _dot(g_n[c], bd(g_n[c]))
        for c in range(nchunks):
            g_p[c] = g_p[c] + _dot(g_p[c], bd(g_n[c]))
    g_u, g_w = [], []
    for c, rs in enumerate(chunks):
        g_u.append(_dot(g_p[c], bd(gv_ref[rs, :] * g_beta[c])))
        g_w.append(_dot(g_p[c], bd(g_k[c] * (g_beta[c] * g_ecum[c]))))

    s_ret, s_ht, s_gdn, s_mc = sret_ref[...], sht_ref[...], sgdn_ref[...], smc_ref[...]
    n_row, m_row = vec_ref[0:1, :], vec_ref[1:2, :]
    for c, rs in enumerate(chunks):
        q_c, k_c = rq_ref[rs, :], rk_ref[rs, :]
        v_c = proj_ref[rs, COL_RET + 2 * GROUP_W:COL_RET + 3 * GROUP_W]
        att = _dot_nt(q_c, bd(k_c)) * rintra_ref[...]
        o_ret = _dot(att, bd(v_c)) + _dot(q_c * rinter_ref[...], s_ret)
        s_ret = s_ret * carry_row + _dot_tn(k_c * rtail_ref[...], v_c) * maskbd

        ce = x_ref[X_CSSM, rs, :]
        ce_last = last_row(ce)
        lmat = jnp.exp(jnp.where(trim, ce - row_form(ce), -jnp.inf))
        b_c, c_c = sbc_ref[rs, 0:2 * HEAD_DIM], sbc_ref[rs, 2 * HEAD_DIM:4 * HEAD_DIM]
        xdt = sx_ref[rs, :] * x_ref[X_DT, rs, :]
        rhs_b = jnp.concatenate([b_c.astype(BF16)] * N_HEADS, axis=0) * maskg_b
        scores = _dot_nt(c_c, rhs_b) * lmat
        o_ssm = _dot(scores, bd(xdt)) + _dot(c_c, s_ht) * jnp.exp(ce)
        s_ht = s_ht * jnp.exp(ce_last) + _dot_tn(b_c, xdt * jnp.exp(ce_last - ce)) * maskht

        ce = x_ref[X_CGDN, rs, :]
        ce_last = last_row(ce)
        q_c = gq_ref[rs, :]
        v_new = g_u[c] - _dot(g_w[c], s_gdn)
        qk = _dot_nt(q_c, bd(g_k[c])) * g_gam[c]
        o_gdn = _dot(q_c * g_ecum[c], s_gdn) + _dot(qk, bd(v_new))
        s_gdn = s_gdn * jnp.exp(ce_last) + _dot_tn(g_k[c] * jnp.exp(ce_last - ce), v_new) * maskbd

        q_c = proj_ref[rs, COL_ML:COL_ML + GROUP_W]
        k_c = proj_ref[rs, COL_ML + GROUP_W:COL_ML + 2 * GROUP_W] * HEAD_DIM ** -0.5
        v_c = proj_ref[rs, COL_ML + 2 * GROUP_W:COL_ML + 3 * GROUP_W]
        b_e, imb_e = x_ref[X_B, rs, :], x_ref[X_IMB, rs, :]
        dmat = jnp.where(trim, b_e + row_form(imb_e), -jnp.inf)
        m_t = jnp.maximum(b_e + m_row, seg_max(dmat))
        wts = jnp.exp(dmat - m_t)
        inter = jnp.exp(b_e + m_row - m_t)
        qk = _dot_nt(q_c, bd(k_c)) * wts
        num = _dot(qk, bd(v_c)) + inter * _dot(q_c, s_mc)
        den = seg_sum(qk) + inter * seg_sum(q_c * n_row)
        o_ml = num / jnp.maximum(jnp.abs(den), jnp.exp(-m_t))
        m_new = last_row(m_t)
        b_last = last_row(b_e)
        kw = k_c * jnp.exp(b_last + imb_e - m_new)
        cs = jnp.exp(b_last + m_row - m_new)
        s_mc = s_mc * cs + _dot_tn(kw, v_c) * maskbd
        n_row = n_row * cs + jnp.sum(kw, axis=0, keepdims=True)
        m_row = m_new

        o_ref[rs, :] = jnp.concatenate([o_ret, o_ssm, o_gdn, o_ml], axis=1)

    sret_ref[...] = s_ret
    sht_ref[...] = s_ht
    sgdn_ref[...] = s_gdn
    smc_ref[...] = s_mc
    vec_ref[0:1, :] = n_row
    vec_ref[1:2, :] = m_row

    def head_norm(x, g):
        return x * lax.rsqrt(seg_sum(x * x) * (1.0 / HEAD_DIM) + EPS) * g

    out_ret = head_norm(o_ref[:, 0:GROUP_W], norms_ref[0:1, :]) * _silu(
        proj_ref[:, COL_RET + 3 * GROUP_W:COL_RET + 4 * GROUP_W])
    mixed_ref[:, 0:GROUP_W] = out_ret.astype(BF16)
    y = o_ref[:, GROUP_W:2 * GROUP_W] + dvec_ref[...] * sx_ref[...]
    out_ssm = _rms(y * _silu(proj_ref[:, COL_SSM_Z:COL_SSM_Z + GROUP_W]), norms_ref[1:2, :])
    mixed_ref[:, GROUP_W:2 * GROUP_W] = out_ssm.astype(BF16)
    out_gdn = head_norm(o_ref[:, 2 * GROUP_W:3 * GROUP_W], norms_ref[2:3, :]) * _silu(
        proj_ref[:, COL_GDN_GATE:COL_GDN_GATE + GROUP_W])
    mixed_ref[:, 2 * GROUP_W:3 * GROUP_W] = out_gdn.astype(BF16)
    out_ml = _sigmoid(proj_ref[:, COL_ML + 3 * GROUP_W:COL_ML + 4 * GROUP_W]) * head_norm(
        o_ref[:, 3 * GROUP_W:4 * GROUP_W], norms_ref[3:4, :])
    mixed_ref[:, 3 * GROUP_W:4 * GROUP_W] = out_ml.astype(BF16)

    @pl.when(t_id == pl.num_programs(1) - 1)
    def _():
        for h in range(N_HEADS):
            hs = slice(h * HEAD_DIM, (h + 1) * HEAD_DIM)
            grp = h // (N_HEADS // 2)
            oret_ref[0, h] = s_ret[hs, hs]
            ogdn_ref[0, h] = s_gdn[hs, hs]
            omc_ref[0, h] = s_mc[hs, hs]
            ossm_ref[0, h] = s_ht[grp * HEAD_DIM:(grp + 1) * HEAD_DIM, hs]
        omn_ref[0] = jnp.concatenate([n_row[:, h * HEAD_DIM:(h + 1) * HEAD_DIM] for h in range(N_HEADS)], axis=0)
        omm_ref[0] = jnp.concatenate([m_row[:, h * HEAD_DIM:h * HEAD_DIM + 1] for h in range(N_HEADS)], axis=1)


def _prompt_consts(rows, chunk):
    lg = np.log(1.0 - np.exp2(-5.0 - np.arange(N_HEADS, dtype=np.float64)))
    idx = np.arange(chunk, dtype=np.float64)
    rel = idx[:, None] - idx[None, :]
    intra = np.where(rel[None] >= 0, np.exp(np.maximum(rel[None], 0.0) * lg[:, None, None]), 0.0)
    intra = np.concatenate(list(intra), axis=1)
    lanes = lambda t: np.repeat(t.T, HEAD_DIM, axis=1)
    inter = lanes(np.exp((idx[None, :] + 1.0) * lg[:, None]))
    tail = lanes(np.exp((chunk - 1.0 - idx[None, :]) * lg[:, None]))
    carry = tuple(float(c) for c in np.exp(chunk * lg))
    r = np.arange(rows)
    tri = (r[:, None] >= r[None, :]) & (r[:, None] // chunk == r[None, :] // chunk)
    expand = np.zeros((LANES, NEXP * GROUP_W), np.float32)
    for j, lo in enumerate((G_DT, G_DTV, G_GA, G_GB, G_MF, G_MI)):
        for h in range(N_HEADS):
            expand[lo + h, j * GROUP_W + h * HEAD_DIM:j * GROUP_W + (h + 1) * HEAD_DIM] = 1.0
    g = np.arange(GROUP_W)
    maskbd = (g[:, None] // HEAD_DIM == g[None, :] // HEAD_DIM)
    n = np.arange(2 * HEAD_DIM)
    maskg = (g[:, None] // (2 * HEAD_DIM) == n[None, :] // HEAD_DIM)
    f = lambda a, dt=F32: jnp.asarray(np.asarray(a, np.float32), dt)
    return (f(inter), f(tail), f(intra), f(tri, BF16), f(expand, BF16), f(maskbd), f(maskg), f(maskg.T)), carry


def _mix_prompt(proj, cos, sin, gvec, convw, convb, dvec, norms, *, nbatch, seqlen):
    rows, chunk = min(PROMPT_TILE, seqlen), PROMPT_CHUNK
    ntile = seqlen // rows
    consts, carry = _prompt_consts(rows, chunk)
    m = proj.shape[0]
    row_map = lambda b, t: (b * ntile + t, 0)
    c2 = lambda b, t: (0, 0)
    full = lambda a: pl.BlockSpec(a.shape, c2)
    in_specs = [pl.BlockSpec((rows, IN_PAD), row_map),
                pl.BlockSpec((rows, GROUP_W), lambda b, t: (t, 0)), pl.BlockSpec((rows, GROUP_W), lambda b, t: (t, 0))]
    in_specs += [full(a) for a in consts] + [full(a) for a in (gvec, convw, convb, dvec, norms)]
    st4 = pl.BlockSpec((1, N_HEADS, HEAD_DIM, HEAD_DIM), lambda b, t: (b, 0, 0, 0))
    st3 = lambda k: pl.BlockSpec((1, k, (N_HEADS * HEAD_DIM) // k if k == N_HEADS else N_HEADS), lambda b, t: (b, 0, 0))
    out_shape = [jax.ShapeDtypeStruct((m, D_MODEL), BF16),
                 jax.ShapeDtypeStruct((nbatch * SUBLANES, CONV_CH), F32)]
    out_shape += [jax.ShapeDtypeStruct((nbatch, N_HEADS, HEAD_DIM, HEAD_DIM), F32) for _ in range(4)]
    out_shape += [jax.ShapeDtypeStruct((nbatch, N_HEADS, HEAD_DIM), F32),
                  jax.ShapeDtypeStruct((nbatch, 1, N_HEADS), F32)]
    out_specs = [pl.BlockSpec((rows, D_MODEL), row_map), pl.BlockSpec((SUBLANES, CONV_CH), lambda b, t: (b, 0)),
                 st4, st4, st4, st4, st3(N_HEADS), st3(1)]
    scratch = [pltpu.VMEM((rows + SUBLANES, CONV_CH), F32), pltpu.VMEM((NEXP, rows, GROUP_W), F32)]
    scratch += [pltpu.VMEM((rows, GROUP_W), F32) for _ in range(7)]
    scratch += [pltpu.VMEM((rows, D_MODEL), F32),
                pltpu.VMEM((GROUP_W, GROUP_W), F32), pltpu.VMEM((2 * HEAD_DIM, GROUP_W), F32),
                pltpu.VMEM((GROUP_W, GROUP_W), F32), pltpu.VMEM((GROUP_W, GROUP_W), F32),
                pltpu.VMEM((SUBLANES, GROUP_W), F32)]
    body = functools.partial(_mixp_body, rows=rows, chunk=chunk, ret_carry=carry)
    outs = pl.pallas_call(
        body, out_shape=out_shape, grid=(nbatch, ntile), in_specs=in_specs, out_specs=out_specs,
        scratch_shapes=scratch,
        compiler_params=pltpu.CompilerParams(dimension_semantics=("parallel", "arbitrary"),
                                             vmem_limit_bytes=VMEM_LIMIT),
        name="mix_prompt",
    )(proj, cos, sin, *consts, gvec, convw, convb, dvec, norms)
    outs = list(outs)
    outs[3] = jnp.swapaxes(outs[3], -1, -2)
    return outs


def _ret_tables(chunk, pad):
    lg = np.log(1.0 - np.exp2(-5.0 - np.arange(N_HEADS, dtype=np.float64)))
    idx = np.arange(chunk, dtype=np.float64)
    rel = idx[:, None] - idx[None, :]
    intra = np.where(rel[None] >= 0, np.exp(np.maximum(rel[None], 0.0) * lg[:, None, None]), 0.0)
    inter = np.where(idx[None, :] >= pad, np.exp((idx[None, :] - pad + 1.0) * lg[:, None]), 0.0)
    tail = np.exp((chunk - 1.0 - idx[None, :]) * lg[:, None])
    carry = np.exp((chunk - pad) * lg)
    lanes = lambda t: np.repeat(t.T, HEAD_DIM, axis=1)
    return (jnp.asarray(lanes(inter), F32), jnp.asarray(lanes(tail), F32), jnp.asarray(intra, F32),
            tuple(float(c) for c in carry))


def _const_mats(rows, chunk):
    r = np.arange(rows)
    tri = ((r[:, None] >= r[None, :]) & (r[:, None] // chunk == r[None, :] // chunk))
    sel = np.zeros((4 * SUBLANES, LANES), np.float32)
    sel[np.arange(24), np.arange(24)] = 1.0
    g = np.arange(GROUP_W)
    ones_bd = (g[:, None] // HEAD_DIM == g[None, :] // HEAD_DIM)
    return (jnp.asarray(tri, BF16), jnp.asarray(sel, BF16), jnp.asarray(ones_bd, BF16))


def _mix(proj, hist, cos, sin, gvec, convw, convb, dvec, norms, states, *, sample, nbatch, seqlen):
    if sample:
        rows, chunk, pad = SAMPLE_GROUP * SAMPLE_PAD, SAMPLE_PAD, SAMPLE_PAD - 4
        grid = (nbatch // SAMPLE_GROUP,)
        nstate = SAMPLE_GROUP
        row_map = lambda i: (i, 0)
        st_map4 = lambda i: (i, 0, 0, 0)
        st_map3 = lambda i: (i, 0, 0)
        const_map2 = lambda i: (0, 0)
        const_map3 = lambda i: (0, 0, 0)
        rot_map = const_map2
        semantics = ("parallel",)
    else:
        rows, chunk, pad = min(PROMPT_TILE, seqlen), PROMPT_CHUNK, 0
        ntile = seqlen // rows
        grid = (nbatch, ntile)
        nstate = 1
        row_map = lambda b, t: (b * ntile + t, 0)
        st_map4 = lambda b, t: (b, 0, 0, 0)
        st_map3 = lambda b, t: (b, 0, 0)
        const_map2 = lambda b, t: (0, 0)
        const_map3 = lambda b, t: (0, 0, 0)
        rot_map = lambda b, t: (t, 0)
        semantics = ("parallel", "arbitrary")
    rinter, rtail, rintra, carry = _ret_tables(chunk, pad)
    tri, sel, ones_bd = _const_mats(rows, chunk)
    m = proj.shape[0]

    in_specs = [pl.BlockSpec((rows, IN_PAD), row_map)]
    args = [proj]
    if sample:
        in_specs.append(pl.BlockSpec((rows, CONV_CH), row_map))
        args.append(hist)
    in_specs += [pl.BlockSpec((rows, GROUP_W), rot_map), pl.BlockSpec((rows, GROUP_W), rot_map),
                 pl.BlockSpec((chunk, GROUP_W), const_map2), pl.BlockSpec((chunk, GROUP_W), const_map2),
                 pl.BlockSpec((N_HEADS, chunk, chunk), const_map3),
                 pl.BlockSpec((rows, rows), const_map2), pl.BlockSpec((4 * SUBLANES, LANES), const_map2),
                 pl.BlockSpec((GROUP_W, GROUP_W), const_map2),
                 pl.BlockSpec((2, LANES), const_map2), pl.BlockSpec((CONV_W, CONV_CH), const_map2),
                 pl.BlockSpec((1, CONV_CH), const_map2), pl.BlockSpec((1, GROUP_W), const_map2),
                 pl.BlockSpec((4, GROUP_W), const_map2)]
    args += [cos, sin, rinter, rtail, rintra, tri, sel, ones_bd, gvec, convw, convb, dvec, norms]
    st4 = pl.BlockSpec((nstate, N_HEADS, HEAD_DIM, HEAD_DIM), st_map4)
    stn = pl.BlockSpec((nstate, N_HEADS, HEAD_DIM), st_map3)
    stm = pl.BlockSpec((nstate, 1, N_HEADS), st_map3)
    state_specs = [st4, st4, st4, st4, stn, stm]
    if sample:
        in_specs += state_specs
        args += list(states)
    conv_rows = rows if sample else SUBLANES
    conv_total = m if sample else nbatch * SUBLANES
    conv_map = row_map if sample else (lambda b, t: (b, 0))
    out_shape = [jax.ShapeDtypeStruct((m, D_MODEL), BF16),
                 jax.ShapeDtypeStruct((conv_total, CONV_CH), F32),
                 jax.ShapeDtypeStruct((nbatch, N_HEADS, HEAD_DIM, HEAD_DIM), F32),
                 jax.ShapeDtypeStruct((nbatch, N_HEADS, HEAD_DIM, HEAD_DIM), F32),
                 jax.ShapeDtypeStruct((nbatch, N_HEADS, HEAD_DIM, HEAD_DIM), F32),
                 jax.ShapeDtypeStruct((nbatch, N_HEADS, HEAD_DIM, HEAD_DIM), F32),
                 jax.ShapeDtypeStruct((nbatch, N_HEADS, HEAD_DIM), F32),
                 jax.ShapeDtypeStruct((nbatch, 1, N_HEADS), F32)]
    out_specs = [pl.BlockSpec((rows, D_MODEL), row_map), pl.BlockSpec((conv_rows, CONV_CH), conv_map)] + state_specs
    scratch = [pltpu.VMEM((rows + SUBLANES, CONV_CH), F32), pltpu.VMEM((rows, LANES), F32)]
    scratch += [pltpu.VMEM((rows, GROUP_W), F32) for _ in range(7)]
    scratch += [pltpu.VMEM((rows, D_MODEL), F32)]
    body = functools.partial(_mix_body, sample=sample, rows=rows, chunk=chunk, ret_carry=carry)
    return pl.pallas_call(
        body, out_shape=out_shape, grid=grid, in_specs=in_specs, out_specs=out_specs,
        scratch_shapes=scratch,
        compiler_params=pltpu.CompilerParams(dimension_semantics=semantics, vmem_limit_bytes=VMEM_LIMIT),
        name="mix_sample" if sample else "mix_prompt",
    )(*args)


def _rot_tables(pos):
    half = HEAD_DIM // 2
    inv = ROPE_BASE ** (-jnp.arange(half, dtype=F32) / half)
    ang = pos.astype(F32)[:, None] * inv[None, :]
    cos, sin = jnp.cos(ang), jnp.sin(ang)
    cos_full = jnp.tile(jnp.concatenate([cos, cos], axis=-1), (1, N_HEADS))
    sin_signed = jnp.tile(jnp.concatenate([-sin, sin], axis=-1), (1, N_HEADS))
    return cos_full, sin_signed


def _layer_params(l, w_in, w_out, norm_sandwich, ffn_w_gate, ffn_w_up, ffn_w_down, ret_norm, ssm_conv_w,
                  ssm_conv_b, ssm_dt_bias, ssm_A_log, ssm_D, ssm_norm, gdn_conv_w, gdn_dt_bias, gdn_A_log,
                  gdn_norm, mlstm_i_bias, mlstm_f_bias, mlstm_norm):
    w = w_in[l]
    zeros4 = jnp.zeros((N_HEADS,), F32)
    w_in_p = jnp.concatenate(
        [w[:, 0:1792], w[:, 1796:2820], w[:, 2828:3852], w[:, 1792:1796], w[:, 2820:2828], w[:, 3852:3860],
         jnp.zeros((D_MODEL, IN_PAD - COL_GATES - 5 * N_HEADS), F32)], axis=1).astype(BF16)
    pad_lanes = jnp.zeros((LANES - 5 * N_HEADS,), F32)
    bias = jnp.concatenate([ssm_dt_bias[l], gdn_dt_bias[l], zeros4, mlstm_i_bias[l], mlstm_f_bias[l], pad_lanes])
    alog = jnp.concatenate([ssm_A_log[l], gdn_A_log[l], zeros4, zeros4, zeros4, pad_lanes])
    ns = norm_sandwich[l]
    return dict(
        w_in=w_in_p, w_out=w_out[l].astype(BF16),
        ffn=[(jnp.stack([ns[4 * k], ns[4 * k + 1]]),
              ffn_w_gate[l, k].astype(BF16), ffn_w_up[l, k].astype(BF16), ffn_w_down[l, k].astype(BF16))
             for k in range(2)],
        g_in=ns[2][None, :], g_out=ns[3][None, :],
        gvec=jnp.stack([bias, alog]),
        convw=jnp.concatenate([ssm_conv_w[l], gdn_conv_w[l]], axis=1),
        convb=jnp.concatenate([ssm_conv_b[l], jnp.zeros((CONV_CH - SSM_CONV_CH,), F32)])[None, :],
        dvec=jnp.repeat(ssm_D[l], HEAD_DIM)[None, :],
        norms=jnp.stack([ret_norm[l], ssm_norm[l], gdn_norm[l], mlstm_norm[l]]),
    )


def _trunk(x, p_layers, cos, sin, states, *, sample, nbatch, seqlen):
    new_states = []
    for l, p in enumerate(p_layers):
        x = _ffn(x, *p['ffn'][0])
        proj = _inproj(x, p['g_in'], p['w_in'])
        if sample:
            outs = _mix(proj, states[l][0], cos, sin, p['gvec'], p['convw'], p['convb'], p['dvec'], p['norms'],
                        states[l][1:], sample=True, nbatch=nbatch, seqlen=seqlen)
        else:
            outs = _mix_prompt(proj, cos, sin, p['gvec'], p['convw'], p['convb'], p['dvec'], p['norms'],
                               nbatch=nbatch, seqlen=seqlen)
        x = _outproj(x, outs[0], p['g_out'], p['w_out'])
        x = _ffn(x, *p['ffn'][1])
        new_states.append(outs[1:])
    return x, new_states


def kernel(x_prompt, x_sample, state_ret, state_ssm, state_ssm_conv, state_gdn, state_gdn_conv,
           state_mlstm_C, state_mlstm_n, state_mlstm_m, w_in, w_out, norm_sandwich, ffn_w_gate, ffn_w_up,
           ffn_w_down, ret_norm, ssm_conv_w, ssm_conv_b, ssm_dt_bias, ssm_A_log, ssm_D, ssm_norm, gdn_conv_w,
           gdn_dt_bias, gdn_A_log, gdn_norm, mlstm_i_bias, mlstm_f_bias, mlstm_norm):
    depth = w_in.shape[0]
    bp, lp, _ = x_prompt.shape
    bs, ls, _ = x_sample.shape
    pad = SAMPLE_PAD - ls
    p_layers = [_layer_params(l, w_in, w_out, norm_sandwich, ffn_w_gate, ffn_w_up, ffn_w_down, ret_norm,
                              ssm_conv_w, ssm_conv_b, ssm_dt_bias, ssm_A_log, ssm_D, ssm_norm, gdn_conv_w,
                              gdn_dt_bias, gdn_A_log, gdn_norm, mlstm_i_bias, mlstm_f_bias, mlstm_norm)
                for l in range(depth)]

    cos_p, sin_p = _rot_tables(jnp.arange(lp, dtype=jnp.int32))
    y_p, st_p = _trunk(x_prompt.reshape(bp * lp, D_MODEL), p_layers, cos_p, sin_p, None,
                       sample=False, nbatch=bp, seqlen=lp)

    cos_s, sin_s = _rot_tables(PAST_LEN + jnp.arange(ls, dtype=jnp.int32))
    tile_rows = lambda t: jnp.tile(jnp.pad(t, ((pad, 0), (0, 0))), (SAMPLE_GROUP, 1))
    xs = jnp.pad(x_sample, ((0, 0), (pad, 0), (0, 0))).reshape(bs * SAMPLE_PAD, D_MODEL)
    hist = jnp.concatenate([state_ssm_conv, state_gdn_conv], axis=-1)
    hist = jnp.pad(hist, ((0, 0), (0, 0), (pad - (CONV_W - 1), ls), (0, 0)))
    states_s = [(hist[l].reshape(bs * SAMPLE_PAD, CONV_CH), state_ret[l], state_ssm[l], state_gdn[l],
                 state_mlstm_C[l], state_mlstm_n[l], state_mlstm_m[l][:, None, :]) for l in range(depth)]
    y_s, st_s = _trunk(xs, p_layers, tile_rows(cos_s), tile_rows(sin_s), states_s,
                       sample=True, nbatch=bs, seqlen=SAMPLE_PAD)

    def collect(sts, nb):
        conv = jnp.stack([s[0].reshape(nb, -1, CONV_CH)[:, -(CONV_W - 1):, :] for s in sts])
        stack = lambda k: jnp.stack([s[k] for s in sts])
        return (stack(1), stack(2), conv[..., :SSM_CONV_CH], stack(3), conv[..., SSM_CONV_CH:],
                stack(4), stack(5), stack(6)[:, :, 0, :])

    y_prompt = y_p.reshape(bp, lp, D_MODEL)
    y_sample = y_s.reshape(bs, SAMPLE_PAD, D_MODEL)[:, pad:, :]
    return (y_prompt, y_sample) + collect(st_p, bp) + collect(st_s, bs)
```

```python
import functools
import math

import numpy as np
import jax
import jax.numpy as jnp
from jax import lax
from jax.experimental import pallas as pl
from jax.experimental.pallas import tpu as pltpu

F32 = jnp.float32
BF16 = jnp.bfloat16

D_MODEL = 1024
N_HEADS = 4
HEAD_DIM = 64
GROUP_W = N_HEADS * HEAD_DIM
D_FF = 2816
CONV_W = 4
EPS = 1e-6
ROPE_BASE = 10000.0
PAST_LEN = 16384
PROMPT_CHUNK = 64

COL_RET = 0
COL_SSM_Z = 1024
COL_CONV = 1280
CONV_CH = 1280
SSM_CONV_CH = 512
COL_GDN_GATE = 2560
COL_ML = 2816
COL_GATES = 3840
IN_PAD = 3968
G_DT, G_GA, G_GB, G_MI, G_MF, G_DTV = 0, 4, 8, 12, 16, 20

SUBLANES = 8
LANES = 128
VMEM_LIMIT = 56 * 1024 * 1024

FF_CHUNK = 256
TOKEN_TILE = 512
PROMPT_TILE = 256
SAMPLE_PAD = 8
SAMPLE_GROUP = 8
NEG_BIG = -1e30


def _sigmoid(x):
    return 1.0 / (1.0 + jnp.exp(-x))


def _silu(x):
    return x * _sigmoid(x)


def _softplus(x):
    return jnp.maximum(x, 0.0) + jnp.log1p(jnp.exp(-jnp.abs(x)))


def _rms(x, g):
    return x * lax.rsqrt(jnp.mean(x * x, axis=-1, keepdims=True) + EPS) * g


def _dot(a, b):
    return jnp.dot(a.astype(BF16), b.astype(BF16), preferred_element_type=F32)


def _dot_nt(a, b):
    return lax.dot_general(a.astype(BF16), b.astype(BF16), (((1,), (1,)), ((), ())),
                           preferred_element_type=F32)


def _dot_tn(a, b):
    return lax.dot_general(a.astype(BF16), b.astype(BF16), (((0,), (0,)), ((), ())),
                           preferred_element_type=F32)


def _split3(x):
    hi = x.astype(BF16)
    r = x - hi.astype(F32)
    mid = r.astype(BF16)
    lo = (r - mid.astype(F32)).astype(BF16)
    return hi, mid, lo


def _dot_sel(sel, x):
    hi, mid, lo = _split3(x)
    f = lambda p: jnp.dot(sel, p, preferred_element_type=F32)
    return (f(lo) + f(mid)) + f(hi)


def _dot_sel_r(x, sel):
    hi, mid, lo = _split3(x)
    f = lambda p: jnp.dot(p, sel, preferred_element_type=F32)
    return (f(lo) + f(mid)) + f(hi)


def _dot_sel_nt(sel, x):
    hi, mid, lo = _split3(x)
    f = lambda p: lax.dot_general(sel, p, (((1,), (1,)), ((), ())), preferred_element_type=F32)
    return (f(lo) + f(mid)) + f(hi)


N_SANDWICH = 6


def _layer_block(tail, lead, ngrid=1, buffered=True):
    idx = tuple(lead) + (0,) * len(tail)
    imap = (lambda i: idx) if ngrid == 1 else (lambda i, j: idx)
    kw = dict(pipeline_mode=pl.Buffered(1)) if buffered else {}
    return pl.BlockSpec((None,) * len(lead) + tuple(tail), imap, **kw)


def _prep_win_body(w_ref, o_ref):
    o_ref[:, 0:1792] = w_ref[:, 0:1792].astype(BF16)
    o_ref[:, 1792:2816] = w_ref[:, 1796:2820].astype(BF16)
    o_ref[:, 2816:3840] = w_ref[:, 2828:3852].astype(BF16)
    rows = w_ref.shape[0]
    gates = jnp.concatenate([w_ref[:, 1792:1796], w_ref[:, 2820:2828], w_ref[:, 3852:3860],
                             jnp.zeros((rows, LANES - 5 * N_HEADS), F32)], axis=1)
    o_ref[:, COL_GATES:IN_PAD] = gates.astype(BF16)


def _prep_win(w_in):
    depth, _, in_dim = w_in.shape
    tr = 128
    return pl.pallas_call(
        _prep_win_body,
        out_shape=jax.ShapeDtypeStruct((depth, D_MODEL, IN_PAD), BF16),
        grid=(depth, D_MODEL // tr),
        in_specs=[pl.BlockSpec((None, tr, in_dim), lambda l, r: (l, r, 0))],
        out_specs=pl.BlockSpec((None, tr, IN_PAD), lambda l, r: (l, r, 0)),
        compiler_params=pltpu.CompilerParams(dimension_semantics=("parallel", "parallel")),
        name="prep_win",
    )(w_in)


def _ffn_body(x_ref, g_ref, wg_ref, wu_ref, wd_ref, o_ref, a_ref, *, k):
    x = x_ref[...]
    h = _rms(x, g_ref[4 * k:4 * k + 1, :]).astype(BF16)
    for j in range(D_FF // FF_CHUNK):
        sl = slice(j * FF_CHUNK, (j + 1) * FF_CHUNK)
        gt = jnp.dot(h, wg_ref[:, sl], preferred_element_type=F32)
        up = jnp.dot(h, wu_ref[:, sl], preferred_element_type=F32)
        a_ref[:, sl] = (_silu(gt) * up).astype(BF16)
    y = jnp.dot(a_ref[...], wd_ref[...], preferred_element_type=F32)
    o_ref[...] = x + 0.5 * _rms(y, g_ref[4 * k + 1:4 * k + 2, :])


def _ffn(x, sandwich, wg, wu, wd, l, k):
    m = x.shape[0]
    tm = min(TOKEN_TILE, m)
    return pl.pallas_call(
        functools.partial(_ffn_body, k=k),
        out_shape=jax.ShapeDtypeStruct((m, D_MODEL), F32),
        grid=(m // tm,),
        in_specs=[pl.BlockSpec((tm, D_MODEL), lambda i: (i, 0)),
                  _layer_block((N_SANDWICH, D_MODEL), (l,)),
                  _layer_block((D_MODEL, D_FF), (l, k)),
                  _layer_block((D_MODEL, D_FF), (l, k)),
                  _layer_block((D_FF, D_MODEL), (l, k))],
        out_specs=pl.BlockSpec((tm, D_MODEL), lambda i: (i, 0)),
        scratch_shapes=[pltpu.VMEM((tm, D_FF), BF16)],
        compiler_params=pltpu.CompilerParams(dimension_semantics=("parallel",),
                                             vmem_limit_bytes=VMEM_LIMIT),
        name="ffn",
    )(x, sandwich, wg, wu, wd)


def _inproj_body(x_ref, g_ref, w_ref, o_ref):
    h = _rms(x_ref[...], g_ref[2:3, :]).astype(BF16)
    o_ref[...] = jnp.dot(h, w_ref[...], preferred_element_type=F32)


def _inproj(x, sandwich, w, l):
    m = x.shape[0]
    tm = min(TOKEN_TILE, m)
    return pl.pallas_call(
        _inproj_body,
        out_shape=jax.ShapeDtypeStruct((m, IN_PAD), F32),
        grid=(m // tm,),
        in_specs=[pl.BlockSpec((tm, D_MODEL), lambda i: (i, 0)),
                  _layer_block((N_SANDWICH, D_MODEL), (l,)),
                  _layer_block((D_MODEL, IN_PAD), (l,))],
        out_specs=pl.BlockSpec((tm, IN_PAD), lambda i: (i, 0)),
        compiler_params=pltpu.CompilerParams(dimension_semantics=("parallel",),
                                             vmem_limit_bytes=VMEM_LIMIT),
        name="inproj",
    )(x, sandwich, w)


def _outproj_body(x_ref, m_ref, g_ref, w_ref, o_ref):
    y = jnp.dot(m_ref[...], w_ref[...], preferred_element_type=F32)
    o_ref[...] = x_ref[...] + _rms(y, g_ref[3:4, :])


def _outproj(x, mixed, sandwich, w, l):
    m = x.shape[0]
    tm = min(TOKEN_TILE, m)
    return pl.pallas_call(
        _outproj_body,
        out_shape=jax.ShapeDtypeStruct((m, D_MODEL), F32),
        grid=(m // tm,),
        in_specs=[pl.BlockSpec((tm, D_MODEL), lambda i: (i, 0)),
                  pl.BlockSpec((tm, D_MODEL), lambda i: (i, 0)),
                  _layer_block((N_SANDWICH, D_MODEL), (l,)),
                  _layer_block((D_MODEL, D_MODEL), (l,))],
        out_specs=pl.BlockSpec((tm, D_MODEL), lambda i: (i, 0)),
        compiler_params=pltpu.CompilerParams(dimension_semantics=("parallel",),
                                             vmem_limit_bytes=VMEM_LIMIT),
        name="outproj",
    )(x, mixed, sandwich, w)


def _head(x, h):
    return x[:, h * HEAD_DIM:(h + 1) * HEAD_DIM]


def _neumann_inverse(amat, c):
    row = lax.broadcasted_iota(jnp.int32, (c, c), 0)
    col = lax.broadcasted_iota(jnp.int32, (c, c), 1)
    eye = (row == col).astype(F32)
    m = -amat
    p = eye + m
    for _ in range(max(int(math.ceil(math.log2(c))) - 1, 0)):
        m = _dot(m, m)
        p = p + _dot(p, m)
    return p


def _mix_body(*refs, sample, rows, chunk, ret_carry):
    nchunks = rows // chunk
    it = iter(refs)
    proj_ref = next(it)
    hist_ref = next(it) if sample else None
    cos_ref, sin_ref = next(it), next(it)
    rinter_ref, rtail_ref, rintra_ref = next(it), next(it), next(it)
    tri_ref, sel_ref, ones_ref = next(it), next(it), next(it)
    gvec_ref, convw_ref, convb_ref, dvec_ref, norms_ref = next(it), next(it), next(it), next(it), next(it)
    if sample:
        sret_in, sssm_in, sgdn_in, smc_in, smn_in, smm_in = (next(it) for _ in range(6))
    mixed_ref, convraw_ref = next(it), next(it)
    sret_ref, sssm_ref, sgdn_ref, smc_ref, smn_ref, smm_ref = (next(it) for _ in range(6))
    e_ref, g_ref, rq_ref, rk_ref, sx_ref, sbc_ref, gq_ref, gk_ref, gv_ref, o_ref = (next(it) for _ in range(10))
    if not sample:
        sret_in, sssm_in, sgdn_in, smc_in, smn_in, smm_in = (
            sret_ref, sssm_ref, sgdn_ref, smc_ref, smn_ref, smm_ref)

    if sample:
        rowid = lax.broadcasted_iota(jnp.int32, (rows, 1), 0)
        valid = (rowid % SAMPLE_PAD) >= (SAMPLE_PAD - 4)
        e_ref[0:SUBLANES, :] = jnp.zeros((SUBLANES, CONV_CH), F32)
    else:
        valid = None

        @pl.when(pl.program_id(1) == 0)
        def _():
            e_ref[0:SUBLANES, :] = jnp.zeros((SUBLANES, CONV_CH), F32)
            sret_ref[...] = jnp.zeros_like(sret_ref)
            sssm_ref[...] = jnp.zeros_like(sssm_ref)
            sgdn_ref[...] = jnp.zeros_like(sgdn_ref)
            smc_ref[...] = jnp.zeros_like(smc_ref)
            smn_ref[...] = jnp.zeros_like(smn_ref)
            smm_ref[...] = jnp.zeros_like(smm_ref)

    def keep(x, fill=0.0):
        return x if valid is None else jnp.where(valid, x, fill)

    lane = lax.broadcasted_iota(jnp.int32, (1, LANES), 1)
    in_lanes = lambda lo: (lane >= lo) & (lane < lo + N_HEADS)
    graw = proj_ref[:, COL_GATES:COL_GATES + LANES]
    xb = graw + gvec_ref[0:1, :]
    sp = _softplus(xb)
    neg_a = -jnp.exp(gvec_ref[1:2, :])
    logf = -_softplus(-xb)
    pre = jnp.where(lane < 2 * N_HEADS, neg_a * sp, jnp.where(in_lanes(G_MF), logf, 0.0))
    cum = _dot_sel(tri_ref[...], keep(pre))
    dtv = pltpu.roll(sp, G_DTV - G_DT, axis=1)
    aux = jnp.where(in_lanes(G_GB), _sigmoid(graw),
                    jnp.where(in_lanes(G_MI), xb, jnp.where(in_lanes(G_DTV), dtv, 0.0)))
    if sample:
        aux = jnp.where(valid, aux, jnp.where(in_lanes(G_MI), NEG_BIG, 0.0))
    g_ref[...] = jnp.where(in_lanes(G_GB) | in_lanes(G_MI) | in_lanes(G_DTV), aux, cum)

    lane_g = lax.broadcasted_iota(jnp.int32, (1, GROUP_W), 1)
    first_half = (lane_g % HEAD_DIM) < (HEAD_DIM // 2)

    def rotary(x):
        swapped = jnp.where(first_half, pltpu.roll(x, GROUP_W - HEAD_DIM // 2, axis=1),
                            pltpu.roll(x, HEAD_DIM // 2, axis=1))
        return x * cos_ref[...] + swapped * sin_ref[...]

    rq_ref[...] = rotary(proj_ref[:, COL_RET:COL_RET + GROUP_W])
    rk_ref[...] = keep(rotary(proj_ref[:, COL_RET + GROUP_W:COL_RET + 2 * GROUP_W]) * HEAD_DIM ** -0.5)

    raw = proj_ref[:, COL_CONV:COL_CONV + CONV_CH]
    if sample:
        raw = raw + hist_ref[...]
    e_ref[SUBLANES:SUBLANES + rows, :] = raw
    conv = raw * convw_ref[CONV_W - 1:CONV_W, :]
    for s in range(1, CONV_W):
        conv = conv + e_ref[SUBLANES - s:SUBLANES - s + rows, :] * convw_ref[CONV_W - 1 - s:CONV_W - s, :]
    conv = keep(_silu(conv + convb_ref[...]))
    if sample:
        convraw_ref[...] = raw
    else:
        convraw_ref[...] = e_ref[rows:rows + SUBLANES, :]
        e_ref[0:SUBLANES, :] = e_ref[rows:rows + SUBLANES, :]
    sx_ref[...] = conv[:, 0:GROUP_W]
    sbc_ref[...] = conv[:, GROUP_W:SSM_CONV_CH]
    ones_bd = ones_ref[...]

    def head_sumsq(x):
        return _dot_sel_r(x * x, ones_bd)

    gq = conv[:, SSM_CONV_CH:SSM_CONV_CH + GROUP_W]
    gk = conv[:, SSM_CONV_CH + GROUP_W:SSM_CONV_CH + 2 * GROUP_W]
    gq_ref[...] = gq * lax.rsqrt(head_sumsq(gq) + EPS) * HEAD_DIM ** -0.5
    gk_ref[...] = gk * lax.rsqrt(head_sumsq(gk) + EPS)
    gv_ref[...] = conv[:, SSM_CONV_CH + 2 * GROUP_W:CONV_CH]

    r_i = lax.broadcasted_iota(jnp.int32, (chunk, chunk), 0)
    c_i = lax.broadcasted_iota(jnp.int32, (chunk, chunk), 1)
    tri = r_i >= c_i
    strict = r_i > c_i
    lane_h = lax.broadcasted_iota(jnp.int32, (1, N_HEADS), 1)

    def chunk_step(i, carry):
        r0 = pl.multiple_of(i * chunk, chunk)
        rs = pl.ds(r0, chunk)
        sidx = i if sample else 0
        gc = g_ref[rs, :]
        gt = _dot_sel_nt(sel_ref[...], gc)
        col = lambda j: gc[:, j:j + 1]
        rowv = lambda j: gt[j:j + 1, :]
        last = lambda j: gc[chunk - 1:chunk, j:j + 1]

        heads = range(N_HEADS)
        s_ret = [sret_in[sidx, h] for h in heads]
        s_ssm = [sssm_in[sidx, h] for h in heads]
        s_gdn = [sgdn_in[sidx, h] for h in heads]
        s_mc = [smc_in[sidx, h] for h in heads]
        n_all = smn_in[sidx]
        m_row = smm_in[sidx]
        rq_all, rk_all = rq_ref[rs, :], rk_ref[rs, :]
        rv_all = proj_ref[rs, COL_RET + 2 * GROUP_W:COL_RET + 3 * GROUP_W]
        x_all, bc_all = sx_ref[rs, :], sbc_ref[rs, :]
        gq_all, gk_all, gv_all = gq_ref[rs, :], gk_ref[rs, :], gv_ref[rs, :]
        mq_all = proj_ref[rs, COL_ML:COL_ML + GROUP_W]
        mk_all = proj_ref[rs, COL_ML + GROUP_W:COL_ML + 2 * GROUP_W] * HEAD_DIM ** -0.5
        mv_all = proj_ref[rs, COL_ML + 2 * GROUP_W:COL_ML + 3 * GROUP_W]

        qi_all = rq_all * rinter_ref[...]
        kt_all = rk_all * rtail_ref[...]
        o_ret, n_ret = [], []
        for h in heads:
            att = _dot_nt(_head(rq_all, h), _head(rk_all, h)) * rintra_ref[h]
            o_ret.append(_dot(att, _head(rv_all, h)) + _dot(_head(qi_all, h), s_ret[h]))
            n_ret.append(s_ret[h] * ret_carry[h] + _dot_tn(_head(kt_all, h), _head(rv_all, h)))

        o_ssm, n_ssm = [], []
        for h in heads:
            grp = h // (N_HEADS // 2)
            b_g = bc_all[:, grp * HEAD_DIM:(grp + 1) * HEAD_DIM]
            c_g = bc_all[:, 2 * HEAD_DIM + grp * HEAD_DIM:2 * HEAD_DIM + (grp + 1) * HEAD_DIM]
            x_h = _head(x_all, h)
            cum_c, cum_r, cum_l = col(G_DT + h), rowv(G_DT + h), last(G_DT + h)
            lmat = jnp.exp(jnp.where(tri, cum_c - cum_r, -jnp.inf))
            scores = _dot_nt(c_g, b_g) * lmat * rowv(G_DTV + h)
            o_ssm.append(_dot(scores, x_h) + _dot_nt(c_g, s_ssm[h]) * jnp.exp(cum_c))
            w_c = jnp.exp(cum_l - cum_c) * col(G_DTV + h)
            n_ssm.append(s_ssm[h] * jnp.exp(cum_l) + _dot_tn(x_h * w_c, b_g))

        o_gdn, n_gdn = [], []
        for h in heads:
            q_h, k_h, v_h = _head(gq_all, h), _head(gk_all, h), _head(gv_all, h)
            cum_c, cum_r, cum_l = col(G_GA + h), rowv(G_GA + h), last(G_GA + h)
            beta_c = col(G_GB + h)
            gam = jnp.exp(jnp.where(tri, cum_c - cum_r, -jnp.inf))
            amat = _dot_nt(k_h, k_h) * jnp.where(strict, gam, 0.0) * beta_c
            ecum = jnp.exp(cum_c)
            tinv = _neumann_inverse(amat, chunk)
            u = _dot(tinv, v_h * beta_c)
            w = _dot(tinv, k_h * (beta_c * ecum))
            v_new = u - _dot(w, s_gdn[h])
            qk = _dot_nt(q_h, k_h) * gam
            o_gdn.append(_dot(q_h * ecum, s_gdn[h]) + _dot(qk, v_new))
            n_gdn.append(s_gdn[h] * jnp.exp(cum_l) + _dot_tn(k_h * jnp.exp(cum_l - cum_c), v_new))

        m_new_row = jnp.zeros((1, N_HEADS), F32)
        o_ml, n_mc, n_mn = [], [], []
        for h in heads:
            q_h, k_h, v_h = _head(mq_all, h), _head(mk_all, h), _head(mv_all, h)
            b_c, b_r, b_l = col(G_MF + h), rowv(G_MF + h), last(G_MF + h)
            i_c, i_r = col(G_MI + h), rowv(G_MI + h)
            m_old = m_row[:, h:h + 1]
            n_h = n_all[h:h + 1, :]
            dmat = jnp.where(tri, b_c - b_r + i_r, -jnp.inf)
            m_t = jnp.maximum(b_c + m_old, jnp.max(dmat, axis=-1, keepdims=True))
            wts = jnp.exp(dmat - m_t)
            inter = jnp.exp(b_c + m_old - m_t)
            qk = _dot_nt(q_h, k_h) * wts
            num = _dot(qk, v_h) + inter * _dot(q_h, s_mc[h])
            den = jnp.sum(qk, axis=-1, keepdims=True) + inter * jnp.sum(q_h * n_h, axis=-1, keepdims=True)
            o_ml.append(num / jnp.maximum(jnp.abs(den), jnp.exp(-m_t)))
            m_new = m_t[chunk - 1:chunk, :]
            ws = jnp.exp(b_l - b_c + i_c - m_new)
            cs = jnp.exp(b_l + m_old - m_new)
            kw = k_h * ws
            n_mc.append(s_mc[h] * cs + _dot_tn(kw, v_h))
            n_mn.append(n_h * cs + jnp.sum(kw, axis=0, keepdims=True))
            m_new_row = jnp.where(lane_h == h, m_new, m_new_row)

        o_ref[rs, :] = jnp.concatenate(o_ret + o_ssm + o_gdn + o_ml, axis=1)
        for h in heads:
            sret_ref[sidx, h] = n_ret[h]
            sssm_ref[sidx, h] = n_ssm[h]
            sgdn_ref[sidx, h] = n_gdn[h]
            smc_ref[sidx, h] = n_mc[h]
        smn_ref[sidx] = jnp.concatenate(n_mn, axis=0)
        smm_ref[sidx] = m_new_row
        return carry

    lax.fori_loop(0, nchunks, chunk_step, 0)

    def head_norm(x, g):
        return x * lax.rsqrt(head_sumsq(x) * (1.0 / HEAD_DIM) + EPS) * g

    out_ret = head_norm(o_ref[:, 0:GROUP_W], norms_ref[0:1, :]) * _silu(
        proj_ref[:, COL_RET + 3 * GROUP_W:COL_RET + 4 * GROUP_W])
    mixed_ref[:, 0:GROUP_W] = keep(out_ret).astype(BF16)
    y = o_ref[:, GROUP_W:2 * GROUP_W] + dvec_ref[...] * sx_ref[...]
    out_ssm = _rms(y * _silu(proj_ref[:, COL_SSM_Z:COL_SSM_Z + GROUP_W]), norms_ref[1:2, :])
    mixed_ref[:, GROUP_W:2 * GROUP_W] = keep(out_ssm).astype(BF16)
    out_gdn = head_norm(o_ref[:, 2 * GROUP_W:3 * GROUP_W], norms_ref[2:3, :]) * _silu(
        proj_ref[:, COL_GDN_GATE:COL_GDN_GATE + GROUP_W])
    mixed_ref[:, 2 * GROUP_W:3 * GROUP_W] = keep(out_gdn).astype(BF16)
    out_ml = _sigmoid(proj_ref[:, COL_ML + 3 * GROUP_W:COL_ML + 4 * GROUP_W]) * head_norm(
        o_ref[:, 3 * GROUP_W:4 * GROUP_W], norms_ref[3:4, :])
    mixed_ref[:, 3 * GROUP_W:4 * GROUP_W] = keep(out_ml).astype(BF16)


NEXP = 6
X_CSSM, X_DT, X_CGDN, X_BETA, X_B, X_IMB = range(NEXP)


def _split2(x):
    hi = x.astype(BF16)
    return hi, (x - hi.astype(F32)).astype(BF16)


def _dot_sel_r2(x, sel):
    hi, lo = _split2(x)
    return jnp.dot(lo, sel, preferred_element_type=F32) + jnp.dot(hi, sel, preferred_element_type=F32)


def _mixp_body(proj_ref, cos_ref, sin_ref, rinter_ref, rtail_ref, rintra_ref, tri_ref, expand_ref,
               maskbd_ref, maskg_ref, maskht_ref, gvec_ref, convw_ref, convb_ref, dvec_ref, norms_ref,
               mixed_ref, convraw_ref, oret_ref, ossm_ref, ogdn_ref, omc_ref, omn_ref, omm_ref,
               e_ref, x_ref, rq_ref, rk_ref, sx_ref, sbc_ref, gq_ref, gk_ref, gv_ref, o_ref,
               sret_ref, sht_ref, sgdn_ref, smc_ref, vec_ref, *, rows, chunk, ret_carry):
    nchunks = rows // chunk
    t_id = pl.program_id(1)

    @pl.when(t_id == 0)
    def _():
        e_ref[0:SUBLANES, :] = jnp.zeros((SUBLANES, CONV_CH), F32)
        sret_ref[...] = jnp.zeros_like(sret_ref)
        sht_ref[...] = jnp.zeros_like(sht_ref)
        sgdn_ref[...] = jnp.zeros_like(sgdn_ref)
        smc_ref[...] = jnp.zeros_like(smc_ref)
        vec_ref[...] = jnp.zeros_like(vec_ref)

    maskbd = maskbd_ref[...]
    maskbd_b = maskbd.astype(BF16)
    maskg = maskg_ref[...]
    maskg_b = maskg.astype(BF16)

    def seg_sum(x):
        return _dot_sel_r2(x, maskbd_b)

    lane = lax.broadcasted_iota(jnp.int32, (1, LANES), 1)
    in_lanes = lambda lo: (lane >= lo) & (lane < lo + N_HEADS)
    graw = proj_ref[:, COL_GATES:COL_GATES + LANES]
    xb = graw + gvec_ref[0:1, :]
    sp = _softplus(xb)
    neg_a = -jnp.exp(gvec_ref[1:2, :])
    logf = -_softplus(-xb)
    pre = jnp.where(lane < 2 * N_HEADS, neg_a * sp, jnp.where(in_lanes(G_MF), logf, 0.0))
    cum = _dot_sel(tri_ref[...], pre)
    imb = xb - pltpu.roll(cum, LANES - (G_MF - G_MI), axis=1)
    dtv = pltpu.roll(sp, G_DTV - G_DT, axis=1)
    gates = jnp.where(in_lanes(G_GB), _sigmoid(graw),
                      jnp.where(in_lanes(G_MI), imb, jnp.where(in_lanes(G_DTV), dtv, cum)))
    for j, lo in enumerate((G_DT, G_DTV, G_GA, G_GB, G_MF, G_MI)):
        x_ref[j] = _dot_sel_r(gates, expand_ref[:, j * GROUP_W:(j + 1) * GROUP_W])

    lane_g = lax.broadcasted_iota(jnp.int32, (1, GROUP_W), 1)
    first_half = (lane_g % HEAD_DIM) < (HEAD_DIM // 2)

    def rotary(x):
        swapped = jnp.where(first_half, pltpu.roll(x, GROUP_W - HEAD_DIM // 2, axis=1),
                            pltpu.roll(x, HEAD_DIM // 2, axis=1))
        return x * cos_ref[...] + swapped * sin_ref[...]

    rq_ref[...] = rotary(proj_ref[:, COL_RET:COL_RET + GROUP_W])
    rk_ref[...] = rotary(proj_ref[:, COL_RET + GROUP_W:COL_RET + 2 * GROUP_W]) * HEAD_DIM ** -0.5

    raw = proj_ref[:, COL_CONV:COL_CONV + CONV_CH]
    e_ref[SUBLANES:SUBLANES + rows, :] = raw
    conv = raw * convw_ref[CONV_W - 1:CONV_W, :]
    for s in range(1, CONV_W):
        conv = conv + e_ref[SUBLANES - s:SUBLANES - s + rows, :] * convw_ref[CONV_W - 1 - s:CONV_W - s, :]
    conv = _silu(conv + convb_ref[...])
    convraw_ref[...] = e_ref[rows:rows + SUBLANES, :]
    e_ref[0:SUBLANES, :] = e_ref[rows:rows + SUBLANES, :]
    sx_ref[...] = conv[:, 0:GROUP_W]
    sbc_ref[...] = conv[:, GROUP_W:SSM_CONV_CH]
    gq = conv[:, SSM_CONV_CH:SSM_CONV_CH + GROUP_W]
    gk = conv[:, SSM_CONV_CH + GROUP_W:SSM_CONV_CH + 2 * GROUP_W]
    gq_ref[...] = gq * lax.rsqrt(seg_sum(gq * gq) + EPS) * HEAD_DIM ** -0.5
    gk_ref[...] = gk * lax.rsqrt(seg_sum(gk * gk) + EPS)
    gv_ref[...] = conv[:, SSM_CONV_CH + 2 * GROUP_W:CONV_CH]

    r_i = lax.broadcasted_iota(jnp.int32, (chunk, GROUP_W), 0)
    c_i = lax.broadcasted_iota(jnp.int32, (chunk, GROUP_W), 1) % HEAD_DIM
    trim = r_i >= c_i
    diag = r_i == c_i
    strict = r_i > c_i
    eye = diag.astype(F32)
    head_of_lane = lax.broadcasted_iota(jnp.int32, (1, GROUP_W), 1) // HEAD_DIM
    carry_row = jnp.zeros((1, GROUP_W), F32)
    for h in range(N_HEADS):
        carry_row = jnp.where(head_of_lane == h, ret_carry[h], carry_row)
    maskht = maskht_ref[...]

    def bd(x):
        xb16 = x.astype(BF16)
        return jnp.concatenate([xb16] * N_HEADS, axis=0) * maskbd_b

    def last_row(x):
        return x[chunk - 1:chunk, :]

    def row_form(x):
        return jnp.sum(jnp.where(diag, x, 0.0), axis=0, keepdims=True)

    def seg_max(x):
        out = None
        for h in range(N_HEADS):
            sel = head_of_lane == h
            m_h = jnp.max(jnp.where(sel, x, -jnp.inf), axis=-1, keepdims=True)
            out = jnp.where(sel, m_h, 0.0) if out is None else jnp.where(sel, m_h, out)
        return out

    chunks = [slice(c * chunk, (c + 1) * chunk) for c in range(nchunks)]

    g_gam, g_beta, g_ecum, g_k, g_n, g_p = [], [], [], [], [], []
    for rs in chunks:
        ce = x_ref[X_CGDN, rs, :]
        gam = jnp.exp(jnp.where(trim, ce - row_form(ce), -jnp.inf))
        k_c = gk_ref[rs, :]
        beta = x_ref[X_BETA, rs, :]
        n0 = -(_dot_nt(k_c, bd(k_c)) * jnp.where(strict, gam, 0.0) * beta)
        g_gam.append(gam), g_beta.append(beta), g_ecum.append(jnp.exp(ce)), g_k.append(k_c)
        g_n.append(n0), g_p.append(eye + n0)
    for _ in range(max(int(math.ceil(math.log2(chunk))) - 1, 0)):
        for c in range(nchunks):
            g_n[c] = _dot(g_n[c], bd(g_n[c]))
        for c in range(nchunks):
            g_p[c] = g_p[c] + _dot(g_p[c], bd(g_n[c]))
    g_u, g_w = [], []
    for c, rs in enumerate(chunks):
        g_u.append(_dot(g_p[c], bd(gv_ref[rs, :] * g_beta[c])))
        g_w.append(_dot(g_p[c], bd(g_k[c] * (g_beta[c] * g_ecum[c]))))

    s_ret, s_ht, s_gdn, s_mc = sret_ref[...], sht_ref[...], sgdn_ref[...], smc_ref[...]
    n_row, m_row = vec_ref[0:1, :], vec_ref[1:2, :]
    for c, rs in enumerate(chunks):
        q_c, k_c = rq_ref[rs, :], rk_ref[rs, :]
        v_c = proj_ref[rs, COL_RET + 2 * GROUP_W:COL_RET + 3 * GROUP_W]
        att = _dot_nt(q_c, bd(k_c)) * rintra_ref[...]
        o_ret = _dot(att, bd(v_c)) + _dot(q_c * rinter_ref[...], s_ret)
        s_ret = s_ret * carry_row + _dot_tn(k_c * rtail_ref[...], v_c) * maskbd

        ce = x_ref[X_CSSM, rs, :]
        ce_last = last_row(ce)
        lmat = jnp.exp(jnp.where(trim, ce - row_form(ce), -jnp.inf))
        b_c, c_c = sbc_ref[rs, 0:2 * HEAD_DIM], sbc_ref[rs, 2 * HEAD_DIM:4 * HEAD_DIM]
        xdt = sx_ref[rs, :] * x_ref[X_DT, rs, :]
        rhs_b = jnp.concatenate([b_c.astype(BF16)] * N_HEADS, axis=0) * maskg_b
        scores = _dot_nt(c_c, rhs_b) * lmat
        o_ssm = _dot(scores, bd(xdt)) + _dot(c_c, s_ht) * jnp.exp(ce)
        s_ht = s_ht * jnp.exp(ce_last) + _dot_tn(b_c, xdt * jnp.exp(ce_last - ce)) * maskht

        ce = x_ref[X_CGDN, rs, :]
        ce_last = last_row(ce)
        q_c = gq_ref[rs, :]
        v_new = g_u[c] - _dot(g_w[c], s_gdn)
        qk = _dot_nt(q_c, bd(g_k[c])) * g_gam[c]
        o_gdn = _dot(q_c * g_ecum[c], s_gdn) + _dot(qk, bd(v_new))
        s_gdn = s_gdn * jnp.exp(ce_last) + _dot_tn(g_k[c] * jnp.exp(ce_last - ce), v_new) * maskbd

        q_c = proj_ref[rs, COL_ML:COL_ML + GROUP_W]
        k_c = proj_ref[rs, COL_ML + GROUP_W:COL_ML + 2 * GROUP_W] * HEAD_DIM ** -0.5
        v_c = proj_ref[rs, COL_ML + 2 * GROUP_W:COL_ML + 3 * GROUP_W]
        b_e, imb_e = x_ref[X_B, rs, :], x_ref[X_IMB, rs, :]
        dmat = jnp.where(trim, b_e + row_form(imb_e), -jnp.inf)
        m_t = jnp.maximum(b_e + m_row, seg_max(dmat))
        wts = jnp.exp(dmat - m_t)
        inter = jnp.exp(b_e + m_row - m_t)
        qk = _dot_nt(q_c, bd(k_c)) * wts
        num = _dot(qk, bd(v_c)) + inter * _dot(q_c, s_mc)
        den = seg_sum(qk) + inter * seg_sum(q_c * n_row)
        o_ml = num / jnp.maximum(jnp.abs(den), jnp.exp(-m_t))
        m_new = last_row(m_t)
        b_last = last_row(b_e)
        kw = k_c * jnp.exp(b_last + imb_e - m_new)
        cs = jnp.exp(b_last + m_row - m_new)
        s_mc = s_mc * cs + _dot_tn(kw, v_c) * maskbd
        n_row = n_row * cs + jnp.sum(kw, axis=0, keepdims=True)
        m_row = m_new

        o_ref[rs, :] = jnp.concatenate([o_ret, o_ssm, o_gdn, o_ml], axis=1)

    sret_ref[...] = s_ret
    sht_ref[...] = s_ht
    sgdn_ref[...] = s_gdn
    smc_ref[...] = s_mc
    vec_ref[0:1, :] = n_row
    vec_ref[1:2, :] = m_row

    def head_norm(x, g):
        return x * lax.rsqrt(seg_sum(x * x) * (1.0 / HEAD_DIM) + EPS) * g

    out_ret = head_norm(o_ref[:, 0:GROUP_W], norms_ref[0:1, :]) * _silu(
        proj_ref[:, COL_RET + 3 * GROUP_W:COL_RET + 4 * GROUP_W])
    mixed_ref[:, 0:GROUP_W] = out_ret.astype(BF16)
    y = o_ref[:, GROUP_W:2 * GROUP_W] + dvec_ref[...] * sx_ref[...]
    out_ssm = _rms(y * _silu(proj_ref[:, COL_SSM_Z:COL_SSM_Z + GROUP_W]), norms_ref[1:2, :])
    mixed_ref[:, GROUP_W:2 * GROUP_W] = out_ssm.astype(BF16)
    out_gdn = head_norm(o_ref[:, 2 * GROUP_W:3 * GROUP_W], norms_ref[2:3, :]) * _silu(
        proj_ref[:, COL_GDN_GATE:COL_GDN_GATE + GROUP_W])
    mixed_ref[:, 2 * GROUP_W:3 * GROUP_W] = out_gdn.astype(BF16)
    out_ml = _sigmoid(proj_ref[:, COL_ML + 3 * GROUP_W:COL_ML + 4 * GROUP_W]) * head_norm(
        o_ref[:, 3 * GROUP_W:4 * GROUP_W], norms_ref[3:4, :])
    mixed_ref[:, 3 * GROUP_W:4 * GROUP_W] = out_ml.astype(BF16)

    @pl.when(t_id == pl.num_programs(1) - 1)
    def _():
        for h in range(N_HEADS):
            hs = slice(h * HEAD_DIM, (h + 1) * HEAD_DIM)
            grp = h // (N_HEADS // 2)
            oret_ref[0, h] = s_ret[hs, hs]
            ogdn_ref[0, h] = s_gdn[hs, hs]
            omc_ref[0, h] = s_mc[hs, hs]
            ossm_ref[0, h] = s_ht[grp * HEAD_DIM:(grp + 1) * HEAD_DIM, hs]
        omn_ref[0] = jnp.concatenate([n_row[:, h * HEAD_DIM:(h + 1) * HEAD_DIM] for h in range(N_HEADS)], axis=0)
        omm_ref[0] = jnp.concatenate([m_row[:, h * HEAD_DIM:h * HEAD_DIM + 1] for h in range(N_HEADS)], axis=1)


def _prompt_consts(rows, chunk):
    lg = np.log(1.0 - np.exp2(-5.0 - np.arange(N_HEADS, dtype=np.float64)))
    idx = np.arange(chunk, dtype=np.float64)
    rel = idx[:, None] - idx[None, :]
    intra = np.where(rel[None] >= 0, np.exp(np.maximum(rel[None], 0.0) * lg[:, None, None]), 0.0)
    intra = np.concatenate(list(intra), axis=1)
    lanes = lambda t: np.repeat(t.T, HEAD_DIM, axis=1)
    inter = lanes(np.exp((idx[None, :] + 1.0) * lg[:, None]))
    tail = lanes(np.exp((chunk - 1.0 - idx[None, :]) * lg[:, None]))
    carry = tuple(float(c) for c in np.exp(chunk * lg))
    r = np.arange(rows)
    tri = (r[:, None] >= r[None, :]) & (r[:, None] // chunk == r[None, :] // chunk)
    expand = np.zeros((LANES, NEXP * GROUP_W), np.float32)
    for j, lo in enumerate((G_DT, G_DTV, G_GA, G_GB, G_MF, G_MI)):
        for h in range(N_HEADS):
            expand[lo + h, j * GROUP_W + h * HEAD_DIM:j * GROUP_W + (h + 1) * HEAD_DIM] = 1.0
    g = np.arange(GROUP_W)
    maskbd = (g[:, None] // HEAD_DIM == g[None, :] // HEAD_DIM)
    n = np.arange(2 * HEAD_DIM)
    maskg = (g[:, None] // (2 * HEAD_DIM) == n[None, :] // HEAD_DIM)
    f = lambda a, dt=F32: jnp.asarray(np.asarray(a, np.float32), dt)
    return (f(inter), f(tail), f(intra), f(tri, BF16), f(expand, BF16), f(maskbd), f(maskg), f(maskg.T)), carry


def _mix_prompt(proj, cos, sin, small, l, *, nbatch, seqlen):
    rows, chunk = min(PROMPT_TILE, seqlen), PROMPT_CHUNK
    ntile = seqlen // rows
    consts, carry = _prompt_consts(rows, chunk)
    m = proj.shape[0]
    row_map = lambda b, t: (b * ntile + t, 0)
    c2 = lambda b, t: (0, 0)
    full = lambda a: pl.BlockSpec(a.shape, c2)
    in_specs = [pl.BlockSpec((rows, IN_PAD), row_map),
                pl.BlockSpec((rows, GROUP_W), lambda b, t: (t, 0)), pl.BlockSpec((rows, GROUP_W), lambda b, t: (t, 0))]
    in_specs += [full(a) for a in consts]
    in_specs += [_layer_block(a.shape[1:], (l,), ngrid=2, buffered=False) for a in small]
    st4 = pl.BlockSpec((1, N_HEADS, HEAD_DIM, HEAD_DIM), lambda b, t: (b, 0, 0, 0))
    st3 = lambda k: pl.BlockSpec((1, k, (N_HEADS * HEAD_DIM) // k if k == N_HEADS else N_HEADS), lambda b, t: (b, 0, 0))
    out_shape = [jax.ShapeDtypeStruct((m, D_MODEL), BF16),
                 jax.ShapeDtypeStruct((nbatch * SUBLANES, CONV_CH), F32)]
    out_shape += [jax.ShapeDtypeStruct((nbatch, N_HEADS, HEAD_DIM, HEAD_DIM), F32) for _ in range(4)]
    out_shape += [jax.ShapeDtypeStruct((nbatch, N_HEADS, HEAD_DIM), F32),
                  jax.ShapeDtypeStruct((nbatch, 1, N_HEADS), F32)]
    out_specs = [pl.BlockSpec((rows, D_MODEL), row_map), pl.BlockSpec((SUBLANES, CONV_CH), lambda b, t: (b, 0)),
                 st4, st4, st4, st4, st3(N_HEADS), st3(1)]
    scratch = [pltpu.VMEM((rows + SUBLANES, CONV_CH), F32), pltpu.VMEM((NEXP, rows, GROUP_W), F32)]
    scratch += [pltpu.VMEM((rows, GROUP_W), F32) for _ in range(7)]
    scratch += [pltpu.VMEM((rows, D_MODEL), F32),
                pltpu.VMEM((GROUP_W, GROUP_W), F32), pltpu.VMEM((2 * HEAD_DIM, GROUP_W), F32),
                pltpu.VMEM((GROUP_W, GROUP_W), F32), pltpu.VMEM((GROUP_W, GROUP_W), F32),
                pltpu.VMEM((SUBLANES, GROUP_W), F32)]
    body = functools.partial(_mixp_body, rows=rows, chunk=chunk, ret_carry=carry)
    outs = pl.pallas_call(
        body, out_shape=out_shape, grid=(nbatch, ntile), in_specs=in_specs, out_specs=out_specs,
        scratch_shapes=scratch,
        compiler_params=pltpu.CompilerParams(dimension_semantics=("parallel", "arbitrary"),
                                             vmem_limit_bytes=VMEM_LIMIT),
        name="mix_prompt",
    )(proj, cos, sin, *consts, *small)
    outs = list(outs)
    outs[3] = jnp.swapaxes(outs[3], -1, -2)
    return outs


def _ret_tables(chunk, pad):
    lg = np.log(1.0 - np.exp2(-5.0 - np.arange(N_HEADS, dtype=np.float64)))
    idx = np.arange(chunk, dtype=np.float64)
    rel = idx[:, None] - idx[None, :]
    intra = np.where(rel[None] >= 0, np.exp(np.maximum(rel[None], 0.0) * lg[:, None, None]), 0.0)
    inter = np.where(idx[None, :] >= pad, np.exp((idx[None, :] - pad + 1.0) * lg[:, None]), 0.0)
    tail = np.exp((chunk - 1.0 - idx[None, :]) * lg[:, None])
    carry = np.exp((chunk - pad) * lg)
    lanes = lambda t: np.repeat(t.T, HEAD_DIM, axis=1)
    return (jnp.asarray(lanes(inter), F32), jnp.asarray(lanes(tail), F32), jnp.asarray(intra, F32),
            tuple(float(c) for c in carry))


def _const_mats(rows, chunk):
    r = np.arange(rows)
    tri = ((r[:, None] >= r[None, :]) & (r[:, None] // chunk == r[None, :] // chunk))
    sel = np.zeros((4 * SUBLANES, LANES), np.float32)
    sel[np.arange(24), np.arange(24)] = 1.0
    g = np.arange(GROUP_W)
    ones_bd = (g[:, None] // HEAD_DIM == g[None, :] // HEAD_DIM)
    return (jnp.asarray(tri, BF16), jnp.asarray(sel, BF16), jnp.asarray(ones_bd, BF16))


def _mix_sample(proj, hist, cos, sin, small, states, l, *, nbatch):
    rows, chunk, pad = SAMPLE_GROUP * SAMPLE_PAD, SAMPLE_PAD, SAMPLE_PAD - 4
    row_map = lambda i: (i, 0)
    c2 = lambda i: (0, 0)
    rinter, rtail, rintra, carry = _ret_tables(chunk, pad)
    tri, sel, ones_bd = _const_mats(rows, chunk)
    m = proj.shape[0]
    in_specs = [pl.BlockSpec((rows, IN_PAD), row_map),
                pl.BlockSpec((None, rows, CONV_CH), lambda i: (l, i, 0)),
                pl.BlockSpec((rows, GROUP_W), c2), pl.BlockSpec((rows, GROUP_W), c2),
                pl.BlockSpec((chunk, GROUP_W), c2), pl.BlockSpec((chunk, GROUP_W), c2),
                pl.BlockSpec((N_HEADS, chunk, chunk), lambda i: (0, 0, 0)),
                pl.BlockSpec((rows, rows), c2), pl.BlockSpec((4 * SUBLANES, LANES), c2),
                pl.BlockSpec((GROUP_W, GROUP_W), c2)]
    in_specs += [_layer_block(a.shape[1:], (l,), buffered=False) for a in small]
    g = SAMPLE_GROUP
    in_specs += [pl.BlockSpec((None, g, N_HEADS, HEAD_DIM, HEAD_DIM), lambda i: (l, i, 0, 0, 0))] * 4
    in_specs += [pl.BlockSpec((None, g, N_HEADS, HEAD_DIM), lambda i: (l, i, 0, 0)),
                 pl.BlockSpec((None, g, 1, N_HEADS), lambda i: (l, i, 0, 0))]
    st4 = pl.BlockSpec((g, N_HEADS, HEAD_DIM, HEAD_DIM), lambda i: (i, 0, 0, 0))
    out_specs = [pl.BlockSpec((rows, D_MODEL), row_map), pl.BlockSpec((rows, CONV_CH), row_map),
                 st4, st4, st4, st4,
                 pl.BlockSpec((g, N_HEADS, HEAD_DIM), lambda i: (i, 0, 0)),
                 pl.BlockSpec((g, 1, N_HEADS), lambda i: (i, 0, 0))]
    out_shape = [jax.ShapeDtypeStruct((m, D_MODEL), BF16), jax.ShapeDtypeStruct((m, CONV_CH), F32)]
    out_shape += [jax.ShapeDtypeStruct((nbatch, N_HEADS, HEAD_DIM, HEAD_DIM), F32) for _ in range(4)]
    out_shape += [jax.ShapeDtypeStruct((nbatch, N_HEADS, HEAD_DIM), F32),
                  jax.ShapeDtypeStruct((nbatch, 1, N_HEADS), F32)]
    scratch = [pltpu.VMEM((rows + SUBLANES, CONV_CH), F32), pltpu.VMEM((rows, LANES), F32)]
    scratch += [pltpu.VMEM((rows, GROUP_W), F32) for _ in range(7)]
    scratch += [pltpu.VMEM((rows, D_MODEL), F32)]
    body = functools.partial(_mix_body, sample=True, rows=rows, chunk=chunk, ret_carry=carry)
    return pl.pallas_call(
        body, out_shape=out_shape, grid=(nbatch // g,), in_specs=in_specs, out_specs=out_specs,
        scratch_shapes=scratch,
        compiler_params=pltpu.CompilerParams(dimension_semantics=("parallel",), vmem_limit_bytes=VMEM_LIMIT),
        name="mix_sample",
    )(proj, hist, cos, sin, rinter, rtail, rintra, tri, sel, ones_bd, *small, *states)


def _rot_tables(pos):
    half = HEAD_DIM // 2
    inv = ROPE_BASE ** (-jnp.arange(half, dtype=F32) / half)
    ang = pos.astype(F32)[:, None] * inv[None, :]
    cos, sin = jnp.cos(ang), jnp.sin(ang)
    cos_full = jnp.tile(jnp.concatenate([cos, cos], axis=-1), (1, N_HEADS))
    sin_signed = jnp.tile(jnp.concatenate([-sin, sin], axis=-1), (1, N_HEADS))
    return cos_full, sin_signed


def _small_params(ret_norm, ssm_conv_w, ssm_conv_b, ssm_dt_bias, ssm_A_log, ssm_D, ssm_norm, gdn_conv_w,
                  gdn_dt_bias, gdn_A_log, gdn_norm, mlstm_i_bias, mlstm_f_bias, mlstm_norm):
    depth = ret_norm.shape[0]
    z = lambda n: jnp.zeros((depth, n), F32)
    pad_lanes = z(LANES - 5 * N_HEADS)
    bias = jnp.concatenate([ssm_dt_bias, gdn_dt_bias, z(N_HEADS), mlstm_i_bias, mlstm_f_bias, pad_lanes], axis=1)
    alog = jnp.concatenate([ssm_A_log, gdn_A_log, z(3 * N_HEADS), pad_lanes], axis=1)
    gvec = jnp.stack([bias, alog], axis=1)
    convw = jnp.concatenate([ssm_conv_w, gdn_conv_w], axis=2)
    convb = jnp.concatenate([ssm_conv_b, z(CONV_CH - SSM_CONV_CH)], axis=1)[:, None, :]
    dvec = jnp.repeat(ssm_D, HEAD_DIM, axis=1)[:, None, :]
    norms = jnp.stack([ret_norm, ssm_norm, gdn_norm, mlstm_norm], axis=1)
    return gvec, convw, convb, dvec, norms


def kernel(x_prompt, x_sample, state_ret, state_ssm, state_ssm_conv, state_gdn, state_gdn_conv,
           state_mlstm_C, state_mlstm_n, state_mlstm_m, w_in, w_out, norm_sandwich, ffn_w_gate, ffn_w_up,
           ffn_w_down, ret_norm, ssm_conv_w, ssm_conv_b, ssm_dt_bias, ssm_A_log, ssm_D, ssm_norm, gdn_conv_w,
           gdn_dt_bias, gdn_A_log, gdn_norm, mlstm_i_bias, mlstm_f_bias, mlstm_norm):
    depth = w_in.shape[0]
    bp, lp, _ = x_prompt.shape
    bs, ls, _ = x_sample.shape
    pad = SAMPLE_PAD - ls
    w_in_p = _prep_win(w_in)
    w_out_b = w_out.astype(BF16)
    wg, wu, wd = ffn_w_gate.astype(BF16), ffn_w_up.astype(BF16), ffn_w_down.astype(BF16)
    small = _small_params(ret_norm, ssm_conv_w, ssm_conv_b, ssm_dt_bias, ssm_A_log, ssm_D, ssm_norm, gdn_conv_w,
                          gdn_dt_bias, gdn_A_log, gdn_norm, mlstm_i_bias, mlstm_f_bias, mlstm_norm)

    def trunk(x, mix):
        new_states = []
        for l in range(depth):
            x = _ffn(x, norm_sandwich, wg, wu, wd, l, 0)
            outs = mix(_inproj(x, norm_sandwich, w_in_p, l), l)
            x = _outproj(x, outs[0], norm_sandwich, w_out_b, l)
            x = _ffn(x, norm_sandwich, wg, wu, wd, l, 1)
            new_states.append(outs[1:])
        return x, new_states

    cos_p, sin_p = _rot_tables(jnp.arange(lp, dtype=jnp.int32))
    y_p, st_p = trunk(x_prompt.reshape(bp * lp, D_MODEL),
                      lambda proj, l: _mix_prompt(proj, cos_p, sin_p, small, l, nbatch=bp, seqlen=lp))

    cos_s, sin_s = _rot_tables(PAST_LEN + jnp.arange(ls, dtype=jnp.int32))
    tile_rows = lambda t: jnp.tile(jnp.pad(t, ((pad, 0), (0, 0))), (SAMPLE_GROUP, 1))
    cos_s, sin_s = tile_rows(cos_s), tile_rows(sin_s)
    xs = jnp.pad(x_sample, ((0, 0), (pad, 0), (0, 0))).reshape(bs * SAMPLE_PAD, D_MODEL)
    hist = jnp.concatenate([state_ssm_conv, state_gdn_conv], axis=-1)
    hist = jnp.pad(hist, ((0, 0), (0, 0), (pad - (CONV_W - 1), ls), (0, 0))).reshape(depth, bs * SAMPLE_PAD, CONV_CH)
    states_s = (state_ret, state_ssm, state_gdn, state_mlstm_C, state_mlstm_n, state_mlstm_m[:, :, None, :])
    y_s, st_s = trunk(xs, lambda proj, l: _mix_sample(proj, hist, cos_s, sin_s, small, states_s, l, nbatch=bs))

    def collect(sts, nb):
        conv = jnp.stack([s[0].reshape(nb, -1, CONV_CH)[:, -(CONV_W - 1):, :] for s in sts])
        stack = lambda k: jnp.stack([s[k] for s in sts])
        return (stack(1), stack(2), conv[..., :SSM_CONV_CH], stack(3), conv[..., SSM_CONV_CH:],
                stack(4), stack(5), stack(6)[:, :, 0, :])

    y_prompt = y_p.reshape(bp, lp, D_MODEL)
    y_sample = y_s.reshape(bs, SAMPLE_PAD, D_MODEL)[:, pad:, :]
    return (y_prompt, y_sample) + collect(st_p, bp) + collect(st_s, bs)
```

```python
import functools
import math

import numpy as np
import jax
import jax.numpy as jnp
from jax import lax
from jax.experimental import pallas as pl
from jax.experimental.pallas import tpu as pltpu

F32 = jnp.float32
BF16 = jnp.bfloat16

D_MODEL = 1024
N_HEADS = 4
HEAD_DIM = 64
GROUP_W = N_HEADS * HEAD_DIM
D_FF = 2816
CONV_W = 4
EPS = 1e-6
ROPE_BASE = 10000.0
PAST_LEN = 16384
PROMPT_CHUNK = 64

COL_RET = 0
COL_SSM_Z = 1024
COL_CONV = 1280
CONV_CH = 1280
SSM_CONV_CH = 512
COL_GDN_GATE = 2560
COL_ML = 2816
COL_GATES = 3840
IN_PAD = 3968
G_DT, G_GA, G_GB, G_MI, G_MF, G_DTV = 0, 4, 8, 12, 16, 20

SUBLANES = 8
LANES = 128
VMEM_LIMIT = 56 * 1024 * 1024

FF_CHUNK = 256
TOKEN_TILE = 512
PROMPT_TILE = 256
SAMPLE_PAD = 8
SAMPLE_GROUP = 8
NEG_BIG = -1e30


def _sigmoid(x):
    return 1.0 / (1.0 + jnp.exp(-x))


def _silu(x):
    return x * _sigmoid(x)


def _softplus(x):
    return jnp.maximum(x, 0.0) + jnp.log1p(jnp.exp(-jnp.abs(x)))


def _rms(x, g):
    return x * lax.rsqrt(jnp.mean(x * x, axis=-1, keepdims=True) + EPS) * g


def _dot(a, b):
    return jnp.dot(a.astype(BF16), b.astype(BF16), preferred_element_type=F32)


def _dot_nt(a, b):
    return lax.dot_general(a.astype(BF16), b.astype(BF16), (((1,), (1,)), ((), ())),
                           preferred_element_type=F32)


def _dot_tn(a, b):
    return lax.dot_general(a.astype(BF16), b.astype(BF16), (((0,), (0,)), ((), ())),
                           preferred_element_type=F32)


def _split3(x):
    hi = x.astype(BF16)
    r = x - hi.astype(F32)
    mid = r.astype(BF16)
    lo = (r - mid.astype(F32)).astype(BF16)
    return hi, mid, lo


def _dot_sel(sel, x):
    hi, mid, lo = _split3(x)
    f = lambda p: jnp.dot(sel, p, preferred_element_type=F32)
    return (f(lo) + f(mid)) + f(hi)


def _dot_sel_r(x, sel):
    hi, mid, lo = _split3(x)
    f = lambda p: jnp.dot(p, sel, preferred_element_type=F32)
    return (f(lo) + f(mid)) + f(hi)


def _dot_sel_nt(sel, x):
    hi, mid, lo = _split3(x)
    f = lambda p: lax.dot_general(sel, p, (((1,), (1,)), ((), ())), preferred_element_type=F32)
    return (f(lo) + f(mid)) + f(hi)


N_SANDWICH = 6


def _layer_block(tail, lead, ngrid=1, buffered=True):
    idx = tuple(lead) + (0,) * len(tail)
    imap = (lambda i: idx) if ngrid == 1 else (lambda i, j: idx)
    kw = dict(pipeline_mode=pl.Buffered(1)) if buffered else {}
    return pl.BlockSpec((None,) * len(lead) + tuple(tail), imap, **kw)


def _prep_win_body(w_ref, o_ref):
    o_ref[:, 0:1792] = w_ref[:, 0:1792].astype(BF16)
    o_ref[:, 1792:2816] = w_ref[:, 1796:2820].astype(BF16)
    o_ref[:, 2816:3840] = w_ref[:, 2828:3852].astype(BF16)
    rows = w_ref.shape[0]
    gates = jnp.concatenate([w_ref[:, 1792:1796], w_ref[:, 2820:2828], w_ref[:, 3852:3860],
                             jnp.zeros((rows, LANES - 5 * N_HEADS), F32)], axis=1)
    o_ref[:, COL_GATES:IN_PAD] = gates.astype(BF16)


def _prep_win(w_in):
    depth, _, in_dim = w_in.shape
    tr = 128
    return pl.pallas_call(
        _prep_win_body,
        out_shape=jax.ShapeDtypeStruct((depth, D_MODEL, IN_PAD), BF16),
        grid=(depth, D_MODEL // tr),
        in_specs=[pl.BlockSpec((None, tr, in_dim), lambda l, r: (l, r, 0))],
        out_specs=pl.BlockSpec((None, tr, IN_PAD), lambda l, r: (l, r, 0)),
        compiler_params=pltpu.CompilerParams(dimension_semantics=("parallel", "parallel")),
        name="prep_win",
    )(w_in)


def _ffn_body(x_ref, g_ref, wg_ref, wu_ref, wd_ref, o_ref, a_ref, *, k):
    x = x_ref[...]
    h = _rms(x, g_ref[4 * k:4 * k + 1, :]).astype(BF16)
    for j in range(D_FF // FF_CHUNK):
        sl = slice(j * FF_CHUNK, (j + 1) * FF_CHUNK)
        gt = jnp.dot(h, wg_ref[:, sl], preferred_element_type=F32)
        up = jnp.dot(h, wu_ref[:, sl], preferred_element_type=F32)
        a_ref[:, sl] = (_silu(gt) * up).astype(BF16)
    y = jnp.dot(a_ref[...], wd_ref[...], preferred_element_type=F32)
    o_ref[...] = x + 0.5 * _rms(y, g_ref[4 * k + 1:4 * k + 2, :])


def _ffn(x, sandwich, wg, wu, wd, l, k):
    m = x.shape[0]
    tm = min(TOKEN_TILE, m)
    return pl.pallas_call(
        functools.partial(_ffn_body, k=k),
        out_shape=jax.ShapeDtypeStruct((m, D_MODEL), F32),
        grid=(m // tm,),
        in_specs=[pl.BlockSpec((tm, D_MODEL), lambda i: (i, 0)),
                  _layer_block((N_SANDWICH, D_MODEL), (l,)),
                  _layer_block((D_MODEL, D_FF), (l, k)),
                  _layer_block((D_MODEL, D_FF), (l, k)),
                  _layer_block((D_FF, D_MODEL), (l, k))],
        out_specs=pl.BlockSpec((tm, D_MODEL), lambda i: (i, 0)),
        scratch_shapes=[pltpu.VMEM((tm, D_FF), BF16)],
        compiler_params=pltpu.CompilerParams(dimension_semantics=("parallel",),
                                             vmem_limit_bytes=VMEM_LIMIT),
        name="ffn",
    )(x, sandwich, wg, wu, wd)


def _inproj_body(x_ref, g_ref, w_ref, o_ref):
    h = _rms(x_ref[...], g_ref[2:3, :]).astype(BF16)
    o_ref[...] = jnp.dot(h, w_ref[...], preferred_element_type=F32)


def _inproj(x, sandwich, w, l):
    m = x.shape[0]
    tm = min(TOKEN_TILE, m)
    return pl.pallas_call(
        _inproj_body,
        out_shape=jax.ShapeDtypeStruct((m, IN_PAD), F32),
        grid=(m // tm,),
        in_specs=[pl.BlockSpec((tm, D_MODEL), lambda i: (i, 0)),
                  _layer_block((N_SANDWICH, D_MODEL), (l,)),
                  _layer_block((D_MODEL, IN_PAD), (l,))],
        out_specs=pl.BlockSpec((tm, IN_PAD), lambda i: (i, 0)),
        compiler_params=pltpu.CompilerParams(dimension_semantics=("parallel",),
                                             vmem_limit_bytes=VMEM_LIMIT),
        name="inproj",
    )(x, sandwich, w)


def _outproj_body(x_ref, m_ref, g_ref, w_ref, o_ref):
    y = jnp.dot(m_ref[...], w_ref[...], preferred_element_type=F32)
    o_ref[...] = x_ref[...] + _rms(y, g_ref[3:4, :])


def _outproj(x, mixed, sandwich, w, l):
    m = x.shape[0]
    tm = min(TOKEN_TILE, m)
    return pl.pallas_call(
        _outproj_body,
        out_shape=jax.ShapeDtypeStruct((m, D_MODEL), F32),
        grid=(m // tm,),
        in_specs=[pl.BlockSpec((tm, D_MODEL), lambda i: (i, 0)),
                  pl.BlockSpec((tm, D_MODEL), lambda i: (i, 0)),
                  _layer_block((N_SANDWICH, D_MODEL), (l,)),
                  _layer_block((D_MODEL, D_MODEL), (l,))],
        out_specs=pl.BlockSpec((tm, D_MODEL), lambda i: (i, 0)),
        compiler_params=pltpu.CompilerParams(dimension_semantics=("parallel",),
                                             vmem_limit_bytes=VMEM_LIMIT),
        name="outproj",
    )(x, mixed, sandwich, w)


def _head(x, h):
    return x[:, h * HEAD_DIM:(h + 1) * HEAD_DIM]


def _neumann_inverse(amat, c):
    row = lax.broadcasted_iota(jnp.int32, (c, c), 0)
    col = lax.broadcasted_iota(jnp.int32, (c, c), 1)
    eye = (row == col).astype(F32)
    m = -amat
    p = eye + m
    for _ in range(max(int(math.ceil(math.log2(c))) - 1, 0)):
        m = _dot(m, m)
        p = p + _dot(p, m)
    return p


def _mix_body(*refs, sample, rows, chunk, ret_carry):
    nchunks = rows // chunk
    it = iter(refs)
    proj_ref = next(it)
    hist_ref = next(it) if sample else None
    cos_ref, sin_ref = next(it), next(it)
    rinter_ref, rtail_ref, rintra_ref = next(it), next(it), next(it)
    tri_ref, sel_ref, ones_ref = next(it), next(it), next(it)
    gvec_ref, convw_ref, convb_ref, dvec_ref, norms_ref = next(it), next(it), next(it), next(it), next(it)
    if sample:
        sret_in, sssm_in, sgdn_in, smc_in, smn_in, smm_in = (next(it) for _ in range(6))
    mixed_ref, convraw_ref = next(it), next(it)
    sret_ref, sssm_ref, sgdn_ref, smc_ref, smn_ref, smm_ref = (next(it) for _ in range(6))
    e_ref, g_ref, rq_ref, rk_ref, sx_ref, sbc_ref, gq_ref, gk_ref, gv_ref, o_ref = (next(it) for _ in range(10))
    if not sample:
        sret_in, sssm_in, sgdn_in, smc_in, smn_in, smm_in = (
            sret_ref, sssm_ref, sgdn_ref, smc_ref, smn_ref, smm_ref)

    if sample:
        rowid = lax.broadcasted_iota(jnp.int32, (rows, 1), 0)
        valid = (rowid % SAMPLE_PAD) >= (SAMPLE_PAD - 4)
        e_ref[0:SUBLANES, :] = jnp.zeros((SUBLANES, CONV_CH), F32)
    else:
        valid = None

        @pl.when(pl.program_id(1) == 0)
        def _():
            e_ref[0:SUBLANES, :] = jnp.zeros((SUBLANES, CONV_CH), F32)
            sret_ref[...] = jnp.zeros_like(sret_ref)
            sssm_ref[...] = jnp.zeros_like(sssm_ref)
            sgdn_ref[...] = jnp.zeros_like(sgdn_ref)
            smc_ref[...] = jnp.zeros_like(smc_ref)
            smn_ref[...] = jnp.zeros_like(smn_ref)
            smm_ref[...] = jnp.zeros_like(smm_ref)

    def keep(x, fill=0.0):
        return x if valid is None else jnp.where(valid, x, fill)

    lane = lax.broadcasted_iota(jnp.int32, (1, LANES), 1)
    in_lanes = lambda lo: (lane >= lo) & (lane < lo + N_HEADS)
    graw = proj_ref[:, COL_GATES:COL_GATES + LANES]
    xb = graw + gvec_ref[0:1, :]
    sp = _softplus(xb)
    neg_a = -jnp.exp(gvec_ref[1:2, :])
    logf = -_softplus(-xb)
    pre = jnp.where(lane < 2 * N_HEADS, neg_a * sp, jnp.where(in_lanes(G_MF), logf, 0.0))
    cum = _dot_sel(tri_ref[...], keep(pre))
    dtv = pltpu.roll(sp, G_DTV - G_DT, axis=1)
    aux = jnp.where(in_lanes(G_GB), _sigmoid(graw),
                    jnp.where(in_lanes(G_MI), xb, jnp.where(in_lanes(G_DTV), dtv, 0.0)))
    if sample:
        aux = jnp.where(valid, aux, jnp.where(in_lanes(G_MI), NEG_BIG, 0.0))
    g_ref[...] = jnp.where(in_lanes(G_GB) | in_lanes(G_MI) | in_lanes(G_DTV), aux, cum)

    lane_g = lax.broadcasted_iota(jnp.int32, (1, GROUP_W), 1)
    first_half = (lane_g % HEAD_DIM) < (HEAD_DIM // 2)

    def rotary(x):
        swapped = jnp.where(first_half, pltpu.roll(x, GROUP_W - HEAD_DIM // 2, axis=1),
                            pltpu.roll(x, HEAD_DIM // 2, axis=1))
        return x * cos_ref[...] + swapped * sin_ref[...]

    rq_ref[...] = rotary(proj_ref[:, COL_RET:COL_RET + GROUP_W])
    rk_ref[...] = keep(rotary(proj_ref[:, COL_RET + GROUP_W:COL_RET + 2 * GROUP_W]) * HEAD_DIM ** -0.5)

    raw = proj_ref[:, COL_CONV:COL_CONV + CONV_CH]
    if sample:
        raw = raw + hist_ref[...]
    e_ref[SUBLANES:SUBLANES + rows, :] = raw
    conv = raw * convw_ref[CONV_W - 1:CONV_W, :]
    for s in range(1, CONV_W):
        conv = conv + e_ref[SUBLANES - s:SUBLANES - s + rows, :] * convw_ref[CONV_W - 1 - s:CONV_W - s, :]
    conv = keep(_silu(conv + convb_ref[...]))
    if sample:
        convraw_ref[...] = raw
    else:
        convraw_ref[...] = e_ref[rows:rows + SUBLANES, :]
        e_ref[0:SUBLANES, :] = e_ref[rows:rows + SUBLANES, :]
    sx_ref[...] = conv[:, 0:GROUP_W]
    sbc_ref[...] = conv[:, GROUP_W:SSM_CONV_CH]
    ones_bd = ones_ref[...]

    def head_sumsq(x):
        return _dot_sel_r(x * x, ones_bd)

    gq = conv[:, SSM_CONV_CH:SSM_CONV_CH + GROUP_W]
    gk = conv[:, SSM_CONV_CH + GROUP_W:SSM_CONV_CH + 2 * GROUP_W]
    gq_ref[...] = gq * lax.rsqrt(head_sumsq(gq) + EPS) * HEAD_DIM ** -0.5
    gk_ref[...] = gk * lax.rsqrt(head_sumsq(gk) + EPS)
    gv_ref[...] = conv[:, SSM_CONV_CH + 2 * GROUP_W:CONV_CH]

    r_i = lax.broadcasted_iota(jnp.int32, (chunk, chunk), 0)
    c_i = lax.broadcasted_iota(jnp.int32, (chunk, chunk), 1)
    tri = r_i >= c_i
    strict = r_i > c_i
    lane_h = lax.broadcasted_iota(jnp.int32, (1, N_HEADS), 1)

    def chunk_step(i, carry):
        r0 = pl.multiple_of(i * chunk, chunk)
        rs = pl.ds(r0, chunk)
        sidx = i if sample else 0
        gc = g_ref[rs, :]
        gt = _dot_sel_nt(sel_ref[...], gc)
        col = lambda j: gc[:, j:j + 1]
        rowv = lambda j: gt[j:j + 1, :]
        last = lambda j: gc[chunk - 1:chunk, j:j + 1]

        heads = range(N_HEADS)
        s_ret = [sret_in[sidx, h] for h in heads]
        s_ssm = [sssm_in[sidx, h] for h in heads]
        s_gdn = [sgdn_in[sidx, h] for h in heads]
        s_mc = [smc_in[sidx, h] for h in heads]
        n_all = smn_in[sidx]
        m_row = smm_in[sidx]
        rq_all, rk_all = rq_ref[rs, :], rk_ref[rs, :]
        rv_all = proj_ref[rs, COL_RET + 2 * GROUP_W:COL_RET + 3 * GROUP_W]
        x_all, bc_all = sx_ref[rs, :], sbc_ref[rs, :]
        gq_all, gk_all, gv_all = gq_ref[rs, :], gk_ref[rs, :], gv_ref[rs, :]
        mq_all = proj_ref[rs, COL_ML:COL_ML + GROUP_W]
        mk_all = proj_ref[rs, COL_ML + GROUP_W:COL_ML + 2 * GROUP_W] * HEAD_DIM ** -0.5
        mv_all = proj_ref[rs, COL_ML + 2 * GROUP_W:COL_ML + 3 * GROUP_W]

        qi_all = rq_all * rinter_ref[...]
        kt_all = rk_all * rtail_ref[...]
        o_ret, n_ret = [], []
        for h in heads:
            att = _dot_nt(_head(rq_all, h), _head(rk_all, h)) * rintra_ref[h]
            o_ret.append(_dot(att, _head(rv_all, h)) + _dot(_head(qi_all, h), s_ret[h]))
            n_ret.append(s_ret[h] * ret_carry[h] + _dot_tn(_head(kt_all, h), _head(rv_all, h)))

        o_ssm, n_ssm = [], []
        for h in heads:
            grp = h // (N_HEADS // 2)
            b_g = bc_all[:, grp * HEAD_DIM:(grp + 1) * HEAD_DIM]
            c_g = bc_all[:, 2 * HEAD_DIM + grp * HEAD_DIM:2 * HEAD_DIM + (grp + 1) * HEAD_DIM]
            x_h = _head(x_all, h)
            cum_c, cum_r, cum_l = col(G_DT + h), rowv(G_DT + h), last(G_DT + h)
            lmat = jnp.exp(jnp.where(tri, cum_c - cum_r, -jnp.inf))
            scores = _dot_nt(c_g, b_g) * lmat * rowv(G_DTV + h)
            o_ssm.append(_dot(scores, x_h) + _dot_nt(c_g, s_ssm[h]) * jnp.exp(cum_c))
            w_c = jnp.exp(cum_l - cum_c) * col(G_DTV + h)
            n_ssm.append(s_ssm[h] * jnp.exp(cum_l) + _dot_tn(x_h * w_c, b_g))

        o_gdn, n_gdn = [], []
        for h in heads:
            q_h, k_h, v_h = _head(gq_all, h), _head(gk_all, h), _head(gv_all, h)
            cum_c, cum_r, cum_l = col(G_GA + h), rowv(G_GA + h), last(G_GA + h)
            beta_c = col(G_GB + h)
            gam = jnp.exp(jnp.where(tri, cum_c - cum_r, -jnp.inf))
            amat = _dot_nt(k_h, k_h) * jnp.where(strict, gam, 0.0) * beta_c
            ecum = jnp.exp(cum_c)
            tinv = _neumann_inverse(amat, chunk)
            u = _dot(tinv, v_h * beta_c)
            w = _dot(tinv, k_h * (beta_c * ecum))
            v_new = u - _dot(w, s_gdn[h])
            qk = _dot_nt(q_h, k_h) * gam
            o_gdn.append(_dot(q_h * ecum, s_gdn[h]) + _dot(qk, v_new))
            n_gdn.append(s_gdn[h] * jnp.exp(cum_l) + _dot_tn(k_h * jnp.exp(cum_l - cum_c), v_new))

        m_new_row = jnp.zeros((1, N_HEADS), F32)
        o_ml, n_mc, n_mn = [], [], []
        for h in heads:
            q_h, k_h, v_h = _head(mq_all, h), _head(mk_all, h), _head(mv_all, h)
            b_c, b_r, b_l = col(G_MF + h), rowv(G_MF + h), last(G_MF + h)
            i_c, i_r = col(G_MI + h), rowv(G_MI + h)
            m_old = m_row[:, h:h + 1]
            n_h = n_all[h:h + 1, :]
            dmat = jnp.where(tri, b_c - b_r + i_r, -jnp.inf)
            m_t = jnp.maximum(b_c + m_old, jnp.max(dmat, axis=-1, keepdims=True))
            wts = jnp.exp(dmat - m_t)
            inter = jnp.exp(b_c + m_old - m_t)
            qk = _dot_nt(q_h, k_h) * wts
            num = _dot(qk, v_h) + inter * _dot(q_h, s_mc[h])
            den = jnp.sum(qk, axis=-1, keepdims=True) + inter * jnp.sum(q_h * n_h, axis=-1, keepdims=True)
            o_ml.append(num / jnp.maximum(jnp.abs(den), jnp.exp(-m_t)))
            m_new = m_t[chunk - 1:chunk, :]
            ws = jnp.exp(b_l - b_c + i_c - m_new)
            cs = jnp.exp(b_l + m_old - m_new)
            kw = k_h * ws
            n_mc.append(s_mc[h] * cs + _dot_tn(kw, v_h))
            n_mn.append(n_h * cs + jnp.sum(kw, axis=0, keepdims=True))
            m_new_row = jnp.where(lane_h == h, m_new, m_new_row)

        o_ref[rs, :] = jnp.concatenate(o_ret + o_ssm + o_gdn + o_ml, axis=1)
        for h in heads:
            sret_ref[sidx, h] = n_ret[h]
            sssm_ref[sidx, h] = n_ssm[h]
            sgdn_ref[sidx, h] = n_gdn[h]
            smc_ref[sidx, h] = n_mc[h]
        smn_ref[sidx] = jnp.concatenate(n_mn, axis=0)
        smm_ref[sidx] = m_new_row
        return carry

    lax.fori_loop(0, nchunks, chunk_step, 0)

    def head_norm(x, g):
        return x * lax.rsqrt(head_sumsq(x) * (1.0 / HEAD_DIM) + EPS) * g

    out_ret = head_norm(o_ref[:, 0:GROUP_W], norms_ref[0:1, :]) * _silu(
        proj_ref[:, COL_RET + 3 * GROUP_W:COL_RET + 4 * GROUP_W])
    mixed_ref[:, 0:GROUP_W] = keep(out_ret).astype(BF16)
    y = o_ref[:, GROUP_W:2 * GROUP_W] + dvec_ref[...] * sx_ref[...]
    out_ssm = _rms(y * _silu(proj_ref[:, COL_SSM_Z:COL_SSM_Z + GROUP_W]), norms_ref[1:2, :])
    mixed_ref[:, GROUP_W:2 * GROUP_W] = keep(out_ssm).astype(BF16)
    out_gdn = head_norm(o_ref[:, 2 * GROUP_W:3 * GROUP_W], norms_ref[2:3, :]) * _silu(
        proj_ref[:, COL_GDN_GATE:COL_GDN_GATE + GROUP_W])
    mixed_ref[:, 2 * GROUP_W:3 * GROUP_W] = keep(out_gdn).astype(BF16)
    out_ml = _sigmoid(proj_ref[:, COL_ML + 3 * GROUP_W:COL_ML + 4 * GROUP_W]) * head_norm(
        o_ref[:, 3 * GROUP_W:4 * GROUP_W], norms_ref[3:4, :])
    mixed_ref[:, 3 * GROUP_W:4 * GROUP_W] = keep(out_ml).astype(BF16)


NEXP = 6
X_CSSM, X_DT, X_CGDN, X_BETA, X_B, X_IMB = range(NEXP)


def _split2(x):
    hi = x.astype(BF16)
    return hi, (x - hi.astype(F32)).astype(BF16)


def _dot_sel_r2(x, sel):
    hi, lo = _split2(x)
    return jnp.dot(lo, sel, preferred_element_type=F32) + jnp.dot(hi, sel, preferred_element_type=F32)


def _mixp_body(proj_ref, cos_ref, sin_ref, rinter_ref, rtail_ref, rintra_ref, tri_ref, expand_ref,
               maskbd_ref, maskg_ref, maskht_ref, gvec_ref, convw_ref, convb_ref, dvec_ref, norms_ref,
               mixed_ref, convraw_ref, oret_ref, ossm_ref, ogdn_ref, omc_ref, omn_ref, omm_ref,
               e_ref, x_ref, rq_ref, rk_ref, sx_ref, sbc_ref, gq_ref, gk_ref, gv_ref, o_ref,
               sret_ref, sht_ref, sgdn_ref, smc_ref, vec_ref, *, rows, chunk, ret_carry):
    nchunks = rows // chunk
    t_id = pl.program_id(1)

    @pl.when(t_id == 0)
    def _():
        e_ref[0:SUBLANES, :] = jnp.zeros((SUBLANES, CONV_CH), F32)
        sret_ref[...] = jnp.zeros_like(sret_ref)
        sht_ref[...] = jnp.zeros_like(sht_ref)
        sgdn_ref[...] = jnp.zeros_like(sgdn_ref)
        smc_ref[...] = jnp.zeros_like(smc_ref)
        vec_ref[...] = jnp.zeros_like(vec_ref)

    maskbd = maskbd_ref[...]
    maskbd_b = maskbd.astype(BF16)
    maskg = maskg_ref[...]
    maskg_b = maskg.astype(BF16)

    def seg_sum(x):
        return _dot_sel_r2(x, maskbd_b)

    lane = lax.broadcasted_iota(jnp.int32, (1, LANES), 1)
    in_lanes = lambda lo: (lane >= lo) & (lane < lo + N_HEADS)
    graw = proj_ref[:, COL_GATES:COL_GATES + LANES]
    xb = graw + gvec_ref[0:1, :]
    sp = _softplus(xb)
    neg_a = -jnp.exp(gvec_ref[1:2, :])
    logf = -_softplus(-xb)
    pre = jnp.where(lane < 2 * N_HEADS, neg_a * sp, jnp.where(in_lanes(G_MF), logf, 0.0))
    cum = _dot_sel(tri_ref[...], pre)
    imb = xb - pltpu.roll(cum, LANES - (G_MF - G_MI), axis=1)
    dtv = pltpu.roll(sp, G_DTV - G_DT, axis=1)
    gates = jnp.where(in_lanes(G_GB), _sigmoid(graw),
                      jnp.where(in_lanes(G_MI), imb, jnp.where(in_lanes(G_DTV), dtv, cum)))
    for j, lo in enumerate((G_DT, G_DTV, G_GA, G_GB, G_MF, G_MI)):
        x_ref[j] = _dot_sel_r(gates, expand_ref[:, j * GROUP_W:(j + 1) * GROUP_W])

    lane_g = lax.broadcasted_iota(jnp.int32, (1, GROUP_W), 1)
    first_half = (lane_g % HEAD_DIM) < (HEAD_DIM // 2)

    def rotary(x):
        swapped = jnp.where(first_half, pltpu.roll(x, GROUP_W - HEAD_DIM // 2, axis=1),
                            pltpu.roll(x, HEAD_DIM // 2, axis=1))
        return x * cos_ref[...] + swapped * sin_ref[...]

    rq_ref[...] = rotary(proj_ref[:, COL_RET:COL_RET + GROUP_W])
    rk_ref[...] = rotary(proj_ref[:, COL_RET + GROUP_W:COL_RET + 2 * GROUP_W]) * HEAD_DIM ** -0.5

    raw = proj_ref[:, COL_CONV:COL_CONV + CONV_CH]
    e_ref[SUBLANES:SUBLANES + rows, :] = raw
    conv = raw * convw_ref[CONV_W - 1:CONV_W, :]
    for s in range(1, CONV_W):
        conv = conv + e_ref[SUBLANES - s:SUBLANES - s + rows, :] * convw_ref[CONV_W - 1 - s:CONV_W - s, :]
    conv = _silu(conv + convb_ref[...])
    convraw_ref[...] = e_ref[rows:rows + SUBLANES, :]
    e_ref[0:SUBLANES, :] = e_ref[rows:rows + SUBLANES, :]
    sx_ref[...] = conv[:, 0:GROUP_W]
    sbc_ref[...] = conv[:, GROUP_W:SSM_CONV_CH]
    gq = conv[:, SSM_CONV_CH:SSM_CONV_CH + GROUP_W]
    gk = conv[:, SSM_CONV_CH + GROUP_W:SSM_CONV_CH + 2 * GROUP_W]
    gq_ref[...] = gq * lax.rsqrt(seg_sum(gq * gq) + EPS) * HEAD_DIM ** -0.5
    gk_ref[...] = gk * lax.rsqrt(seg_sum(gk * gk) + EPS)
    gv_ref[...] = conv[:, SSM_CONV_CH + 2 * GROUP_W:CONV_CH]

    r_i = lax.broadcasted_iota(jnp.int32, (chunk, GROUP_W), 0)
    c_i = lax.broadcasted_iota(jnp.int32, (chunk, GROUP_W), 1) % HEAD_DIM
    trim = r_i >= c_i
    diag = r_i == c_i
    strict = r_i > c_i
    eye = diag.astype(F32)
    head_of_lane = lax.broadcasted_iota(jnp.int32, (1, GROUP_W), 1) // HEAD_DIM
    carry_row = jnp.zeros((1, GROUP_W), F32)
    for h in range(N_HEADS):
        carry_row = jnp.where(head_of_lane == h, ret_carry[h], carry_row)
    maskht = maskht_ref[...]

    def bd(x):
        xb16 = x.astype(BF16)
        return jnp.concatenate([xb16] * N_HEADS, axis=0) * maskbd_b

    def last_row(x):
        return x[chunk - 1:chunk, :]

    def row_form(x):
        return jnp.sum(jnp.where(diag, x, 0.0), axis=0, keepdims=True)

    def seg_max(x):
        out = None
        for h in range(N_HEADS):
            sel = head_of_lane == h
            m_h = jnp.max(jnp.where(sel, x, -jnp.inf), axis=-1, keepdims=True)
            out = jnp.where(sel, m_h, 0.0) if out is None else jnp.where(sel, m_h, out)
        return out

    chunks = [slice(c * chunk, (c + 1) * chunk) for c in range(nchunks)]

    g_gam, g_beta, g_ecum, g_k, g_n, g_p = [], [], [], [], [], []
    for rs in chunks:
        ce = x_ref[X_CGDN, rs, :]
        gam = jnp.exp(jnp.where(trim, ce - row_form(ce), -jnp.inf))
        k_c = gk_ref[rs, :]
        beta = x_ref[X_BETA, rs, :]
        n0 = -(_dot_nt(k_c, bd(k_c)) * jnp.where(strict, gam, 0.0) * beta)
        g_gam.append(gam), g_beta.append(beta), g_ecum.append(jnp.exp(ce)), g_k.append(k_c)
        g_n.append(n0), g_p.append(eye + n0)
    for _ in range(max(int(math.ceil(math.log2(chunk))) - 1, 0)):
        for c in range(nchunks):
            g_n[c] = _dot(g_n[c], bd(g_n[c]))
        for c in range(nchunks):
            g_p[c] = g_p[c] + _dot(g_p[c], bd(g_n[c]))
    g_u, g_w = [], []
    for c, rs in enumerate(chunks):
        g_u.append(_dot(g_p[c], bd(gv_ref[rs, :] * g_beta[c])))
        g_w.append(_dot(g_p[c], bd(g_k[c] * (g_beta[c] * g_ecum[c]))))

    s_ret, s_ht, s_gdn, s_mc = sret_ref[...], sht_ref[...], sgdn_ref[...], smc_ref[...]
    n_row, m_row = vec_ref[0:1, :], vec_ref[1:2, :]
    for c, rs in enumerate(chunks):
        q_c, k_c = rq_ref[rs, :], rk_ref[rs, :]
        v_c = proj_ref[rs, COL_RET + 2 * GROUP_W:COL_RET + 3 * GROUP_W]
        att = _dot_nt(q_c, bd(k_c)) * rintra_ref[...]
        o_ret = _dot(att, bd(v_c)) + _dot(q_c * rinter_ref[...], s_ret)
        s_ret = s_ret * carry_row + _dot_tn(k_c * rtail_ref[...], v_c) * maskbd

        ce = x_ref[X_CSSM, rs, :]
        ce_last = last_row(ce)
        lmat = jnp.exp(jnp.where(trim, ce - row_form(ce), -jnp.inf))
        b_c, c_c = sbc_ref[rs, 0:2 * HEAD_DIM], sbc_ref[rs, 2 * HEAD_DIM:4 * HEAD_DIM]
        xdt = sx_ref[rs, :] * x_ref[X_DT, rs, :]
        rhs_b = jnp.concatenate([b_c.astype(BF16)] * N_HEADS, axis=0) * maskg_b
        scores = _dot_nt(c_c, rhs_b) * lmat
        o_ssm = _dot(scores, bd(xdt)) + _dot(c_c, s_ht) * jnp.exp(ce)
        s_ht = s_ht * jnp.exp(ce_last) + _dot_tn(b_c, xdt * jnp.exp(ce_last - ce)) * maskht

        ce = x_ref[X_CGDN, rs, :]
        ce_last = last_row(ce)
        q_c = gq_ref[rs, :]
        v_new = g_u[c] - _dot(g_w[c], s_gdn)
        qk = _dot_nt(q_c, bd(g_k[c])) * g_gam[c]
        o_gdn = _dot(q_c * g_ecum[c], s_gdn) + _dot(qk, bd(v_new))
        s_gdn = s_gdn * jnp.exp(ce_last) + _dot_tn(g_k[c] * jnp.exp(ce_last - ce), v_new) * maskbd

        q_c = proj_ref[rs, COL_ML:COL_ML + GROUP_W]
        k_c = proj_ref[rs, COL_ML + GROUP_W:COL_ML + 2 * GROUP_W] * HEAD_DIM ** -0.5
        v_c = proj_ref[rs, COL_ML + 2 * GROUP_W:COL_ML + 3 * GROUP_W]
        b_e, imb_e = x_ref[X_B, rs, :], x_ref[X_IMB, rs, :]
        dmat = jnp.where(trim, b_e + row_form(imb_e), -jnp.inf)
        m_t = jnp.maximum(b_e + m_row, seg_max(dmat))
        wts = jnp.exp(dmat - m_t)
        inter = jnp.exp(b_e + m_row - m_t)
        qk = _dot_nt(q_c, bd(k_c)) * wts
        num = _dot(qk, bd(v_c)) + inter * _dot(q_c, s_mc)
        den = seg_sum(qk) + inter * seg_sum(q_c * n_row)
        o_ml = num / jnp.maximum(jnp.abs(den), jnp.exp(-m_t))
        m_new = last_row(m_t)
        b_last = last_row(b_e)
        kw = k_c * jnp.exp(b_last + imb_e - m_new)
        cs = jnp.exp(b_last + m_row - m_new)
        s_mc = s_mc * cs + _dot_tn(kw, v_c) * maskbd
        n_row = n_row * cs + jnp.sum(kw, axis=0, keepdims=True)
        m_row = m_new

        o_ref[rs, :] = jnp.concatenate([o_ret, o_ssm, o_gdn, o_ml], axis=1)

    sret_ref[...] = s_ret
    sht_ref[...] = s_ht
    sgdn_ref[...] = s_gdn
    smc_ref[...] = s_mc
    vec_ref[0:1, :] = n_row
    vec_ref[1:2, :] = m_row

    def head_norm(x, g):
        return x * lax.rsqrt(seg_sum(x * x) * (1.0 / HEAD_DIM) + EPS) * g

    out_ret = head_norm(o_ref[:, 0:GROUP_W], norms_ref[0:1, :]) * _silu(
        proj_ref[:, COL_RET + 3 * GROUP_W:COL_RET + 4 * GROUP_W])
    mixed_ref[:, 0:GROUP_W] = out_ret.astype(BF16)
    y = o_ref[:, GROUP_W:2 * GROUP_W] + dvec_ref[...] * sx_ref[...]
    out_ssm = _rms(y * _silu(proj_ref[:, COL_SSM_Z:COL_SSM_Z + GROUP_W]), norms_ref[1:2, :])
    mixed_ref[:, GROUP_W:2 * GROUP_W] = out_ssm.astype(BF16)
    out_gdn = head_norm(o_ref[:, 2 * GROUP_W:3 * GROUP_W], norms_ref[2:3, :]) * _silu(
        proj_ref[:, COL_GDN_GATE:COL_GDN_GATE + GROUP_W])
    mixed_ref[:, 2 * GROUP_W:3 * GROUP_W] = out_gdn.astype(BF16)
    out_ml = _sigmoid(proj_ref[:, COL_ML + 3 * GROUP_W:COL_ML + 4 * GROUP_W]) * head_norm(
        o_ref[:, 3 * GROUP_W:4 * GROUP_W], norms_ref[3:4, :])
    mixed_ref[:, 3 * GROUP_W:4 * GROUP_W] = out_ml.astype(BF16)

    @pl.when(t_id == pl.num_programs(1) - 1)
    def _():
        for h in range(N_HEADS):
            hs = slice(h * HEAD_DIM, (h + 1) * HEAD_DIM)
            grp = h // (N_HEADS // 2)
            oret_ref[0, h] = s_ret[hs, hs]
            ogdn_ref[0, h] = s_gdn[hs, hs]
            omc_ref[0, h] = s_mc[hs, hs]
            ossm_ref[0, h] = s_ht[grp * HEAD_DIM:(grp + 1) * HEAD_DIM, hs]
        omn_ref[0] = jnp.concatenate([n_row[:, h * HEAD_DIM:(h + 1) * HEAD_DIM] for h in range(N_HEADS)], axis=0)
        omm_ref[0] = jnp.concatenate([m_row[:, h * HEAD_DIM:h * HEAD_DIM + 1] for h in range(N_HEADS)], axis=1)


def _prompt_consts(rows, chunk):
    lg = np.log(1.0 - np.exp2(-5.0 - np.arange(N_HEADS, dtype=np.float64)))
    idx = np.arange(chunk, dtype=np.float64)
    rel = idx[:, None] - idx[None, :]
    intra = np.where(rel[None] >= 0, np.exp(np.maximum(rel[None], 0.0) * lg[:, None, None]), 0.0)
    intra = np.concatenate(list(intra), axis=1)
    lanes = lambda t: np.repeat(t.T, HEAD_DIM, axis=1)
    inter = lanes(np.exp((idx[None, :] + 1.0) * lg[:, None]))
    tail = lanes(np.exp((chunk - 1.0 - idx[None, :]) * lg[:, None]))
    carry = tuple(float(c) for c in np.exp(chunk * lg))
    r = np.arange(rows)
    tri = (r[:, None] >= r[None, :]) & (r[:, None] // chunk == r[None, :] // chunk)
    expand = np.zeros((LANES, NEXP * GROUP_W), np.float32)
    for j, lo in enumerate((G_DT, G_DTV, G_GA, G_GB, G_MF, G_MI)):
        for h in range(N_HEADS):
            expand[lo + h, j * GROUP_W + h * HEAD_DIM:j * GROUP_W + (h + 1) * HEAD_DIM] = 1.0
    g = np.arange(GROUP_W)
    maskbd = (g[:, None] // HEAD_DIM == g[None, :] // HEAD_DIM)
    n = np.arange(2 * HEAD_DIM)
    maskg = (g[:, None] // (2 * HEAD_DIM) == n[None, :] // HEAD_DIM)
    f = lambda a, dt=F32: jnp.asarray(np.asarray(a, np.float32), dt)
    return (f(inter), f(tail), f(intra), f(tri, BF16), f(expand, BF16), f(maskbd), f(maskg), f(maskg.T)), carry


def _mix_prompt(proj, cos, sin, small, l, *, nbatch, seqlen):
    rows, chunk = min(PROMPT_TILE, seqlen), PROMPT_CHUNK
    ntile = seqlen // rows
    consts, carry = _prompt_consts(rows, chunk)
    m = proj.shape[0]
    row_map = lambda b, t: (b * ntile + t, 0)
    c2 = lambda b, t: (0, 0)
    full = lambda a: pl.BlockSpec(a.shape, c2)
    in_specs = [pl.BlockSpec((rows, IN_PAD), row_map),
                pl.BlockSpec((rows, GROUP_W), lambda b, t: (t, 0)), pl.BlockSpec((rows, GROUP_W), lambda b, t: (t, 0))]
    in_specs += [full(a) for a in consts]
    in_specs += [_layer_block(a.shape[1:], (l,), ngrid=2, buffered=False) for a in small]
    st4 = pl.BlockSpec((1, N_HEADS, HEAD_DIM, HEAD_DIM), lambda b, t: (b, 0, 0, 0))
    st3 = lambda k: pl.BlockSpec((1, k, (N_HEADS * HEAD_DIM) // k if k == N_HEADS else N_HEADS), lambda b, t: (b, 0, 0))
    out_shape = [jax.ShapeDtypeStruct((m, D_MODEL), BF16),
                 jax.ShapeDtypeStruct((nbatch * SUBLANES, CONV_CH), F32)]
    out_shape += [jax.ShapeDtypeStruct((nbatch, N_HEADS, HEAD_DIM, HEAD_DIM), F32) for _ in range(4)]
    out_shape += [jax.ShapeDtypeStruct((nbatch, N_HEADS, HEAD_DIM), F32),
                  jax.ShapeDtypeStruct((nbatch, 1, N_HEADS), F32)]
    out_specs = [pl.BlockSpec((rows, D_MODEL), row_map), pl.BlockSpec((SUBLANES, CONV_CH), lambda b, t: (b, 0)),
                 st4, st4, st4, st4, st3(N_HEADS), st3(1)]
    scratch = [pltpu.VMEM((rows + SUBLANES, CONV_CH), F32), pltpu.VMEM((NEXP, rows, GROUP_W), F32)]
    scratch += [pltpu.VMEM((rows, GROUP_W), F32) for _ in range(7)]
    scratch += [pltpu.VMEM((rows, D_MODEL), F32),
                pltpu.VMEM((GROUP_W, GROUP_W), F32), pltpu.VMEM((2 * HEAD_DIM, GROUP_W), F32),
                pltpu.VMEM((GROUP_W, GROUP_W), F32), pltpu.VMEM((GROUP_W, GROUP_W), F32),
                pltpu.VMEM((SUBLANES, GROUP_W), F32)]
    body = functools.partial(_mixp_body, rows=rows, chunk=chunk, ret_carry=carry)
    outs = pl.pallas_call(
        body, out_shape=out_shape, grid=(nbatch, ntile), in_specs=in_specs, out_specs=out_specs,
        scratch_shapes=scratch,
        compiler_params=pltpu.CompilerParams(dimension_semantics=("parallel", "arbitrary"),
                                             vmem_limit_bytes=VMEM_LIMIT),
        name="mix_prompt",
    )(proj, cos, sin, *consts, *small)
    outs = list(outs)
    outs[3] = jnp.swapaxes(outs[3], -1, -2)
    return outs


def _ret_tables(chunk, pad):
    lg = np.log(1.0 - np.exp2(-5.0 - np.arange(N_HEADS, dtype=np.float64)))
    idx = np.arange(chunk, dtype=np.float64)
    rel = idx[:, None] - idx[None, :]
    intra = np.where(rel[None] >= 0, np.exp(np.maximum(rel[None], 0.0) * lg[:, None, None]), 0.0)
    inter = np.where(idx[None, :] >= pad, np.exp((idx[None, :] - pad + 1.0) * lg[:, None]), 0.0)
    tail = np.exp((chunk - 1.0 - idx[None, :]) * lg[:, None])
    carry = np.exp((chunk - pad) * lg)
    lanes = lambda t: np.repeat(t.T, HEAD_DIM, axis=1)
    return (jnp.asarray(lanes(inter), F32), jnp.asarray(lanes(tail), F32), jnp.asarray(intra, F32),
            tuple(float(c) for c in carry))


def _const_mats(rows, chunk):
    r = np.arange(rows)
    tri = ((r[:, None] >= r[None, :]) & (r[:, None] // chunk == r[None, :] // chunk))
    sel = np.zeros((4 * SUBLANES, LANES), np.float32)
    sel[np.arange(24), np.arange(24)] = 1.0
    g = np.arange(GROUP_W)
    ones_bd = (g[:, None] // HEAD_DIM == g[None, :] // HEAD_DIM)
    return (jnp.asarray(tri, BF16), jnp.asarray(sel, BF16), jnp.asarray(ones_bd, BF16))


def _mix_sample(proj, hist, cos, sin, small, states, l, *, nbatch):
    rows, chunk, pad = SAMPLE_GROUP * SAMPLE_PAD, SAMPLE_PAD, SAMPLE_PAD - 4
    row_map = lambda i: (i, 0)
    c2 = lambda i: (0, 0)
    rinter, rtail, rintra, carry = _ret_tables(chunk, pad)
    tri, sel, ones_bd = _const_mats(rows, chunk)
    m = proj.shape[0]
    in_specs = [pl.BlockSpec((rows, IN_PAD), row_map),
                pl.BlockSpec((None, rows, CONV_CH), lambda i: (l, i, 0)),
                pl.BlockSpec((rows, GROUP_W), c2), pl.BlockSpec((rows, GROUP_W), c2),
                pl.BlockSpec((chunk, GROUP_W), c2), pl.BlockSpec((chunk, GROUP_W), c2),
                pl.BlockSpec((N_HEADS, chunk, chunk), lambda i: (0, 0, 0)),
                pl.BlockSpec((rows, rows), c2), pl.BlockSpec((4 * SUBLANES, LANES), c2),
                pl.BlockSpec((GROUP_W, GROUP_W), c2)]
    in_specs += [_layer_block(a.shape[1:], (l,), buffered=False) for a in small]
    g = SAMPLE_GROUP
    in_specs += [pl.BlockSpec((None, g, N_HEADS, HEAD_DIM, HEAD_DIM), lambda i: (l, i, 0, 0, 0))] * 4
    in_specs += [pl.BlockSpec((None, g, N_HEADS, HEAD_DIM), lambda i: (l, i, 0, 0)),
                 pl.BlockSpec((None, g, 1, N_HEADS), lambda i: (l, i, 0, 0))]
    st4 = pl.BlockSpec((g, N_HEADS, HEAD_DIM, HEAD_DIM), lambda i: (i, 0, 0, 0))
    out_specs = [pl.BlockSpec((rows, D_MODEL), row_map), pl.BlockSpec((rows, CONV_CH), row_map),
                 st4, st4, st4, st4,
                 pl.BlockSpec((g, N_HEADS, HEAD_DIM), lambda i: (i, 0, 0)),
                 pl.BlockSpec((g, 1, N_HEADS), lambda i: (i, 0, 0))]
    out_shape = [jax.ShapeDtypeStruct((m, D_MODEL), BF16), jax.ShapeDtypeStruct((m, CONV_CH), F32)]
    out_shape += [jax.ShapeDtypeStruct((nbatch, N_HEADS, HEAD_DIM, HEAD_DIM), F32) for _ in range(4)]
    out_shape += [jax.ShapeDtypeStruct((nbatch, N_HEADS, HEAD_DIM), F32),
                  jax.ShapeDtypeStruct((nbatch, 1, N_HEADS), F32)]
    scratch = [pltpu.VMEM((rows + SUBLANES, CONV_CH), F32), pltpu.VMEM((rows, LANES), F32)]
    scratch += [pltpu.VMEM((rows, GROUP_W), F32) for _ in range(7)]
    scratch += [pltpu.VMEM((rows, D_MODEL), F32)]
    body = functools.partial(_mix_body, sample=True, rows=rows, chunk=chunk, ret_carry=carry)
    return pl.pallas_call(
        body, out_shape=out_shape, grid=(nbatch // g,), in_specs=in_specs, out_specs=out_specs,
        scratch_shapes=scratch,
        compiler_params=pltpu.CompilerParams(dimension_semantics=("parallel",), vmem_limit_bytes=VMEM_LIMIT),
        name="mix_sample",
    )(proj, hist, cos, sin, rinter, rtail, rintra, tri, sel, ones_bd, *small, *states)


S_RQ, S_RK, S_RV, S_XDT, S_B, S_C, S_GQ, S_GK, S_GV, S_MQ, S_MK, S_MV, S_GATE = (
    HEAD_DIM * i for i in range(13))
SLAB_ROWS = S_GATE + SUBLANES
GT_SSM_A, GT_GDN_A, GT_BETA, GT_LOGI, GT_LOGF, GT_GAMMA = range(6)
P_RG, P_Z, P_XS, P_GG, P_MO = (GROUP_W * i for i in range(5))
POST_ROWS = 5 * GROUP_W
C_CONVW, C_CONVB = 0, CONV_W * CONV_CH
C_BIAS = C_CONVB + CONV_CH
C_ALOG = C_BIAS + 4 * SUBLANES
C_GAMMA = C_ALOG + 4 * SUBLANES
C_ROWS = C_GAMMA + SUBLANES
N_ROWS = 5 * GROUP_W
N_TOK = 4


def _spre_body(x_ref, g_ref, w_ref, hist_ref, rot_ref, colp_ref, slab_ref, post_ref, convraw_ref):
    nb = LANES
    h_in = _rms(x_ref[...], g_ref[2:3, :]).astype(BF16)
    proj = jnp.dot(h_in, w_ref[...], preferred_element_type=F32)
    for j in range(CONV_W - 1):
        convraw_ref[j] = proj[(j + 1) * nb:(j + 2) * nb, COL_CONV:COL_CONV + CONV_CH]
    pt = [proj[t * nb:(t + 1) * nb, :].T for t in range(N_TOK)]
    src = [hist_ref[j].T for j in range(CONV_W - 1)] + [p[COL_CONV:COL_CONV + CONV_CH] for p in pt]
    bias = colp_ref[C_BIAS:C_BIAS + 4 * SUBLANES, :]
    neg_a = -jnp.exp(colp_ref[C_ALOG:C_ALOG + 4 * SUBLANES, :])
    gamma = colp_ref[C_GAMMA:C_GAMMA + SUBLANES, :]
    zero_row = jnp.zeros((1, LANES), F32)

    def rotary(x, t, k):
        swapped = jnp.concatenate(
            [x[h * HEAD_DIM + off:h * HEAD_DIM + off + HEAD_DIM // 2]
             for h in range(N_HEADS) for off in (HEAD_DIM // 2, 0)], axis=0)
        return x * rot_ref[0, t] + swapped * rot_ref[1, t]

    for t in range(N_TOK):
        p = pt[t]
        g = p[COL_GATES:COL_GATES + 4 * SUBLANES]
        xb = g + bias
        sp = _softplus(xb)
        decay = jnp.exp(neg_a * sp)
        beta = _sigmoid(g)
        logf = -_softplus(-xb)
        conv = src[t] * colp_ref[C_CONVW:C_CONVW + CONV_CH, :]
        for j in range(1, CONV_W):
            conv = conv + src[t + j] * colp_ref[C_CONVW + j * CONV_CH:C_CONVW + (j + 1) * CONV_CH, :]
        cv = _silu(conv + colp_ref[C_CONVB:C_CONVB + CONV_CH, :])
        rq = rotary(p[COL_RET:COL_RET + GROUP_W], t, 0)
        rk = rotary(p[COL_RET + GROUP_W:COL_RET + 2 * GROUP_W], t, 1) * HEAD_DIM ** -0.5
        post_ref[t, P_RG:P_RG + GROUP_W, :] = p[COL_RET + 3 * GROUP_W:COL_RET + 4 * GROUP_W]
        post_ref[t, P_Z:P_Z + GROUP_W, :] = p[COL_SSM_Z:COL_SSM_Z + GROUP_W]
        post_ref[t, P_XS:P_XS + GROUP_W, :] = cv[0:GROUP_W]
        post_ref[t, P_GG:P_GG + GROUP_W, :] = p[COL_GDN_GATE:COL_GDN_GATE + GROUP_W]
        post_ref[t, P_MO:P_MO + GROUP_W, :] = p[COL_ML + 3 * GROUP_W:COL_ML + 4 * GROUP_W]
        for h in range(N_HEADS):
            hs = slice(h * HEAD_DIM, (h + 1) * HEAD_DIM)
            grp = h // (N_HEADS // 2)
            put = lambda r, v: slab_ref.__setitem__((h, t, slice(r, r + v.shape[0]), slice(None)), v)
            l2 = lambda v: v * lax.rsqrt(jnp.sum(v * v, axis=0, keepdims=True) + EPS)
            put(S_RQ, rq[hs])
            put(S_RK, rk[hs])
            put(S_RV, p[COL_RET + 2 * GROUP_W + h * HEAD_DIM:COL_RET + 2 * GROUP_W + (h + 1) * HEAD_DIM])
            put(S_XDT, cv[hs] * sp[G_DT + h:G_DT + h + 1])
            put(S_B, cv[GROUP_W + grp * HEAD_DIM:GROUP_W + (grp + 1) * HEAD_DIM])
            put(S_C, cv[GROUP_W + 2 * HEAD_DIM + grp * HEAD_DIM:GROUP_W + 2 * HEAD_DIM + (grp + 1) * HEAD_DIM])
            put(S_GQ, l2(cv[SSM_CONV_CH + h * HEAD_DIM:SSM_CONV_CH + (h + 1) * HEAD_DIM]) * HEAD_DIM ** -0.5)
            put(S_GK, l2(cv[SSM_CONV_CH + GROUP_W + h * HEAD_DIM:SSM_CONV_CH + GROUP_W + (h + 1) * HEAD_DIM]))
            put(S_GV, cv[SSM_CONV_CH + 2 * GROUP_W + h * HEAD_DIM:SSM_CONV_CH + 2 * GROUP_W + (h + 1) * HEAD_DIM])
            put(S_MQ, p[COL_ML + h * HEAD_DIM:COL_ML + (h + 1) * HEAD_DIM])
            put(S_MK, p[COL_ML + GROUP_W + h * HEAD_DIM:COL_ML + GROUP_W + (h + 1) * HEAD_DIM] * HEAD_DIM ** -0.5)
            put(S_MV, p[COL_ML + 2 * GROUP_W + h * HEAD_DIM:COL_ML + 2 * GROUP_W + (h + 1) * HEAD_DIM])
            row = lambda a, j: a[j + h:j + h + 1]
            put(S_GATE, jnp.concatenate([row(decay, G_DT), row(decay, G_GA), row(beta, G_GB), row(xb, G_MI),
                                         row(logf, G_MF), gamma[h:h + 1], zero_row, zero_row], axis=0))


def _spre(x, sandwich, w, hist, rot, colp, l):
    m = x.shape[0]
    return pl.pallas_call(
        _spre_body,
        out_shape=[jax.ShapeDtypeStruct((N_HEADS, N_TOK, SLAB_ROWS, LANES), F32),
                   jax.ShapeDtypeStruct((N_TOK, POST_ROWS, LANES), F32),
                   jax.ShapeDtypeStruct((CONV_W - 1, LANES, CONV_CH), F32)],
        grid=(1,),
        in_specs=[pl.BlockSpec((m, D_MODEL), lambda i: (0, 0)),
                  _layer_block((N_SANDWICH, D_MODEL), (l,), buffered=False),
                  _layer_block((D_MODEL, IN_PAD), (l,), buffered=False),
                  _layer_block((CONV_W - 1, LANES, CONV_CH), (l,), buffered=False),
                  pl.BlockSpec(rot.shape, lambda i: (0, 0, 0, 0)),
                  _layer_block((C_ROWS, LANES), (l,), buffered=False)],
        out_specs=[pl.BlockSpec((N_HEADS, N_TOK, SLAB_ROWS, LANES), lambda i: (0, 0, 0, 0)),
                   pl.BlockSpec((N_TOK, POST_ROWS, LANES), lambda i: (0, 0, 0)),
                   pl.BlockSpec((CONV_W - 1, LANES, CONV_CH), lambda i: (0, 0, 0))],
        compiler_params=pltpu.CompilerParams(dimension_semantics=("arbitrary",), vmem_limit_bytes=VMEM_LIMIT),
        name="spre",
    )(x, sandwich, w, hist, rot, colp)


def _srec_body(*refs, aliased):
    slab_ref, sret_in, sssm_in, sgdn_in, smc_in, smn_in, smm_in = refs[:7]
    outs = refs[7 + (6 if aliased else 0):]
    ot_ref, sret_out, sssm_out, sgdn_out, smc_out, smn_out, smm_out, y8_ref = outs
    tile = lambda t, r: slab_ref[t, r:r + HEAD_DIM, :]
    gate = lambda t, j: slab_ref[t, S_GATE + j:S_GATE + j + 1, :]
    drow = lambda t, r, d: slab_ref[t, pl.ds(r + d, 1), :]
    zeros = jnp.zeros((HEAD_DIM, LANES), F32)
    sweep = lambda body, init: lax.fori_loop(0, HEAD_DIM, body, init, unroll=8)
    blk = lambda d: pl.ds(pl.multiple_of(d * HEAD_DIM, HEAD_DIM), HEAD_DIM)

    gamma = gate(0, GT_GAMMA)
    for t in range(N_TOK):
        src = sret_in if t == 0 else sret_out
        v = tile(t, S_RV)

        def body(d, acc, t=t, src=src, v=v):
            s = src[blk(d), :] * gamma + drow(t, S_RK, d) * v
            sret_out[blk(d), :] = s
            return acc + drow(t, S_RQ, d) * s
        ot_ref[t, 0:HEAD_DIM, :] = sweep(body, zeros)

    for t in range(N_TOK):
        src = sssm_in if t == 0 else sssm_out
        a, b_t, c_t = gate(t, GT_SSM_A), tile(t, S_B), tile(t, S_C)

        def body(p, carry, t=t, src=src, a=a, b_t=b_t, c_t=c_t):
            hp = src[blk(p), :] * a + drow(t, S_XDT, p) * b_t
            sssm_out[blk(p), :] = hp
            part = jnp.sum((hp * c_t).reshape(HEAD_DIM // SUBLANES, SUBLANES, LANES), axis=0)
            y8_ref[pl.ds(pl.multiple_of(p * SUBLANES, SUBLANES), SUBLANES), :] = part
            return carry
        sweep(body, 0)
        ot_ref[t, HEAD_DIM:2 * HEAD_DIM, :] = jnp.sum(y8_ref[...].reshape(HEAD_DIM, SUBLANES, LANES), axis=1)

    ks = sweep(lambda d, acc: acc + drow(0, S_GK, d) * sgdn_in[blk(d), :], zeros)
    for t in range(N_TOK):
        src = sgdn_in if t == 0 else sgdn_out
        a = gate(t, GT_GDN_A)
        u = gate(t, GT_BETA) * (tile(t, S_GV) - a * ks)
        nxt = t + 1 < N_TOK

        def body(d, acc, t=t, src=src, a=a, u=u, nxt=nxt):
            s = src[blk(d), :] * a + drow(t, S_GK, d) * u
            sgdn_out[blk(d), :] = s
            o = acc[0] + drow(t, S_GQ, d) * s
            return (o, acc[1] + drow(t + 1, S_GK, d) * s) if nxt else (o, acc[1])
        o, ks = sweep(body, (zeros, zeros))
        ot_ref[t, 2 * HEAD_DIM:3 * HEAD_DIM, :] = o

    m_row, n_tile = smm_in[...], smn_in[...]
    for t in range(N_TOK):
        src = smc_in if t == 0 else smc_out
        logi, logf = gate(t, GT_LOGI), gate(t, GT_LOGF)
        m_new = jnp.maximum(logf + m_row, logi)
        fs, ins = jnp.exp(logf + m_row - m_new), jnp.exp(logi - m_new)
        n_tile = fs * n_tile + ins * tile(t, S_MK)
        qn = jnp.sum(tile(t, S_MQ) * n_tile, axis=0, keepdims=True)
        v = tile(t, S_MV) * ins

        def body(d, acc, t=t, src=src, fs=fs, v=v):
            c = src[blk(d), :] * fs + drow(t, S_MK, d) * v
            smc_out[blk(d), :] = c
            return acc + drow(t, S_MQ, d) * c
        num = sweep(body, zeros)
        ot_ref[t, 3 * HEAD_DIM:4 * HEAD_DIM, :] = num / jnp.maximum(jnp.abs(qn), jnp.exp(-m_new))
        m_row = m_new
    smn_out[...] = n_tile
    smm_out[...] = m_row


def _srec(slab, states, prev, l):
    depth = states[0].shape[0]
    rows = HEAD_DIM * HEAD_DIM
    st_shapes = [(rows, LANES)] * 4 + [(HEAD_DIM, LANES), (1, LANES)]
    lh = lambda tail: pl.BlockSpec((None, None) + tail, lambda h: (l, h, 0, 0))
    in_specs = [pl.BlockSpec((None, N_TOK, SLAB_ROWS, LANES), lambda h: (h, 0, 0, 0))]
    in_specs += [lh(s) for s in st_shapes]
    args = [slab, *states]
    aliases = {}
    if prev is not None:
        in_specs += [pl.BlockSpec(memory_space=pl.ANY)] * 6
        args += list(prev)
        aliases = {7 + i: 1 + i for i in range(6)}
    out_shape = [jax.ShapeDtypeStruct((N_HEADS, N_TOK, GROUP_W, LANES), F32)]
    out_shape += [jax.ShapeDtypeStruct((depth, N_HEADS) + s, F32) for s in st_shapes]
    out_specs = [pl.BlockSpec((None, N_TOK, GROUP_W, LANES), lambda h: (h, 0, 0, 0))] + [lh(s) for s in st_shapes]
    return pl.pallas_call(
        functools.partial(_srec_body, aliased=prev is not None),
        out_shape=out_shape, grid=(N_HEADS,), in_specs=in_specs, out_specs=out_specs,
        input_output_aliases=aliases,
        scratch_shapes=[pltpu.VMEM((HEAD_DIM * SUBLANES, LANES), F32)],
        compiler_params=pltpu.CompilerParams(dimension_semantics=("arbitrary",), vmem_limit_bytes=VMEM_LIMIT),
        name="srec",
    )(*args)


def _spost_body(x_ref, ot_ref, post_ref, cn_ref, g_ref, w_ref, o_ref):
    nb = LANES

    def head_norm(x):
        out = []
        for h in range(N_HEADS):
            b = x[h * HEAD_DIM:(h + 1) * HEAD_DIM]
            out.append(b * lax.rsqrt(jnp.mean(b * b, axis=0, keepdims=True) + EPS))
        return jnp.concatenate(out, axis=0)

    gain = lambda i: cn_ref[i * GROUP_W:(i + 1) * GROUP_W, :]
    for t in range(N_TOK):
        mixer = lambda i: jnp.concatenate(
            [ot_ref[h, t, i * HEAD_DIM:(i + 1) * HEAD_DIM, :] for h in range(N_HEADS)], axis=0)
        pin = lambda r: post_ref[t, r:r + GROUP_W, :]
        out_ret = head_norm(mixer(0)) * gain(0) * _silu(pin(P_RG))
        y = (mixer(1) + gain(4) * pin(P_XS)) * _silu(pin(P_Z))
        out_ssm = y * lax.rsqrt(jnp.mean(y * y, axis=0, keepdims=True) + EPS) * gain(1)
        out_gdn = head_norm(mixer(2)) * gain(2) * _silu(pin(P_GG))
        out_ml = _sigmoid(pin(P_MO)) * (head_norm(mixer(3)) * gain(3))
        mixed = jnp.concatenate([out_ret, out_ssm, out_gdn, out_ml], axis=0).T.astype(BF16)
        y = jnp.dot(mixed, w_ref[...], preferred_element_type=F32)
        rs = slice(t * nb, (t + 1) * nb)
        o_ref[rs, :] = x_ref[rs, :] + _rms(y, g_ref[3:4, :])


def _spost(x, ot, post, cn, sandwich, w, l):
    m = x.shape[0]
    return pl.pallas_call(
        _spost_body,
        out_shape=jax.ShapeDtypeStruct((m, D_MODEL), F32),
        grid=(1,),
        in_specs=[pl.BlockSpec((m, D_MODEL), lambda i: (0, 0)),
                  pl.BlockSpec(ot.shape, lambda i: (0, 0, 0, 0)),
                  pl.BlockSpec(post.shape, lambda i: (0, 0, 0)),
                  _layer_block((N_ROWS, LANES), (l,), buffered=False),
                  _layer_block((N_SANDWICH, D_MODEL), (l,), buffered=False),
                  _layer_block((D_MODEL, D_MODEL), (l,), buffered=False)],
        out_specs=pl.BlockSpec((m, D_MODEL), lambda i: (0, 0)),
        compiler_params=pltpu.CompilerParams(dimension_semantics=("arbitrary",), vmem_limit_bytes=VMEM_LIMIT),
        name="spost",
    )(x, ot, post, cn, sandwich, w)


def _rot_tables(pos):
    half = HEAD_DIM // 2
    inv = ROPE_BASE ** (-jnp.arange(half, dtype=F32) / half)
    ang = pos.astype(F32)[:, None] * inv[None, :]
    cos, sin = jnp.cos(ang), jnp.sin(ang)
    cos_full = jnp.tile(jnp.concatenate([cos, cos], axis=-1), (1, N_HEADS))
    sin_signed = jnp.tile(jnp.concatenate([-sin, sin], axis=-1), (1, N_HEADS))
    return cos_full, sin_signed


def _small_params(ret_norm, ssm_conv_w, ssm_conv_b, ssm_dt_bias, ssm_A_log, ssm_D, ssm_norm, gdn_conv_w,
                  gdn_dt_bias, gdn_A_log, gdn_norm, mlstm_i_bias, mlstm_f_bias, mlstm_norm):
    depth = ret_norm.shape[0]
    z = lambda n: jnp.zeros((depth, n), F32)
    pad_lanes = z(LANES - 5 * N_HEADS)
    bias = jnp.concatenate([ssm_dt_bias, gdn_dt_bias, z(N_HEADS), mlstm_i_bias, mlstm_f_bias, pad_lanes], axis=1)
    alog = jnp.concatenate([ssm_A_log, gdn_A_log, z(3 * N_HEADS), pad_lanes], axis=1)
    gvec = jnp.stack([bias, alog], axis=1)
    convw = jnp.concatenate([ssm_conv_w, gdn_conv_w], axis=2)
    convb = jnp.concatenate([ssm_conv_b, z(CONV_CH - SSM_CONV_CH)], axis=1)[:, None, :]
    dvec = jnp.repeat(ssm_D, HEAD_DIM, axis=1)[:, None, :]
    norms = jnp.stack([ret_norm, ssm_norm, gdn_norm, mlstm_norm], axis=1)
    return gvec, convw, convb, dvec, norms


def kernel(x_prompt, x_sample, state_ret, state_ssm, state_ssm_conv, state_gdn, state_gdn_conv,
           state_mlstm_C, state_mlstm_n, state_mlstm_m, w_in, w_out, norm_sandwich, ffn_w_gate, ffn_w_up,
           ffn_w_down, ret_norm, ssm_conv_w, ssm_conv_b, ssm_dt_bias, ssm_A_log, ssm_D, ssm_norm, gdn_conv_w,
           gdn_dt_bias, gdn_A_log, gdn_norm, mlstm_i_bias, mlstm_f_bias, mlstm_norm):
    depth = w_in.shape[0]
    bp, lp, _ = x_prompt.shape
    bs, ls, _ = x_sample.shape
    assert (bs, ls) == (LANES, N_TOK)
    w_in_p = _prep_win(w_in)
    w_out_b = w_out.astype(BF16)
    wg, wu, wd = ffn_w_gate.astype(BF16), ffn_w_up.astype(BF16), ffn_w_down.astype(BF16)
    small = _small_params(ret_norm, ssm_conv_w, ssm_conv_b, ssm_dt_bias, ssm_A_log, ssm_D, ssm_norm, gdn_conv_w,
                          gdn_dt_bias, gdn_A_log, gdn_norm, mlstm_i_bias, mlstm_f_bias, mlstm_norm)
    gvec, convw, convb, dvec, norms = small

    cos_p, sin_p = _rot_tables(jnp.arange(lp, dtype=jnp.int32))
    x = x_prompt.reshape(bp * lp, D_MODEL)
    st_p = []
    for l in range(depth):
        x = _ffn(x, norm_sandwich, wg, wu, wd, l, 0)
        outs = _mix_prompt(_inproj(x, norm_sandwich, w_in_p, l), cos_p, sin_p, small, l, nbatch=bp, seqlen=lp)
        x = _outproj(x, outs[0], norm_sandwich, w_out_b, l)
        x = _ffn(x, norm_sandwich, wg, wu, wd, l, 1)
        st_p.append(outs[1:])
    y_prompt = x.reshape(bp, lp, D_MODEL)
    conv_p = jnp.stack([s[0].reshape(bp, -1, CONV_CH)[:, -(CONV_W - 1):, :] for s in st_p])
    stack_p = lambda k: jnp.stack([s[k] for s in st_p])
    out_p = (stack_p(1), stack_p(2), conv_p[..., :SSM_CONV_CH], stack_p(3), conv_p[..., SSM_CONV_CH:],
             stack_p(4), stack_p(5), stack_p(6)[:, :, 0, :])

    lanes = lambda a: jnp.broadcast_to(a[..., None], a.shape + (LANES,))
    cos_s, sin_s = _rot_tables(PAST_LEN + jnp.arange(ls, dtype=jnp.int32))
    rot = lanes(jnp.stack([cos_s, sin_s]))
    lg = np.log(1.0 - np.exp2(-5.0 - np.arange(N_HEADS, dtype=np.float64)))
    gamma = jnp.asarray(np.concatenate([np.exp(lg), np.zeros(SUBLANES - N_HEADS)]), F32)
    colp = lanes(jnp.concatenate(
        [convw.reshape(depth, CONV_W * CONV_CH), convb[:, 0, :], gvec[:, 0, :4 * SUBLANES], gvec[:, 1, :4 * SUBLANES],
         jnp.broadcast_to(gamma, (depth, SUBLANES))], axis=1))
    cn = lanes(jnp.concatenate([norms.reshape(depth, 4 * GROUP_W), dvec[:, 0, :]], axis=1))
    hist = jnp.transpose(jnp.concatenate([state_ssm_conv, state_gdn_conv], axis=-1), (0, 2, 1, 3))
    mat = lambda s: jnp.transpose(s, (0, 2, 3, 4, 1)).reshape(depth, N_HEADS, HEAD_DIM * HEAD_DIM, bs)
    states_s = (mat(state_ret), mat(state_ssm), mat(state_gdn), mat(state_mlstm_C),
                jnp.transpose(state_mlstm_n, (0, 2, 3, 1)), jnp.transpose(state_mlstm_m, (0, 2, 1))[:, :, None, :])
    x = jnp.transpose(x_sample, (1, 0, 2)).reshape(ls * bs, D_MODEL)
    prev, convs = None, []
    for l in range(depth):
        x = _ffn(x, norm_sandwich, wg, wu, wd, l, 0)
        slab, post, convraw = _spre(x, norm_sandwich, w_in_p, hist, rot, colp, l)
        outs = _srec(slab, states_s, prev, l)
        prev = outs[1:]
        x = _spost(x, outs[0], post, cn, norm_sandwich, w_out_b, l)
        x = _ffn(x, norm_sandwich, wg, wu, wd, l, 1)
        convs.append(convraw)
    y_sample = jnp.transpose(x.reshape(ls, bs, D_MODEL), (1, 0, 2))
    conv_s = jnp.transpose(jnp.stack(convs), (0, 2, 1, 3))
    unmat = lambda s: jnp.transpose(s.reshape(depth, N_HEADS, HEAD_DIM, HEAD_DIM, bs), (0, 4, 1, 2, 3))
    out_s = (unmat(prev[0]), unmat(prev[1]), conv_s[..., :SSM_CONV_CH], unmat(prev[2]), conv_s[..., SSM_CONV_CH:],
             unmat(prev[3]), jnp.transpose(prev[4], (0, 3, 1, 2)), jnp.transpose(prev[5][:, :, 0, :], (0, 2, 1)))
    return (y_prompt, y_sample) + out_p + out_s
```

```python
import functools
import math

import numpy as np
import jax
import jax.numpy as jnp
from jax import lax
from jax.experimental import pallas as pl
from jax.experimental.pallas import tpu as pltpu

F32 = jnp.float32
BF16 = jnp.bfloat16

D_MODEL = 1024
N_HEADS = 4
HEAD_DIM = 64
GROUP_W = N_HEADS * HEAD_DIM
D_FF = 2816
CONV_W = 4
EPS = 1e-6
ROPE_BASE = 10000.0
PAST_LEN = 16384
PROMPT_CHUNK = 64

COL_RET = 0
COL_SSM_Z = 1024
COL_CONV = 1280
CONV_CH = 1280
SSM_CONV_CH = 512
COL_GDN_GATE = 2560
COL_ML = 2816
COL_GATES = 3840
IN_PAD = 3968
G_DT, G_GA, G_GB, G_MI, G_MF, G_DTV = 0, 4, 8, 12, 16, 20

SUBLANES = 8
LANES = 128
VMEM_LIMIT = 56 * 1024 * 1024

FF_CHUNK = 256
TOKEN_TILE = 512
PROMPT_TILE = 256
SAMPLE_PAD = 8
SAMPLE_GROUP = 8
NEG_BIG = -1e30


def _sigmoid(x):
    return 1.0 / (1.0 + jnp.exp(-x))


def _silu(x):
    return x * _sigmoid(x)


def _softplus(x):
    return jnp.maximum(x, 0.0) + jnp.log1p(jnp.exp(-jnp.abs(x)))


def _rms(x, g):
    return x * lax.rsqrt(jnp.mean(x * x, axis=-1, keepdims=True) + EPS) * g


def _dot(a, b):
    return jnp.dot(a.astype(BF16), b.astype(BF16), preferred_element_type=F32)


def _dot_nt(a, b):
    return lax.dot_general(a.astype(BF16), b.astype(BF16), (((1,), (1,)), ((), ())),
                           preferred_element_type=F32)


def _dot_tn(a, b):
    return lax.dot_general(a.astype(BF16), b.astype(BF16), (((0,), (0,)), ((), ())),
                           preferred_element_type=F32)


def _split3(x):
    hi = x.astype(BF16)
    r = x - hi.astype(F32)
    mid = r.astype(BF16)
    lo = (r - mid.astype(F32)).astype(BF16)
    return hi, mid, lo


def _dot_sel(sel, x):
    hi, mid, lo = _split3(x)
    f = lambda p: jnp.dot(sel, p, preferred_element_type=F32)
    return (f(lo) + f(mid)) + f(hi)


def _dot_sel_r(x, sel):
    hi, mid, lo = _split3(x)
    f = lambda p: jnp.dot(p, sel, preferred_element_type=F32)
    return (f(lo) + f(mid)) + f(hi)


def _dot_sel_nt(sel, x):
    hi, mid, lo = _split3(x)
    f = lambda p: lax.dot_general(sel, p, (((1,), (1,)), ((), ())), preferred_element_type=F32)
    return (f(lo) + f(mid)) + f(hi)


N_SANDWICH = 6


def _layer_block(tail, lead, ngrid=1, buffered=True):
    idx = tuple(lead) + (0,) * len(tail)
    imap = (lambda i: idx) if ngrid == 1 else (lambda i, j: idx)
    kw = dict(pipeline_mode=pl.Buffered(1)) if buffered else {}
    return pl.BlockSpec((None,) * len(lead) + tuple(tail), imap, **kw)


WIN_SEGMENTS = ((0, 0, 1792), (1792, 1796, 1024), (2816, 2828, 1024))
WIN_GATES = ((1792, 1796), (2820, 2828), (3852, 3860))


def _prep_win_body(w_ref, o_ref):
    l = pl.program_id(0)
    step = 2 * LANES
    for dst, src, n in WIN_SEGMENTS:
        for c in range(0, n, step):
            o_ref[:, dst + c:dst + c + step] = w_ref[src + c:src + c + step, l, :].T.astype(BF16)
    gates = jnp.concatenate([w_ref[a:b, l, :] for a, b in WIN_GATES]
                            + [jnp.zeros((LANES - 5 * N_HEADS, D_MODEL), F32)], axis=0)
    o_ref[:, COL_GATES:IN_PAD] = gates.T.astype(BF16)


def _prep_win(w_in):
    depth, _, in_dim = w_in.shape
    w_t = jnp.transpose(w_in, (2, 0, 1))
    return pl.pallas_call(
        _prep_win_body,
        out_shape=jax.ShapeDtypeStruct((depth, D_MODEL, IN_PAD), BF16),
        grid=(depth,),
        in_specs=[pl.BlockSpec((in_dim, depth, D_MODEL), lambda l: (0, 0, 0), pipeline_mode=pl.Buffered(1))],
        out_specs=pl.BlockSpec((None, D_MODEL, IN_PAD), lambda l: (l, 0, 0)),
        compiler_params=pltpu.CompilerParams(dimension_semantics=("arbitrary",), vmem_limit_bytes=VMEM_LIMIT),
        name="prep_win",
    )(w_t)


def _ffn_body(*refs, k, mixer_out):
    if mixer_out:
        x_ref, m_ref, wo_ref, g_ref, wg_ref, wu_ref, wd_ref, o_ref, a_ref = refs
        y = jnp.dot(m_ref[...], wo_ref[...], preferred_element_type=F32)
        x = x_ref[...] + _rms(y, g_ref[3:4, :])
    else:
        x_ref, g_ref, wg_ref, wu_ref, wd_ref, o_ref, a_ref = refs
        x = x_ref[...]
    h = _rms(x, g_ref[4 * k:4 * k + 1, :]).astype(BF16)
    for j in range(D_FF // FF_CHUNK):
        sl = slice(j * FF_CHUNK, (j + 1) * FF_CHUNK)
        gt = jnp.dot(h, wg_ref[:, sl].astype(BF16), preferred_element_type=F32)
        up = jnp.dot(h, wu_ref[:, sl].astype(BF16), preferred_element_type=F32)
        a_ref[:, sl] = (_silu(gt) * up).astype(BF16)
    y = jnp.dot(a_ref[...], wd_ref[...], preferred_element_type=F32)
    o_ref[...] = x + 0.5 * _rms(y, g_ref[4 * k + 1:4 * k + 2, :])


def _ffn(x, sandwich, wg, wu, wd, l, k, mixed=None, w_out=None):
    m = x.shape[0]
    tm = min(TOKEN_TILE, m)
    tile = pl.BlockSpec((tm, D_MODEL), lambda i: (i, 0))
    fused = mixed is not None
    in_specs = [tile] + ([tile, _layer_block((D_MODEL, D_MODEL), (l,))] if fused else [])
    in_specs += [_layer_block((N_SANDWICH, D_MODEL), (l,)),
                 _layer_block((D_MODEL, D_FF), (l, k)),
                 _layer_block((D_MODEL, D_FF), (l, k)),
                 _layer_block((D_FF, D_MODEL), (l, k))]
    args = (x,) + ((mixed, w_out) if fused else ()) + (sandwich, wg, wu, wd)
    return pl.pallas_call(
        functools.partial(_ffn_body, k=k, mixer_out=fused),
        out_shape=jax.ShapeDtypeStruct((m, D_MODEL), F32),
        grid=(m // tm,),
        in_specs=in_specs,
        out_specs=tile,
        scratch_shapes=[pltpu.VMEM((tm, D_FF), BF16)],
        compiler_params=pltpu.CompilerParams(dimension_semantics=("parallel",),
                                             vmem_limit_bytes=VMEM_LIMIT),
        name="ffn_out" if fused else "ffn",
    )(*args)


def _inproj_body(x_ref, g_ref, w_ref, o_ref):
    h = _rms(x_ref[...], g_ref[2:3, :]).astype(BF16)
    o_ref[...] = jnp.dot(h, w_ref[...], preferred_element_type=F32)


def _inproj(x, sandwich, w, l):
    m = x.shape[0]
    tm = min(TOKEN_TILE, m)
    return pl.pallas_call(
        _inproj_body,
        out_shape=jax.ShapeDtypeStruct((m, IN_PAD), F32),
        grid=(m // tm,),
        in_specs=[pl.BlockSpec((tm, D_MODEL), lambda i: (i, 0)),
                  _layer_block((N_SANDWICH, D_MODEL), (l,)),
                  _layer_block((D_MODEL, IN_PAD), (l,))],
        out_specs=pl.BlockSpec((tm, IN_PAD), lambda i: (i, 0)),
        compiler_params=pltpu.CompilerParams(dimension_semantics=("parallel",),
                                             vmem_limit_bytes=VMEM_LIMIT),
        name="inproj",
    )(x, sandwich, w)


def _outproj_body(x_ref, m_ref, g_ref, w_ref, o_ref):
    y = jnp.dot(m_ref[...], w_ref[...], preferred_element_type=F32)
    o_ref[...] = x_ref[...] + _rms(y, g_ref[3:4, :])


def _outproj(x, mixed, sandwich, w, l):
    m = x.shape[0]
    tm = min(TOKEN_TILE, m)
    return pl.pallas_call(
        _outproj_body,
        out_shape=jax.ShapeDtypeStruct((m, D_MODEL), F32),
        grid=(m // tm,),
        in_specs=[pl.BlockSpec((tm, D_MODEL), lambda i: (i, 0)),
                  pl.BlockSpec((tm, D_MODEL), lambda i: (i, 0)),
                  _layer_block((N_SANDWICH, D_MODEL), (l,)),
                  _layer_block((D_MODEL, D_MODEL), (l,))],
        out_specs=pl.BlockSpec((tm, D_MODEL), lambda i: (i, 0)),
        compiler_params=pltpu.CompilerParams(dimension_semantics=("parallel",),
                                             vmem_limit_bytes=VMEM_LIMIT),
        name="outproj",
    )(x, mixed, sandwich, w)


def _head(x, h):
    return x[:, h * HEAD_DIM:(h + 1) * HEAD_DIM]


def _neumann_inverse(amat, c):
    row = lax.broadcasted_iota(jnp.int32, (c, c), 0)
    col = lax.broadcasted_iota(jnp.int32, (c, c), 1)
    eye = (row == col).astype(F32)
    m = -amat
    p = eye + m
    for _ in range(max(int(math.ceil(math.log2(c))) - 1, 0)):
        m = _dot(m, m)
        p = p + _dot(p, m)
    return p


def _mix_body(*refs, sample, rows, chunk, ret_carry):
    nchunks = rows // chunk
    it = iter(refs)
    proj_ref = next(it)
    hist_ref = next(it) if sample else None
    cos_ref, sin_ref = next(it), next(it)
    rinter_ref, rtail_ref, rintra_ref = next(it), next(it), next(it)
    tri_ref, sel_ref, ones_ref = next(it), next(it), next(it)
    gvec_ref, convw_ref, convb_ref, dvec_ref, norms_ref = next(it), next(it), next(it), next(it), next(it)
    if sample:
        sret_in, sssm_in, sgdn_in, smc_in, smn_in, smm_in = (next(it) for _ in range(6))
    mixed_ref, convraw_ref = next(it), next(it)
    sret_ref, sssm_ref, sgdn_ref, smc_ref, smn_ref, smm_ref = (next(it) for _ in range(6))
    e_ref, g_ref, rq_ref, rk_ref, sx_ref, sbc_ref, gq_ref, gk_ref, gv_ref, o_ref = (next(it) for _ in range(10))
    if not sample:
        sret_in, sssm_in, sgdn_in, smc_in, smn_in, smm_in = (
            sret_ref, sssm_ref, sgdn_ref, smc_ref, smn_ref, smm_ref)

    if sample:
        rowid = lax.broadcasted_iota(jnp.int32, (rows, 1), 0)
        valid = (rowid % SAMPLE_PAD) >= (SAMPLE_PAD - 4)
        e_ref[0:SUBLANES, :] = jnp.zeros((SUBLANES, CONV_CH), F32)
    else:
        valid = None

        @pl.when(pl.program_id(1) == 0)
        def _():
            e_ref[0:SUBLANES, :] = jnp.zeros((SUBLANES, CONV_CH), F32)
            sret_ref[...] = jnp.zeros_like(sret_ref)
            sssm_ref[...] = jnp.zeros_like(sssm_ref)
            sgdn_ref[...] = jnp.zeros_like(sgdn_ref)
            smc_ref[...] = jnp.zeros_like(smc_ref)
            smn_ref[...] = jnp.zeros_like(smn_ref)
            smm_ref[...] = jnp.zeros_like(smm_ref)

    def keep(x, fill=0.0):
        return x if valid is None else jnp.where(valid, x, fill)

    lane = lax.broadcasted_iota(jnp.int32, (1, LANES), 1)
    in_lanes = lambda lo: (lane >= lo) & (lane < lo + N_HEADS)
    graw = proj_ref[:, COL_GATES:COL_GATES + LANES]
    xb = graw + gvec_ref[0:1, :]
    sp = _softplus(xb)
    neg_a = -jnp.exp(gvec_ref[1:2, :])
    logf = -_softplus(-xb)
    pre = jnp.where(lane < 2 * N_HEADS, neg_a * sp, jnp.where(in_lanes(G_MF), logf, 0.0))
    cum = _dot_sel(tri_ref[...], keep(pre))
    dtv = pltpu.roll(sp, G_DTV - G_DT, axis=1)
    aux = jnp.where(in_lanes(G_GB), _sigmoid(graw),
                    jnp.where(in_lanes(G_MI), xb, jnp.where(in_lanes(G_DTV), dtv, 0.0)))
    if sample:
        aux = jnp.where(valid, aux, jnp.where(in_lanes(G_MI), NEG_BIG, 0.0))
    g_ref[...] = jnp.where(in_lanes(G_GB) | in_lanes(G_MI) | in_lanes(G_DTV), aux, cum)

    lane_g = lax.broadcasted_iota(jnp.int32, (1, GROUP_W), 1)
    first_half = (lane_g % HEAD_DIM) < (HEAD_DIM // 2)

    def rotary(x):
        swapped = jnp.where(first_half, pltpu.roll(x, GROUP_W - HEAD_DIM // 2, axis=1),
                            pltpu.roll(x, HEAD_DIM // 2, axis=1))
        return x * cos_ref[...] + swapped * sin_ref[...]

    rq_ref[...] = rotary(proj_ref[:, COL_RET:COL_RET + GROUP_W])
    rk_ref[...] = keep(rotary(proj_ref[:, COL_RET + GROUP_W:COL_RET + 2 * GROUP_W]) * HEAD_DIM ** -0.5)

    raw = proj_ref[:, COL_CONV:COL_CONV + CONV_CH]
    if sample:
        raw = raw + hist_ref[...]
    e_ref[SUBLANES:SUBLANES + rows, :] = raw
    conv = raw * convw_ref[CONV_W - 1:CONV_W, :]
    for s in range(1, CONV_W):
        conv = conv + e_ref[SUBLANES - s:SUBLANES - s + rows, :] * convw_ref[CONV_W - 1 - s:CONV_W - s, :]
    conv = keep(_silu(conv + convb_ref[...]))
    if sample:
        convraw_ref[...] = raw
    else:
        convraw_ref[...] = e_ref[rows:rows + SUBLANES, :]
        e_ref[0:SUBLANES, :] = e_ref[rows:rows + SUBLANES, :]
    sx_ref[...] = conv[:, 0:GROUP_W]
    sbc_ref[...] = conv[:, GROUP_W:SSM_CONV_CH]
    ones_bd = ones_ref[...]

    def head_sumsq(x):
        return _dot_sel_r(x * x, ones_bd)

    gq = conv[:, SSM_CONV_CH:SSM_CONV_CH + GROUP_W]
    gk = conv[:, SSM_CONV_CH + GROUP_W:SSM_CONV_CH + 2 * GROUP_W]
    gq_ref[...] = gq * lax.rsqrt(head_sumsq(gq) + EPS) * HEAD_DIM ** -0.5
    gk_ref[...] = gk * lax.rsqrt(head_sumsq(gk) + EPS)
    gv_ref[...] = conv[:, SSM_CONV_CH + 2 * GROUP_W:CONV_CH]

    r_i = lax.broadcasted_iota(jnp.int32, (chunk, chunk), 0)
    c_i = lax.broadcasted_iota(jnp.int32, (chunk, chunk), 1)
    tri = r_i >= c_i
    strict = r_i > c_i
    lane_h = lax.broadcasted_iota(jnp.int32, (1, N_HEADS), 1)

    def chunk_step(i, carry):
        r0 = pl.multiple_of(i * chunk, chunk)
        rs = pl.ds(r0, chunk)
        sidx = i if sample else 0
        gc = g_ref[rs, :]
        gt = _dot_sel_nt(sel_ref[...], gc)
        col = lambda j: gc[:, j:j + 1]
        rowv = lambda j: gt[j:j + 1, :]
        last = lambda j: gc[chunk - 1:chunk, j:j + 1]

        heads = range(N_HEADS)
        s_ret = [sret_in[sidx, h] for h in heads]
        s_ssm = [sssm_in[sidx, h] for h in heads]
        s_gdn = [sgdn_in[sidx, h] for h in heads]
        s_mc = [smc_in[sidx, h] for h in heads]
        n_all = smn_in[sidx]
        m_row = smm_in[sidx]
        rq_all, rk_all = rq_ref[rs, :], rk_ref[rs, :]
        rv_all = proj_ref[rs, COL_RET + 2 * GROUP_W:COL_RET + 3 * GROUP_W]
        x_all, bc_all = sx_ref[rs, :], sbc_ref[rs, :]
        gq_all, gk_all, gv_all = gq_ref[rs, :], gk_ref[rs, :], gv_ref[rs, :]
        mq_all = proj_ref[rs, COL_ML:COL_ML + GROUP_W]
        mk_all = proj_ref[rs, COL_ML + GROUP_W:COL_ML + 2 * GROUP_W] * HEAD_DIM ** -0.5
        mv_all = proj_ref[rs, COL_ML + 2 * GROUP_W:COL_ML + 3 * GROUP_W]

        qi_all = rq_all * rinter_ref[...]
        kt_all = rk_all * rtail_ref[...]
        o_ret, n_ret = [], []
        for h in heads:
            att = _dot_nt(_head(rq_all, h), _head(rk_all, h)) * rintra_ref[h]
            o_ret.append(_dot(att, _head(rv_all, h)) + _dot(_head(qi_all, h), s_ret[h]))
            n_ret.append(s_ret[h] * ret_carry[h] + _dot_tn(_head(kt_all, h), _head(rv_all, h)))

        o_ssm, n_ssm = [], []
        for h in heads:
            grp = h // (N_HEADS // 2)
            b_g = bc_all[:, grp * HEAD_DIM:(grp + 1) * HEAD_DIM]
            c_g = bc_all[:, 2 * HEAD_DIM + grp * HEAD_DIM:2 * HEAD_DIM + (grp + 1) * HEAD_DIM]
            x_h = _head(x_all, h)
            cum_c, cum_r, cum_l = col(G_DT + h), rowv(G_DT + h), last(G_DT + h)
            lmat = jnp.exp(jnp.where(tri, cum_c - cum_r, -jnp.inf))
            scores = _dot_nt(c_g, b_g) * lmat * rowv(G_DTV + h)
            o_ssm.append(_dot(scores, x_h) + _dot_nt(c_g, s_ssm[h]) * jnp.exp(cum_c))
            w_c = jnp.exp(cum_l - cum_c) * col(G_DTV + h)
            n_ssm.append(s_ssm[h] * jnp.exp(cum_l) + _dot_tn(x_h * w_c, b_g))

        o_gdn, n_gdn = [], []
        for h in heads:
            q_h, k_h, v_h = _head(gq_all, h), _head(gk_all, h), _head(gv_all, h)
            cum_c, cum_r, cum_l = col(G_GA + h), rowv(G_GA + h), last(G_GA + h)
            beta_c = col(G_GB + h)
            gam = jnp.exp(jnp.where(tri, cum_c - cum_r, -jnp.inf))
            amat = _dot_nt(k_h, k_h) * jnp.where(strict, gam, 0.0) * beta_c
            ecum = jnp.exp(cum_c)
            tinv = _neumann_inverse(amat, chunk)
            u = _dot(tinv, v_h * beta_c)
            w = _dot(tinv, k_h * (beta_c * ecum))
            v_new = u - _dot(w, s_gdn[h])
            qk = _dot_nt(q_h, k_h) * gam
            o_gdn.append(_dot(q_h * ecum, s_gdn[h]) + _dot(qk, v_new))
            n_gdn.append(s_gdn[h] * jnp.exp(cum_l) + _dot_tn(k_h * jnp.exp(cum_l - cum_c), v_new))

        m_new_row = jnp.zeros((1, N_HEADS), F32)
        o_ml, n_mc, n_mn = [], [], []
        for h in heads:
            q_h, k_h, v_h = _head(mq_all, h), _head(mk_all, h), _head(mv_all, h)
            b_c, b_r, b_l = col(G_MF + h), rowv(G_MF + h), last(G_MF + h)
            i_c, i_r = col(G_MI + h), rowv(G_MI + h)
            m_old = m_row[:, h:h + 1]
            n_h = n_all[h:h + 1, :]
            dmat = jnp.where(tri, b_c - b_r + i_r, -jnp.inf)
            m_t = jnp.maximum(b_c + m_old, jnp.max(dmat, axis=-1, keepdims=True))
            wts = jnp.exp(dmat - m_t)
            inter = jnp.exp(b_c + m_old - m_t)
            qk = _dot_nt(q_h, k_h) * wts
            num = _dot(qk, v_h) + inter * _dot(q_h, s_mc[h])
            den = jnp.sum(qk, axis=-1, keepdims=True) + inter * jnp.sum(q_h * n_h, axis=-1, keepdims=True)
            o_ml.append(num / jnp.maximum(jnp.abs(den), jnp.exp(-m_t)))
            m_new = m_t[chunk - 1:chunk, :]
            ws = jnp.exp(b_l - b_c + i_c - m_new)
            cs = jnp.exp(b_l + m_old - m_new)
            kw = k_h * ws
            n_mc.append(s_mc[h] * cs + _dot_tn(kw, v_h))
            n_mn.append(n_h * cs + jnp.sum(kw, axis=0, keepdims=True))
            m_new_row = jnp.where(lane_h == h, m_new, m_new_row)

        o_ref[rs, :] = jnp.concatenate(o_ret + o_ssm + o_gdn + o_ml, axis=1)
        for h in heads:
            sret_ref[sidx, h] = n_ret[h]
            sssm_ref[sidx, h] = n_ssm[h]
            sgdn_ref[sidx, h] = n_gdn[h]
            smc_ref[sidx, h] = n_mc[h]
        smn_ref[sidx] = jnp.concatenate(n_mn, axis=0)
        smm_ref[sidx] = m_new_row
        return carry

    lax.fori_loop(0, nchunks, chunk_step, 0)

    def head_norm(x, g):
        return x * lax.rsqrt(head_sumsq(x) * (1.0 / HEAD_DIM) + EPS) * g

    out_ret = head_norm(o_ref[:, 0:GROUP_W], norms_ref[0:1, :]) * _silu(
        proj_ref[:, COL_RET + 3 * GROUP_W:COL_RET + 4 * GROUP_W])
    mixed_ref[:, 0:GROUP_W] = keep(out_ret).astype(BF16)
    y = o_ref[:, GROUP_W:2 * GROUP_W] + dvec_ref[...] * sx_ref[...]
    out_ssm = _rms(y * _silu(proj_ref[:, COL_SSM_Z:COL_SSM_Z + GROUP_W]), norms_ref[1:2, :])
    mixed_ref[:, GROUP_W:2 * GROUP_W] = keep(out_ssm).astype(BF16)
    out_gdn = head_norm(o_ref[:, 2 * GROUP_W:3 * GROUP_W], norms_ref[2:3, :]) * _silu(
        proj_ref[:, COL_GDN_GATE:COL_GDN_GATE + GROUP_W])
    mixed_ref[:, 2 * GROUP_W:3 * GROUP_W] = keep(out_gdn).astype(BF16)
    out_ml = _sigmoid(proj_ref[:, COL_ML + 3 * GROUP_W:COL_ML + 4 * GROUP_W]) * head_norm(
        o_ref[:, 3 * GROUP_W:4 * GROUP_W], norms_ref[3:4, :])
    mixed_ref[:, 3 * GROUP_W:4 * GROUP_W] = keep(out_ml).astype(BF16)


NEXP = 6
X_CSSM, X_DT, X_CGDN, X_BETA, X_B, X_IMB = range(NEXP)


def _split2(x):
    hi = x.astype(BF16)
    return hi, (x - hi.astype(F32)).astype(BF16)


def _dot_sel_r2(x, sel):
    hi, lo = _split2(x)
    return jnp.dot(lo, sel, preferred_element_type=F32) + jnp.dot(hi, sel, preferred_element_type=F32)


def _mixp_body(proj_ref, cos_ref, sin_ref, rinter_ref, rtail_ref, rintra_ref, tri_ref, expand_ref,
               maskbd_ref, maskg_ref, maskht_ref, gvec_ref, convw_ref, convb_ref, dvec_ref, norms_ref,
               mixed_ref, convraw_ref, oret_ref, ossm_ref, ogdn_ref, omc_ref, omn_ref, omm_ref,
               e_ref, x_ref, rq_ref, rk_ref, sx_ref, sbc_ref, gq_ref, gk_ref, gv_ref, o_ref,
               sret_ref, sht_ref, sgdn_ref, smc_ref, vec_ref, *, rows, chunk, ret_carry):
    nchunks = rows // chunk
    t_id = pl.program_id(1)

    @pl.when(t_id == 0)
    def _():
        e_ref[0:SUBLANES, :] = jnp.zeros((SUBLANES, CONV_CH), F32)
        sret_ref[...] = jnp.zeros_like(sret_ref)
        sht_ref[...] = jnp.zeros_like(sht_ref)
        sgdn_ref[...] = jnp.zeros_like(sgdn_ref)
        smc_ref[...] = jnp.zeros_like(smc_ref)
        vec_ref[...] = jnp.zeros_like(vec_ref)

    maskbd = maskbd_ref[...]
    maskbd_b = maskbd.astype(BF16)
    maskg = maskg_ref[...]
    maskg_b = maskg.astype(BF16)

    def seg_sum(x):
        return _dot_sel_r2(x, maskbd_b)

    lane = lax.broadcasted_iota(jnp.int32, (1, LANES), 1)
    in_lanes = lambda lo: (lane >= lo) & (lane < lo + N_HEADS)
    graw = proj_ref[:, COL_GATES:COL_GATES + LANES]
    xb = graw + gvec_ref[0:1, :]
    sp = _softplus(xb)
    neg_a = -jnp.exp(gvec_ref[1:2, :])
    logf = -_softplus(-xb)
    pre = jnp.where(lane < 2 * N_HEADS, neg_a * sp, jnp.where(in_lanes(G_MF), logf, 0.0))
    cum = _dot_sel(tri_ref[...], pre)
    imb = xb - pltpu.roll(cum, LANES - (G_MF - G_MI), axis=1)
    dtv = pltpu.roll(sp, G_DTV - G_DT, axis=1)
    gates = jnp.where(in_lanes(G_GB), _sigmoid(graw),
                      jnp.where(in_lanes(G_MI), imb, jnp.where(in_lanes(G_DTV), dtv, cum)))
    for j, lo in enumerate((G_DT, G_DTV, G_GA, G_GB, G_MF, G_MI)):
        x_ref[j] = _dot_sel_r(gates, expand_ref[:, j * GROUP_W:(j + 1) * GROUP_W])

    lane_g = lax.broadcasted_iota(jnp.int32, (1, GROUP_W), 1)
    first_half = (lane_g % HEAD_DIM) < (HEAD_DIM // 2)

    def rotary(x):
        swapped = jnp.where(first_half, pltpu.roll(x, GROUP_W - HEAD_DIM // 2, axis=1),
                            pltpu.roll(x, HEAD_DIM // 2, axis=1))
        return x * cos_ref[...] + swapped * sin_ref[...]

    rq_ref[...] = rotary(proj_ref[:, COL_RET:COL_RET + GROUP_W])
    rk_ref[...] = rotary(proj_ref[:, COL_RET + GROUP_W:COL_RET + 2 * GROUP_W]) * HEAD_DIM ** -0.5

    raw = proj_ref[:, COL_CONV:COL_CONV + CONV_CH]
    e_ref[SUBLANES:SUBLANES + rows, :] = raw
    conv = raw * convw_ref[CONV_W - 1:CONV_W, :]
    for s in range(1, CONV_W):
        conv = conv + e_ref[SUBLANES - s:SUBLANES - s + rows, :] * convw_ref[CONV_W - 1 - s:CONV_W - s, :]
    conv = _silu(conv + convb_ref[...])
    convraw_ref[...] = e_ref[rows:rows + SUBLANES, :]
    e_ref[0:SUBLANES, :] = e_ref[rows:rows + SUBLANES, :]
    sx_ref[...] = conv[:, 0:GROUP_W]
    sbc_ref[...] = conv[:, GROUP_W:SSM_CONV_CH]
    gq = conv[:, SSM_CONV_CH:SSM_CONV_CH + GROUP_W]
    gk = conv[:, SSM_CONV_CH + GROUP_W:SSM_CONV_CH + 2 * GROUP_W]
    gq_ref[...] = gq * lax.rsqrt(seg_sum(gq * gq) + EPS) * HEAD_DIM ** -0.5
    gk_ref[...] = gk * lax.rsqrt(seg_sum(gk * gk) + EPS)
    gv_ref[...] = conv[:, SSM_CONV_CH + 2 * GROUP_W:CONV_CH]

    r_i = lax.broadcasted_iota(jnp.int32, (chunk, GROUP_W), 0)
    c_i = lax.broadcasted_iota(jnp.int32, (chunk, GROUP_W), 1) % HEAD_DIM
    trim = r_i >= c_i
    diag = r_i == c_i
    strict = r_i > c_i
    eye = diag.astype(F32)
    head_of_lane = lax.broadcasted_iota(jnp.int32, (1, GROUP_W), 1) // HEAD_DIM
    carry_row = jnp.zeros((1, GROUP_W), F32)
    for h in range(N_HEADS):
        carry_row = jnp.where(head_of_lane == h, ret_carry[h], carry_row)
    maskht = maskht_ref[...]

    def bd(x):
        xb16 = x.astype(BF16)
        return jnp.concatenate([xb16] * N_HEADS, axis=0) * maskbd_b

    def last_row(x):
        return x[chunk - 1:chunk, :]

    def row_form(x):
        return jnp.sum(jnp.where(diag, x, 0.0), axis=0, keepdims=True)

    def seg_max(x):
        out = None
        for h in range(N_HEADS):
            sel = head_of_lane == h
            m_h = jnp.max(jnp.where(sel, x, -jnp.inf), axis=-1, keepdims=True)
            out = jnp.where(sel, m_h, 0.0) if out is None else jnp.where(sel, m_h, out)
        return out

    chunks = [slice(c * chunk, (c + 1) * chunk) for c in range(nchunks)]

    g_gam, g_beta, g_ecum, g_k, g_n, g_p = [], [], [], [], [], []
    for rs in chunks:
        ce = x_ref[X_CGDN, rs, :]
        gam = jnp.exp(jnp.where(trim, ce - row_form(ce), -jnp.inf))
        k_c = gk_ref[rs, :]
        beta = x_ref[X_BETA, rs, :]
        n0 = -(_dot_nt(k_c, bd(k_c)) * jnp.where(strict, gam, 0.0) * beta)
        g_gam.append(gam), g_beta.append(beta), g_ecum.append(jnp.exp(ce)), g_k.append(k_c)
        g_n.append(n0), g_p.append(eye + n0)
    for _ in range(max(int(math.ceil(math.log2(chunk))) - 1, 0)):
        for c in range(nchunks):
            g_n[c] = _dot(g_n[c], bd(g_n[c]))
        for c in range(nchunks):
            g_p[c] = g_p[c] + _dot(g_p[c], bd(g_n[c]))
    g_u, g_w = [], []
    for c, rs in enumerate(chunks):
        g_u.append(_dot(g_p[c], bd(gv_ref[rs, :] * g_beta[c])))
        g_w.append(_dot(g_p[c], bd(g_k[c] * (g_beta[c] * g_ecum[c]))))

    s_ret, s_ht, s_gdn, s_mc = sret_ref[...], sht_ref[...], sgdn_ref[...], smc_ref[...]
    n_row, m_row = vec_ref[0:1, :], vec_ref[1:2, :]
    for c, rs in enumerate(chunks):
        q_c, k_c = rq_ref[rs, :], rk_ref[rs, :]
        v_c = proj_ref[rs, COL_RET + 2 * GROUP_W:COL_RET + 3 * GROUP_W]
        att = _dot_nt(q_c, bd(k_c)) * rintra_ref[...]
        o_ret = _dot(att, bd(v_c)) + _dot(q_c * rinter_ref[...], s_ret)
        s_ret = s_ret * carry_row + _dot_tn(k_c * rtail_ref[...], v_c) * maskbd

        ce = x_ref[X_CSSM, rs, :]
        ce_last = last_row(ce)
        lmat = jnp.exp(jnp.where(trim, ce - row_form(ce), -jnp.inf))
        b_c, c_c = sbc_ref[rs, 0:2 * HEAD_DIM], sbc_ref[rs, 2 * HEAD_DIM:4 * HEAD_DIM]
        xdt = sx_ref[rs, :] * x_ref[X_DT, rs, :]
        rhs_b = jnp.concatenate([b_c.astype(BF16)] * N_HEADS, axis=0) * maskg_b
        scores = _dot_nt(c_c, rhs_b) * lmat
        o_ssm = _dot(scores, bd(xdt)) + _dot(c_c, s_ht) * jnp.exp(ce)
        s_ht = s_ht * jnp.exp(ce_last) + _dot_tn(b_c, xdt * jnp.exp(ce_last - ce)) * maskht

        ce = x_ref[X_CGDN, rs, :]
        ce_last = last_row(ce)
        q_c = gq_ref[rs, :]
        v_new = g_u[c] - _dot(g_w[c], s_gdn)
        qk = _dot_nt(q_c, bd(g_k[c])) * g_gam[c]
        o_gdn = _dot(q_c * g_ecum[c], s_gdn) + _dot(qk, bd(v_new))
        s_gdn = s_gdn * jnp.exp(ce_last) + _dot_tn(g_k[c] * jnp.exp(ce_last - ce), v_new) * maskbd

        q_c = proj_ref[rs, COL_ML:COL_ML + GROUP_W]
        k_c = proj_ref[rs, COL_ML + GROUP_W:COL_ML + 2 * GROUP_W] * HEAD_DIM ** -0.5
        v_c = proj_ref[rs, COL_ML + 2 * GROUP_W:COL_ML + 3 * GROUP_W]
        b_e, imb_e = x_ref[X_B, rs, :], x_ref[X_IMB, rs, :]
        dmat = jnp.where(trim, b_e + row_form(imb_e), -jnp.inf)
        m_t = jnp.maximum(b_e + m_row, seg_max(dmat))
        wts = jnp.exp(dmat - m_t)
        inter = jnp.exp(b_e + m_row - m_t)
        qk = _dot_nt(q_c, bd(k_c)) * wts
        num = _dot(qk, bd(v_c)) + inter * _dot(q_c, s_mc)
        den = seg_sum(qk) + inter * seg_sum(q_c * n_row)
        o_ml = num / jnp.maximum(jnp.abs(den), jnp.exp(-m_t))
        m_new = last_row(m_t)
        b_last = last_row(b_e)
        kw = k_c * jnp.exp(b_last + imb_e - m_new)
        cs = jnp.exp(b_last + m_row - m_new)
        s_mc = s_mc * cs + _dot_tn(kw, v_c) * maskbd
        n_row = n_row * cs + jnp.sum(kw, axis=0, keepdims=True)
        m_row = m_new

        o_ref[rs, :] = jnp.concatenate([o_ret, o_ssm, o_gdn, o_ml], axis=1)

    sret_ref[...] = s_ret
    sht_ref[...] = s_ht
    sgdn_ref[...] = s_gdn
    smc_ref[...] = s_mc
    vec_ref[0:1, :] = n_row
    vec_ref[1:2, :] = m_row

    def head_norm(x, g):
        return x * lax.rsqrt(seg_sum(x * x) * (1.0 / HEAD_DIM) + EPS) * g

    out_ret = head_norm(o_ref[:, 0:GROUP_W], norms_ref[0:1, :]) * _silu(
        proj_ref[:, COL_RET + 3 * GROUP_W:COL_RET + 4 * GROUP_W])
    mixed_ref[:, 0:GROUP_W] = out_ret.astype(BF16)
    y = o_ref[:, GROUP_W:2 * GROUP_W] + dvec_ref[...] * sx_ref[...]
    out_ssm = _rms(y * _silu(proj_ref[:, COL_SSM_Z:COL_SSM_Z + GROUP_W]), norms_ref[1:2, :])
    mixed_ref[:, GROUP_W:2 * GROUP_W] = out_ssm.astype(BF16)
    out_gdn = head_norm(o_ref[:, 2 * GROUP_W:3 * GROUP_W], norms_ref[2:3, :]) * _silu(
        proj_ref[:, COL_GDN_GATE:COL_GDN_GATE + GROUP_W])
    mixed_ref[:, 2 * GROUP_W:3 * GROUP_W] = out_gdn.astype(BF16)
    out_ml = _sigmoid(proj_ref[:, COL_ML + 3 * GROUP_W:COL_ML + 4 * GROUP_W]) * head_norm(
        o_ref[:, 3 * GROUP_W:4 * GROUP_W], norms_ref[3:4, :])
    mixed_ref[:, 3 * GROUP_W:4 * GROUP_W] = out_ml.astype(BF16)

    @pl.when(t_id == pl.num_programs(1) - 1)
    def _():
        for h in range(N_HEADS):
            hs = slice(h * HEAD_DIM, (h + 1) * HEAD_DIM)
            grp = h // (N_HEADS // 2)
            oret_ref[0, h] = s_ret[hs, hs]
            ogdn_ref[0, h] = s_gdn[hs, hs]
            omc_ref[0, h] = s_mc[hs, hs]
            ossm_ref[0, h] = s_ht[grp * HEAD_DIM:(grp + 1) * HEAD_DIM, hs]
        omn_ref[0] = jnp.concatenate([n_row[:, h * HEAD_DIM:(h + 1) * HEAD_DIM] for h in range(N_HEADS)], axis=0)
        omm_ref[0] = jnp.concatenate([m_row[:, h * HEAD_DIM:h * HEAD_DIM + 1] for h in range(N_HEADS)], axis=1)


def _prompt_consts(rows, chunk):
    lg = np.log(1.0 - np.exp2(-5.0 - np.arange(N_HEADS, dtype=np.float64)))
    idx = np.arange(chunk, dtype=np.float64)
    rel = idx[:, None] - idx[None, :]
    intra = np.where(rel[None] >= 0, np.exp(np.maximum(rel[None], 0.0) * lg[:, None, None]), 0.0)
    intra = np.concatenate(list(intra), axis=1)
    lanes = lambda t: np.repeat(t.T, HEAD_DIM, axis=1)
    inter = lanes(np.exp((idx[None, :] + 1.0) * lg[:, None]))
    tail = lanes(np.exp((chunk - 1.0 - idx[None, :]) * lg[:, None]))
    carry = tuple(float(c) for c in np.exp(chunk * lg))
    r = np.arange(rows)
    tri = (r[:, None] >= r[None, :]) & (r[:, None] // chunk == r[None, :] // chunk)
    expand = np.zeros((LANES, NEXP * GROUP_W), np.float32)
    for j, lo in enumerate((G_DT, G_DTV, G_GA, G_GB, G_MF, G_MI)):
        for h in range(N_HEADS):
            expand[lo + h, j * GROUP_W + h * HEAD_DIM:j * GROUP_W + (h + 1) * HEAD_DIM] = 1.0
    g = np.arange(GROUP_W)
    maskbd = (g[:, None] // HEAD_DIM == g[None, :] // HEAD_DIM)
    n = np.arange(2 * HEAD_DIM)
    maskg = (g[:, None] // (2 * HEAD_DIM) == n[None, :] // HEAD_DIM)
    f = lambda a, dt=F32: jnp.asarray(np.asarray(a, np.float32), dt)
    return (f(inter), f(tail), f(intra), f(tri, BF16), f(expand, BF16), f(maskbd), f(maskg), f(maskg.T)), carry


def _mix_prompt(proj, cos, sin, small, l, *, nbatch, seqlen):
    rows, chunk = min(PROMPT_TILE, seqlen), PROMPT_CHUNK
    ntile = seqlen // rows
    consts, carry = _prompt_consts(rows, chunk)
    m = proj.shape[0]
    row_map = lambda b, t: (b * ntile + t, 0)
    c2 = lambda b, t: (0, 0)
    full = lambda a: pl.BlockSpec(a.shape, c2)
    in_specs = [pl.BlockSpec((rows, IN_PAD), row_map),
                pl.BlockSpec((rows, GROUP_W), lambda b, t: (t, 0)), pl.BlockSpec((rows, GROUP_W), lambda b, t: (t, 0))]
    in_specs += [full(a) for a in consts]
    in_specs += [_layer_block(a.shape[1:], (l,), ngrid=2, buffered=False) for a in small]
    st4 = pl.BlockSpec((1, N_HEADS, HEAD_DIM, HEAD_DIM), lambda b, t: (b, 0, 0, 0))
    st3 = lambda k: pl.BlockSpec((1, k, (N_HEADS * HEAD_DIM) // k if k == N_HEADS else N_HEADS), lambda b, t: (b, 0, 0))
    out_shape = [jax.ShapeDtypeStruct((m, D_MODEL), BF16),
                 jax.ShapeDtypeStruct((nbatch * SUBLANES, CONV_CH), F32)]
    out_shape += [jax.ShapeDtypeStruct((nbatch, N_HEADS, HEAD_DIM, HEAD_DIM), F32) for _ in range(4)]
    out_shape += [jax.ShapeDtypeStruct((nbatch, N_HEADS, HEAD_DIM), F32),
                  jax.ShapeDtypeStruct((nbatch, 1, N_HEADS), F32)]
    out_specs = [pl.BlockSpec((rows, D_MODEL), row_map), pl.BlockSpec((SUBLANES, CONV_CH), lambda b, t: (b, 0)),
                 st4, st4, st4, st4, st3(N_HEADS), st3(1)]
    scratch = [pltpu.VMEM((rows + SUBLANES, CONV_CH), F32), pltpu.VMEM((NEXP, rows, GROUP_W), F32)]
    scratch += [pltpu.VMEM((rows, GROUP_W), F32) for _ in range(7)]
    scratch += [pltpu.VMEM((rows, D_MODEL), F32),
                pltpu.VMEM((GROUP_W, GROUP_W), F32), pltpu.VMEM((2 * HEAD_DIM, GROUP_W), F32),
                pltpu.VMEM((GROUP_W, GROUP_W), F32), pltpu.VMEM((GROUP_W, GROUP_W), F32),
                pltpu.VMEM((SUBLANES, GROUP_W), F32)]
    body = functools.partial(_mixp_body, rows=rows, chunk=chunk, ret_carry=carry)
    outs = pl.pallas_call(
        body, out_shape=out_shape, grid=(nbatch, ntile), in_specs=in_specs, out_specs=out_specs,
        scratch_shapes=scratch,
        compiler_params=pltpu.CompilerParams(dimension_semantics=("parallel", "arbitrary"),
                                             vmem_limit_bytes=VMEM_LIMIT),
        name="mix_prompt",
    )(proj, cos, sin, *consts, *small)
    outs = list(outs)
    outs[3] = jnp.swapaxes(outs[3], -1, -2)
    return outs


def _ret_tables(chunk, pad):
    lg = np.log(1.0 - np.exp2(-5.0 - np.arange(N_HEADS, dtype=np.float64)))
    idx = np.arange(chunk, dtype=np.float64)
    rel = idx[:, None] - idx[None, :]
    intra = np.where(rel[None] >= 0, np.exp(np.maximum(rel[None], 0.0) * lg[:, None, None]), 0.0)
    inter = np.where(idx[None, :] >= pad, np.exp((idx[None, :] - pad + 1.0) * lg[:, None]), 0.0)
    tail = np.exp((chunk - 1.0 - idx[None, :]) * lg[:, None])
    carry = np.exp((chunk - pad) * lg)
    lanes = lambda t: np.repeat(t.T, HEAD_DIM, axis=1)
    return (jnp.asarray(lanes(inter), F32), jnp.asarray(lanes(tail), F32), jnp.asarray(intra, F32),
            tuple(float(c) for c in carry))


def _const_mats(rows, chunk):
    r = np.arange(rows)
    tri = ((r[:, None] >= r[None, :]) & (r[:, None] // chunk == r[None, :] // chunk))
    sel = np.zeros((4 * SUBLANES, LANES), np.float32)
    sel[np.arange(24), np.arange(24)] = 1.0
    g = np.arange(GROUP_W)
    ones_bd = (g[:, None] // HEAD_DIM == g[None, :] // HEAD_DIM)
    return (jnp.asarray(tri, BF16), jnp.asarray(sel, BF16), jnp.asarray(ones_bd, BF16))


def _mix_sample(proj, hist, cos, sin, small, states, l, *, nbatch):
    rows, chunk, pad = SAMPLE_GROUP * SAMPLE_PAD, SAMPLE_PAD, SAMPLE_PAD - 4
    row_map = lambda i: (i, 0)
    c2 = lambda i: (0, 0)
    rinter, rtail, rintra, carry = _ret_tables(chunk, pad)
    tri, sel, ones_bd = _const_mats(rows, chunk)
    m = proj.shape[0]
    in_specs = [pl.BlockSpec((rows, IN_PAD), row_map),
                pl.BlockSpec((None, rows, CONV_CH), lambda i: (l, i, 0)),
                pl.BlockSpec((rows, GROUP_W), c2), pl.BlockSpec((rows, GROUP_W), c2),
                pl.BlockSpec((chunk, GROUP_W), c2), pl.BlockSpec((chunk, GROUP_W), c2),
                pl.BlockSpec((N_HEADS, chunk, chunk), lambda i: (0, 0, 0)),
                pl.BlockSpec((rows, rows), c2), pl.BlockSpec((4 * SUBLANES, LANES), c2),
                pl.BlockSpec((GROUP_W, GROUP_W), c2)]
    in_specs += [_layer_block(a.shape[1:], (l,), buffered=False) for a in small]
    g = SAMPLE_GROUP
    in_specs += [pl.BlockSpec((None, g, N_HEADS, HEAD_DIM, HEAD_DIM), lambda i: (l, i, 0, 0, 0))] * 4
    in_specs += [pl.BlockSpec((None, g, N_HEADS, HEAD_DIM), lambda i: (l, i, 0, 0)),
                 pl.BlockSpec((None, g, 1, N_HEADS), lambda i: (l, i, 0, 0))]
    st4 = pl.BlockSpec((g, N_HEADS, HEAD_DIM, HEAD_DIM), lambda i: (i, 0, 0, 0))
    out_specs = [pl.BlockSpec((rows, D_MODEL), row_map), pl.BlockSpec((rows, CONV_CH), row_map),
                 st4, st4, st4, st4,
                 pl.BlockSpec((g, N_HEADS, HEAD_DIM), lambda i: (i, 0, 0)),
                 pl.BlockSpec((g, 1, N_HEADS), lambda i: (i, 0, 0))]
    out_shape = [jax.ShapeDtypeStruct((m, D_MODEL), BF16), jax.ShapeDtypeStruct((m, CONV_CH), F32)]
    out_shape += [jax.ShapeDtypeStruct((nbatch, N_HEADS, HEAD_DIM, HEAD_DIM), F32) for _ in range(4)]
    out_shape += [jax.ShapeDtypeStruct((nbatch, N_HEADS, HEAD_DIM), F32),
                  jax.ShapeDtypeStruct((nbatch, 1, N_HEADS), F32)]
    scratch = [pltpu.VMEM((rows + SUBLANES, CONV_CH), F32), pltpu.VMEM((rows, LANES), F32)]
    scratch += [pltpu.VMEM((rows, GROUP_W), F32) for _ in range(7)]
    scratch += [pltpu.VMEM((rows, D_MODEL), F32)]
    body = functools.partial(_mix_body, sample=True, rows=rows, chunk=chunk, ret_carry=carry)
    return pl.pallas_call(
        body, out_shape=out_shape, grid=(nbatch // g,), in_specs=in_specs, out_specs=out_specs,
        scratch_shapes=scratch,
        compiler_params=pltpu.CompilerParams(dimension_semantics=("parallel",), vmem_limit_bytes=VMEM_LIMIT),
        name="mix_sample",
    )(proj, hist, cos, sin, rinter, rtail, rintra, tri, sel, ones_bd, *small, *states)


S_RQ, S_RK, S_RV, S_XDT, S_B, S_C, S_GQ, S_GK, S_GV, S_MQ, S_MK, S_MV, S_GATE = (
    HEAD_DIM * i for i in range(13))
SLAB_ROWS = S_GATE + SUBLANES
GT_SSM_A, GT_GDN_A, GT_BETA, GT_LOGI, GT_LOGF, GT_GAMMA = range(6)
P_RG, P_Z, P_XS, P_GG, P_MO = (GROUP_W * i for i in range(5))
POST_ROWS = 5 * GROUP_W
C_CONVW, C_CONVB = 0, CONV_W * CONV_CH
C_BIAS = C_CONVB + CONV_CH
C_ALOG = C_BIAS + 4 * SUBLANES
C_GAMMA = C_ALOG + 4 * SUBLANES
C_ROWS = C_GAMMA + SUBLANES
N_ROWS = 5 * GROUP_W
N_TOK = 4


def _spre_body(x_ref, g_ref, w_ref, hist_ref, rot_ref, colp_ref, slab_ref, post_ref, convraw_ref):
    nb = LANES
    h_in = _rms(x_ref[...], g_ref[2:3, :]).astype(BF16)
    proj = jnp.dot(h_in, w_ref[...], preferred_element_type=F32)
    for j in range(CONV_W - 1):
        convraw_ref[j] = proj[(j + 1) * nb:(j + 2) * nb, COL_CONV:COL_CONV + CONV_CH]
    pt = [proj[t * nb:(t + 1) * nb, :].T for t in range(N_TOK)]
    src = [hist_ref[j].T for j in range(CONV_W - 1)] + [p[COL_CONV:COL_CONV + CONV_CH] for p in pt]
    bias = colp_ref[C_BIAS:C_BIAS + 4 * SUBLANES, :]
    neg_a = -jnp.exp(colp_ref[C_ALOG:C_ALOG + 4 * SUBLANES, :])
    gamma = colp_ref[C_GAMMA:C_GAMMA + SUBLANES, :]
    zero_row = jnp.zeros((1, LANES), F32)

    def rotary(x, t, k):
        swapped = jnp.concatenate(
            [x[h * HEAD_DIM + off:h * HEAD_DIM + off + HEAD_DIM // 2]
             for h in range(N_HEADS) for off in (HEAD_DIM // 2, 0)], axis=0)
        return x * rot_ref[0, t] + swapped * rot_ref[1, t]

    for t in range(N_TOK):
        p = pt[t]
        g = p[COL_GATES:COL_GATES + 4 * SUBLANES]
        xb = g + bias
        sp = _softplus(xb)
        decay = jnp.exp(neg_a * sp)
        beta = _sigmoid(g)
        logf = -_softplus(-xb)
        conv = src[t] * colp_ref[C_CONVW:C_CONVW + CONV_CH, :]
        for j in range(1, CONV_W):
            conv = conv + src[t + j] * colp_ref[C_CONVW + j * CONV_CH:C_CONVW + (j + 1) * CONV_CH, :]
        cv = _silu(conv + colp_ref[C_CONVB:C_CONVB + CONV_CH, :])
        rq = rotary(p[COL_RET:COL_RET + GROUP_W], t, 0)
        rk = rotary(p[COL_RET + GROUP_W:COL_RET + 2 * GROUP_W], t, 1) * HEAD_DIM ** -0.5
        post_ref[t, P_RG:P_RG + GROUP_W, :] = p[COL_RET + 3 * GROUP_W:COL_RET + 4 * GROUP_W]
        post_ref[t, P_Z:P_Z + GROUP_W, :] = p[COL_SSM_Z:COL_SSM_Z + GROUP_W]
        post_ref[t, P_XS:P_XS + GROUP_W, :] = cv[0:GROUP_W]
        post_ref[t, P_GG:P_GG + GROUP_W, :] = p[COL_GDN_GATE:COL_GDN_GATE + GROUP_W]
        post_ref[t, P_MO:P_MO + GROUP_W, :] = p[COL_ML + 3 * GROUP_W:COL_ML + 4 * GROUP_W]
        for h in range(N_HEADS):
            hs = slice(h * HEAD_DIM, (h + 1) * HEAD_DIM)
            grp = h // (N_HEADS // 2)
            put = lambda r, v: slab_ref.__setitem__((h, t, slice(r, r + v.shape[0]), slice(None)), v)
            l2 = lambda v: v * lax.rsqrt(jnp.sum(v * v, axis=0, keepdims=True) + EPS)
            put(S_RQ, rq[hs])
            put(S_RK, rk[hs])
            put(S_RV, p[COL_RET + 2 * GROUP_W + h * HEAD_DIM:COL_RET + 2 * GROUP_W + (h + 1) * HEAD_DIM])
            put(S_XDT, cv[hs] * sp[G_DT + h:G_DT + h + 1])
            put(S_B, cv[GROUP_W + grp * HEAD_DIM:GROUP_W + (grp + 1) * HEAD_DIM])
            put(S_C, cv[GROUP_W + 2 * HEAD_DIM + grp * HEAD_DIM:GROUP_W + 2 * HEAD_DIM + (grp + 1) * HEAD_DIM])
            put(S_GQ, l2(cv[SSM_CONV_CH + h * HEAD_DIM:SSM_CONV_CH + (h + 1) * HEAD_DIM]) * HEAD_DIM ** -0.5)
            put(S_GK, l2(cv[SSM_CONV_CH + GROUP_W + h * HEAD_DIM:SSM_CONV_CH + GROUP_W + (h + 1) * HEAD_DIM]))
            put(S_GV, cv[SSM_CONV_CH + 2 * GROUP_W + h * HEAD_DIM:SSM_CONV_CH + 2 * GROUP_W + (h + 1) * HEAD_DIM])
            put(S_MQ, p[COL_ML + h * HEAD_DIM:COL_ML + (h + 1) * HEAD_DIM])
            put(S_MK, p[COL_ML + GROUP_W + h * HEAD_DIM:COL_ML + GROUP_W + (h + 1) * HEAD_DIM] * HEAD_DIM ** -0.5)
            put(S_MV, p[COL_ML + 2 * GROUP_W + h * HEAD_DIM:COL_ML + 2 * GROUP_W + (h + 1) * HEAD_DIM])
            row = lambda a, j: a[j + h:j + h + 1]
            put(S_GATE, jnp.concatenate([row(decay, G_DT), row(decay, G_GA), row(beta, G_GB), row(xb, G_MI),
                                         row(logf, G_MF), gamma[h:h + 1], zero_row, zero_row], axis=0))


def _spre(x, sandwich, w, hist, rot, colp, l):
    m = x.shape[0]
    return pl.pallas_call(
        _spre_body,
        out_shape=[jax.ShapeDtypeStruct((N_HEADS, N_TOK, SLAB_ROWS, LANES), F32),
                   jax.ShapeDtypeStruct((N_TOK, POST_ROWS, LANES), F32),
                   jax.ShapeDtypeStruct((CONV_W - 1, LANES, CONV_CH), F32)],
        grid=(1,),
        in_specs=[pl.BlockSpec((m, D_MODEL), lambda i: (0, 0)),
                  _layer_block((N_SANDWICH, D_MODEL), (l,), buffered=False),
                  _layer_block((D_MODEL, IN_PAD), (l,), buffered=False),
                  _layer_block((CONV_W - 1, LANES, CONV_CH), (l,), buffered=False),
                  pl.BlockSpec(rot.shape, lambda i: (0, 0, 0, 0)),
                  _layer_block((C_ROWS, LANES), (l,), buffered=False)],
        out_specs=[pl.BlockSpec((N_HEADS, N_TOK, SLAB_ROWS, LANES), lambda i: (0, 0, 0, 0)),
                   pl.BlockSpec((N_TOK, POST_ROWS, LANES), lambda i: (0, 0, 0)),
                   pl.BlockSpec((CONV_W - 1, LANES, CONV_CH), lambda i: (0, 0, 0))],
        compiler_params=pltpu.CompilerParams(dimension_semantics=("arbitrary",), vmem_limit_bytes=VMEM_LIMIT),
        name="spre",
    )(x, sandwich, w, hist, rot, colp)


def _srec_body(*refs, aliased):
    slab_ref, sret_in, sssm_in, sgdn_in, smc_in, smn_in, smm_in = refs[:7]
    outs = refs[7 + (6 if aliased else 0):]
    ot_ref, sret_out, sssm_out, sgdn_out, smc_out, smn_out, smm_out, y8_ref = outs
    tile = lambda t, r: slab_ref[t, r:r + HEAD_DIM, :]
    gate = lambda t, j: slab_ref[t, S_GATE + j:S_GATE + j + 1, :]
    drow = lambda t, r, d: slab_ref[t, pl.ds(r + d, 1), :]
    zeros = jnp.zeros((HEAD_DIM, LANES), F32)
    sweep = lambda body, init: lax.fori_loop(0, HEAD_DIM, body, init, unroll=8)
    blk = lambda d: pl.ds(pl.multiple_of(d * HEAD_DIM, HEAD_DIM), HEAD_DIM)

    gamma = gate(0, GT_GAMMA)
    for t in range(N_TOK):
        src = sret_in if t == 0 else sret_out
        v = tile(t, S_RV)

        def body(d, acc, t=t, src=src, v=v):
            s = src[blk(d), :] * gamma + drow(t, S_RK, d) * v
            sret_out[blk(d), :] = s
            return acc + drow(t, S_RQ, d) * s
        ot_ref[t, 0:HEAD_DIM, :] = sweep(body, zeros)

    for t in range(N_TOK):
        src = sssm_in if t == 0 else sssm_out
        a, b_t, c_t = gate(t, GT_SSM_A), tile(t, S_B), tile(t, S_C)

        def body(p, carry, t=t, src=src, a=a, b_t=b_t, c_t=c_t):
            hp = src[blk(p), :] * a + drow(t, S_XDT, p) * b_t
            sssm_out[blk(p), :] = hp
            part = jnp.sum((hp * c_t).reshape(HEAD_DIM // SUBLANES, SUBLANES, LANES), axis=0)
            y8_ref[pl.ds(pl.multiple_of(p * SUBLANES, SUBLANES), SUBLANES), :] = part
            return carry
        sweep(body, 0)
        ot_ref[t, HEAD_DIM:2 * HEAD_DIM, :] = jnp.sum(y8_ref[...].reshape(HEAD_DIM, SUBLANES, LANES), axis=1)

    ks = sweep(lambda d, acc: acc + drow(0, S_GK, d) * sgdn_in[blk(d), :], zeros)
    for t in range(N_TOK):
        src = sgdn_in if t == 0 else sgdn_out
        a = gate(t, GT_GDN_A)
        u = gate(t, GT_BETA) * (tile(t, S_GV) - a * ks)
        nxt = t + 1 < N_TOK

        def body(d, acc, t=t, src=src, a=a, u=u, nxt=nxt):
            s = src[blk(d), :] * a + drow(t, S_GK, d) * u
            sgdn_out[blk(d), :] = s
            o = acc[0] + drow(t, S_GQ, d) * s
            return (o, acc[1] + drow(t + 1, S_GK, d) * s) if nxt else (o, acc[1])
        o, ks = sweep(body, (zeros, zeros))
        ot_ref[t, 2 * HEAD_DIM:3 * HEAD_DIM, :] = o

    m_row, n_tile = smm_in[...], smn_in[...]
    for t in range(N_TOK):
        src = smc_in if t == 0 else smc_out
        logi, logf = gate(t, GT_LOGI), gate(t, GT_LOGF)
        m_new = jnp.maximum(logf + m_row, logi)
        fs, ins = jnp.exp(logf + m_row - m_new), jnp.exp(logi - m_new)
        n_tile = fs * n_tile + ins * tile(t, S_MK)
        qn = jnp.sum(tile(t, S_MQ) * n_tile, axis=0, keepdims=True)
        v = tile(t, S_MV) * ins

        def body(d, acc, t=t, src=src, fs=fs, v=v):
            c = src[blk(d), :] * fs + drow(t, S_MK, d) * v
            smc_out[blk(d), :] = c
            return acc + drow(t, S_MQ, d) * c
        num = sweep(body, zeros)
        ot_ref[t, 3 * HEAD_DIM:4 * HEAD_DIM, :] = num / jnp.maximum(jnp.abs(qn), jnp.exp(-m_new))
        m_row = m_new
    smn_out[...] = n_tile
    smm_out[...] = m_row


def _srec(slab, states, prev, l):
    depth = states[0].shape[0]
    rows = HEAD_DIM * HEAD_DIM
    st_shapes = [(rows, LANES)] * 4 + [(HEAD_DIM, LANES), (1, LANES)]
    lh = lambda tail: pl.BlockSpec((None, None) + tail, lambda h: (l, h, 0, 0))
    in_specs = [pl.BlockSpec((None, N_TOK, SLAB_ROWS, LANES), lambda h: (h, 0, 0, 0))]
    in_specs += [lh(s) for s in st_shapes]
    args = [slab, *states]
    aliases = {}
    if prev is not None:
        in_specs += [pl.BlockSpec(memory_space=pl.ANY)] * 6
        args += list(prev)
        aliases = {7 + i: 1 + i for i in range(6)}
    out_shape = [jax.ShapeDtypeStruct((N_HEADS, N_TOK, GROUP_W, LANES), F32)]
    out_shape += [jax.ShapeDtypeStruct((depth, N_HEADS) + s, F32) for s in st_shapes]
    out_specs = [pl.BlockSpec((None, N_TOK, GROUP_W, LANES), lambda h: (h, 0, 0, 0))] + [lh(s) for s in st_shapes]
    return pl.pallas_call(
        functools.partial(_srec_body, aliased=prev is not None),
        out_shape=out_shape, grid=(N_HEADS,), in_specs=in_specs, out_specs=out_specs,
        input_output_aliases=aliases,
        scratch_shapes=[pltpu.VMEM((HEAD_DIM * SUBLANES, LANES), F32)],
        compiler_params=pltpu.CompilerParams(dimension_semantics=("arbitrary",), vmem_limit_bytes=VMEM_LIMIT),
        name="srec",
    )(*args)


def _spost_body(x_ref, ot_ref, post_ref, cn_ref, g_ref, w_ref, o_ref):
    nb = LANES

    def head_norm(x):
        out = []
        for h in range(N_HEADS):
            b = x[h * HEAD_DIM:(h + 1) * HEAD_DIM]
            out.append(b * lax.rsqrt(jnp.mean(b * b, axis=0, keepdims=True) + EPS))
        return jnp.concatenate(out, axis=0)

    gain = lambda i: cn_ref[i * GROUP_W:(i + 1) * GROUP_W, :]
    for t in range(N_TOK):
        mixer = lambda i: jnp.concatenate(
            [ot_ref[h, t, i * HEAD_DIM:(i + 1) * HEAD_DIM, :] for h in range(N_HEADS)], axis=0)
        pin = lambda r: post_ref[t, r:r + GROUP_W, :]
        out_ret = head_norm(mixer(0)) * gain(0) * _silu(pin(P_RG))
        y = (mixer(1) + gain(4) * pin(P_XS)) * _silu(pin(P_Z))
        out_ssm = y * lax.rsqrt(jnp.mean(y * y, axis=0, keepdims=True) + EPS) * gain(1)
        out_gdn = head_norm(mixer(2)) * gain(2) * _silu(pin(P_GG))
        out_ml = _sigmoid(pin(P_MO)) * (head_norm(mixer(3)) * gain(3))
        mixed = jnp.concatenate([out_ret, out_ssm, out_gdn, out_ml], axis=0).T.astype(BF16)
        y = jnp.dot(mixed, w_ref[...], preferred_element_type=F32)
        rs = slice(t * nb, (t + 1) * nb)
        o_ref[rs, :] = x_ref[rs, :] + _rms(y, g_ref[3:4, :])


def _spost(x, ot, post, cn, sandwich, w, l):
    m = x.shape[0]
    return pl.pallas_call(
        _spost_body,
        out_shape=jax.ShapeDtypeStruct((m, D_MODEL), F32),
        grid=(1,),
        in_specs=[pl.BlockSpec((m, D_MODEL), lambda i: (0, 0)),
                  pl.BlockSpec(ot.shape, lambda i: (0, 0, 0, 0)),
                  pl.BlockSpec(post.shape, lambda i: (0, 0, 0)),
                  _layer_block((N_ROWS, LANES), (l,), buffered=False),
                  _layer_block((N_SANDWICH, D_MODEL), (l,), buffered=False),
                  _layer_block((D_MODEL, D_MODEL), (l,), buffered=False)],
        out_specs=pl.BlockSpec((m, D_MODEL), lambda i: (0, 0)),
        compiler_params=pltpu.CompilerParams(dimension_semantics=("arbitrary",), vmem_limit_bytes=VMEM_LIMIT),
        name="spost",
    )(x, ot, post, cn, sandwich, w)


def _rot_tables(pos):
    half = HEAD_DIM // 2
    inv = ROPE_BASE ** (-jnp.arange(half, dtype=F32) / half)
    ang = pos.astype(F32)[:, None] * inv[None, :]
    cos, sin = jnp.cos(ang), jnp.sin(ang)
    cos_full = jnp.tile(jnp.concatenate([cos, cos], axis=-1), (1, N_HEADS))
    sin_signed = jnp.tile(jnp.concatenate([-sin, sin], axis=-1), (1, N_HEADS))
    return cos_full, sin_signed


def _small_params(ret_norm, ssm_conv_w, ssm_conv_b, ssm_dt_bias, ssm_A_log, ssm_D, ssm_norm, gdn_conv_w,
                  gdn_dt_bias, gdn_A_log, gdn_norm, mlstm_i_bias, mlstm_f_bias, mlstm_norm):
    depth = ret_norm.shape[0]
    z = lambda n: jnp.zeros((depth, n), F32)
    pad_lanes = z(LANES - 5 * N_HEADS)
    bias = jnp.concatenate([ssm_dt_bias, gdn_dt_bias, z(N_HEADS), mlstm_i_bias, mlstm_f_bias, pad_lanes], axis=1)
    alog = jnp.concatenate([ssm_A_log, gdn_A_log, z(3 * N_HEADS), pad_lanes], axis=1)
    gvec = jnp.stack([bias, alog], axis=1)
    convw = jnp.concatenate([ssm_conv_w, gdn_conv_w], axis=2)
    convb = jnp.concatenate([ssm_conv_b, z(CONV_CH - SSM_CONV_CH)], axis=1)[:, None, :]
    dvec = jnp.repeat(ssm_D, HEAD_DIM, axis=1)[:, None, :]
    norms = jnp.stack([ret_norm, ssm_norm, gdn_norm, mlstm_norm], axis=1)
    return gvec, convw, convb, dvec, norms


def kernel(x_prompt, x_sample, state_ret, state_ssm, state_ssm_conv, state_gdn, state_gdn_conv,
           state_mlstm_C, state_mlstm_n, state_mlstm_m, w_in, w_out, norm_sandwich, ffn_w_gate, ffn_w_up,
           ffn_w_down, ret_norm, ssm_conv_w, ssm_conv_b, ssm_dt_bias, ssm_A_log, ssm_D, ssm_norm, gdn_conv_w,
           gdn_dt_bias, gdn_A_log, gdn_norm, mlstm_i_bias, mlstm_f_bias, mlstm_norm):
    depth = w_in.shape[0]
    bp, lp, _ = x_prompt.shape
    bs, ls, _ = x_sample.shape
    assert (bs, ls) == (LANES, N_TOK)
    w_in_p = _prep_win(w_in)
    w_out_b = w_out.astype(BF16)
    wg, wu, wd = ffn_w_gate, ffn_w_up, ffn_w_down.astype(BF16)
    small = _small_params(ret_norm, ssm_conv_w, ssm_conv_b, ssm_dt_bias, ssm_A_log, ssm_D, ssm_norm, gdn_conv_w,
                          gdn_dt_bias, gdn_A_log, gdn_norm, mlstm_i_bias, mlstm_f_bias, mlstm_norm)
    gvec, convw, convb, dvec, norms = small

    cos_p, sin_p = _rot_tables(jnp.arange(lp, dtype=jnp.int32))
    x = x_prompt.reshape(bp * lp, D_MODEL)
    st_p = []
    for l in range(depth):
        x = _ffn(x, norm_sandwich, wg, wu, wd, l, 0)
        outs = _mix_prompt(_inproj(x, norm_sandwich, w_in_p, l), cos_p, sin_p, small, l, nbatch=bp, seqlen=lp)
        x = _ffn(x, norm_sandwich, wg, wu, wd, l, 1, mixed=outs[0], w_out=w_out_b)
        st_p.append(outs[1:])
    y_prompt = x.reshape(bp, lp, D_MODEL)
    conv_p = jnp.stack([s[0].reshape(bp, -1, CONV_CH)[:, -(CONV_W - 1):, :] for s in st_p])
    stack_p = lambda k: jnp.stack([s[k] for s in st_p])
    out_p = (stack_p(1), stack_p(2), conv_p[..., :SSM_CONV_CH], stack_p(3), conv_p[..., SSM_CONV_CH:],
             stack_p(4), stack_p(5), stack_p(6)[:, :, 0, :])

    lanes = lambda a: jnp.broadcast_to(a[..., None], a.shape + (LANES,))
    cos_s, sin_s = _rot_tables(PAST_LEN + jnp.arange(ls, dtype=jnp.int32))
    rot = lanes(jnp.stack([cos_s, sin_s]))
    lg = np.log(1.0 - np.exp2(-5.0 - np.arange(N_HEADS, dtype=np.float64)))
    gamma = jnp.asarray(np.concatenate([np.exp(lg), np.zeros(SUBLANES - N_HEADS)]), F32)
    colp = lanes(jnp.concatenate(
        [convw.reshape(depth, CONV_W * CONV_CH), convb[:, 0, :], gvec[:, 0, :4 * SUBLANES], gvec[:, 1, :4 * SUBLANES],
         jnp.broadcast_to(gamma, (depth, SUBLANES))], axis=1))
    cn = lanes(jnp.concatenate([norms.reshape(depth, 4 * GROUP_W), dvec[:, 0, :]], axis=1))
    hist = jnp.transpose(jnp.concatenate([state_ssm_conv, state_gdn_conv], axis=-1), (0, 2, 1, 3))
    mat = lambda s: jnp.transpose(s, (0, 2, 3, 4, 1)).reshape(depth, N_HEADS, HEAD_DIM * HEAD_DIM, bs)
    states_s = (mat(state_ret), mat(state_ssm), mat(state_gdn), mat(state_mlstm_C),
                jnp.transpose(state_mlstm_n, (0, 2, 3, 1)), jnp.transpose(state_mlstm_m, (0, 2, 1))[:, :, None, :])
    x = jnp.transpose(x_sample, (1, 0, 2)).reshape(ls * bs, D_MODEL)
    prev, convs = None, []
    for l in range(depth):
        x = _ffn(x, norm_sandwich, wg, wu, wd, l, 0)
        slab, post, convraw = _spre(x, norm_sandwich, w_in_p, hist, rot, colp, l)
        outs = _srec(slab, states_s, prev, l)
        prev = outs[1:]
        x = _spost(x, outs[0], post, cn, norm_sandwich, w_out_b, l)
        x = _ffn(x, norm_sandwich, wg, wu, wd, l, 1)
        convs.append(convraw)
    y_sample = jnp.transpose(x.reshape(ls, bs, D_MODEL), (1, 0, 2))
    conv_s = jnp.transpose(jnp.stack(convs), (0, 2, 1, 3))
    unmat = lambda s: jnp.transpose(s.reshape(depth, N_HEADS, HEAD_DIM, HEAD_DIM, bs), (0, 4, 1, 2, 3))
    out_s = (unmat(prev[0]), unmat(prev[1]), conv_s[..., :SSM_CONV_CH], unmat(prev[2]), conv_s[..., SSM_CONV_CH:],
             unmat(prev[3]), jnp.transpose(prev[4], (0, 3, 1, 2)), jnp.transpose(prev[5][:, :, 0, :], (0, 2, 1)))
    return (y_prompt, y_sample) + out_p + out_s
```

```python
import functools
import math

import numpy as np
import jax
import jax.numpy as jnp
from jax import lax
from jax.experimental import pallas as pl
from jax.experimental.pallas import tpu as pltpu

F32 = jnp.float32
BF16 = jnp.bfloat16

D_MODEL = 1024
N_HEADS = 4
HEAD_DIM = 64
GROUP_W = N_HEADS * HEAD_DIM
D_FF = 2816
CONV_W = 4
EPS = 1e-6
ROPE_BASE = 10000.0
PAST_LEN = 16384
PROMPT_CHUNK = 64

COL_RET = 0
COL_SSM_Z = 1024
COL_CONV = 1280
CONV_CH = 1280
SSM_CONV_CH = 512
COL_GDN_GATE = 2560
COL_ML = 2816
COL_GATES = 3840
IN_PAD = 3968
G_DT, G_GA, G_GB, G_MI, G_MF, G_DTV = 0, 4, 8, 12, 16, 20

SUBLANES = 8
LANES = 128
VMEM_LIMIT = 56 * 1024 * 1024

FF_CHUNK = 256
TOKEN_TILE = 512
PROMPT_TILE = 256
SAMPLE_PAD = 8
SAMPLE_GROUP = 8
NEG_BIG = -1e30


def _sigmoid(x):
    return 1.0 / (1.0 + jnp.exp(-x))


def _silu(x):
    return x * _sigmoid(x)


def _softplus(x):
    return jnp.maximum(x, 0.0) + jnp.log1p(jnp.exp(-jnp.abs(x)))


def _rms(x, g):
    return x * lax.rsqrt(jnp.mean(x * x, axis=-1, keepdims=True) + EPS) * g


def _dot(a, b):
    return jnp.dot(a.astype(BF16), b.astype(BF16), preferred_element_type=F32)


def _dot_nt(a, b):
    return lax.dot_general(a.astype(BF16), b.astype(BF16), (((1,), (1,)), ((), ())),
                           preferred_element_type=F32)


def _dot_tn(a, b):
    return lax.dot_general(a.astype(BF16), b.astype(BF16), (((0,), (0,)), ((), ())),
                           preferred_element_type=F32)


def _split3(x):
    hi = x.astype(BF16)
    r = x - hi.astype(F32)
    mid = r.astype(BF16)
    lo = (r - mid.astype(F32)).astype(BF16)
    return hi, mid, lo


def _dot_sel(sel, x):
    hi, mid, lo = _split3(x)
    f = lambda p: jnp.dot(sel, p, preferred_element_type=F32)
    return (f(lo) + f(mid)) + f(hi)


def _dot_sel_r(x, sel):
    hi, mid, lo = _split3(x)
    f = lambda p: jnp.dot(p, sel, preferred_element_type=F32)
    return (f(lo) + f(mid)) + f(hi)


def _dot_sel_nt(sel, x):
    hi, mid, lo = _split3(x)
    f = lambda p: lax.dot_general(sel, p, (((1,), (1,)), ((), ())), preferred_element_type=F32)
    return (f(lo) + f(mid)) + f(hi)


N_SANDWICH = 6


def _layer_block(tail, lead, ngrid=1, buffered=True):
    idx = tuple(lead) + (0,) * len(tail)
    imap = (lambda i: idx) if ngrid == 1 else (lambda i, j: idx)
    kw = dict(pipeline_mode=pl.Buffered(1)) if buffered else {}
    return pl.BlockSpec((None,) * len(lead) + tuple(tail), imap, **kw)


WIN_SEGMENTS = ((0, 0, 1792), (1792, 1796, 1024), (2816, 2828, 1024))
WIN_GATES = ((1792, 1796), (2820, 2828), (3852, 3860))


def _prep_win_body(w_ref, o_ref):
    l = pl.program_id(0)
    step = 2 * LANES
    for dst, src, n in WIN_SEGMENTS:
        for c in range(0, n, step):
            o_ref[:, dst + c:dst + c + step] = w_ref[src + c:src + c + step, l, :].T.astype(BF16)
    gates = jnp.concatenate([w_ref[a:b, l, :] for a, b in WIN_GATES]
                            + [jnp.zeros((LANES - 5 * N_HEADS, D_MODEL), F32)], axis=0)
    o_ref[:, COL_GATES:IN_PAD] = gates.T.astype(BF16)


def _prep_win(w_in):
    depth, _, in_dim = w_in.shape
    w_t = jnp.transpose(w_in, (2, 0, 1))
    return pl.pallas_call(
        _prep_win_body,
        out_shape=jax.ShapeDtypeStruct((depth, D_MODEL, IN_PAD), BF16),
        grid=(depth,),
        in_specs=[pl.BlockSpec((in_dim, depth, D_MODEL), lambda l: (0, 0, 0), pipeline_mode=pl.Buffered(1))],
        out_specs=pl.BlockSpec((None, D_MODEL, IN_PAD), lambda l: (l, 0, 0)),
        compiler_params=pltpu.CompilerParams(dimension_semantics=("arbitrary",), vmem_limit_bytes=VMEM_LIMIT),
        name="prep_win",
    )(w_t)


def _ffn_body(*refs, k, mixer_out):
    if mixer_out:
        x_ref, m_ref, wo_ref, g_ref, wg_ref, wu_ref, wd_ref, o_ref, a_ref = refs
        y = jnp.dot(m_ref[...], wo_ref[...], preferred_element_type=F32)
        x = x_ref[...] + _rms(y, g_ref[3:4, :])
    else:
        x_ref, g_ref, wg_ref, wu_ref, wd_ref, o_ref, a_ref = refs
        x = x_ref[...]
    h = _rms(x, g_ref[4 * k:4 * k + 1, :]).astype(BF16)
    for j in range(D_FF // FF_CHUNK):
        sl = slice(j * FF_CHUNK, (j + 1) * FF_CHUNK)
        gt = jnp.dot(h, wg_ref[:, sl].astype(BF16), preferred_element_type=F32)
        up = jnp.dot(h, wu_ref[:, sl].astype(BF16), preferred_element_type=F32)
        a_ref[:, sl] = (_silu(gt) * up).astype(BF16)
    y = jnp.dot(a_ref[...], wd_ref[...], preferred_element_type=F32)
    o_ref[...] = x + 0.5 * _rms(y, g_ref[4 * k + 1:4 * k + 2, :])


def _ffn(x, sandwich, wg, wu, wd, l, k, mixed=None, w_out=None):
    m = x.shape[0]
    tm = min(TOKEN_TILE, m)
    tile = pl.BlockSpec((tm, D_MODEL), lambda i: (i, 0))
    fused = mixed is not None
    in_specs = [tile] + ([tile, _layer_block((D_MODEL, D_MODEL), (l,))] if fused else [])
    in_specs += [_layer_block((N_SANDWICH, D_MODEL), (l,)),
                 _layer_block((D_MODEL, D_FF), (l, k)),
                 _layer_block((D_MODEL, D_FF), (l, k)),
                 _layer_block((D_FF, D_MODEL), (l, k))]
    args = (x,) + ((mixed, w_out) if fused else ()) + (sandwich, wg, wu, wd)
    return pl.pallas_call(
        functools.partial(_ffn_body, k=k, mixer_out=fused),
        out_shape=jax.ShapeDtypeStruct((m, D_MODEL), F32),
        grid=(m // tm,),
        in_specs=in_specs,
        out_specs=tile,
        scratch_shapes=[pltpu.VMEM((tm, D_FF), BF16)],
        compiler_params=pltpu.CompilerParams(dimension_semantics=("parallel",),
                                             vmem_limit_bytes=VMEM_LIMIT),
        name="ffn_out" if fused else "ffn",
    )(*args)


def _inproj_body(x_ref, g_ref, w_ref, o_ref):
    h = _rms(x_ref[...], g_ref[2:3, :]).astype(BF16)
    o_ref[...] = jnp.dot(h, w_ref[...], preferred_element_type=F32)


def _inproj(x, sandwich, w, l):
    m = x.shape[0]
    tm = min(TOKEN_TILE, m)
    return pl.pallas_call(
        _inproj_body,
        out_shape=jax.ShapeDtypeStruct((m, IN_PAD), F32),
        grid=(m // tm,),
        in_specs=[pl.BlockSpec((tm, D_MODEL), lambda i: (i, 0)),
                  _layer_block((N_SANDWICH, D_MODEL), (l,)),
                  _layer_block((D_MODEL, IN_PAD), (l,))],
        out_specs=pl.BlockSpec((tm, IN_PAD), lambda i: (i, 0)),
        compiler_params=pltpu.CompilerParams(dimension_semantics=("parallel",),
                                             vmem_limit_bytes=VMEM_LIMIT),
        name="inproj",
    )(x, sandwich, w)


def _outproj_body(x_ref, m_ref, g_ref, w_ref, o_ref):
    y = jnp.dot(m_ref[...], w_ref[...], preferred_element_type=F32)
    o_ref[...] = x_ref[...] + _rms(y, g_ref[3:4, :])


def _outproj(x, mixed, sandwich, w, l):
    m = x.shape[0]
    tm = min(TOKEN_TILE, m)
    return pl.pallas_call(
        _outproj_body,
        out_shape=jax.ShapeDtypeStruct((m, D_MODEL), F32),
        grid=(m // tm,),
        in_specs=[pl.BlockSpec((tm, D_MODEL), lambda i: (i, 0)),
                  pl.BlockSpec((tm, D_MODEL), lambda i: (i, 0)),
                  _layer_block((N_SANDWICH, D_MODEL), (l,)),
                  _layer_block((D_MODEL, D_MODEL), (l,))],
        out_specs=pl.BlockSpec((tm, D_MODEL), lambda i: (i, 0)),
        compiler_params=pltpu.CompilerParams(dimension_semantics=("parallel",),
                                             vmem_limit_bytes=VMEM_LIMIT),
        name="outproj",
    )(x, mixed, sandwich, w)


def _head(x, h):
    return x[:, h * HEAD_DIM:(h + 1) * HEAD_DIM]


def _neumann_inverse(amat, c):
    row = lax.broadcasted_iota(jnp.int32, (c, c), 0)
    col = lax.broadcasted_iota(jnp.int32, (c, c), 1)
    eye = (row == col).astype(F32)
    m = -amat
    p = eye + m
    for _ in range(max(int(math.ceil(math.log2(c))) - 1, 0)):
        m = _dot(m, m)
        p = p + _dot(p, m)
    return p


def _mix_body(*refs, sample, rows, chunk, ret_carry):
    nchunks = rows // chunk
    it = iter(refs)
    proj_ref = next(it)
    hist_ref = next(it) if sample else None
    cos_ref, sin_ref = next(it), next(it)
    rinter_ref, rtail_ref, rintra_ref = next(it), next(it), next(it)
    tri_ref, sel_ref, ones_ref = next(it), next(it), next(it)
    gvec_ref, convw_ref, convb_ref, dvec_ref, norms_ref = next(it), next(it), next(it), next(it), next(it)
    if sample:
        sret_in, sssm_in, sgdn_in, smc_in, smn_in, smm_in = (next(it) for _ in range(6))
    mixed_ref, convraw_ref = next(it), next(it)
    sret_ref, sssm_ref, sgdn_ref, smc_ref, smn_ref, smm_ref = (next(it) for _ in range(6))
    e_ref, g_ref, rq_ref, rk_ref, sx_ref, sbc_ref, gq_ref, gk_ref, gv_ref, o_ref = (next(it) for _ in range(10))
    if not sample:
        sret_in, sssm_in, sgdn_in, smc_in, smn_in, smm_in = (
            sret_ref, sssm_ref, sgdn_ref, smc_ref, smn_ref, smm_ref)

    if sample:
        rowid = lax.broadcasted_iota(jnp.int32, (rows, 1), 0)
        valid = (rowid % SAMPLE_PAD) >= (SAMPLE_PAD - 4)
        e_ref[0:SUBLANES, :] = jnp.zeros((SUBLANES, CONV_CH), F32)
    else:
        valid = None

        @pl.when(pl.program_id(1) == 0)
        def _():
            e_ref[0:SUBLANES, :] = jnp.zeros((SUBLANES, CONV_CH), F32)
            sret_ref[...] = jnp.zeros_like(sret_ref)
            sssm_ref[...] = jnp.zeros_like(sssm_ref)
            sgdn_ref[...] = jnp.zeros_like(sgdn_ref)
            smc_ref[...] = jnp.zeros_like(smc_ref)
            smn_ref[...] = jnp.zeros_like(smn_ref)
            smm_ref[...] = jnp.zeros_like(smm_ref)

    def keep(x, fill=0.0):
        return x if valid is None else jnp.where(valid, x, fill)

    lane = lax.broadcasted_iota(jnp.int32, (1, LANES), 1)
    in_lanes = lambda lo: (lane >= lo) & (lane < lo + N_HEADS)
    graw = proj_ref[:, COL_GATES:COL_GATES + LANES]
    xb = graw + gvec_ref[0:1, :]
    sp = _softplus(xb)
    neg_a = -jnp.exp(gvec_ref[1:2, :])
    logf = -_softplus(-xb)
    pre = jnp.where(lane < 2 * N_HEADS, neg_a * sp, jnp.where(in_lanes(G_MF), logf, 0.0))
    cum = _dot_sel(tri_ref[...], keep(pre))
    dtv = pltpu.roll(sp, G_DTV - G_DT, axis=1)
    aux = jnp.where(in_lanes(G_GB), _sigmoid(graw),
                    jnp.where(in_lanes(G_MI), xb, jnp.where(in_lanes(G_DTV), dtv, 0.0)))
    if sample:
        aux = jnp.where(valid, aux, jnp.where(in_lanes(G_MI), NEG_BIG, 0.0))
    g_ref[...] = jnp.where(in_lanes(G_GB) | in_lanes(G_MI) | in_lanes(G_DTV), aux, cum)

    lane_g = lax.broadcasted_iota(jnp.int32, (1, GROUP_W), 1)
    first_half = (lane_g % HEAD_DIM) < (HEAD_DIM // 2)

    def rotary(x):
        swapped = jnp.where(first_half, pltpu.roll(x, GROUP_W - HEAD_DIM // 2, axis=1),
                            pltpu.roll(x, HEAD_DIM // 2, axis=1))
        return x * cos_ref[...] + swapped * sin_ref[...]

    rq_ref[...] = rotary(proj_ref[:, COL_RET:COL_RET + GROUP_W])
    rk_ref[...] = keep(rotary(proj_ref[:, COL_RET + GROUP_W:COL_RET + 2 * GROUP_W]) * HEAD_DIM ** -0.5)

    raw = proj_ref[:, COL_CONV:COL_CONV + CONV_CH]
    if sample:
        raw = raw + hist_ref[...]
    e_ref[SUBLANES:SUBLANES + rows, :] = raw
    conv = raw * convw_ref[CONV_W - 1:CONV_W, :]
    for s in range(1, CONV_W):
        conv = conv + e_ref[SUBLANES - s:SUBLANES - s + rows, :] * convw_ref[CONV_W - 1 - s:CONV_W - s, :]
    conv = keep(_silu(conv + convb_ref[...]))
    if sample:
        convraw_ref[...] = raw
    else:
        convraw_ref[...] = e_ref[rows:rows + SUBLANES, :]
        e_ref[0:SUBLANES, :] = e_ref[rows:rows + SUBLANES, :]
    sx_ref[...] = conv[:, 0:GROUP_W]
    sbc_ref[...] = conv[:, GROUP_W:SSM_CONV_CH]
    ones_bd = ones_ref[...]

    def head_sumsq(x):
        return _dot_sel_r(x * x, ones_bd)

    gq = conv[:, SSM_CONV_CH:SSM_CONV_CH + GROUP_W]
    gk = conv[:, SSM_CONV_CH + GROUP_W:SSM_CONV_CH + 2 * GROUP_W]
    gq_ref[...] = gq * lax.rsqrt(head_sumsq(gq) + EPS) * HEAD_DIM ** -0.5
    gk_ref[...] = gk * lax.rsqrt(head_sumsq(gk) + EPS)
    gv_ref[...] = conv[:, SSM_CONV_CH + 2 * GROUP_W:CONV_CH]

    r_i = lax.broadcasted_iota(jnp.int32, (chunk, chunk), 0)
    c_i = lax.broadcasted_iota(jnp.int32, (chunk, chunk), 1)
    tri = r_i >= c_i
    strict = r_i > c_i
    lane_h = lax.broadcasted_iota(jnp.int32, (1, N_HEADS), 1)

    def chunk_step(i, carry):
        r0 = pl.multiple_of(i * chunk, chunk)
        rs = pl.ds(r0, chunk)
        sidx = i if sample else 0
        gc = g_ref[rs, :]
        gt = _dot_sel_nt(sel_ref[...], gc)
        col = lambda j: gc[:, j:j + 1]
        rowv = lambda j: gt[j:j + 1, :]
        last = lambda j: gc[chunk - 1:chunk, j:j + 1]

        heads = range(N_HEADS)
        s_ret = [sret_in[sidx, h] for h in heads]
        s_ssm = [sssm_in[sidx, h] for h in heads]
        s_gdn = [sgdn_in[sidx, h] for h in heads]
        s_mc = [smc_in[sidx, h] for h in heads]
        n_all = smn_in[sidx]
        m_row = smm_in[sidx]
        rq_all, rk_all = rq_ref[rs, :], rk_ref[rs, :]
        rv_all = proj_ref[rs, COL_RET + 2 * GROUP_W:COL_RET + 3 * GROUP_W]
        x_all, bc_all = sx_ref[rs, :], sbc_ref[rs, :]
        gq_all, gk_all, gv_all = gq_ref[rs, :], gk_ref[rs, :], gv_ref[rs, :]
        mq_all = proj_ref[rs, COL_ML:COL_ML + GROUP_W]
        mk_all = proj_ref[rs, COL_ML + GROUP_W:COL_ML + 2 * GROUP_W] * HEAD_DIM ** -0.5
        mv_all = proj_ref[rs, COL_ML + 2 * GROUP_W:COL_ML + 3 * GROUP_W]

        qi_all = rq_all * rinter_ref[...]
        kt_all = rk_all * rtail_ref[...]
        o_ret, n_ret = [], []
        for h in heads:
            att = _dot_nt(_head(rq_all, h), _head(rk_all, h)) * rintra_ref[h]
            o_ret.append(_dot(att, _head(rv_all, h)) + _dot(_head(qi_all, h), s_ret[h]))
            n_ret.append(s_ret[h] * ret_carry[h] + _dot_tn(_head(kt_all, h), _head(rv_all, h)))

        o_ssm, n_ssm = [], []
        for h in heads:
            grp = h // (N_HEADS // 2)
            b_g = bc_all[:, grp * HEAD_DIM:(grp + 1) * HEAD_DIM]
            c_g = bc_all[:, 2 * HEAD_DIM + grp * HEAD_DIM:2 * HEAD_DIM + (grp + 1) * HEAD_DIM]
            x_h = _head(x_all, h)
            cum_c, cum_r, cum_l = col(G_DT + h), rowv(G_DT + h), last(G_DT + h)
            lmat = jnp.exp(jnp.where(tri, cum_c - cum_r, -jnp.inf))
            scores = _dot_nt(c_g, b_g) * lmat * rowv(G_DTV + h)
            o_ssm.append(_dot(scores, x_h) + _dot_nt(c_g, s_ssm[h]) * jnp.exp(cum_c))
            w_c = jnp.exp(cum_l - cum_c) * col(G_DTV + h)
            n_ssm.append(s_ssm[h] * jnp.exp(cum_l) + _dot_tn(x_h * w_c, b_g))

        o_gdn, n_gdn = [], []
        for h in heads:
            q_h, k_h, v_h = _head(gq_all, h), _head(gk_all, h), _head(gv_all, h)
            cum_c, cum_r, cum_l = col(G_GA + h), rowv(G_GA + h), last(G_GA + h)
            beta_c = col(G_GB + h)
            gam = jnp.exp(jnp.where(tri, cum_c - cum_r, -jnp.inf))
            amat = _dot_nt(k_h, k_h) * jnp.where(strict, gam, 0.0) * beta_c
            ecum = jnp.exp(cum_c)
            tinv = _neumann_inverse(amat, chunk)
            u = _dot(tinv, v_h * beta_c)
            w = _dot(tinv, k_h * (beta_c * ecum))
            v_new = u - _dot(w, s_gdn[h])
            qk = _dot_nt(q_h, k_h) * gam
            o_gdn.append(_dot(q_h * ecum, s_gdn[h]) + _dot(qk, v_new))
            n_gdn.append(s_gdn[h] * jnp.exp(cum_l) + _dot_tn(k_h * jnp.exp(cum_l - cum_c), v_new))

        m_new_row = jnp.zeros((1, N_HEADS), F32)
        o_ml, n_mc, n_mn = [], [], []
        for h in heads:
            q_h, k_h, v_h = _head(mq_all, h), _head(mk_all, h), _head(mv_all, h)
            b_c, b_r, b_l = col(G_MF + h), rowv(G_MF + h), last(G_MF + h)
            i_c, i_r = col(G_MI + h), rowv(G_MI + h)
            m_old = m_row[:, h:h + 1]
            n_h = n_all[h:h + 1, :]
            dmat = jnp.where(tri, b_c - b_r + i_r, -jnp.inf)
            m_t = jnp.maximum(b_c + m_old, jnp.max(dmat, axis=-1, keepdims=True))
            wts = jnp.exp(dmat - m_t)
            inter = jnp.exp(b_c + m_old - m_t)
            qk = _dot_nt(q_h, k_h) * wts
            num = _dot(qk, v_h) + inter * _dot(q_h, s_mc[h])
            den = jnp.sum(qk, axis=-1, keepdims=True) + inter * jnp.sum(q_h * n_h, axis=-1, keepdims=True)
            o_ml.append(num / jnp.maximum(jnp.abs(den), jnp.exp(-m_t)))
            m_new = m_t[chunk - 1:chunk, :]
            ws = jnp.exp(b_l - b_c + i_c - m_new)
            cs = jnp.exp(b_l + m_old - m_new)
            kw = k_h * ws
            n_mc.append(s_mc[h] * cs + _dot_tn(kw, v_h))
            n_mn.append(n_h * cs + jnp.sum(kw, axis=0, keepdims=True))
            m_new_row = jnp.where(lane_h == h, m_new, m_new_row)

        o_ref[rs, :] = jnp.concatenate(o_ret + o_ssm + o_gdn + o_ml, axis=1)
        for h in heads:
            sret_ref[sidx, h] = n_ret[h]
            sssm_ref[sidx, h] = n_ssm[h]
            sgdn_ref[sidx, h] = n_gdn[h]
            smc_ref[sidx, h] = n_mc[h]
        smn_ref[sidx] = jnp.concatenate(n_mn, axis=0)
        smm_ref[sidx] = m_new_row
        return carry

    lax.fori_loop(0, nchunks, chunk_step, 0)

    def head_norm(x, g):
        return x * lax.rsqrt(head_sumsq(x) * (1.0 / HEAD_DIM) + EPS) * g

    out_ret = head_norm(o_ref[:, 0:GROUP_W], norms_ref[0:1, :]) * _silu(
        proj_ref[:, COL_RET + 3 * GROUP_W:COL_RET + 4 * GROUP_W])
    mixed_ref[:, 0:GROUP_W] = keep(out_ret).astype(BF16)
    y = o_ref[:, GROUP_W:2 * GROUP_W] + dvec_ref[...] * sx_ref[...]
    out_ssm = _rms(y * _silu(proj_ref[:, COL_SSM_Z:COL_SSM_Z + GROUP_W]), norms_ref[1:2, :])
    mixed_ref[:, GROUP_W:2 * GROUP_W] = keep(out_ssm).astype(BF16)
    out_gdn = head_norm(o_ref[:, 2 * GROUP_W:3 * GROUP_W], norms_ref[2:3, :]) * _silu(
        proj_ref[:, COL_GDN_GATE:COL_GDN_GATE + GROUP_W])
    mixed_ref[:, 2 * GROUP_W:3 * GROUP_W] = keep(out_gdn).astype(BF16)
    out_ml = _sigmoid(proj_ref[:, COL_ML + 3 * GROUP_W:COL_ML + 4 * GROUP_W]) * head_norm(
        o_ref[:, 3 * GROUP_W:4 * GROUP_W], norms_ref[3:4, :])
    mixed_ref[:, 3 * GROUP_W:4 * GROUP_W] = keep(out_ml).astype(BF16)


NEXP = 6
X_CSSM, X_DT, X_CGDN, X_BETA, X_B, X_IMB = range(NEXP)


def _split2(x):
    hi = x.astype(BF16)
    return hi, (x - hi.astype(F32)).astype(BF16)


def _dot_sel_r2(x, sel):
    hi, lo = _split2(x)
    return jnp.dot(lo, sel, preferred_element_type=F32) + jnp.dot(hi, sel, preferred_element_type=F32)


(PB_RQ, PB_RK, PB_RV, PB_RG, PB_Z, PB_SX, PB_SBC, PB_GQ, PB_GK, PB_GV, PB_GG, PB_MQ, PB_MK, PB_MV, PB_MO,
 PB_X0) = range(16)
PRE_COLS = (PB_X0 + NEXP) * GROUP_W


def _pproj_body(x_ref, g_ref, w_ref, cos_ref, sin_ref, tri_ref, expand_ref, maskbd_ref, gvec_ref, convw_ref,
                convb_ref, pre_ref, convraw_ref, e_ref, *, rows):
    @pl.when(pl.program_id(1) == 0)
    def _():
        e_ref[0:SUBLANES, :] = jnp.zeros((SUBLANES, CONV_CH), F32)

    h_in = _rms(x_ref[...], g_ref[2:3, :]).astype(BF16)
    proj = lambda a, n: jnp.dot(h_in, w_ref[:, a:a + n], preferred_element_type=F32)

    def put(b, v):
        pre_ref[:, b * GROUP_W:(b + 1) * GROUP_W] = v

    maskbd_b = maskbd_ref[...].astype(BF16)
    seg_sum = lambda x: _dot_sel_r2(x, maskbd_b)

    lane = lax.broadcasted_iota(jnp.int32, (1, LANES), 1)
    in_lanes = lambda lo: (lane >= lo) & (lane < lo + N_HEADS)
    graw = proj(COL_GATES, LANES)
    xb = graw + gvec_ref[0:1, :]
    sp = _softplus(xb)
    neg_a = -jnp.exp(gvec_ref[1:2, :])
    logf = -_softplus(-xb)
    pre = jnp.where(lane < 2 * N_HEADS, neg_a * sp, jnp.where(in_lanes(G_MF), logf, 0.0))
    cum = _dot_sel(tri_ref[...], pre)
    imb = xb - pltpu.roll(cum, LANES - (G_MF - G_MI), axis=1)
    dtv = pltpu.roll(sp, G_DTV - G_DT, axis=1)
    gates = jnp.where(in_lanes(G_GB), _sigmoid(graw),
                      jnp.where(in_lanes(G_MI), imb, jnp.where(in_lanes(G_DTV), dtv, cum)))

    raw = proj(COL_CONV, CONV_CH)
    e_ref[SUBLANES:SUBLANES + rows, :] = raw
    for j in range(NEXP):
        put(PB_X0 + j, _dot_sel_r(gates, expand_ref[:, j * GROUP_W:(j + 1) * GROUP_W]))
    conv = raw * convw_ref[CONV_W - 1:CONV_W, :]
    for s in range(1, CONV_W):
        conv = conv + e_ref[SUBLANES - s:SUBLANES - s + rows, :] * convw_ref[CONV_W - 1 - s:CONV_W - s, :]
    conv = _silu(conv + convb_ref[...])
    convraw_ref[...] = e_ref[rows:rows + SUBLANES, :]
    e_ref[0:SUBLANES, :] = e_ref[rows:rows + SUBLANES, :]

    lane_g = lax.broadcasted_iota(jnp.int32, (1, GROUP_W), 1)
    first_half = (lane_g % HEAD_DIM) < (HEAD_DIM // 2)

    def rotary(x):
        swapped = jnp.where(first_half, pltpu.roll(x, GROUP_W - HEAD_DIM // 2, axis=1),
                            pltpu.roll(x, HEAD_DIM // 2, axis=1))
        return x * cos_ref[...] + swapped * sin_ref[...]

    put(PB_RQ, rotary(proj(COL_RET, GROUP_W)))
    put(PB_RK, rotary(proj(COL_RET + GROUP_W, GROUP_W)) * HEAD_DIM ** -0.5)
    put(PB_SX, conv[:, 0:GROUP_W])
    put(PB_SBC, conv[:, GROUP_W:SSM_CONV_CH])
    put(PB_RV, proj(COL_RET + 2 * GROUP_W, GROUP_W))
    put(PB_RG, proj(COL_RET + 3 * GROUP_W, GROUP_W))
    gq = conv[:, SSM_CONV_CH:SSM_CONV_CH + GROUP_W]
    put(PB_GQ, gq * lax.rsqrt(seg_sum(gq * gq) + EPS) * HEAD_DIM ** -0.5)
    put(PB_Z, proj(COL_SSM_Z, GROUP_W))
    put(PB_GG, proj(COL_GDN_GATE, GROUP_W))
    gk = conv[:, SSM_CONV_CH + GROUP_W:SSM_CONV_CH + 2 * GROUP_W]
    put(PB_GK, gk * lax.rsqrt(seg_sum(gk * gk) + EPS))
    put(PB_GV, conv[:, SSM_CONV_CH + 2 * GROUP_W:CONV_CH])
    put(PB_MQ, proj(COL_ML, GROUP_W))
    put(PB_MK, proj(COL_ML + GROUP_W, GROUP_W) * HEAD_DIM ** -0.5)
    put(PB_MV, proj(COL_ML + 2 * GROUP_W, GROUP_W))
    put(PB_MO, proj(COL_ML + 3 * GROUP_W, GROUP_W))


def _pproj(x, sandwich, w, cos, sin, small, l, *, nbatch, seqlen):
    rows = min(TOKEN_TILE, seqlen)
    ntile = seqlen // rows
    tri, expand, maskbd = _pproj_consts(rows, PROMPT_CHUNK)
    gvec, convw, convb = small[:3]
    row_map = lambda b, t: (b * ntile + t, 0)
    full = lambda a: pl.BlockSpec(a.shape, lambda b, t: (0, 0))
    in_specs = [pl.BlockSpec((rows, D_MODEL), row_map),
                _layer_block((N_SANDWICH, D_MODEL), (l,), ngrid=2),
                _layer_block((D_MODEL, IN_PAD), (l,), ngrid=2),
                pl.BlockSpec((rows, GROUP_W), lambda b, t: (t, 0)), pl.BlockSpec((rows, GROUP_W), lambda b, t: (t, 0)),
                full(tri), full(expand), full(maskbd)]
    in_specs += [_layer_block(a.shape[1:], (l,), ngrid=2, buffered=False) for a in (gvec, convw, convb)]
    m = x.shape[0]
    return pl.pallas_call(
        functools.partial(_pproj_body, rows=rows),
        out_shape=[jax.ShapeDtypeStruct((m, PRE_COLS), F32),
                   jax.ShapeDtypeStruct((nbatch * SUBLANES, CONV_CH), F32)],
        grid=(nbatch, ntile), in_specs=in_specs,
        out_specs=[pl.BlockSpec((rows, PRE_COLS), row_map), pl.BlockSpec((SUBLANES, CONV_CH), lambda b, t: (b, 0))],
        scratch_shapes=[pltpu.VMEM((rows + SUBLANES, CONV_CH), F32)],
        compiler_params=pltpu.CompilerParams(dimension_semantics=("parallel", "arbitrary"),
                                             vmem_limit_bytes=VMEM_LIMIT),
        name="pproj",
    )(x, sandwich, w, cos, sin, tri, expand, maskbd, gvec, convw, convb)


def _mixp_body(pre_ref, rinter_ref, rtail_ref, rintra_ref, maskbd_ref, maskg_ref, maskht_ref, dvec_ref, norms_ref,
               mixed_ref, oret_ref, ossm_ref, ogdn_ref, omc_ref, omn_ref, omm_ref,
               o_ref, sret_ref, sht_ref, sgdn_ref, smc_ref, vec_ref, *, rows, chunk, ret_carry):
    nchunks = rows // chunk
    t_id = pl.program_id(1)
    blk = lambda b, rs=slice(None): pre_ref[rs, b * GROUP_W:(b + 1) * GROUP_W]

    @pl.when(t_id == 0)
    def _():
        sret_ref[...] = jnp.zeros_like(sret_ref)
        sht_ref[...] = jnp.zeros_like(sht_ref)
        sgdn_ref[...] = jnp.zeros_like(sgdn_ref)
        smc_ref[...] = jnp.zeros_like(smc_ref)
        vec_ref[...] = jnp.zeros_like(vec_ref)

    maskbd = maskbd_ref[...]
    maskbd_b = maskbd.astype(BF16)
    maskg = maskg_ref[...]
    maskg_b = maskg.astype(BF16)

    def seg_sum(x):
        return _dot_sel_r2(x, maskbd_b)

    r_i = lax.broadcasted_iota(jnp.int32, (chunk, GROUP_W), 0)
    c_i = lax.broadcasted_iota(jnp.int32, (chunk, GROUP_W), 1) % HEAD_DIM
    trim = r_i >= c_i
    diag = r_i == c_i
    strict = r_i > c_i
    eye = diag.astype(F32)
    head_of_lane = lax.broadcasted_iota(jnp.int32, (1, GROUP_W), 1) // HEAD_DIM
    carry_row = jnp.zeros((1, GROUP_W), F32)
    for h in range(N_HEADS):
        carry_row = jnp.where(head_of_lane == h, ret_carry[h], carry_row)
    maskht = maskht_ref[...]

    def bd(x):
        xb16 = x.astype(BF16)
        return jnp.concatenate([xb16] * N_HEADS, axis=0) * maskbd_b

    def last_row(x):
        return x[chunk - 1:chunk, :]

    def row_form(x):
        return jnp.sum(jnp.where(diag, x, 0.0), axis=0, keepdims=True)

    def seg_max(x):
        out = None
        for h in range(N_HEADS):
            sel = head_of_lane == h
            m_h = jnp.max(jnp.where(sel, x, -jnp.inf), axis=-1, keepdims=True)
            out = jnp.where(sel, m_h, 0.0) if out is None else jnp.where(sel, m_h, out)
        return out

    chunks = [slice(c * chunk, (c + 1) * chunk) for c in range(nchunks)]

    g_gam, g_beta, g_ecum, g_k, g_n, g_p = [], [], [], [], [], []
    for rs in chunks:
        ce = blk(PB_X0 + X_CGDN, rs)
        gam = jnp.exp(jnp.where(trim, ce - row_form(ce), -jnp.inf))
        k_c = blk(PB_GK, rs)
        beta = blk(PB_X0 + X_BETA, rs)
        n0 = -(_dot_nt(k_c, bd(k_c)) * jnp.where(strict, gam, 0.0) * beta)
        g_gam.append(gam), g_beta.append(beta), g_ecum.append(jnp.exp(ce)), g_k.append(k_c)
        g_n.append(n0), g_p.append(eye + n0)
    for _ in range(max(int(math.ceil(math.log2(chunk))) - 1, 0)):
        for c in range(nchunks):
            g_n[c] = _dot(g_n[c], bd(g_n[c]))
        for c in range(nchunks):
            g_p[c] = g_p[c] + _dot(g_p[c], bd(g_n[c]))
    g_u, g_w = [], []
    for c, rs in enumerate(chunks):
        g_u.append(_dot(g_p[c], bd(blk(PB_GV, rs) * g_beta[c])))
        g_w.append(_dot(g_p[c], bd(g_k[c] * (g_beta[c] * g_ecum[c]))))

    s_ret, s_ht, s_gdn, s_mc = sret_ref[...], sht_ref[...], sgdn_ref[...], smc_ref[...]
    n_row, m_row = vec_ref[0:1, :], vec_ref[1:2, :]
    for c, rs in enumerate(chunks):
        q_c, k_c, v_c = blk(PB_RQ, rs), blk(PB_RK, rs), blk(PB_RV, rs)
        att = _dot_nt(q_c, bd(k_c)) * rintra_ref[...]
        o_ret = _dot(att, bd(v_c)) + _dot(q_c * rinter_ref[...], s_ret)
        s_ret = s_ret * carry_row + _dot_tn(k_c * rtail_ref[...], v_c) * maskbd

        ce = blk(PB_X0 + X_CSSM, rs)
        ce_last = last_row(ce)
        lmat = jnp.exp(jnp.where(trim, ce - row_form(ce), -jnp.inf))
        b_c = pre_ref[rs, PB_SBC * GROUP_W:PB_SBC * GROUP_W + 2 * HEAD_DIM]
        c_c = pre_ref[rs, PB_SBC * GROUP_W + 2 * HEAD_DIM:(PB_SBC + 1) * GROUP_W]
        xdt = blk(PB_SX, rs) * blk(PB_X0 + X_DT, rs)
        rhs_b = jnp.concatenate([b_c.astype(BF16)] * N_HEADS, axis=0) * maskg_b
        scores = _dot_nt(c_c, rhs_b) * lmat
        o_ssm = _dot(scores, bd(xdt)) + _dot(c_c, s_ht) * jnp.exp(ce)
        s_ht = s_ht * jnp.exp(ce_last) + _dot_tn(b_c, xdt * jnp.exp(ce_last - ce)) * maskht

        ce = blk(PB_X0 + X_CGDN, rs)
        ce_last = last_row(ce)
        q_c = blk(PB_GQ, rs)
        v_new = g_u[c] - _dot(g_w[c], s_gdn)
        qk = _dot_nt(q_c, bd(g_k[c])) * g_gam[c]
        o_gdn = _dot(q_c * g_ecum[c], s_gdn) + _dot(qk, bd(v_new))
        s_gdn = s_gdn * jnp.exp(ce_last) + _dot_tn(g_k[c] * jnp.exp(ce_last - ce), v_new) * maskbd

        q_c, k_c, v_c = blk(PB_MQ, rs), blk(PB_MK, rs), blk(PB_MV, rs)
        b_e, imb_e = blk(PB_X0 + X_B, rs), blk(PB_X0 + X_IMB, rs)
        dmat = jnp.where(trim, b_e + row_form(imb_e), -jnp.inf)
        m_t = jnp.maximum(b_e + m_row, seg_max(dmat))
        wts = jnp.exp(dmat - m_t)
        inter = jnp.exp(b_e + m_row - m_t)
        qk = _dot_nt(q_c, bd(k_c)) * wts
        num = _dot(qk, bd(v_c)) + inter * _dot(q_c, s_mc)
        den = seg_sum(qk) + inter * seg_sum(q_c * n_row)
        o_ml = num / jnp.maximum(jnp.abs(den), jnp.exp(-m_t))
        m_new = last_row(m_t)
        b_last = last_row(b_e)
        kw = k_c * jnp.exp(b_last + imb_e - m_new)
        cs = jnp.exp(b_last + m_row - m_new)
        s_mc = s_mc * cs + _dot_tn(kw, v_c) * maskbd
        n_row = n_row * cs + jnp.sum(kw, axis=0, keepdims=True)
        m_row = m_new

        o_ref[rs, :] = jnp.concatenate([o_ret, o_ssm, o_gdn, o_ml], axis=1)

    sret_ref[...] = s_ret
    sht_ref[...] = s_ht
    sgdn_ref[...] = s_gdn
    smc_ref[...] = s_mc
    vec_ref[0:1, :] = n_row
    vec_ref[1:2, :] = m_row

    def head_norm(x, g):
        return x * lax.rsqrt(seg_sum(x * x) * (1.0 / HEAD_DIM) + EPS) * g

    out_ret = head_norm(o_ref[:, 0:GROUP_W], norms_ref[0:1, :]) * _silu(blk(PB_RG))
    mixed_ref[:, 0:GROUP_W] = out_ret.astype(BF16)
    y = o_ref[:, GROUP_W:2 * GROUP_W] + dvec_ref[...] * blk(PB_SX)
    out_ssm = _rms(y * _silu(blk(PB_Z)), norms_ref[1:2, :])
    mixed_ref[:, GROUP_W:2 * GROUP_W] = out_ssm.astype(BF16)
    out_gdn = head_norm(o_ref[:, 2 * GROUP_W:3 * GROUP_W], norms_ref[2:3, :]) * _silu(blk(PB_GG))
    mixed_ref[:, 2 * GROUP_W:3 * GROUP_W] = out_gdn.astype(BF16)
    out_ml = _sigmoid(blk(PB_MO)) * head_norm(o_ref[:, 3 * GROUP_W:4 * GROUP_W], norms_ref[3:4, :])
    mixed_ref[:, 3 * GROUP_W:4 * GROUP_W] = out_ml.astype(BF16)

    @pl.when(t_id == pl.num_programs(1) - 1)
    def _():
        for h in range(N_HEADS):
            hs = slice(h * HEAD_DIM, (h + 1) * HEAD_DIM)
            grp = h // (N_HEADS // 2)
            oret_ref[0, h] = s_ret[hs, hs]
            ogdn_ref[0, h] = s_gdn[hs, hs]
            omc_ref[0, h] = s_mc[hs, hs]
            ossm_ref[0, h] = s_ht[grp * HEAD_DIM:(grp + 1) * HEAD_DIM, hs]
        omn_ref[0] = jnp.concatenate([n_row[:, h * HEAD_DIM:(h + 1) * HEAD_DIM] for h in range(N_HEADS)], axis=0)
        omm_ref[0] = jnp.concatenate([m_row[:, h * HEAD_DIM:h * HEAD_DIM + 1] for h in range(N_HEADS)], axis=1)


def _prompt_consts(rows, chunk):
    lg = np.log(1.0 - np.exp2(-5.0 - np.arange(N_HEADS, dtype=np.float64)))
    idx = np.arange(chunk, dtype=np.float64)
    rel = idx[:, None] - idx[None, :]
    intra = np.where(rel[None] >= 0, np.exp(np.maximum(rel[None], 0.0) * lg[:, None, None]), 0.0)
    intra = np.concatenate(list(intra), axis=1)
    lanes = lambda t: np.repeat(t.T, HEAD_DIM, axis=1)
    inter = lanes(np.exp((idx[None, :] + 1.0) * lg[:, None]))
    tail = lanes(np.exp((chunk - 1.0 - idx[None, :]) * lg[:, None]))
    carry = tuple(float(c) for c in np.exp(chunk * lg))
    g = np.arange(GROUP_W)
    maskbd = (g[:, None] // HEAD_DIM == g[None, :] // HEAD_DIM)
    n = np.arange(2 * HEAD_DIM)
    maskg = (g[:, None] // (2 * HEAD_DIM) == n[None, :] // HEAD_DIM)
    f = lambda a, dt=F32: jnp.asarray(np.asarray(a, np.float32), dt)
    return (f(inter), f(tail), f(intra), f(maskbd), f(maskg), f(maskg.T)), carry


def _pproj_consts(rows, chunk):
    r = np.arange(rows)
    tri = (r[:, None] >= r[None, :]) & (r[:, None] // chunk == r[None, :] // chunk)
    expand = np.zeros((LANES, NEXP * GROUP_W), np.float32)
    for j, lo in enumerate((G_DT, G_DTV, G_GA, G_GB, G_MF, G_MI)):
        for h in range(N_HEADS):
            expand[lo + h, j * GROUP_W + h * HEAD_DIM:j * GROUP_W + (h + 1) * HEAD_DIM] = 1.0
    g = np.arange(GROUP_W)
    maskbd = (g[:, None] // HEAD_DIM == g[None, :] // HEAD_DIM)
    f = lambda a, dt: jnp.asarray(np.asarray(a, np.float32), dt)
    return f(tri, BF16), f(expand, BF16), f(maskbd, F32)


def _mix_prompt(pre, small, l, *, nbatch, seqlen):
    rows, chunk = min(PROMPT_TILE, seqlen), PROMPT_CHUNK
    ntile = seqlen // rows
    consts, carry = _prompt_consts(rows, chunk)
    dvec, norms = small[3:]
    m = pre.shape[0]
    row_map = lambda b, t: (b * ntile + t, 0)
    full = lambda a: pl.BlockSpec(a.shape, lambda b, t: (0, 0))
    in_specs = [pl.BlockSpec((rows, PRE_COLS), row_map)] + [full(a) for a in consts]
    in_specs += [_layer_block(a.shape[1:], (l,), ngrid=2, buffered=False) for a in (dvec, norms)]
    st4 = pl.BlockSpec((1, N_HEADS, HEAD_DIM, HEAD_DIM), lambda b, t: (b, 0, 0, 0))
    out_shape = [jax.ShapeDtypeStruct((m, D_MODEL), BF16)]
    out_shape += [jax.ShapeDtypeStruct((nbatch, N_HEADS, HEAD_DIM, HEAD_DIM), F32) for _ in range(4)]
    out_shape += [jax.ShapeDtypeStruct((nbatch, N_HEADS, HEAD_DIM), F32),
                  jax.ShapeDtypeStruct((nbatch, 1, N_HEADS), F32)]
    out_specs = [pl.BlockSpec((rows, D_MODEL), row_map), st4, st4, st4, st4,
                 pl.BlockSpec((1, N_HEADS, HEAD_DIM), lambda b, t: (b, 0, 0)),
                 pl.BlockSpec((1, 1, N_HEADS), lambda b, t: (b, 0, 0))]
    scratch = [pltpu.VMEM((rows, D_MODEL), F32),
               pltpu.VMEM((GROUP_W, GROUP_W), F32), pltpu.VMEM((2 * HEAD_DIM, GROUP_W), F32),
               pltpu.VMEM((GROUP_W, GROUP_W), F32), pltpu.VMEM((GROUP_W, GROUP_W), F32),
               pltpu.VMEM((SUBLANES, GROUP_W), F32)]
    body = functools.partial(_mixp_body, rows=rows, chunk=chunk, ret_carry=carry)
    outs = pl.pallas_call(
        body, out_shape=out_shape, grid=(nbatch, ntile), in_specs=in_specs, out_specs=out_specs,
        scratch_shapes=scratch,
        compiler_params=pltpu.CompilerParams(dimension_semantics=("parallel", "arbitrary"),
                                             vmem_limit_bytes=VMEM_LIMIT),
        name="mix_prompt",
    )(pre, *consts, dvec, norms)
    outs = list(outs)
    outs[2] = jnp.swapaxes(outs[2], -1, -2)
    return outs


def _ret_tables(chunk, pad):
    lg = np.log(1.0 - np.exp2(-5.0 - np.arange(N_HEADS, dtype=np.float64)))
    idx = np.arange(chunk, dtype=np.float64)
    rel = idx[:, None] - idx[None, :]
    intra = np.where(rel[None] >= 0, np.exp(np.maximum(rel[None], 0.0) * lg[:, None, None]), 0.0)
    inter = np.where(idx[None, :] >= pad, np.exp((idx[None, :] - pad + 1.0) * lg[:, None]), 0.0)
    tail = np.exp((chunk - 1.0 - idx[None, :]) * lg[:, None])
    carry = np.exp((chunk - pad) * lg)
    lanes = lambda t: np.repeat(t.T, HEAD_DIM, axis=1)
    return (jnp.asarray(lanes(inter), F32), jnp.asarray(lanes(tail), F32), jnp.asarray(intra, F32),
            tuple(float(c) for c in carry))


def _const_mats(rows, chunk):
    r = np.arange(rows)
    tri = ((r[:, None] >= r[None, :]) & (r[:, None] // chunk == r[None, :] // chunk))
    sel = np.zeros((4 * SUBLANES, LANES), np.float32)
    sel[np.arange(24), np.arange(24)] = 1.0
    g = np.arange(GROUP_W)
    ones_bd = (g[:, None] // HEAD_DIM == g[None, :] // HEAD_DIM)
    return (jnp.asarray(tri, BF16), jnp.asarray(sel, BF16), jnp.asarray(ones_bd, BF16))


def _mix_sample(proj, hist, cos, sin, small, states, l, *, nbatch):
    rows, chunk, pad = SAMPLE_GROUP * SAMPLE_PAD, SAMPLE_PAD, SAMPLE_PAD - 4
    row_map = lambda i: (i, 0)
    c2 = lambda i: (0, 0)
    rinter, rtail, rintra, carry = _ret_tables(chunk, pad)
    tri, sel, ones_bd = _const_mats(rows, chunk)
    m = proj.shape[0]
    in_specs = [pl.BlockSpec((rows, IN_PAD), row_map),
                pl.BlockSpec((None, rows, CONV_CH), lambda i: (l, i, 0)),
                pl.BlockSpec((rows, GROUP_W), c2), pl.BlockSpec((rows, GROUP_W), c2),
                pl.BlockSpec((chunk, GROUP_W), c2), pl.BlockSpec((chunk, GROUP_W), c2),
                pl.BlockSpec((N_HEADS, chunk, chunk), lambda i: (0, 0, 0)),
                pl.BlockSpec((rows, rows), c2), pl.BlockSpec((4 * SUBLANES, LANES), c2),
                pl.BlockSpec((GROUP_W, GROUP_W), c2)]
    in_specs += [_layer_block(a.shape[1:], (l,), buffered=False) for a in small]
    g = SAMPLE_GROUP
    in_specs += [pl.BlockSpec((None, g, N_HEADS, HEAD_DIM, HEAD_DIM), lambda i: (l, i, 0, 0, 0))] * 4
    in_specs += [pl.BlockSpec((None, g, N_HEADS, HEAD_DIM), lambda i: (l, i, 0, 0)),
                 pl.BlockSpec((None, g, 1, N_HEADS), lambda i: (l, i, 0, 0))]
    st4 = pl.BlockSpec((g, N_HEADS, HEAD_DIM, HEAD_DIM), lambda i: (i, 0, 0, 0))
    out_specs = [pl.BlockSpec((rows, D_MODEL), row_map), pl.BlockSpec((rows, CONV_CH), row_map),
                 st4, st4, st4, st4,
                 pl.BlockSpec((g, N_HEADS, HEAD_DIM), lambda i: (i, 0, 0)),
                 pl.BlockSpec((g, 1, N_HEADS), lambda i: (i, 0, 0))]
    out_shape = [jax.ShapeDtypeStruct((m, D_MODEL), BF16), jax.ShapeDtypeStruct((m, CONV_CH), F32)]
    out_shape += [jax.ShapeDtypeStruct((nbatch, N_HEADS, HEAD_DIM, HEAD_DIM), F32) for _ in range(4)]
    out_shape += [jax.ShapeDtypeStruct((nbatch, N_HEADS, HEAD_DIM), F32),
                  jax.ShapeDtypeStruct((nbatch, 1, N_HEADS), F32)]
    scratch = [pltpu.VMEM((rows + SUBLANES, CONV_CH), F32), pltpu.VMEM((rows, LANES), F32)]
    scratch += [pltpu.VMEM((rows, GROUP_W), F32) for _ in range(7)]
    scratch += [pltpu.VMEM((rows, D_MODEL), F32)]
    body = functools.partial(_mix_body, sample=True, rows=rows, chunk=chunk, ret_carry=carry)
    return pl.pallas_call(
        body, out_shape=out_shape, grid=(nbatch // g,), in_specs=in_specs, out_specs=out_specs,
        scratch_shapes=scratch,
        compiler_params=pltpu.CompilerParams(dimension_semantics=("parallel",), vmem_limit_bytes=VMEM_LIMIT),
        name="mix_sample",
    )(proj, hist, cos, sin, rinter, rtail, rintra, tri, sel, ones_bd, *small, *states)


S_RQ, S_RK, S_RV, S_XDT, S_B, S_C, S_GQ, S_GK, S_GV, S_MQ, S_MK, S_MV, S_GATE = (
    HEAD_DIM * i for i in range(13))
SLAB_ROWS = S_GATE + SUBLANES
GT_SSM_A, GT_GDN_A, GT_BETA, GT_LOGI, GT_LOGF, GT_GAMMA = range(6)
P_RG, P_Z, P_XS, P_GG, P_MO = (GROUP_W * i for i in range(5))
POST_ROWS = 5 * GROUP_W
C_CONVW, C_CONVB = 0, CONV_W * CONV_CH
C_BIAS = C_CONVB + CONV_CH
C_ALOG = C_BIAS + 4 * SUBLANES
C_GAMMA = C_ALOG + 4 * SUBLANES
C_ROWS = C_GAMMA + SUBLANES
N_ROWS = 5 * GROUP_W
N_TOK = 4


def _spre_body(x_ref, g_ref, w_ref, hist_ref, rot_ref, colp_ref, slab_ref, post_ref, convraw_ref):
    nb = LANES
    h_in = _rms(x_ref[...], g_ref[2:3, :]).astype(BF16)
    proj = jnp.dot(h_in, w_ref[...], preferred_element_type=F32)
    for j in range(CONV_W - 1):
        convraw_ref[j] = proj[(j + 1) * nb:(j + 2) * nb, COL_CONV:COL_CONV + CONV_CH]
    pt = [proj[t * nb:(t + 1) * nb, :].T for t in range(N_TOK)]
    src = [hist_ref[j].T for j in range(CONV_W - 1)] + [p[COL_CONV:COL_CONV + CONV_CH] for p in pt]
    bias = colp_ref[C_BIAS:C_BIAS + 4 * SUBLANES, :]
    neg_a = -jnp.exp(colp_ref[C_ALOG:C_ALOG + 4 * SUBLANES, :])
    gamma = colp_ref[C_GAMMA:C_GAMMA + SUBLANES, :]
    zero_row = jnp.zeros((1, LANES), F32)

    def rotary(x, t, k):
        swapped = jnp.concatenate(
            [x[h * HEAD_DIM + off:h * HEAD_DIM + off + HEAD_DIM // 2]
             for h in range(N_HEADS) for off in (HEAD_DIM // 2, 0)], axis=0)
        return x * rot_ref[0, t] + swapped * rot_ref[1, t]

    for t in range(N_TOK):
        p = pt[t]
        g = p[COL_GATES:COL_GATES + 4 * SUBLANES]
        xb = g + bias
        sp = _softplus(xb)
        decay = jnp.exp(neg_a * sp)
        beta = _sigmoid(g)
        logf = -_softplus(-xb)
        conv = src[t] * colp_ref[C_CONVW:C_CONVW + CONV_CH, :]
        for j in range(1, CONV_W):
            conv = conv + src[t + j] * colp_ref[C_CONVW + j * CONV_CH:C_CONVW + (j + 1) * CONV_CH, :]
        cv = _silu(conv + colp_ref[C_CONVB:C_CONVB + CONV_CH, :])
        rq = rotary(p[COL_RET:COL_RET + GROUP_W], t, 0)
        rk = rotary(p[COL_RET + GROUP_W:COL_RET + 2 * GROUP_W], t, 1) * HEAD_DIM ** -0.5
        post_ref[t, P_RG:P_RG + GROUP_W, :] = p[COL_RET + 3 * GROUP_W:COL_RET + 4 * GROUP_W]
        post_ref[t, P_Z:P_Z + GROUP_W, :] = p[COL_SSM_Z:COL_SSM_Z + GROUP_W]
        post_ref[t, P_XS:P_XS + GROUP_W, :] = cv[0:GROUP_W]
        post_ref[t, P_GG:P_GG + GROUP_W, :] = p[COL_GDN_GATE:COL_GDN_GATE + GROUP_W]
        post_ref[t, P_MO:P_MO + GROUP_W, :] = p[COL_ML + 3 * GROUP_W:COL_ML + 4 * GROUP_W]
        for h in range(N_HEADS):
            hs = slice(h * HEAD_DIM, (h + 1) * HEAD_DIM)
            grp = h // (N_HEADS // 2)
            put = lambda r, v: slab_ref.__setitem__((h, t, slice(r, r + v.shape[0]), slice(None)), v)
            l2 = lambda v: v * lax.rsqrt(jnp.sum(v * v, axis=0, keepdims=True) + EPS)
            put(S_RQ, rq[hs])
            put(S_RK, rk[hs])
            put(S_RV, p[COL_RET + 2 * GROUP_W + h * HEAD_DIM:COL_RET + 2 * GROUP_W + (h + 1) * HEAD_DIM])
            put(S_XDT, cv[hs] * sp[G_DT + h:G_DT + h + 1])
            put(S_B, cv[GROUP_W + grp * HEAD_DIM:GROUP_W + (grp + 1) * HEAD_DIM])
            put(S_C, cv[GROUP_W + 2 * HEAD_DIM + grp * HEAD_DIM:GROUP_W + 2 * HEAD_DIM + (grp + 1) * HEAD_DIM])
            put(S_GQ, l2(cv[SSM_CONV_CH + h * HEAD_DIM:SSM_CONV_CH + (h + 1) * HEAD_DIM]) * HEAD_DIM ** -0.5)
            put(S_GK, l2(cv[SSM_CONV_CH + GROUP_W + h * HEAD_DIM:SSM_CONV_CH + GROUP_W + (h + 1) * HEAD_DIM]))
            put(S_GV, cv[SSM_CONV_CH + 2 * GROUP_W + h * HEAD_DIM:SSM_CONV_CH + 2 * GROUP_W + (h + 1) * HEAD_DIM])
            put(S_MQ, p[COL_ML + h * HEAD_DIM:COL_ML + (h + 1) * HEAD_DIM])
            put(S_MK, p[COL_ML + GROUP_W + h * HEAD_DIM:COL_ML + GROUP_W + (h + 1) * HEAD_DIM] * HEAD_DIM ** -0.5)
            put(S_MV, p[COL_ML + 2 * GROUP_W + h * HEAD_DIM:COL_ML + 2 * GROUP_W + (h + 1) * HEAD_DIM])
            row = lambda a, j: a[j + h:j + h + 1]
            put(S_GATE, jnp.concatenate([row(decay, G_DT), row(decay, G_GA), row(beta, G_GB), row(xb, G_MI),
                                         row(logf, G_MF), gamma[h:h + 1], zero_row, zero_row], axis=0))


def _spre(x, sandwich, w, hist, rot, colp, l):
    m = x.shape[0]
    return pl.pallas_call(
        _spre_body,
        out_shape=[jax.ShapeDtypeStruct((N_HEADS, N_TOK, SLAB_ROWS, LANES), F32),
                   jax.ShapeDtypeStruct((N_TOK, POST_ROWS, LANES), F32),
                   jax.ShapeDtypeStruct((CONV_W - 1, LANES, CONV_CH), F32)],
        grid=(1,),
        in_specs=[pl.BlockSpec((m, D_MODEL), lambda i: (0, 0)),
                  _layer_block((N_SANDWICH, D_MODEL), (l,), buffered=False),
                  _layer_block((D_MODEL, IN_PAD), (l,), buffered=False),
                  _layer_block((CONV_W - 1, LANES, CONV_CH), (l,), buffered=False),
                  pl.BlockSpec(rot.shape, lambda i: (0, 0, 0, 0)),
                  _layer_block((C_ROWS, LANES), (l,), buffered=False)],
        out_specs=[pl.BlockSpec((N_HEADS, N_TOK, SLAB_ROWS, LANES), lambda i: (0, 0, 0, 0)),
                   pl.BlockSpec((N_TOK, POST_ROWS, LANES), lambda i: (0, 0, 0)),
                   pl.BlockSpec((CONV_W - 1, LANES, CONV_CH), lambda i: (0, 0, 0))],
        compiler_params=pltpu.CompilerParams(dimension_semantics=("arbitrary",), vmem_limit_bytes=VMEM_LIMIT),
        name="spre",
    )(x, sandwich, w, hist, rot, colp)


def _srec_body(*refs, aliased):
    slab_ref, sret_in, sssm_in, sgdn_in, smc_in, smn_in, smm_in = refs[:7]
    outs = refs[7 + (6 if aliased else 0):]
    ot_ref, sret_out, sssm_out, sgdn_out, smc_out, smn_out, smm_out, y8_ref = outs
    tile = lambda t, r: slab_ref[t, r:r + HEAD_DIM, :]
    gate = lambda t, j: slab_ref[t, S_GATE + j:S_GATE + j + 1, :]
    drow = lambda t, r, d: slab_ref[t, pl.ds(r + d, 1), :]
    zeros = jnp.zeros((HEAD_DIM, LANES), F32)
    sweep = lambda body, init: lax.fori_loop(0, HEAD_DIM, body, init, unroll=8)
    blk = lambda d: pl.ds(pl.multiple_of(d * HEAD_DIM, HEAD_DIM), HEAD_DIM)

    gamma = gate(0, GT_GAMMA)
    for t in range(N_TOK):
        src = sret_in if t == 0 else sret_out
        v = tile(t, S_RV)

        def body(d, acc, t=t, src=src, v=v):
            s = src[blk(d), :] * gamma + drow(t, S_RK, d) * v
            sret_out[blk(d), :] = s
            return acc + drow(t, S_RQ, d) * s
        ot_ref[t, 0:HEAD_DIM, :] = sweep(body, zeros)

    for t in range(N_TOK):
        src = sssm_in if t == 0 else sssm_out
        a, b_t, c_t = gate(t, GT_SSM_A), tile(t, S_B), tile(t, S_C)

        def body(p, carry, t=t, src=src, a=a, b_t=b_t, c_t=c_t):
            hp = src[blk(p), :] * a + drow(t, S_XDT, p) * b_t
            sssm_out[blk(p), :] = hp
            part = jnp.sum((hp * c_t).reshape(HEAD_DIM // SUBLANES, SUBLANES, LANES), axis=0)
            y8_ref[pl.ds(pl.multiple_of(p * SUBLANES, SUBLANES), SUBLANES), :] = part
            return carry
        sweep(body, 0)
        ot_ref[t, HEAD_DIM:2 * HEAD_DIM, :] = jnp.sum(y8_ref[...].reshape(HEAD_DIM, SUBLANES, LANES), axis=1)

    ks = sweep(lambda d, acc: acc + drow(0, S_GK, d) * sgdn_in[blk(d), :], zeros)
    for t in range(N_TOK):
        src = sgdn_in if t == 0 else sgdn_out
        a = gate(t, GT_GDN_A)
        u = gate(t, GT_BETA) * (tile(t, S_GV) - a * ks)
        nxt = t + 1 < N_TOK

        def body(d, acc, t=t, src=src, a=a, u=u, nxt=nxt):
            s = src[blk(d), :] * a + drow(t, S_GK, d) * u
            sgdn_out[blk(d), :] = s
            o = acc[0] + drow(t, S_GQ, d) * s
            return (o, acc[1] + drow(t + 1, S_GK, d) * s) if nxt else (o, acc[1])
        o, ks = sweep(body, (zeros, zeros))
        ot_ref[t, 2 * HEAD_DIM:3 * HEAD_DIM, :] = o

    m_row, n_tile = smm_in[...], smn_in[...]
    for t in range(N_TOK):
        src = smc_in if t == 0 else smc_out
        logi, logf = gate(t, GT_LOGI), gate(t, GT_LOGF)
        m_new = jnp.maximum(logf + m_row, logi)
        fs, ins = jnp.exp(logf + m_row - m_new), jnp.exp(logi - m_new)
        n_tile = fs * n_tile + ins * tile(t, S_MK)
        qn = jnp.sum(tile(t, S_MQ) * n_tile, axis=0, keepdims=True)
        v = tile(t, S_MV) * ins

        def body(d, acc, t=t, src=src, fs=fs, v=v):
            c = src[blk(d), :] * fs + drow(t, S_MK, d) * v
            smc_out[blk(d), :] = c
            return acc + drow(t, S_MQ, d) * c
        num = sweep(body, zeros)
        ot_ref[t, 3 * HEAD_DIM:4 * HEAD_DIM, :] = num / jnp.maximum(jnp.abs(qn), jnp.exp(-m_new))
        m_row = m_new
    smn_out[...] = n_tile
    smm_out[...] = m_row


def _srec(slab, states, prev, l):
    depth = states[0].shape[0]
    rows = HEAD_DIM * HEAD_DIM
    st_shapes = [(rows, LANES)] * 4 + [(HEAD_DIM, LANES), (1, LANES)]
    lh = lambda tail: pl.BlockSpec((None, None) + tail, lambda h: (l, h, 0, 0))
    in_specs = [pl.BlockSpec((None, N_TOK, SLAB_ROWS, LANES), lambda h: (h, 0, 0, 0))]
    in_specs += [lh(s) for s in st_shapes]
    args = [slab, *states]
    aliases = {}
    if prev is not None:
        in_specs += [pl.BlockSpec(memory_space=pl.ANY)] * 6
        args += list(prev)
        aliases = {7 + i: 1 + i for i in range(6)}
    out_shape = [jax.ShapeDtypeStruct((N_HEADS, N_TOK, GROUP_W, LANES), F32)]
    out_shape += [jax.ShapeDtypeStruct((depth, N_HEADS) + s, F32) for s in st_shapes]
    out_specs = [pl.BlockSpec((None, N_TOK, GROUP_W, LANES), lambda h: (h, 0, 0, 0))] + [lh(s) for s in st_shapes]
    return pl.pallas_call(
        functools.partial(_srec_body, aliased=prev is not None),
        out_shape=out_shape, grid=(N_HEADS,), in_specs=in_specs, out_specs=out_specs,
        input_output_aliases=aliases,
        scratch_shapes=[pltpu.VMEM((HEAD_DIM * SUBLANES, LANES), F32)],
        compiler_params=pltpu.CompilerParams(dimension_semantics=("arbitrary",), vmem_limit_bytes=VMEM_LIMIT),
        name="srec",
    )(*args)


def _spost_body(x_ref, ot_ref, post_ref, cn_ref, g_ref, w_ref, o_ref):
    nb = LANES

    def head_norm(x):
        out = []
        for h in range(N_HEADS):
            b = x[h * HEAD_DIM:(h + 1) * HEAD_DIM]
            out.append(b * lax.rsqrt(jnp.mean(b * b, axis=0, keepdims=True) + EPS))
        return jnp.concatenate(out, axis=0)

    gain = lambda i: cn_ref[i * GROUP_W:(i + 1) * GROUP_W, :]
    for t in range(N_TOK):
        mixer = lambda i: jnp.concatenate(
            [ot_ref[h, t, i * HEAD_DIM:(i + 1) * HEAD_DIM, :] for h in range(N_HEADS)], axis=0)
        pin = lambda r: post_ref[t, r:r + GROUP_W, :]
        out_ret = head_norm(mixer(0)) * gain(0) * _silu(pin(P_RG))
        y = (mixer(1) + gain(4) * pin(P_XS)) * _silu(pin(P_Z))
        out_ssm = y * lax.rsqrt(jnp.mean(y * y, axis=0, keepdims=True) + EPS) * gain(1)
        out_gdn = head_norm(mixer(2)) * gain(2) * _silu(pin(P_GG))
        out_ml = _sigmoid(pin(P_MO)) * (head_norm(mixer(3)) * gain(3))
        mixed = jnp.concatenate([out_ret, out_ssm, out_gdn, out_ml], axis=0).T.astype(BF16)
        y = jnp.dot(mixed, w_ref[...], preferred_element_type=F32)
        rs = slice(t * nb, (t + 1) * nb)
        o_ref[rs, :] = x_ref[rs, :] + _rms(y, g_ref[3:4, :])


def _spost(x, ot, post, cn, sandwich, w, l):
    m = x.shape[0]
    return pl.pallas_call(
        _spost_body,
        out_shape=jax.ShapeDtypeStruct((m, D_MODEL), F32),
        grid=(1,),
        in_specs=[pl.BlockSpec((m, D_MODEL), lambda i: (0, 0)),
                  pl.BlockSpec(ot.shape, lambda i: (0, 0, 0, 0)),
                  pl.BlockSpec(post.shape, lambda i: (0, 0, 0)),
                  _layer_block((N_ROWS, LANES), (l,), buffered=False),
                  _layer_block((N_SANDWICH, D_MODEL), (l,), buffered=False),
                  _layer_block((D_MODEL, D_MODEL), (l,), buffered=False)],
        out_specs=pl.BlockSpec((m, D_MODEL), lambda i: (0, 0)),
        compiler_params=pltpu.CompilerParams(dimension_semantics=("arbitrary",), vmem_limit_bytes=VMEM_LIMIT),
        name="spost",
    )(x, ot, post, cn, sandwich, w)


def _rot_tables(pos):
    half = HEAD_DIM // 2
    inv = ROPE_BASE ** (-jnp.arange(half, dtype=F32) / half)
    ang = pos.astype(F32)[:, None] * inv[None, :]
    cos, sin = jnp.cos(ang), jnp.sin(ang)
    cos_full = jnp.tile(jnp.concatenate([cos, cos], axis=-1), (1, N_HEADS))
    sin_signed = jnp.tile(jnp.concatenate([-sin, sin], axis=-1), (1, N_HEADS))
    return cos_full, sin_signed


def _small_params(ret_norm, ssm_conv_w, ssm_conv_b, ssm_dt_bias, ssm_A_log, ssm_D, ssm_norm, gdn_conv_w,
                  gdn_dt_bias, gdn_A_log, gdn_norm, mlstm_i_bias, mlstm_f_bias, mlstm_norm):
    depth = ret_norm.shape[0]
    z = lambda n: jnp.zeros((depth, n), F32)
    pad_lanes = z(LANES - 5 * N_HEADS)
    bias = jnp.concatenate([ssm_dt_bias, gdn_dt_bias, z(N_HEADS), mlstm_i_bias, mlstm_f_bias, pad_lanes], axis=1)
    alog = jnp.concatenate([ssm_A_log, gdn_A_log, z(3 * N_HEADS), pad_lanes], axis=1)
    gvec = jnp.stack([bias, alog], axis=1)
    convw = jnp.concatenate([ssm_conv_w, gdn_conv_w], axis=2)
    convb = jnp.concatenate([ssm_conv_b, z(CONV_CH - SSM_CONV_CH)], axis=1)[:, None, :]
    dvec = jnp.repeat(ssm_D, HEAD_DIM, axis=1)[:, None, :]
    norms = jnp.stack([ret_norm, ssm_norm, gdn_norm, mlstm_norm], axis=1)
    return gvec, convw, convb, dvec, norms


def kernel(x_prompt, x_sample, state_ret, state_ssm, state_ssm_conv, state_gdn, state_gdn_conv,
           state_mlstm_C, state_mlstm_n, state_mlstm_m, w_in, w_out, norm_sandwich, ffn_w_gate, ffn_w_up,
           ffn_w_down, ret_norm, ssm_conv_w, ssm_conv_b, ssm_dt_bias, ssm_A_log, ssm_D, ssm_norm, gdn_conv_w,
           gdn_dt_bias, gdn_A_log, gdn_norm, mlstm_i_bias, mlstm_f_bias, mlstm_norm):
    depth = w_in.shape[0]
    bp, lp, _ = x_prompt.shape
    bs, ls, _ = x_sample.shape
    assert (bs, ls) == (LANES, N_TOK)
    w_in_p = _prep_win(w_in)
    w_out_b = w_out.astype(BF16)
    wg, wu, wd = ffn_w_gate, ffn_w_up, ffn_w_down.astype(BF16)
    small = _small_params(ret_norm, ssm_conv_w, ssm_conv_b, ssm_dt_bias, ssm_A_log, ssm_D, ssm_norm, gdn_conv_w,
                          gdn_dt_bias, gdn_A_log, gdn_norm, mlstm_i_bias, mlstm_f_bias, mlstm_norm)
    gvec, convw, convb, dvec, norms = small

    cos_p, sin_p = _rot_tables(jnp.arange(lp, dtype=jnp.int32))
    x = x_prompt.reshape(bp * lp, D_MODEL)
    st_p = []
    for l in range(depth):
        x = _ffn(x, norm_sandwich, wg, wu, wd, l, 0)
        pre, convraw = _pproj(x, norm_sandwich, w_in_p, cos_p, sin_p, small, l, nbatch=bp, seqlen=lp)
        outs = _mix_prompt(pre, small, l, nbatch=bp, seqlen=lp)
        x = _ffn(x, norm_sandwich, wg, wu, wd, l, 1, mixed=outs[0], w_out=w_out_b)
        st_p.append([convraw] + outs[1:])
    y_prompt = x.reshape(bp, lp, D_MODEL)
    conv_p = jnp.stack([s[0].reshape(bp, -1, CONV_CH)[:, -(CONV_W - 1):, :] for s in st_p])
    stack_p = lambda k: jnp.stack([s[k] for s in st_p])
    out_p = (stack_p(1), stack_p(2), conv_p[..., :SSM_CONV_CH], stack_p(3), conv_p[..., SSM_CONV_CH:],
             stack_p(4), stack_p(5), stack_p(6)[:, :, 0, :])

    lanes = lambda a: jnp.broadcast_to(a[..., None], a.shape + (LANES,))
    cos_s, sin_s = _rot_tables(PAST_LEN + jnp.arange(ls, dtype=jnp.int32))
    rot = lanes(jnp.stack([cos_s, sin_s]))
    lg = np.log(1.0 - np.exp2(-5.0 - np.arange(N_HEADS, dtype=np.float64)))
    gamma = jnp.asarray(np.concatenate([np.exp(lg), np.zeros(SUBLANES - N_HEADS)]), F32)
    colp = lanes(jnp.concatenate(
        [convw.reshape(depth, CONV_W * CONV_CH), convb[:, 0, :], gvec[:, 0, :4 * SUBLANES], gvec[:, 1, :4 * SUBLANES],
         jnp.broadcast_to(gamma, (depth, SUBLANES))], axis=1))
    cn = lanes(jnp.concatenate([norms.reshape(depth, 4 * GROUP_W), dvec[:, 0, :]], axis=1))
    hist = jnp.transpose(jnp.concatenate([state_ssm_conv, state_gdn_conv], axis=-1), (0, 2, 1, 3))
    mat = lambda s: jnp.transpose(s, (0, 2, 3, 4, 1)).reshape(depth, N_HEADS, HEAD_DIM * HEAD_DIM, bs)
    states_s = (mat(state_ret), mat(state_ssm), mat(state_gdn), mat(state_mlstm_C),
                jnp.transpose(state_mlstm_n, (0, 2, 3, 1)), jnp.transpose(state_mlstm_m, (0, 2, 1))[:, :, None, :])
    x = jnp.transpose(x_sample, (1, 0, 2)).reshape(ls * bs, D_MODEL)
    prev, convs = None, []
    for l in range(depth):
        x = _ffn(x, norm_sandwich, wg, wu, wd, l, 0)
        slab, post, convraw = _spre(x, norm_sandwich, w_in_p, hist, rot, colp, l)
        outs = _srec(slab, states_s, prev, l)
        prev = outs[1:]
        x = _spost(x, outs[0], post, cn, norm_sandwich, w_out_b, l)
        x = _ffn(x, norm_sandwich, wg, wu, wd, l, 1)
        convs.append(convraw)
    y_sample = jnp.transpose(x.reshape(ls, bs, D_MODEL), (1, 0, 2))
    conv_s = jnp.transpose(jnp.stack(convs), (0, 2, 1, 3))
    unmat = lambda s: jnp.transpose(s.reshape(depth, N_HEADS, HEAD_DIM, HEAD_DIM, bs), (0, 4, 1, 2, 3))
    out_s = (unmat(prev[0]), unmat(prev[1]), conv_s[..., :SSM_CONV_CH], unmat(prev[2]), conv_s[..., SSM_CONV_CH:],
             unmat(prev[3]), jnp.transpose(prev[4], (0, 3, 1, 2)), jnp.transpose(prev[5][:, :, 0, :], (0, 2, 1)))
    return (y_prompt, y_sample) + out_p + out_s
```

```python
import functools
import math

import numpy as np
import jax
import jax.numpy as jnp
from jax import lax
from jax.experimental import pallas as pl
from jax.experimental.pallas import tpu as pltpu

F32 = jnp.float32
BF16 = jnp.bfloat16

D_MODEL = 1024
N_HEADS = 4
HEAD_DIM = 64
GROUP_W = N_HEADS * HEAD_DIM
D_FF = 2816
CONV_W = 4
EPS = 1e-6
ROPE_BASE = 10000.0
PAST_LEN = 16384
PROMPT_CHUNK = 64

COL_RET = 0
COL_SSM_Z = 1024
COL_CONV = 1280
CONV_CH = 1280
SSM_CONV_CH = 512
COL_GDN_GATE = 2560
COL_ML = 2816
COL_GATES = 3840
IN_PAD = 3968
G_DT, G_GA, G_GB, G_MI, G_MF, G_DTV = 0, 4, 8, 12, 16, 20

SUBLANES = 8
LANES = 128
VMEM_LIMIT = 56 * 1024 * 1024

FF_CHUNK = 256
TOKEN_TILE = 512
PROMPT_TILE = 512
SAMPLE_PAD = 8
SAMPLE_GROUP = 8
NEG_BIG = -1e30


def _sigmoid(x):
    return 1.0 / (1.0 + jnp.exp(-x))


def _silu(x):
    return x * _sigmoid(x)


def _softplus(x):
    return jnp.maximum(x, 0.0) + jnp.log1p(jnp.exp(-jnp.abs(x)))


def _rms(x, g):
    return x * lax.rsqrt(jnp.mean(x * x, axis=-1, keepdims=True) + EPS) * g


def _dot(a, b):
    return jnp.dot(a.astype(BF16), b.astype(BF16), preferred_element_type=F32)


def _dot_nt(a, b):
    return lax.dot_general(a.astype(BF16), b.astype(BF16), (((1,), (1,)), ((), ())),
                           preferred_element_type=F32)


def _dot_tn(a, b):
    return lax.dot_general(a.astype(BF16), b.astype(BF16), (((0,), (0,)), ((), ())),
                           preferred_element_type=F32)


def _split3(x):
    hi = x.astype(BF16)
    r = x - hi.astype(F32)
    mid = r.astype(BF16)
    lo = (r - mid.astype(F32)).astype(BF16)
    return hi, mid, lo


def _dot_sel(sel, x):
    hi, mid, lo = _split3(x)
    f = lambda p: jnp.dot(sel, p, preferred_element_type=F32)
    return (f(lo) + f(mid)) + f(hi)


def _dot_sel_r(x, sel):
    hi, mid, lo = _split3(x)
    f = lambda p: jnp.dot(p, sel, preferred_element_type=F32)
    return (f(lo) + f(mid)) + f(hi)


def _dot_sel_nt(sel, x):
    hi, mid, lo = _split3(x)
    f = lambda p: lax.dot_general(sel, p, (((1,), (1,)), ((), ())), preferred_element_type=F32)
    return (f(lo) + f(mid)) + f(hi)


N_SANDWICH = 6


def _layer_block(tail, lead, ngrid=1, buffered=True):
    idx = tuple(lead) + (0,) * len(tail)
    imap = (lambda i: idx) if ngrid == 1 else (lambda i, j: idx)
    kw = dict(pipeline_mode=pl.Buffered(1)) if buffered else {}
    return pl.BlockSpec((None,) * len(lead) + tuple(tail), imap, **kw)


WIN_SEGMENTS = ((0, 0, 1792), (1792, 1796, 1024), (2816, 2828, 1024))
WIN_GATES = ((1792, 1796), (2820, 2828), (3852, 3860))


def _prep_win_body(w_ref, o_ref):
    l = pl.program_id(0)
    step = 2 * LANES
    for dst, src, n in WIN_SEGMENTS:
        for c in range(0, n, step):
            o_ref[:, dst + c:dst + c + step] = w_ref[src + c:src + c + step, l, :].T.astype(BF16)
    gates = jnp.concatenate([w_ref[a:b, l, :] for a, b in WIN_GATES]
                            + [jnp.zeros((LANES - 5 * N_HEADS, D_MODEL), F32)], axis=0)
    o_ref[:, COL_GATES:IN_PAD] = gates.T.astype(BF16)


def _prep_win(w_in):
    depth, _, in_dim = w_in.shape
    w_t = jnp.transpose(w_in, (2, 0, 1))
    return pl.pallas_call(
        _prep_win_body,
        out_shape=jax.ShapeDtypeStruct((depth, D_MODEL, IN_PAD), BF16),
        grid=(depth,),
        in_specs=[pl.BlockSpec((in_dim, depth, D_MODEL), lambda l: (0, 0, 0), pipeline_mode=pl.Buffered(1))],
        out_specs=pl.BlockSpec((None, D_MODEL, IN_PAD), lambda l: (l, 0, 0)),
        compiler_params=pltpu.CompilerParams(dimension_semantics=("arbitrary",), vmem_limit_bytes=VMEM_LIMIT),
        name="prep_win",
    )(w_t)


def _ffn_body(*refs, k, mixer_out):
    if mixer_out:
        x_ref, m_ref, wo_ref, g_ref, wg_ref, wu_ref, wd_ref, o_ref, a_ref = refs
        y = jnp.dot(m_ref[...], wo_ref[...], preferred_element_type=F32)
        x = x_ref[...] + _rms(y, g_ref[3:4, :])
    else:
        x_ref, g_ref, wg_ref, wu_ref, wd_ref, o_ref, a_ref = refs
        x = x_ref[...]
    h = _rms(x, g_ref[4 * k:4 * k + 1, :]).astype(BF16)
    for j in range(D_FF // FF_CHUNK):
        sl = slice(j * FF_CHUNK, (j + 1) * FF_CHUNK)
        gt = jnp.dot(h, wg_ref[:, sl].astype(BF16), preferred_element_type=F32)
        up = jnp.dot(h, wu_ref[:, sl].astype(BF16), preferred_element_type=F32)
        a_ref[:, sl] = (_silu(gt) * up).astype(BF16)
    y = jnp.dot(a_ref[...], wd_ref[...], preferred_element_type=F32)
    o_ref[...] = x + 0.5 * _rms(y, g_ref[4 * k + 1:4 * k + 2, :])


def _ffn(x, sandwich, wg, wu, wd, l, k, mixed=None, w_out=None):
    m = x.shape[0]
    tm = min(TOKEN_TILE, m)
    tile = pl.BlockSpec((tm, D_MODEL), lambda i: (i, 0))
    fused = mixed is not None
    in_specs = [tile] + ([tile, _layer_block((D_MODEL, D_MODEL), (l,))] if fused else [])
    in_specs += [_layer_block((N_SANDWICH, D_MODEL), (l,)),
                 _layer_block((D_MODEL, D_FF), (l, k)),
                 _layer_block((D_MODEL, D_FF), (l, k)),
                 _layer_block((D_FF, D_MODEL), (l, k))]
    args = (x,) + ((mixed, w_out) if fused else ()) + (sandwich, wg, wu, wd)
    return pl.pallas_call(
        functools.partial(_ffn_body, k=k, mixer_out=fused),
        out_shape=jax.ShapeDtypeStruct((m, D_MODEL), F32),
        grid=(m // tm,),
        in_specs=in_specs,
        out_specs=tile,
        scratch_shapes=[pltpu.VMEM((tm, D_FF), BF16)],
        compiler_params=pltpu.CompilerParams(dimension_semantics=("parallel",),
                                             vmem_limit_bytes=VMEM_LIMIT),
        name="ffn_out" if fused else "ffn",
    )(*args)


def _inproj_body(x_ref, g_ref, w_ref, o_ref):
    h = _rms(x_ref[...], g_ref[2:3, :]).astype(BF16)
    o_ref[...] = jnp.dot(h, w_ref[...], preferred_element_type=F32)


def _inproj(x, sandwich, w, l):
    m = x.shape[0]
    tm = min(TOKEN_TILE, m)
    return pl.pallas_call(
        _inproj_body,
        out_shape=jax.ShapeDtypeStruct((m, IN_PAD), F32),
        grid=(m // tm,),
        in_specs=[pl.BlockSpec((tm, D_MODEL), lambda i: (i, 0)),
                  _layer_block((N_SANDWICH, D_MODEL), (l,)),
                  _layer_block((D_MODEL, IN_PAD), (l,))],
        out_specs=pl.BlockSpec((tm, IN_PAD), lambda i: (i, 0)),
        compiler_params=pltpu.CompilerParams(dimension_semantics=("parallel",),
                                             vmem_limit_bytes=VMEM_LIMIT),
        name="inproj",
    )(x, sandwich, w)


def _outproj_body(x_ref, m_ref, g_ref, w_ref, o_ref):
    y = jnp.dot(m_ref[...], w_ref[...], preferred_element_type=F32)
    o_ref[...] = x_ref[...] + _rms(y, g_ref[3:4, :])


def _outproj(x, mixed, sandwich, w, l):
    m = x.shape[0]
    tm = min(TOKEN_TILE, m)
    return pl.pallas_call(
        _outproj_body,
        out_shape=jax.ShapeDtypeStruct((m, D_MODEL), F32),
        grid=(m // tm,),
        in_specs=[pl.BlockSpec((tm, D_MODEL), lambda i: (i, 0)),
                  pl.BlockSpec((tm, D_MODEL), lambda i: (i, 0)),
                  _layer_block((N_SANDWICH, D_MODEL), (l,)),
                  _layer_block((D_MODEL, D_MODEL), (l,))],
        out_specs=pl.BlockSpec((tm, D_MODEL), lambda i: (i, 0)),
        compiler_params=pltpu.CompilerParams(dimension_semantics=("parallel",),
                                             vmem_limit_bytes=VMEM_LIMIT),
        name="outproj",
    )(x, mixed, sandwich, w)


def _head(x, h):
    return x[:, h * HEAD_DIM:(h + 1) * HEAD_DIM]


def _neumann_inverse(amat, c):
    row = lax.broadcasted_iota(jnp.int32, (c, c), 0)
    col = lax.broadcasted_iota(jnp.int32, (c, c), 1)
    eye = (row == col).astype(F32)
    m = -amat
    p = eye + m
    for _ in range(max(int(math.ceil(math.log2(c))) - 1, 0)):
        m = _dot(m, m)
        p = p + _dot(p, m)
    return p


def _mix_body(*refs, sample, rows, chunk, ret_carry):
    nchunks = rows // chunk
    it = iter(refs)
    proj_ref = next(it)
    hist_ref = next(it) if sample else None
    cos_ref, sin_ref = next(it), next(it)
    rinter_ref, rtail_ref, rintra_ref = next(it), next(it), next(it)
    tri_ref, sel_ref, ones_ref = next(it), next(it), next(it)
    gvec_ref, convw_ref, convb_ref, dvec_ref, norms_ref = next(it), next(it), next(it), next(it), next(it)
    if sample:
        sret_in, sssm_in, sgdn_in, smc_in, smn_in, smm_in = (next(it) for _ in range(6))
    mixed_ref, convraw_ref = next(it), next(it)
    sret_ref, sssm_ref, sgdn_ref, smc_ref, smn_ref, smm_ref = (next(it) for _ in range(6))
    e_ref, g_ref, rq_ref, rk_ref, sx_ref, sbc_ref, gq_ref, gk_ref, gv_ref, o_ref = (next(it) for _ in range(10))
    if not sample:
        sret_in, sssm_in, sgdn_in, smc_in, smn_in, smm_in = (
            sret_ref, sssm_ref, sgdn_ref, smc_ref, smn_ref, smm_ref)

    if sample:
        rowid = lax.broadcasted_iota(jnp.int32, (rows, 1), 0)
        valid = (rowid % SAMPLE_PAD) >= (SAMPLE_PAD - 4)
        e_ref[0:SUBLANES, :] = jnp.zeros((SUBLANES, CONV_CH), F32)
    else:
        valid = None

        @pl.when(pl.program_id(1) == 0)
        def _():
            e_ref[0:SUBLANES, :] = jnp.zeros((SUBLANES, CONV_CH), F32)
            sret_ref[...] = jnp.zeros_like(sret_ref)
            sssm_ref[...] = jnp.zeros_like(sssm_ref)
            sgdn_ref[...] = jnp.zeros_like(sgdn_ref)
            smc_ref[...] = jnp.zeros_like(smc_ref)
            smn_ref[...] = jnp.zeros_like(smn_ref)
            smm_ref[...] = jnp.zeros_like(smm_ref)

    def keep(x, fill=0.0):
        return x if valid is None else jnp.where(valid, x, fill)

    lane = lax.broadcasted_iota(jnp.int32, (1, LANES), 1)
    in_lanes = lambda lo: (lane >= lo) & (lane < lo + N_HEADS)
    graw = proj_ref[:, COL_GATES:COL_GATES + LANES]
    xb = graw + gvec_ref[0:1, :]
    sp = _softplus(xb)
    neg_a = -jnp.exp(gvec_ref[1:2, :])
    logf = -_softplus(-xb)
    pre = jnp.where(lane < 2 * N_HEADS, neg_a * sp, jnp.where(in_lanes(G_MF), logf, 0.0))
    cum = _dot_sel(tri_ref[...], keep(pre))
    dtv = pltpu.roll(sp, G_DTV - G_DT, axis=1)
    aux = jnp.where(in_lanes(G_GB), _sigmoid(graw),
                    jnp.where(in_lanes(G_MI), xb, jnp.where(in_lanes(G_DTV), dtv, 0.0)))
    if sample:
        aux = jnp.where(valid, aux, jnp.where(in_lanes(G_MI), NEG_BIG, 0.0))
    g_ref[...] = jnp.where(in_lanes(G_GB) | in_lanes(G_MI) | in_lanes(G_DTV), aux, cum)

    lane_g = lax.broadcasted_iota(jnp.int32, (1, GROUP_W), 1)
    first_half = (lane_g % HEAD_DIM) < (HEAD_DIM // 2)

    def rotary(x):
        swapped = jnp.where(first_half, pltpu.roll(x, GROUP_W - HEAD_DIM // 2, axis=1),
                            pltpu.roll(x, HEAD_DIM // 2, axis=1))
        return x * cos_ref[...] + swapped * sin_ref[...]

    rq_ref[...] = rotary(proj_ref[:, COL_RET:COL_RET + GROUP_W])
    rk_ref[...] = keep(rotary(proj_ref[:, COL_RET + GROUP_W:COL_RET + 2 * GROUP_W]) * HEAD_DIM ** -0.5)

    raw = proj_ref[:, COL_CONV:COL_CONV + CONV_CH]
    if sample:
        raw = raw + hist_ref[...]
    e_ref[SUBLANES:SUBLANES + rows, :] = raw
    conv = raw * convw_ref[CONV_W - 1:CONV_W, :]
    for s in range(1, CONV_W):
        conv = conv + e_ref[SUBLANES - s:SUBLANES - s + rows, :] * convw_ref[CONV_W - 1 - s:CONV_W - s, :]
    conv = keep(_silu(conv + convb_ref[...]))
    if sample:
        convraw_ref[...] = raw
    else:
        convraw_ref[...] = e_ref[rows:rows + SUBLANES, :]
        e_ref[0:SUBLANES, :] = e_ref[rows:rows + SUBLANES, :]
    sx_ref[...] = conv[:, 0:GROUP_W]
    sbc_ref[...] = conv[:, GROUP_W:SSM_CONV_CH]
    ones_bd = ones_ref[...]

    def head_sumsq(x):
        return _dot_sel_r(x * x, ones_bd)

    gq = conv[:, SSM_CONV_CH:SSM_CONV_CH + GROUP_W]
    gk = conv[:, SSM_CONV_CH + GROUP_W:SSM_CONV_CH + 2 * GROUP_W]
    gq_ref[...] = gq * lax.rsqrt(head_sumsq(gq) + EPS) * HEAD_DIM ** -0.5
    gk_ref[...] = gk * lax.rsqrt(head_sumsq(gk) + EPS)
    gv_ref[...] = conv[:, SSM_CONV_CH + 2 * GROUP_W:CONV_CH]

    r_i = lax.broadcasted_iota(jnp.int32, (chunk, chunk), 0)
    c_i = lax.broadcasted_iota(jnp.int32, (chunk, chunk), 1)
    tri = r_i >= c_i
    strict = r_i > c_i
    lane_h = lax.broadcasted_iota(jnp.int32, (1, N_HEADS), 1)

    def chunk_step(i, carry):
        r0 = pl.multiple_of(i * chunk, chunk)
        rs = pl.ds(r0, chunk)
        sidx = i if sample else 0
        gc = g_ref[rs, :]
        gt = _dot_sel_nt(sel_ref[...], gc)
        col = lambda j: gc[:, j:j + 1]
        rowv = lambda j: gt[j:j + 1, :]
        last = lambda j: gc[chunk - 1:chunk, j:j + 1]

        heads = range(N_HEADS)
        s_ret = [sret_in[sidx, h] for h in heads]
        s_ssm = [sssm_in[sidx, h] for h in heads]
        s_gdn = [sgdn_in[sidx, h] for h in heads]
        s_mc = [smc_in[sidx, h] for h in heads]
        n_all = smn_in[sidx]
        m_row = smm_in[sidx]
        rq_all, rk_all = rq_ref[rs, :], rk_ref[rs, :]
        rv_all = proj_ref[rs, COL_RET + 2 * GROUP_W:COL_RET + 3 * GROUP_W]
        x_all, bc_all = sx_ref[rs, :], sbc_ref[rs, :]
        gq_all, gk_all, gv_all = gq_ref[rs, :], gk_ref[rs, :], gv_ref[rs, :]
        mq_all = proj_ref[rs, COL_ML:COL_ML + GROUP_W]
        mk_all = proj_ref[rs, COL_ML + GROUP_W:COL_ML + 2 * GROUP_W] * HEAD_DIM ** -0.5
        mv_all = proj_ref[rs, COL_ML + 2 * GROUP_W:COL_ML + 3 * GROUP_W]

        qi_all = rq_all * rinter_ref[...]
        kt_all = rk_all * rtail_ref[...]
        o_ret, n_ret = [], []
        for h in heads:
            att = _dot_nt(_head(rq_all, h), _head(rk_all, h)) * rintra_ref[h]
            o_ret.append(_dot(att, _head(rv_all, h)) + _dot(_head(qi_all, h), s_ret[h]))
            n_ret.append(s_ret[h] * ret_carry[h] + _dot_tn(_head(kt_all, h), _head(rv_all, h)))

        o_ssm, n_ssm = [], []
        for h in heads:
            grp = h // (N_HEADS // 2)
            b_g = bc_all[:, grp * HEAD_DIM:(grp + 1) * HEAD_DIM]
            c_g = bc_all[:, 2 * HEAD_DIM + grp * HEAD_DIM:2 * HEAD_DIM + (grp + 1) * HEAD_DIM]
            x_h = _head(x_all, h)
            cum_c, cum_r, cum_l = col(G_DT + h), rowv(G_DT + h), last(G_DT + h)
            lmat = jnp.exp(jnp.where(tri, cum_c - cum_r, -jnp.inf))
            scores = _dot_nt(c_g, b_g) * lmat * rowv(G_DTV + h)
            o_ssm.append(_dot(scores, x_h) + _dot_nt(c_g, s_ssm[h]) * jnp.exp(cum_c))
            w_c = jnp.exp(cum_l - cum_c) * col(G_DTV + h)
            n_ssm.append(s_ssm[h] * jnp.exp(cum_l) + _dot_tn(x_h * w_c, b_g))

        o_gdn, n_gdn = [], []
        for h in heads:
            q_h, k_h, v_h = _head(gq_all, h), _head(gk_all, h), _head(gv_all, h)
            cum_c, cum_r, cum_l = col(G_GA + h), rowv(G_GA + h), last(G_GA + h)
            beta_c = col(G_GB + h)
            gam = jnp.exp(jnp.where(tri, cum_c - cum_r, -jnp.inf))
            amat = _dot_nt(k_h, k_h) * jnp.where(strict, gam, 0.0) * beta_c
            ecum = jnp.exp(cum_c)
            tinv = _neumann_inverse(amat, chunk)
            u = _dot(tinv, v_h * beta_c)
            w = _dot(tinv, k_h * (beta_c * ecum))
            v_new = u - _dot(w, s_gdn[h])
            qk = _dot_nt(q_h, k_h) * gam
            o_gdn.append(_dot(q_h * ecum, s_gdn[h]) + _dot(qk, v_new))
            n_gdn.append(s_gdn[h] * jnp.exp(cum_l) + _dot_tn(k_h * jnp.exp(cum_l - cum_c), v_new))

        m_new_row = jnp.zeros((1, N_HEADS), F32)
        o_ml, n_mc, n_mn = [], [], []
        for h in heads:
            q_h, k_h, v_h = _head(mq_all, h), _head(mk_all, h), _head(mv_all, h)
            b_c, b_r, b_l = col(G_MF + h), rowv(G_MF + h), last(G_MF + h)
            i_c, i_r = col(G_MI + h), rowv(G_MI + h)
            m_old = m_row[:, h:h + 1]
            n_h = n_all[h:h + 1, :]
            dmat = jnp.where(tri, b_c - b_r + i_r, -jnp.inf)
            m_t = jnp.maximum(b_c + m_old, jnp.max(dmat, axis=-1, keepdims=True))
            wts = jnp.exp(dmat - m_t)
            inter = jnp.exp(b_c + m_old - m_t)
            qk = _dot_nt(q_h, k_h) * wts
            num = _dot(qk, v_h) + inter * _dot(q_h, s_mc[h])
            den = jnp.sum(qk, axis=-1, keepdims=True) + inter * jnp.sum(q_h * n_h, axis=-1, keepdims=True)
            o_ml.append(num / jnp.maximum(jnp.abs(den), jnp.exp(-m_t)))
            m_new = m_t[chunk - 1:chunk, :]
            ws = jnp.exp(b_l - b_c + i_c - m_new)
            cs = jnp.exp(b_l + m_old - m_new)
            kw = k_h * ws
            n_mc.append(s_mc[h] * cs + _dot_tn(kw, v_h))
            n_mn.append(n_h * cs + jnp.sum(kw, axis=0, keepdims=True))
            m_new_row = jnp.where(lane_h == h, m_new, m_new_row)

        o_ref[rs, :] = jnp.concatenate(o_ret + o_ssm + o_gdn + o_ml, axis=1)
        for h in heads:
            sret_ref[sidx, h] = n_ret[h]
            sssm_ref[sidx, h] = n_ssm[h]
            sgdn_ref[sidx, h] = n_gdn[h]
            smc_ref[sidx, h] = n_mc[h]
        smn_ref[sidx] = jnp.concatenate(n_mn, axis=0)
        smm_ref[sidx] = m_new_row
        return carry

    lax.fori_loop(0, nchunks, chunk_step, 0)

    def head_norm(x, g):
        return x * lax.rsqrt(head_sumsq(x) * (1.0 / HEAD_DIM) + EPS) * g

    out_ret = head_norm(o_ref[:, 0:GROUP_W], norms_ref[0:1, :]) * _silu(
        proj_ref[:, COL_RET + 3 * GROUP_W:COL_RET + 4 * GROUP_W])
    mixed_ref[:, 0:GROUP_W] = keep(out_ret).astype(BF16)
    y = o_ref[:, GROUP_W:2 * GROUP_W] + dvec_ref[...] * sx_ref[...]
    out_ssm = _rms(y * _silu(proj_ref[:, COL_SSM_Z:COL_SSM_Z + GROUP_W]), norms_ref[1:2, :])
    mixed_ref[:, GROUP_W:2 * GROUP_W] = keep(out_ssm).astype(BF16)
    out_gdn = head_norm(o_ref[:, 2 * GROUP_W:3 * GROUP_W], norms_ref[2:3, :]) * _silu(
        proj_ref[:, COL_GDN_GATE:COL_GDN_GATE + GROUP_W])
    mixed_ref[:, 2 * GROUP_W:3 * GROUP_W] = keep(out_gdn).astype(BF16)
    out_ml = _sigmoid(proj_ref[:, COL_ML + 3 * GROUP_W:COL_ML + 4 * GROUP_W]) * head_norm(
        o_ref[:, 3 * GROUP_W:4 * GROUP_W], norms_ref[3:4, :])
    mixed_ref[:, 3 * GROUP_W:4 * GROUP_W] = keep(out_ml).astype(BF16)


NEXP = 6
X_CSSM, X_DT, X_CGDN, X_BETA, X_B, X_IMB = range(NEXP)


def _split2(x):
    hi = x.astype(BF16)
    return hi, (x - hi.astype(F32)).astype(BF16)


def _dot_sel_r2(x, sel):
    hi, lo = _split2(x)
    return jnp.dot(lo, sel, preferred_element_type=F32) + jnp.dot(hi, sel, preferred_element_type=F32)


(PB_RQ, PB_RK, PB_RV, PB_RG, PB_Z, PB_SX, PB_SBC, PB_GQ, PB_GK, PB_GV, PB_GG, PB_MQ, PB_MK, PB_MV, PB_MO,
 PB_X0) = range(16)
PRE_COLS = (PB_X0 + NEXP) * GROUP_W


def _pproj_body(x_ref, g_ref, w_ref, cos_ref, sin_ref, tri_ref, expand_ref, maskbd_ref, gvec_ref, convw_ref,
                convb_ref, pre_ref, convraw_ref, e_ref, *, rows):
    @pl.when(pl.program_id(1) == 0)
    def _():
        e_ref[0:SUBLANES, :] = jnp.zeros((SUBLANES, CONV_CH), F32)

    h_in = _rms(x_ref[...], g_ref[2:3, :]).astype(BF16)
    proj = lambda a, n: jnp.dot(h_in, w_ref[:, a:a + n], preferred_element_type=F32)

    def put(b, v):
        pre_ref[:, b * GROUP_W:(b + 1) * GROUP_W] = v

    maskbd_b = maskbd_ref[...].astype(BF16)
    seg_sum = lambda x: _dot_sel_r2(x, maskbd_b)

    lane = lax.broadcasted_iota(jnp.int32, (1, LANES), 1)
    in_lanes = lambda lo: (lane >= lo) & (lane < lo + N_HEADS)
    graw = proj(COL_GATES, LANES)
    put(PB_MQ, proj(COL_ML, GROUP_W))
    put(PB_MK, proj(COL_ML + GROUP_W, GROUP_W) * HEAD_DIM ** -0.5)
    put(PB_MV, proj(COL_ML + 2 * GROUP_W, GROUP_W))
    put(PB_MO, proj(COL_ML + 3 * GROUP_W, GROUP_W))
    xb = graw + gvec_ref[0:1, :]
    sp = _softplus(xb)
    neg_a = -jnp.exp(gvec_ref[1:2, :])
    logf = -_softplus(-xb)
    pre = jnp.where(lane < 2 * N_HEADS, neg_a * sp, jnp.where(in_lanes(G_MF), logf, 0.0))
    parts = jnp.dot(tri_ref[...], jnp.concatenate(_split3(pre), axis=1), preferred_element_type=F32)
    cum = (parts[:, 2 * LANES:] + parts[:, LANES:2 * LANES]) + parts[:, :LANES]
    imb = xb - pltpu.roll(cum, LANES - (G_MF - G_MI), axis=1)
    dtv = pltpu.roll(sp, G_DTV - G_DT, axis=1)
    gates = jnp.where(in_lanes(G_GB), _sigmoid(graw),
                      jnp.where(in_lanes(G_MI), imb, jnp.where(in_lanes(G_DTV), dtv, cum)))
    gates3 = jnp.concatenate(_split3(gates), axis=1)

    lane_g = lax.broadcasted_iota(jnp.int32, (1, GROUP_W), 1)
    first_half = (lane_g % HEAD_DIM) < (HEAD_DIM // 2)

    def rotary(x):
        swapped = jnp.where(first_half, pltpu.roll(x, GROUP_W - HEAD_DIM // 2, axis=1),
                            pltpu.roll(x, HEAD_DIM // 2, axis=1))
        return x * cos_ref[...] + swapped * sin_ref[...]

    def conv_in(c):
        e_ref[SUBLANES:SUBLANES + rows, c * GROUP_W:(c + 1) * GROUP_W] = proj(COL_CONV + c * GROUP_W, GROUP_W)

    def conv_out(c):
        cs = slice(c * GROUP_W, (c + 1) * GROUP_W)
        acc = e_ref[SUBLANES:SUBLANES + rows, cs] * convw_ref[CONV_W - 1:CONV_W, cs]
        for s in range(1, CONV_W):
            acc = acc + e_ref[SUBLANES - s:SUBLANES - s + rows, cs] * convw_ref[CONV_W - 1 - s:CONV_W - s, cs]
        return _silu(acc + convb_ref[:, cs])

    def expand(j):
        put(PB_X0 + j, jnp.dot(gates3, expand_ref[:, j * GROUP_W:(j + 1) * GROUP_W], preferred_element_type=F32))

    conv_in(0)
    conv_in(1)
    expand(0)
    put(PB_SX, conv_out(0))
    conv_in(2)
    expand(1)
    put(PB_SBC, conv_out(1))
    conv_in(3)
    expand(2)
    gq = conv_out(2)
    put(PB_GQ, gq * lax.rsqrt(seg_sum(gq * gq) + EPS) * HEAD_DIM ** -0.5)
    conv_in(4)
    expand(3)
    gk = conv_out(3)
    put(PB_GK, gk * lax.rsqrt(seg_sum(gk * gk) + EPS))
    rq = proj(COL_RET, GROUP_W)
    expand(4)
    put(PB_GV, conv_out(4))
    convraw_ref[...] = e_ref[rows:rows + SUBLANES, :]
    e_ref[0:SUBLANES, :] = e_ref[rows:rows + SUBLANES, :]
    rk = proj(COL_RET + GROUP_W, GROUP_W)
    expand(5)
    put(PB_RQ, rotary(rq))
    put(PB_RV, proj(COL_RET + 2 * GROUP_W, GROUP_W))
    put(PB_RG, proj(COL_RET + 3 * GROUP_W, GROUP_W))
    put(PB_RK, rotary(rk) * HEAD_DIM ** -0.5)
    put(PB_Z, proj(COL_SSM_Z, GROUP_W))
    put(PB_GG, proj(COL_GDN_GATE, GROUP_W))


def _pproj(x, sandwich, w, cos, sin, small, l, *, nbatch, seqlen):
    rows = min(TOKEN_TILE, seqlen)
    ntile = seqlen // rows
    tri, expand, maskbd = _pproj_consts(rows, PROMPT_CHUNK)
    gvec, convw, convb = small[:3]
    row_map = lambda b, t: (b * ntile + t, 0)
    full = lambda a: pl.BlockSpec(a.shape, lambda b, t: (0, 0))
    in_specs = [pl.BlockSpec((rows, D_MODEL), row_map),
                _layer_block((N_SANDWICH, D_MODEL), (l,), ngrid=2),
                _layer_block((D_MODEL, IN_PAD), (l,), ngrid=2),
                pl.BlockSpec((rows, GROUP_W), lambda b, t: (t, 0)), pl.BlockSpec((rows, GROUP_W), lambda b, t: (t, 0)),
                full(tri), full(expand), full(maskbd)]
    in_specs += [_layer_block(a.shape[1:], (l,), ngrid=2, buffered=False) for a in (gvec, convw, convb)]
    m = x.shape[0]
    return pl.pallas_call(
        functools.partial(_pproj_body, rows=rows),
        out_shape=[jax.ShapeDtypeStruct((m, PRE_COLS), F32),
                   jax.ShapeDtypeStruct((nbatch * SUBLANES, CONV_CH), F32)],
        grid=(nbatch, ntile), in_specs=in_specs,
        out_specs=[pl.BlockSpec((rows, PRE_COLS), row_map), pl.BlockSpec((SUBLANES, CONV_CH), lambda b, t: (b, 0))],
        scratch_shapes=[pltpu.VMEM((rows + SUBLANES, CONV_CH), F32)],
        compiler_params=pltpu.CompilerParams(dimension_semantics=("parallel", "arbitrary"),
                                             vmem_limit_bytes=VMEM_LIMIT),
        name="pproj",
    )(x, sandwich, w, cos, sin, tri, expand, maskbd, gvec, convw, convb)


def _mixp_body(pre_ref, rinter_ref, rtail_ref, rintra_ref, maskbd_ref, maskg_ref, maskht_ref, dvec_ref, norms_ref,
               mixed_ref, oret_ref, ossm_ref, ogdn_ref, omc_ref, omn_ref, omm_ref,
               o_ref, sret_ref, sht_ref, sgdn_ref, smc_ref, vec_ref, *, rows, chunk, ret_carry):
    nchunks = rows // chunk
    t_id = pl.program_id(1)
    blk = lambda b, rs=slice(None): pre_ref[rs, b * GROUP_W:(b + 1) * GROUP_W]

    @pl.when(t_id == 0)
    def _():
        sret_ref[...] = jnp.zeros_like(sret_ref)
        sht_ref[...] = jnp.zeros_like(sht_ref)
        sgdn_ref[...] = jnp.zeros_like(sgdn_ref)
        smc_ref[...] = jnp.zeros_like(smc_ref)
        vec_ref[...] = jnp.zeros_like(vec_ref)

    maskbd = maskbd_ref[...]
    maskbd_b = maskbd.astype(BF16)
    maskg = maskg_ref[...]
    maskg_b = maskg.astype(BF16)

    def seg_sum(x):
        return _dot_sel_r2(x, maskbd_b)

    r_i = lax.broadcasted_iota(jnp.int32, (chunk, GROUP_W), 0)
    c_i = lax.broadcasted_iota(jnp.int32, (chunk, GROUP_W), 1) % HEAD_DIM
    trim = r_i >= c_i
    diag = r_i == c_i
    strict = r_i > c_i
    eye = diag.astype(F32)
    head_of_lane = lax.broadcasted_iota(jnp.int32, (1, GROUP_W), 1) // HEAD_DIM
    carry_row = jnp.zeros((1, GROUP_W), F32)
    for h in range(N_HEADS):
        carry_row = jnp.where(head_of_lane == h, ret_carry[h], carry_row)
    maskht = maskht_ref[...]

    def bd(x):
        xb16 = x.astype(BF16)
        return jnp.concatenate([xb16] * N_HEADS, axis=0) * maskbd_b

    def last_row(x):
        return x[chunk - 1:chunk, :]

    def row_form(x):
        return jnp.sum(jnp.where(diag, x, 0.0), axis=0, keepdims=True)

    def seg_max(x):
        out = None
        for h in range(N_HEADS):
            sel = head_of_lane == h
            m_h = jnp.max(jnp.where(sel, x, -jnp.inf), axis=-1, keepdims=True)
            out = jnp.where(sel, m_h, 0.0) if out is None else jnp.where(sel, m_h, out)
        return out

    chunks = [slice(c * chunk, (c + 1) * chunk) for c in range(nchunks)]

    g_gam, g_beta, g_ecum, g_k, g_n, g_p = [], [], [], [], [], []
    for rs in chunks:
        ce = blk(PB_X0 + X_CGDN, rs)
        gam = jnp.exp(jnp.where(trim, ce - row_form(ce), -jnp.inf))
        k_c = blk(PB_GK, rs)
        beta = blk(PB_X0 + X_BETA, rs)
        n0 = -(_dot_nt(k_c, bd(k_c)) * jnp.where(strict, gam, 0.0) * beta)
        g_gam.append(gam), g_beta.append(beta), g_ecum.append(jnp.exp(ce)), g_k.append(k_c)
        g_n.append(n0), g_p.append(eye + n0)
    for _ in range(max(int(math.ceil(math.log2(chunk))) - 1, 0)):
        for c in range(nchunks):
            g_n[c] = _dot(g_n[c], bd(g_n[c]))
        for c in range(nchunks):
            g_p[c] = g_p[c] + _dot(g_p[c], bd(g_n[c]))
    g_u, g_w = [], []
    for c, rs in enumerate(chunks):
        g_u.append(_dot(g_p[c], bd(blk(PB_GV, rs) * g_beta[c])))
        g_w.append(_dot(g_p[c], bd(g_k[c] * (g_beta[c] * g_ecum[c]))))

    s_ret, s_ht, s_gdn, s_mc = sret_ref[...], sht_ref[...], sgdn_ref[...], smc_ref[...]
    n_row, m_row = vec_ref[0:1, :], vec_ref[1:2, :]
    for c, rs in enumerate(chunks):
        q_c, k_c, v_c = blk(PB_RQ, rs), blk(PB_RK, rs), blk(PB_RV, rs)
        att = _dot_nt(q_c, bd(k_c)) * rintra_ref[...]
        o_ret = _dot(att, bd(v_c)) + _dot(q_c * rinter_ref[...], s_ret)
        s_ret = s_ret * carry_row + _dot_tn(k_c * rtail_ref[...], v_c) * maskbd

        ce = blk(PB_X0 + X_CSSM, rs)
        ce_last = last_row(ce)
        lmat = jnp.exp(jnp.where(trim, ce - row_form(ce), -jnp.inf))
        b_c = pre_ref[rs, PB_SBC * GROUP_W:PB_SBC * GROUP_W + 2 * HEAD_DIM]
        c_c = pre_ref[rs, PB_SBC * GROUP_W + 2 * HEAD_DIM:(PB_SBC + 1) * GROUP_W]
        xdt = blk(PB_SX, rs) * blk(PB_X0 + X_DT, rs)
        rhs_b = jnp.concatenate([b_c.astype(BF16)] * N_HEADS, axis=0) * maskg_b
        scores = _dot_nt(c_c, rhs_b) * lmat
        o_ssm = _dot(scores, bd(xdt)) + _dot(c_c, s_ht) * jnp.exp(ce)
        s_ht = s_ht * jnp.exp(ce_last) + _dot_tn(b_c, xdt * jnp.exp(ce_last - ce)) * maskht

        ce = blk(PB_X0 + X_CGDN, rs)
        ce_last = last_row(ce)
        q_c = blk(PB_GQ, rs)
        v_new = g_u[c] - _dot(g_w[c], s_gdn)
        qk = _dot_nt(q_c, bd(g_k[c])) * g_gam[c]
        o_gdn = _dot(q_c * g_ecum[c], s_gdn) + _dot(qk, bd(v_new))
        s_gdn = s_gdn * jnp.exp(ce_last) + _dot_tn(g_k[c] * jnp.exp(ce_last - ce), v_new) * maskbd

        q_c, k_c, v_c = blk(PB_MQ, rs), blk(PB_MK, rs), blk(PB_MV, rs)
        b_e, imb_e = blk(PB_X0 + X_B, rs), blk(PB_X0 + X_IMB, rs)
        dmat = jnp.where(trim, b_e + row_form(imb_e), -jnp.inf)
        m_t = jnp.maximum(b_e + m_row, seg_max(dmat))
        wts = jnp.exp(dmat - m_t)
        inter = jnp.exp(b_e + m_row - m_t)
        qk = _dot_nt(q_c, bd(k_c)) * wts
        num = _dot(qk, bd(v_c)) + inter * _dot(q_c, s_mc)
        den = seg_sum(qk) + inter * seg_sum(q_c * n_row)
        o_ml = num / jnp.maximum(jnp.abs(den), jnp.exp(-m_t))
        m_new = last_row(m_t)
        b_last = last_row(b_e)
        kw = k_c * jnp.exp(b_last + imb_e - m_new)
        cs = jnp.exp(b_last + m_row - m_new)
        s_mc = s_mc * cs + _dot_tn(kw, v_c) * maskbd
        n_row = n_row * cs + jnp.sum(kw, axis=0, keepdims=True)
        m_row = m_new

        o_ref[rs, :] = jnp.concatenate([o_ret, o_ssm, o_gdn, o_ml], axis=1)

    sret_ref[...] = s_ret
    sht_ref[...] = s_ht
    sgdn_ref[...] = s_gdn
    smc_ref[...] = s_mc
    vec_ref[0:1, :] = n_row
    vec_ref[1:2, :] = m_row

    def head_norm(x, g):
        return x * lax.rsqrt(seg_sum(x * x) * (1.0 / HEAD_DIM) + EPS) * g

    out_ret = head_norm(o_ref[:, 0:GROUP_W], norms_ref[0:1, :]) * _silu(blk(PB_RG))
    mixed_ref[:, 0:GROUP_W] = out_ret.astype(BF16)
    y = o_ref[:, GROUP_W:2 * GROUP_W] + dvec_ref[...] * blk(PB_SX)
    out_ssm = _rms(y * _silu(blk(PB_Z)), norms_ref[1:2, :])
    mixed_ref[:, GROUP_W:2 * GROUP_W] = out_ssm.astype(BF16)
    out_gdn = head_norm(o_ref[:, 2 * GROUP_W:3 * GROUP_W], norms_ref[2:3, :]) * _silu(blk(PB_GG))
    mixed_ref[:, 2 * GROUP_W:3 * GROUP_W] = out_gdn.astype(BF16)
    out_ml = _sigmoid(blk(PB_MO)) * head_norm(o_ref[:, 3 * GROUP_W:4 * GROUP_W], norms_ref[3:4, :])
    mixed_ref[:, 3 * GROUP_W:4 * GROUP_W] = out_ml.astype(BF16)

    @pl.when(t_id == pl.num_programs(1) - 1)
    def _():
        for h in range(N_HEADS):
            hs = slice(h * HEAD_DIM, (h + 1) * HEAD_DIM)
            grp = h // (N_HEADS // 2)
            oret_ref[0, h] = s_ret[hs, hs]
            ogdn_ref[0, h] = s_gdn[hs, hs]
            omc_ref[0, h] = s_mc[hs, hs]
            ossm_ref[0, h] = s_ht[grp * HEAD_DIM:(grp + 1) * HEAD_DIM, hs]
        omn_ref[0] = jnp.concatenate([n_row[:, h * HEAD_DIM:(h + 1) * HEAD_DIM] for h in range(N_HEADS)], axis=0)
        omm_ref[0] = jnp.concatenate([m_row[:, h * HEAD_DIM:h * HEAD_DIM + 1] for h in range(N_HEADS)], axis=1)


def _prompt_consts(rows, chunk):
    lg = np.log(1.0 - np.exp2(-5.0 - np.arange(N_HEADS, dtype=np.float64)))
    idx = np.arange(chunk, dtype=np.float64)
    rel = idx[:, None] - idx[None, :]
    intra = np.where(rel[None] >= 0, np.exp(np.maximum(rel[None], 0.0) * lg[:, None, None]), 0.0)
    intra = np.concatenate(list(intra), axis=1)
    lanes = lambda t: np.repeat(t.T, HEAD_DIM, axis=1)
    inter = lanes(np.exp((idx[None, :] + 1.0) * lg[:, None]))
    tail = lanes(np.exp((chunk - 1.0 - idx[None, :]) * lg[:, None]))
    carry = tuple(float(c) for c in np.exp(chunk * lg))
    g = np.arange(GROUP_W)
    maskbd = (g[:, None] // HEAD_DIM == g[None, :] // HEAD_DIM)
    n = np.arange(2 * HEAD_DIM)
    maskg = (g[:, None] // (2 * HEAD_DIM) == n[None, :] // HEAD_DIM)
    f = lambda a, dt=F32: jnp.asarray(np.asarray(a, np.float32), dt)
    return (f(inter), f(tail), f(intra), f(maskbd), f(maskg), f(maskg.T)), carry


def _pproj_consts(rows, chunk):
    r = np.arange(rows)
    tri = (r[:, None] >= r[None, :]) & (r[:, None] // chunk == r[None, :] // chunk)
    expand = np.zeros((LANES, NEXP * GROUP_W), np.float32)
    for j, lo in enumerate((G_DT, G_DTV, G_GA, G_GB, G_MF, G_MI)):
        for h in range(N_HEADS):
            expand[lo + h, j * GROUP_W + h * HEAD_DIM:j * GROUP_W + (h + 1) * HEAD_DIM] = 1.0
    expand = np.tile(expand, (3, 1))
    g = np.arange(GROUP_W)
    maskbd = (g[:, None] // HEAD_DIM == g[None, :] // HEAD_DIM)
    f = lambda a, dt: jnp.asarray(np.asarray(a, np.float32), dt)
    return f(tri, BF16), f(expand, BF16), f(maskbd, F32)


def _mix_prompt(pre, small, l, *, nbatch, seqlen):
    rows, chunk = min(PROMPT_TILE, seqlen), PROMPT_CHUNK
    ntile = seqlen // rows
    consts, carry = _prompt_consts(rows, chunk)
    dvec, norms = small[3:]
    m = pre.shape[0]
    row_map = lambda b, t: (b * ntile + t, 0)
    full = lambda a: pl.BlockSpec(a.shape, lambda b, t: (0, 0))
    in_specs = [pl.BlockSpec((rows, PRE_COLS), row_map)] + [full(a) for a in consts]
    in_specs += [_layer_block(a.shape[1:], (l,), ngrid=2, buffered=False) for a in (dvec, norms)]
    st4 = pl.BlockSpec((1, N_HEADS, HEAD_DIM, HEAD_DIM), lambda b, t: (b, 0, 0, 0))
    out_shape = [jax.ShapeDtypeStruct((m, D_MODEL), BF16)]
    out_shape += [jax.ShapeDtypeStruct((nbatch, N_HEADS, HEAD_DIM, HEAD_DIM), F32) for _ in range(4)]
    out_shape += [jax.ShapeDtypeStruct((nbatch, N_HEADS, HEAD_DIM), F32),
                  jax.ShapeDtypeStruct((nbatch, 1, N_HEADS), F32)]
    out_specs = [pl.BlockSpec((rows, D_MODEL), row_map), st4, st4, st4, st4,
                 pl.BlockSpec((1, N_HEADS, HEAD_DIM), lambda b, t: (b, 0, 0)),
                 pl.BlockSpec((1, 1, N_HEADS), lambda b, t: (b, 0, 0))]
    scratch = [pltpu.VMEM((rows, D_MODEL), F32),
               pltpu.VMEM((GROUP_W, GROUP_W), F32), pltpu.VMEM((2 * HEAD_DIM, GROUP_W), F32),
               pltpu.VMEM((GROUP_W, GROUP_W), F32), pltpu.VMEM((GROUP_W, GROUP_W), F32),
               pltpu.VMEM((SUBLANES, GROUP_W), F32)]
    body = functools.partial(_mixp_body, rows=rows, chunk=chunk, ret_carry=carry)
    outs = pl.pallas_call(
        body, out_shape=out_shape, grid=(nbatch, ntile), in_specs=in_specs, out_specs=out_specs,
        scratch_shapes=scratch,
        compiler_params=pltpu.CompilerParams(dimension_semantics=("parallel", "arbitrary"),
                                             vmem_limit_bytes=VMEM_LIMIT),
        name="mix_prompt",
    )(pre, *consts, dvec, norms)
    outs = list(outs)
    outs[2] = jnp.swapaxes(outs[2], -1, -2)
    return outs


def _ret_tables(chunk, pad):
    lg = np.log(1.0 - np.exp2(-5.0 - np.arange(N_HEADS, dtype=np.float64)))
    idx = np.arange(chunk, dtype=np.float64)
    rel = idx[:, None] - idx[None, :]
    intra = np.where(rel[None] >= 0, np.exp(np.maximum(rel[None], 0.0) * lg[:, None, None]), 0.0)
    inter = np.where(idx[None, :] >= pad, np.exp((idx[None, :] - pad + 1.0) * lg[:, None]), 0.0)
    tail = np.exp((chunk - 1.0 - idx[None, :]) * lg[:, None])
    carry = np.exp((chunk - pad) * lg)
    lanes = lambda t: np.repeat(t.T, HEAD_DIM, axis=1)
    return (jnp.asarray(lanes(inter), F32), jnp.asarray(lanes(tail), F32), jnp.asarray(intra, F32),
            tuple(float(c) for c in carry))


def _const_mats(rows, chunk):
    r = np.arange(rows)
    tri = ((r[:, None] >= r[None, :]) & (r[:, None] // chunk == r[None, :] // chunk))
    sel = np.zeros((4 * SUBLANES, LANES), np.float32)
    sel[np.arange(24), np.arange(24)] = 1.0
    g = np.arange(GROUP_W)
    ones_bd = (g[:, None] // HEAD_DIM == g[None, :] // HEAD_DIM)
    return (jnp.asarray(tri, BF16), jnp.asarray(sel, BF16), jnp.asarray(ones_bd, BF16))


def _mix_sample(proj, hist, cos, sin, small, states, l, *, nbatch):
    rows, chunk, pad = SAMPLE_GROUP * SAMPLE_PAD, SAMPLE_PAD, SAMPLE_PAD - 4
    row_map = lambda i: (i, 0)
    c2 = lambda i: (0, 0)
    rinter, rtail, rintra, carry = _ret_tables(chunk, pad)
    tri, sel, ones_bd = _const_mats(rows, chunk)
    m = proj.shape[0]
    in_specs = [pl.BlockSpec((rows, IN_PAD), row_map),
                pl.BlockSpec((None, rows, CONV_CH), lambda i: (l, i, 0)),
                pl.BlockSpec((rows, GROUP_W), c2), pl.BlockSpec((rows, GROUP_W), c2),
                pl.BlockSpec((chunk, GROUP_W), c2), pl.BlockSpec((chunk, GROUP_W), c2),
                pl.BlockSpec((N_HEADS, chunk, chunk), lambda i: (0, 0, 0)),
                pl.BlockSpec((rows, rows), c2), pl.BlockSpec((4 * SUBLANES, LANES), c2),
                pl.BlockSpec((GROUP_W, GROUP_W), c2)]
    in_specs += [_layer_block(a.shape[1:], (l,), buffered=False) for a in small]
    g = SAMPLE_GROUP
    in_specs += [pl.BlockSpec((None, g, N_HEADS, HEAD_DIM, HEAD_DIM), lambda i: (l, i, 0, 0, 0))] * 4
    in_specs += [pl.BlockSpec((None, g, N_HEADS, HEAD_DIM), lambda i: (l, i, 0, 0)),
                 pl.BlockSpec((None, g, 1, N_HEADS), lambda i: (l, i, 0, 0))]
    st4 = pl.BlockSpec((g, N_HEADS, HEAD_DIM, HEAD_DIM), lambda i: (i, 0, 0, 0))
    out_specs = [pl.BlockSpec((rows, D_MODEL), row_map), pl.BlockSpec((rows, CONV_CH), row_map),
                 st4, st4, st4, st4,
                 pl.BlockSpec((g, N_HEADS, HEAD_DIM), lambda i: (i, 0, 0)),
                 pl.BlockSpec((g, 1, N_HEADS), lambda i: (i, 0, 0))]
    out_shape = [jax.ShapeDtypeStruct((m, D_MODEL), BF16), jax.ShapeDtypeStruct((m, CONV_CH), F32)]
    out_shape += [jax.ShapeDtypeStruct((nbatch, N_HEADS, HEAD_DIM, HEAD_DIM), F32) for _ in range(4)]
    out_shape += [jax.ShapeDtypeStruct((nbatch, N_HEADS, HEAD_DIM), F32),
                  jax.ShapeDtypeStruct((nbatch, 1, N_HEADS), F32)]
    scratch = [pltpu.VMEM((rows + SUBLANES, CONV_CH), F32), pltpu.VMEM((rows, LANES), F32)]
    scratch += [pltpu.VMEM((rows, GROUP_W), F32) for _ in range(7)]
    scratch += [pltpu.VMEM((rows, D_MODEL), F32)]
    body = functools.partial(_mix_body, sample=True, rows=rows, chunk=chunk, ret_carry=carry)
    return pl.pallas_call(
        body, out_shape=out_shape, grid=(nbatch // g,), in_specs=in_specs, out_specs=out_specs,
        scratch_shapes=scratch,
        compiler_params=pltpu.CompilerParams(dimension_semantics=("parallel",), vmem_limit_bytes=VMEM_LIMIT),
        name="mix_sample",
    )(proj, hist, cos, sin, rinter, rtail, rintra, tri, sel, ones_bd, *small, *states)


S_RQ, S_RK, S_RV, S_XDT, S_B, S_C, S_GQ, S_GK, S_GV, S_MQ, S_MK, S_MV, S_GATE = (
    HEAD_DIM * i for i in range(13))
SLAB_ROWS = S_GATE + SUBLANES
GT_SSM_A, GT_GDN_A, GT_BETA, GT_LOGI, GT_LOGF, GT_GAMMA = range(6)
P_RG, P_Z, P_XS, P_GG, P_MO = (GROUP_W * i for i in range(5))
POST_ROWS = 5 * GROUP_W
C_CONVW, C_CONVB = 0, CONV_W * CONV_CH
C_BIAS = C_CONVB + CONV_CH
C_ALOG = C_BIAS + 4 * SUBLANES
C_GAMMA = C_ALOG + 4 * SUBLANES
C_ROWS = C_GAMMA + SUBLANES
N_ROWS = 5 * GROUP_W
N_TOK = 4


def _spre_body(x_ref, g_ref, w_ref, hist_ref, rot_ref, colp_ref, slab_ref, post_ref, convraw_ref):
    nb = LANES
    h_in = _rms(x_ref[...], g_ref[2:3, :]).astype(BF16)
    proj = jnp.dot(h_in, w_ref[...], preferred_element_type=F32)
    for j in range(CONV_W - 1):
        convraw_ref[j] = proj[(j + 1) * nb:(j + 2) * nb, COL_CONV:COL_CONV + CONV_CH]
    pt = [proj[t * nb:(t + 1) * nb, :].T for t in range(N_TOK)]
    src = [hist_ref[j].T for j in range(CONV_W - 1)] + [p[COL_CONV:COL_CONV + CONV_CH] for p in pt]
    bias = colp_ref[C_BIAS:C_BIAS + 4 * SUBLANES, :]
    neg_a = -jnp.exp(colp_ref[C_ALOG:C_ALOG + 4 * SUBLANES, :])
    gamma = colp_ref[C_GAMMA:C_GAMMA + SUBLANES, :]
    zero_row = jnp.zeros((1, LANES), F32)

    def rotary(x, t, k):
        swapped = jnp.concatenate(
            [x[h * HEAD_DIM + off:h * HEAD_DIM + off + HEAD_DIM // 2]
             for h in range(N_HEADS) for off in (HEAD_DIM // 2, 0)], axis=0)
        return x * rot_ref[0, t] + swapped * rot_ref[1, t]

    for t in range(N_TOK):
        p = pt[t]
        g = p[COL_GATES:COL_GATES + 4 * SUBLANES]
        xb = g + bias
        sp = _softplus(xb)
        decay = jnp.exp(neg_a * sp)
        beta = _sigmoid(g)
        logf = -_softplus(-xb)
        conv = src[t] * colp_ref[C_CONVW:C_CONVW + CONV_CH, :]
        for j in range(1, CONV_W):
            conv = conv + src[t + j] * colp_ref[C_CONVW + j * CONV_CH:C_CONVW + (j + 1) * CONV_CH, :]
        cv = _silu(conv + colp_ref[C_CONVB:C_CONVB + CONV_CH, :])
        rq = rotary(p[COL_RET:COL_RET + GROUP_W], t, 0)
        rk = rotary(p[COL_RET + GROUP_W:COL_RET + 2 * GROUP_W], t, 1) * HEAD_DIM ** -0.5
        post_ref[t, P_RG:P_RG + GROUP_W, :] = p[COL_RET + 3 * GROUP_W:COL_RET + 4 * GROUP_W]
        post_ref[t, P_Z:P_Z + GROUP_W, :] = p[COL_SSM_Z:COL_SSM_Z + GROUP_W]
        post_ref[t, P_XS:P_XS + GROUP_W, :] = cv[0:GROUP_W]
        post_ref[t, P_GG:P_GG + GROUP_W, :] = p[COL_GDN_GATE:COL_GDN_GATE + GROUP_W]
        post_ref[t, P_MO:P_MO + GROUP_W, :] = p[COL_ML + 3 * GROUP_W:COL_ML + 4 * GROUP_W]
        for h in range(N_HEADS):
            hs = slice(h * HEAD_DIM, (h + 1) * HEAD_DIM)
            grp = h // (N_HEADS // 2)
            put = lambda r, v: slab_ref.__setitem__((h, t, slice(r, r + v.shape[0]), slice(None)), v)
            l2 = lambda v: v * lax.rsqrt(jnp.sum(v * v, axis=0, keepdims=True) + EPS)
            put(S_RQ, rq[hs])
            put(S_RK, rk[hs])
            put(S_RV, p[COL_RET + 2 * GROUP_W + h * HEAD_DIM:COL_RET + 2 * GROUP_W + (h + 1) * HEAD_DIM])
            put(S_XDT, cv[hs] * sp[G_DT + h:G_DT + h + 1])
            put(S_B, cv[GROUP_W + grp * HEAD_DIM:GROUP_W + (grp + 1) * HEAD_DIM])
            put(S_C, cv[GROUP_W + 2 * HEAD_DIM + grp * HEAD_DIM:GROUP_W + 2 * HEAD_DIM + (grp + 1) * HEAD_DIM])
            put(S_GQ, l2(cv[SSM_CONV_CH + h * HEAD_DIM:SSM_CONV_CH + (h + 1) * HEAD_DIM]) * HEAD_DIM ** -0.5)
            put(S_GK, l2(cv[SSM_CONV_CH + GROUP_W + h * HEAD_DIM:SSM_CONV_CH + GROUP_W + (h + 1) * HEAD_DIM]))
            put(S_GV, cv[SSM_CONV_CH + 2 * GROUP_W + h * HEAD_DIM:SSM_CONV_CH + 2 * GROUP_W + (h + 1) * HEAD_DIM])
            put(S_MQ, p[COL_ML + h * HEAD_DIM:COL_ML + (h + 1) * HEAD_DIM])
            put(S_MK, p[COL_ML + GROUP_W + h * HEAD_DIM:COL_ML + GROUP_W + (h + 1) * HEAD_DIM] * HEAD_DIM ** -0.5)
            put(S_MV, p[COL_ML + 2 * GROUP_W + h * HEAD_DIM:COL_ML + 2 * GROUP_W + (h + 1) * HEAD_DIM])
            row = lambda a, j: a[j + h:j + h + 1]
            put(S_GATE, jnp.concatenate([row(decay, G_DT), row(decay, G_GA), row(beta, G_GB), row(xb, G_MI),
                                         row(logf, G_MF), gamma[h:h + 1], zero_row, zero_row], axis=0))


def _spre(x, sandwich, w, hist, rot, colp, l):
    m = x.shape[0]
    return pl.pallas_call(
        _spre_body,
        out_shape=[jax.ShapeDtypeStruct((N_HEADS, N_TOK, SLAB_ROWS, LANES), F32),
                   jax.ShapeDtypeStruct((N_TOK, POST_ROWS, LANES), F32),
                   jax.ShapeDtypeStruct((CONV_W - 1, LANES, CONV_CH), F32)],
        grid=(1,),
        in_specs=[pl.BlockSpec((m, D_MODEL), lambda i: (0, 0)),
                  _layer_block((N_SANDWICH, D_MODEL), (l,), buffered=False),
                  _layer_block((D_MODEL, IN_PAD), (l,), buffered=False),
                  _layer_block((CONV_W - 1, LANES, CONV_CH), (l,), buffered=False),
                  pl.BlockSpec(rot.shape, lambda i: (0, 0, 0, 0)),
                  _layer_block((C_ROWS, LANES), (l,), buffered=False)],
        out_specs=[pl.BlockSpec((N_HEADS, N_TOK, SLAB_ROWS, LANES), lambda i: (0, 0, 0, 0)),
                   pl.BlockSpec((N_TOK, POST_ROWS, LANES), lambda i: (0, 0, 0)),
                   pl.BlockSpec((CONV_W - 1, LANES, CONV_CH), lambda i: (0, 0, 0))],
        compiler_params=pltpu.CompilerParams(dimension_semantics=("arbitrary",), vmem_limit_bytes=VMEM_LIMIT),
        name="spre",
    )(x, sandwich, w, hist, rot, colp)


def _srec_body(*refs, ncopy):
    nprev = 6 if ncopy else 0
    ins, prev, outs = refs[:7], refs[7:7 + nprev], refs[7 + nprev:]
    j = pl.program_id(1)

    @pl.when(j == 0)
    def _():
        _srec_compute(*ins, *outs)

    if ncopy:
        @pl.when(j > 0)
        def _():
            for src, dst in zip(prev, outs[1:7]):
                dst[...] = src[...]


def _srec_compute(slab_ref, sret_in, sssm_in, sgdn_in, smc_in, smn_in, smm_in,
                  ot_ref, sret_out, sssm_out, sgdn_out, smc_out, smn_out, smm_out, y8_ref):
    tile = lambda t, r: slab_ref[t, r:r + HEAD_DIM, :]
    gate = lambda t, j: slab_ref[t, S_GATE + j:S_GATE + j + 1, :]
    drow = lambda t, r, d: slab_ref[t, pl.ds(r + d, 1), :]
    zeros = jnp.zeros((HEAD_DIM, LANES), F32)
    sweep = lambda body, init: lax.fori_loop(0, HEAD_DIM, body, init, unroll=8)
    blk = lambda d: pl.ds(pl.multiple_of(d * HEAD_DIM, HEAD_DIM), HEAD_DIM)

    gamma = gate(0, GT_GAMMA)
    for t in range(N_TOK):
        src = sret_in if t == 0 else sret_out
        v = tile(t, S_RV)

        def body(d, acc, t=t, src=src, v=v):
            s = src[blk(d), :] * gamma + drow(t, S_RK, d) * v
            sret_out[blk(d), :] = s
            return acc + drow(t, S_RQ, d) * s
        ot_ref[t, 0:HEAD_DIM, :] = sweep(body, zeros)

    for t in range(N_TOK):
        src = sssm_in if t == 0 else sssm_out
        a, b_t, c_t = gate(t, GT_SSM_A), tile(t, S_B), tile(t, S_C)

        def body(p, carry, t=t, src=src, a=a, b_t=b_t, c_t=c_t):
            hp = src[blk(p), :] * a + drow(t, S_XDT, p) * b_t
            sssm_out[blk(p), :] = hp
            part = jnp.sum((hp * c_t).reshape(HEAD_DIM // SUBLANES, SUBLANES, LANES), axis=0)
            y8_ref[pl.ds(pl.multiple_of(p * SUBLANES, SUBLANES), SUBLANES), :] = part
            return carry
        sweep(body, 0)
        ot_ref[t, HEAD_DIM:2 * HEAD_DIM, :] = jnp.sum(y8_ref[...].reshape(HEAD_DIM, SUBLANES, LANES), axis=1)

    ks = sweep(lambda d, acc: acc + drow(0, S_GK, d) * sgdn_in[blk(d), :], zeros)
    for t in range(N_TOK):
        src = sgdn_in if t == 0 else sgdn_out
        a = gate(t, GT_GDN_A)
        u = gate(t, GT_BETA) * (tile(t, S_GV) - a * ks)
        nxt = t + 1 < N_TOK

        def body(d, acc, t=t, src=src, a=a, u=u, nxt=nxt):
            s = src[blk(d), :] * a + drow(t, S_GK, d) * u
            sgdn_out[blk(d), :] = s
            o = acc[0] + drow(t, S_GQ, d) * s
            return (o, acc[1] + drow(t + 1, S_GK, d) * s) if nxt else (o, acc[1])
        o, ks = sweep(body, (zeros, zeros))
        ot_ref[t, 2 * HEAD_DIM:3 * HEAD_DIM, :] = o

    m_row, n_tile = smm_in[...], smn_in[...]
    for t in range(N_TOK):
        src = smc_in if t == 0 else smc_out
        logi, logf = gate(t, GT_LOGI), gate(t, GT_LOGF)
        m_new = jnp.maximum(logf + m_row, logi)
        fs, ins = jnp.exp(logf + m_row - m_new), jnp.exp(logi - m_new)
        n_tile = fs * n_tile + ins * tile(t, S_MK)
        qn = jnp.sum(tile(t, S_MQ) * n_tile, axis=0, keepdims=True)
        v = tile(t, S_MV) * ins

        def body(d, acc, t=t, src=src, fs=fs, v=v):
            c = src[blk(d), :] * fs + drow(t, S_MK, d) * v
            smc_out[blk(d), :] = c
            return acc + drow(t, S_MQ, d) * c
        num = sweep(body, zeros)
        ot_ref[t, 3 * HEAD_DIM:4 * HEAD_DIM, :] = num / jnp.maximum(jnp.abs(qn), jnp.exp(-m_new))
        m_row = m_new
    smn_out[...] = n_tile
    smm_out[...] = m_row


def _srec(slab, states, prev, l):
    rows = HEAD_DIM * HEAD_DIM
    st_shapes = [(rows, LANES)] * 4 + [(HEAD_DIM, LANES), (1, LANES)]
    spec = lambda tail, lead: pl.BlockSpec((None, None) + tail, lambda h, j: (lead(j), h, 0, 0))
    slab_spec = lambda r: pl.BlockSpec((None, N_TOK, r, LANES), lambda h, j: (h, 0, 0, 0))
    in_specs = [slab_spec(SLAB_ROWS)] + [spec(s, lambda j: l) for s in st_shapes]
    args = [slab, *states]
    if l:
        in_specs += [spec(s, lambda j: jnp.maximum(j - 1, 0)) for s in st_shapes]
        args += list(prev)
    out_shape = [jax.ShapeDtypeStruct((N_HEADS, N_TOK, GROUP_W, LANES), F32)]
    out_shape += [jax.ShapeDtypeStruct((l + 1, N_HEADS) + s, F32) for s in st_shapes]
    out_specs = [slab_spec(GROUP_W)] + [spec(s, lambda j: jnp.where(j == 0, l, j - 1)) for s in st_shapes]
    return pl.pallas_call(
        functools.partial(_srec_body, ncopy=l),
        out_shape=out_shape, grid=(N_HEADS, l + 1), in_specs=in_specs, out_specs=out_specs,
        scratch_shapes=[pltpu.VMEM((HEAD_DIM * SUBLANES, LANES), F32)],
        compiler_params=pltpu.CompilerParams(dimension_semantics=("arbitrary", "arbitrary"),
                                             vmem_limit_bytes=VMEM_LIMIT),
        name="srec",
    )(*args)


def _spost_body(x_ref, ot_ref, post_ref, cn_ref, g_ref, w_ref, o_ref):
    nb = LANES

    def head_norm(x):
        out = []
        for h in range(N_HEADS):
            b = x[h * HEAD_DIM:(h + 1) * HEAD_DIM]
            out.append(b * lax.rsqrt(jnp.mean(b * b, axis=0, keepdims=True) + EPS))
        return jnp.concatenate(out, axis=0)

    gain = lambda i: cn_ref[i * GROUP_W:(i + 1) * GROUP_W, :]
    for t in range(N_TOK):
        mixer = lambda i: jnp.concatenate(
            [ot_ref[h, t, i * HEAD_DIM:(i + 1) * HEAD_DIM, :] for h in range(N_HEADS)], axis=0)
        pin = lambda r: post_ref[t, r:r + GROUP_W, :]
        out_ret = head_norm(mixer(0)) * gain(0) * _silu(pin(P_RG))
        y = (mixer(1) + gain(4) * pin(P_XS)) * _silu(pin(P_Z))
        out_ssm = y * lax.rsqrt(jnp.mean(y * y, axis=0, keepdims=True) + EPS) * gain(1)
        out_gdn = head_norm(mixer(2)) * gain(2) * _silu(pin(P_GG))
        out_ml = _sigmoid(pin(P_MO)) * (head_norm(mixer(3)) * gain(3))
        mixed = jnp.concatenate([out_ret, out_ssm, out_gdn, out_ml], axis=0).T.astype(BF16)
        y = jnp.dot(mixed, w_ref[...], preferred_element_type=F32)
        rs = slice(t * nb, (t + 1) * nb)
        o_ref[rs, :] = x_ref[rs, :] + _rms(y, g_ref[3:4, :])


def _spost(x, ot, post, cn, sandwich, w, l):
    m = x.shape[0]
    return pl.pallas_call(
        _spost_body,
        out_shape=jax.ShapeDtypeStruct((m, D_MODEL), F32),
        grid=(1,),
        in_specs=[pl.BlockSpec((m, D_MODEL), lambda i: (0, 0)),
                  pl.BlockSpec(ot.shape, lambda i: (0, 0, 0, 0)),
                  pl.BlockSpec(post.shape, lambda i: (0, 0, 0)),
                  _layer_block((N_ROWS, LANES), (l,), buffered=False),
                  _layer_block((N_SANDWICH, D_MODEL), (l,), buffered=False),
                  _layer_block((D_MODEL, D_MODEL), (l,), buffered=False)],
        out_specs=pl.BlockSpec((m, D_MODEL), lambda i: (0, 0)),
        compiler_params=pltpu.CompilerParams(dimension_semantics=("arbitrary",), vmem_limit_bytes=VMEM_LIMIT),
        name="spost",
    )(x, ot, post, cn, sandwich, w)


def _rot_tables(pos):
    half = HEAD_DIM // 2
    inv = ROPE_BASE ** (-jnp.arange(half, dtype=F32) / half)
    ang = pos.astype(F32)[:, None] * inv[None, :]
    cos, sin = jnp.cos(ang), jnp.sin(ang)
    cos_full = jnp.tile(jnp.concatenate([cos, cos], axis=-1), (1, N_HEADS))
    sin_signed = jnp.tile(jnp.concatenate([-sin, sin], axis=-1), (1, N_HEADS))
    return cos_full, sin_signed


def _small_params(ret_norm, ssm_conv_w, ssm_conv_b, ssm_dt_bias, ssm_A_log, ssm_D, ssm_norm, gdn_conv_w,
                  gdn_dt_bias, gdn_A_log, gdn_norm, mlstm_i_bias, mlstm_f_bias, mlstm_norm):
    depth = ret_norm.shape[0]
    z = lambda n: jnp.zeros((depth, n), F32)
    pad_lanes = z(LANES - 5 * N_HEADS)
    bias = jnp.concatenate([ssm_dt_bias, gdn_dt_bias, z(N_HEADS), mlstm_i_bias, mlstm_f_bias, pad_lanes], axis=1)
    alog = jnp.concatenate([ssm_A_log, gdn_A_log, z(3 * N_HEADS), pad_lanes], axis=1)
    gvec = jnp.stack([bias, alog], axis=1)
    convw = jnp.concatenate([ssm_conv_w, gdn_conv_w], axis=2)
    convb = jnp.concatenate([ssm_conv_b, z(CONV_CH - SSM_CONV_CH)], axis=1)[:, None, :]
    dvec = jnp.repeat(ssm_D, HEAD_DIM, axis=1)[:, None, :]
    norms = jnp.stack([ret_norm, ssm_norm, gdn_norm, mlstm_norm], axis=1)
    return gvec, convw, convb, dvec, norms


def kernel(x_prompt, x_sample, state_ret, state_ssm, state_ssm_conv, state_gdn, state_gdn_conv,
           state_mlstm_C, state_mlstm_n, state_mlstm_m, w_in, w_out, norm_sandwich, ffn_w_gate, ffn_w_up,
           ffn_w_down, ret_norm, ssm_conv_w, ssm_conv_b, ssm_dt_bias, ssm_A_log, ssm_D, ssm_norm, gdn_conv_w,
           gdn_dt_bias, gdn_A_log, gdn_norm, mlstm_i_bias, mlstm_f_bias, mlstm_norm):
    depth = w_in.shape[0]
    bp, lp, _ = x_prompt.shape
    bs, ls, _ = x_sample.shape
    assert (bs, ls) == (LANES, N_TOK)
    w_in_p = _prep_win(w_in)
    w_out_b = w_out.astype(BF16)
    wg, wu, wd = ffn_w_gate, ffn_w_up, ffn_w_down.astype(BF16)
    small = _small_params(ret_norm, ssm_conv_w, ssm_conv_b, ssm_dt_bias, ssm_A_log, ssm_D, ssm_norm, gdn_conv_w,
                          gdn_dt_bias, gdn_A_log, gdn_norm, mlstm_i_bias, mlstm_f_bias, mlstm_norm)
    gvec, convw, convb, dvec, norms = small

    cos_p, sin_p = _rot_tables(jnp.arange(lp, dtype=jnp.int32))
    x = x_prompt.reshape(bp * lp, D_MODEL)
    st_p = []
    for l in range(depth):
        x = _ffn(x, norm_sandwich, wg, wu, wd, l, 0)
        pre, convraw = _pproj(x, norm_sandwich, w_in_p, cos_p, sin_p, small, l, nbatch=bp, seqlen=lp)
        outs = _mix_prompt(pre, small, l, nbatch=bp, seqlen=lp)
        x = _ffn(x, norm_sandwich, wg, wu, wd, l, 1, mixed=outs[0], w_out=w_out_b)
        st_p.append([convraw] + outs[1:])
    y_prompt = x.reshape(bp, lp, D_MODEL)
    conv_p = jnp.stack([s[0].reshape(bp, -1, CONV_CH)[:, -(CONV_W - 1):, :] for s in st_p])
    stack_p = lambda k: jnp.stack([s[k] for s in st_p])
    out_p = (stack_p(1), stack_p(2), conv_p[..., :SSM_CONV_CH], stack_p(3), conv_p[..., SSM_CONV_CH:],
             stack_p(4), stack_p(5), stack_p(6)[:, :, 0, :])

    lanes = lambda a: jnp.broadcast_to(a[..., None], a.shape + (LANES,))
    cos_s, sin_s = _rot_tables(PAST_LEN + jnp.arange(ls, dtype=jnp.int32))
    rot = lanes(jnp.stack([cos_s, sin_s]))
    lg = np.log(1.0 - np.exp2(-5.0 - np.arange(N_HEADS, dtype=np.float64)))
    gamma = jnp.asarray(np.concatenate([np.exp(lg), np.zeros(SUBLANES - N_HEADS)]), F32)
    colp = lanes(jnp.concatenate(
        [convw.reshape(depth, CONV_W * CONV_CH), convb[:, 0, :], gvec[:, 0, :4 * SUBLANES], gvec[:, 1, :4 * SUBLANES],
         jnp.broadcast_to(gamma, (depth, SUBLANES))], axis=1))
    cn = lanes(jnp.concatenate([norms.reshape(depth, 4 * GROUP_W), dvec[:, 0, :]], axis=1))
    hist = jnp.transpose(jnp.concatenate([state_ssm_conv, state_gdn_conv], axis=-1), (0, 2, 1, 3))
    mat = lambda s: jnp.transpose(s, (0, 2, 3, 4, 1)).reshape(depth, N_HEADS, HEAD_DIM * HEAD_DIM, bs)
    states_s = (mat(state_ret), mat(state_ssm), mat(state_gdn), mat(state_mlstm_C),
                jnp.transpose(state_mlstm_n, (0, 2, 3, 1)), jnp.transpose(state_mlstm_m, (0, 2, 1))[:, :, None, :])
    x = jnp.transpose(x_sample, (1, 0, 2)).reshape(ls * bs, D_MODEL)
    prev, convs = None, []
    for l in range(depth):
        x = _ffn(x, norm_sandwich, wg, wu, wd, l, 0)
        slab, post, convraw = _spre(x, norm_sandwich, w_in_p, hist, rot, colp, l)
        outs = _srec(slab, states_s, prev, l)
        prev = outs[1:]
        x = _spost(x, outs[0], post, cn, norm_sandwich, w_out_b, l)
        x = _ffn(x, norm_sandwich, wg, wu, wd, l, 1)
        convs.append(convraw)
    y_sample = jnp.transpose(x.reshape(ls, bs, D_MODEL), (1, 0, 2))
    conv_s = jnp.transpose(jnp.stack(convs), (0, 2, 1, 3))
    unmat = lambda s: jnp.transpose(s.reshape(depth, N_HEADS, HEAD_DIM, HEAD_DIM, bs), (0, 4, 1, 2, 3))
    out_s = (unmat(prev[0]), unmat(prev[1]), conv_s[..., :SSM_CONV_CH], unmat(prev[2]), conv_s[..., SSM_CONV_CH:],
             unmat(prev[3]), jnp.transpose(prev[4], (0, 3, 1, 2)), jnp.transpose(prev[5][:, :, 0, :], (0, 2, 1)))
    return (y_prompt, y_sample) + out_p + out_s
```

```python
import functools
import math

import numpy as np
import jax
import jax.numpy as jnp
from jax import lax
from jax.experimental import pallas as pl
from jax.experimental.pallas import tpu as pltpu

F32 = jnp.float32
BF16 = jnp.bfloat16

D_MODEL = 1024
N_HEADS = 4
HEAD_DIM = 64
GROUP_W = N_HEADS * HEAD_DIM
D_FF = 2816
CONV_W = 4
EPS = 1e-6
ROPE_BASE = 10000.0
PAST_LEN = 16384
PROMPT_CHUNK = 64

COL_RET = 0
COL_SSM_Z = 1024
COL_CONV = 1280
CONV_CH = 1280
SSM_CONV_CH = 512
COL_GDN_GATE = 2560
COL_ML = 2816
COL_GATES = 3840
IN_PAD = 3968
G_DT, G_GA, G_GB, G_MI, G_MF, G_DTV = 0, 4, 8, 12, 16, 20

SUBLANES = 8
LANES = 128
VMEM_LIMIT = 56 * 1024 * 1024

FF_CHUNK = 256
TOKEN_TILE = 512
PROMPT_TILE = 512
SAMPLE_PAD = 8
SAMPLE_GROUP = 8
NEG_BIG = -1e30


def _sigmoid(x):
    return 1.0 / (1.0 + jnp.exp(-x))


def _silu(x):
    return x * _sigmoid(x)


def _softplus(x):
    return jnp.maximum(x, 0.0) + jnp.log1p(jnp.exp(-jnp.abs(x)))


def _rms(x, g):
    return x * lax.rsqrt(jnp.mean(x * x, axis=-1, keepdims=True) + EPS) * g


def _dot(a, b):
    return jnp.dot(a.astype(BF16), b.astype(BF16), preferred_element_type=F32)


def _dot_nt(a, b):
    return lax.dot_general(a.astype(BF16), b.astype(BF16), (((1,), (1,)), ((), ())),
                           preferred_element_type=F32)


def _dot_tn(a, b):
    return lax.dot_general(a.astype(BF16), b.astype(BF16), (((0,), (0,)), ((), ())),
                           preferred_element_type=F32)


def _split3(x):
    hi = x.astype(BF16)
    r = x - hi.astype(F32)
    mid = r.astype(BF16)
    lo = (r - mid.astype(F32)).astype(BF16)
    return hi, mid, lo


def _dot_sel(sel, x):
    hi, mid, lo = _split3(x)
    f = lambda p: jnp.dot(sel, p, preferred_element_type=F32)
    return (f(lo) + f(mid)) + f(hi)


def _dot_sel_r(x, sel):
    hi, mid, lo = _split3(x)
    f = lambda p: jnp.dot(p, sel, preferred_element_type=F32)
    return (f(lo) + f(mid)) + f(hi)


def _dot_sel_nt(sel, x):
    hi, mid, lo = _split3(x)
    f = lambda p: lax.dot_general(sel, p, (((1,), (1,)), ((), ())), preferred_element_type=F32)
    return (f(lo) + f(mid)) + f(hi)


N_SANDWICH = 6


def _layer_block(tail, lead, ngrid=1, buffered=True):
    idx = tuple(lead) + (0,) * len(tail)
    imap = (lambda i: idx) if ngrid == 1 else (lambda i, j: idx)
    kw = dict(pipeline_mode=pl.Buffered(1)) if buffered else {}
    return pl.BlockSpec((None,) * len(lead) + tuple(tail), imap, **kw)


WIN_SEGMENTS = ((0, 0, 1792), (1792, 1796, 1024), (2816, 2828, 1024))
WIN_GATES = ((1792, 1796), (2820, 2828), (3852, 3860))


def _prep_win_body(w_ref, o_ref):
    l = pl.program_id(0)
    step = 2 * LANES
    for dst, src, n in WIN_SEGMENTS:
        for c in range(0, n, step):
            o_ref[:, dst + c:dst + c + step] = w_ref[src + c:src + c + step, l, :].T.astype(BF16)
    gates = jnp.concatenate([w_ref[a:b, l, :] for a, b in WIN_GATES]
                            + [jnp.zeros((LANES - 5 * N_HEADS, D_MODEL), F32)], axis=0)
    o_ref[:, COL_GATES:IN_PAD] = gates.T.astype(BF16)


def _prep_win(w_in):
    depth, _, in_dim = w_in.shape
    w_t = jnp.transpose(w_in, (2, 0, 1))
    return pl.pallas_call(
        _prep_win_body,
        out_shape=jax.ShapeDtypeStruct((depth, D_MODEL, IN_PAD), BF16),
        grid=(depth,),
        in_specs=[pl.BlockSpec((in_dim, depth, D_MODEL), lambda l: (0, 0, 0), pipeline_mode=pl.Buffered(1))],
        out_specs=pl.BlockSpec((None, D_MODEL, IN_PAD), lambda l: (l, 0, 0)),
        compiler_params=pltpu.CompilerParams(dimension_semantics=("arbitrary",), vmem_limit_bytes=VMEM_LIMIT),
        name="prep_win",
    )(w_t)


def _ffn_body(*refs, k, mixer_out):
    if mixer_out:
        x_ref, m_ref, wo_ref, g_ref, wg_ref, wu_ref, wd_ref, o_ref, a_ref = refs
        y = jnp.dot(m_ref[...], wo_ref[...], preferred_element_type=F32)
        x = x_ref[...] + _rms(y, g_ref[3:4, :])
    else:
        x_ref, g_ref, wg_ref, wu_ref, wd_ref, o_ref, a_ref = refs
        x = x_ref[...]
    h = _rms(x, g_ref[4 * k:4 * k + 1, :]).astype(BF16)
    for j in range(D_FF // FF_CHUNK):
        sl = slice(j * FF_CHUNK, (j + 1) * FF_CHUNK)
        gt = jnp.dot(h, wg_ref[:, sl].astype(BF16), preferred_element_type=F32)
        up = jnp.dot(h, wu_ref[:, sl].astype(BF16), preferred_element_type=F32)
        a_ref[:, sl] = (_silu(gt) * up).astype(BF16)
    y = jnp.dot(a_ref[...], wd_ref[...], preferred_element_type=F32)
    o_ref[...] = x + 0.5 * _rms(y, g_ref[4 * k + 1:4 * k + 2, :])


def _ffn(x, sandwich, wg, wu, wd, l, k, mixed=None, w_out=None):
    m = x.shape[0]
    tm = min(TOKEN_TILE, m)
    tile = pl.BlockSpec((tm, D_MODEL), lambda i: (i, 0))
    fused = mixed is not None
    in_specs = [tile] + ([tile, _layer_block((D_MODEL, D_MODEL), (l,))] if fused else [])
    in_specs += [_layer_block((N_SANDWICH, D_MODEL), (l,)),
                 _layer_block((D_MODEL, D_FF), (l, k)),
                 _layer_block((D_MODEL, D_FF), (l, k)),
                 _layer_block((D_FF, D_MODEL), (l, k))]
    args = (x,) + ((mixed, w_out) if fused else ()) + (sandwich, wg, wu, wd)
    return pl.pallas_call(
        functools.partial(_ffn_body, k=k, mixer_out=fused),
        out_shape=jax.ShapeDtypeStruct((m, D_MODEL), F32),
        grid=(m // tm,),
        in_specs=in_specs,
        out_specs=tile,
        scratch_shapes=[pltpu.VMEM((tm, D_FF), BF16)],
        compiler_params=pltpu.CompilerParams(dimension_semantics=("parallel",),
                                             vmem_limit_bytes=VMEM_LIMIT),
        name="ffn_out" if fused else "ffn",
    )(*args)


def _inproj_body(x_ref, g_ref, w_ref, o_ref):
    h = _rms(x_ref[...], g_ref[2:3, :]).astype(BF16)
    o_ref[...] = jnp.dot(h, w_ref[...], preferred_element_type=F32)


def _inproj(x, sandwich, w, l):
    m = x.shape[0]
    tm = min(TOKEN_TILE, m)
    return pl.pallas_call(
        _inproj_body,
        out_shape=jax.ShapeDtypeStruct((m, IN_PAD), F32),
        grid=(m // tm,),
        in_specs=[pl.BlockSpec((tm, D_MODEL), lambda i: (i, 0)),
                  _layer_block((N_SANDWICH, D_MODEL), (l,)),
                  _layer_block((D_MODEL, IN_PAD), (l,))],
        out_specs=pl.BlockSpec((tm, IN_PAD), lambda i: (i, 0)),
        compiler_params=pltpu.CompilerParams(dimension_semantics=("parallel",),
                                             vmem_limit_bytes=VMEM_LIMIT),
        name="inproj",
    )(x, sandwich, w)


def _outproj_body(x_ref, m_ref, g_ref, w_ref, o_ref):
    y = jnp.dot(m_ref[...], w_ref[...], preferred_element_type=F32)
    o_ref[...] = x_ref[...] + _rms(y, g_ref[3:4, :])


def _outproj(x, mixed, sandwich, w, l):
    m = x.shape[0]
    tm = min(TOKEN_TILE, m)
    return pl.pallas_call(
        _outproj_body,
        out_shape=jax.ShapeDtypeStruct((m, D_MODEL), F32),
        grid=(m // tm,),
        in_specs=[pl.BlockSpec((tm, D_MODEL), lambda i: (i, 0)),
                  pl.BlockSpec((tm, D_MODEL), lambda i: (i, 0)),
                  _layer_block((N_SANDWICH, D_MODEL), (l,)),
                  _layer_block((D_MODEL, D_MODEL), (l,))],
        out_specs=pl.BlockSpec((tm, D_MODEL), lambda i: (i, 0)),
        compiler_params=pltpu.CompilerParams(dimension_semantics=("parallel",),
                                             vmem_limit_bytes=VMEM_LIMIT),
        name="outproj",
    )(x, mixed, sandwich, w)


def _head(x, h):
    return x[:, h * HEAD_DIM:(h + 1) * HEAD_DIM]


def _neumann_inverse(amat, c):
    row = lax.broadcasted_iota(jnp.int32, (c, c), 0)
    col = lax.broadcasted_iota(jnp.int32, (c, c), 1)
    eye = (row == col).astype(F32)
    m = -amat
    p = eye + m
    for _ in range(max(int(math.ceil(math.log2(c))) - 1, 0)):
        m = _dot(m, m)
        p = p + _dot(p, m)
    return p


def _mix_body(*refs, sample, rows, chunk, ret_carry):
    nchunks = rows // chunk
    it = iter(refs)
    proj_ref = next(it)
    hist_ref = next(it) if sample else None
    cos_ref, sin_ref = next(it), next(it)
    rinter_ref, rtail_ref, rintra_ref = next(it), next(it), next(it)
    tri_ref, sel_ref, ones_ref = next(it), next(it), next(it)
    gvec_ref, convw_ref, convb_ref, dvec_ref, norms_ref = next(it), next(it), next(it), next(it), next(it)
    if sample:
        sret_in, sssm_in, sgdn_in, smc_in, smn_in, smm_in = (next(it) for _ in range(6))
    mixed_ref, convraw_ref = next(it), next(it)
    sret_ref, sssm_ref, sgdn_ref, smc_ref, smn_ref, smm_ref = (next(it) for _ in range(6))
    e_ref, g_ref, rq_ref, rk_ref, sx_ref, sbc_ref, gq_ref, gk_ref, gv_ref, o_ref = (next(it) for _ in range(10))
    if not sample:
        sret_in, sssm_in, sgdn_in, smc_in, smn_in, smm_in = (
            sret_ref, sssm_ref, sgdn_ref, smc_ref, smn_ref, smm_ref)

    if sample:
        rowid = lax.broadcasted_iota(jnp.int32, (rows, 1), 0)
        valid = (rowid % SAMPLE_PAD) >= (SAMPLE_PAD - 4)
        e_ref[0:SUBLANES, :] = jnp.zeros((SUBLANES, CONV_CH), F32)
    else:
        valid = None

        @pl.when(pl.program_id(1) == 0)
        def _():
            e_ref[0:SUBLANES, :] = jnp.zeros((SUBLANES, CONV_CH), F32)
            sret_ref[...] = jnp.zeros_like(sret_ref)
            sssm_ref[...] = jnp.zeros_like(sssm_ref)
            sgdn_ref[...] = jnp.zeros_like(sgdn_ref)
            smc_ref[...] = jnp.zeros_like(smc_ref)
            smn_ref[...] = jnp.zeros_like(smn_ref)
            smm_ref[...] = jnp.zeros_like(smm_ref)

    def keep(x, fill=0.0):
        return x if valid is None else jnp.where(valid, x, fill)

    lane = lax.broadcasted_iota(jnp.int32, (1, LANES), 1)
    in_lanes = lambda lo: (lane >= lo) & (lane < lo + N_HEADS)
    graw = proj_ref[:, COL_GATES:COL_GATES + LANES]
    xb = graw + gvec_ref[0:1, :]
    sp = _softplus(xb)
    neg_a = -jnp.exp(gvec_ref[1:2, :])
    logf = -_softplus(-xb)
    pre = jnp.where(lane < 2 * N_HEADS, neg_a * sp, jnp.where(in_lanes(G_MF), logf, 0.0))
    cum = _dot_sel(tri_ref[...], keep(pre))
    dtv = pltpu.roll(sp, G_DTV - G_DT, axis=1)
    aux = jnp.where(in_lanes(G_GB), _sigmoid(graw),
                    jnp.where(in_lanes(G_MI), xb, jnp.where(in_lanes(G_DTV), dtv, 0.0)))
    if sample:
        aux = jnp.where(valid, aux, jnp.where(in_lanes(G_MI), NEG_BIG, 0.0))
    g_ref[...] = jnp.where(in_lanes(G_GB) | in_lanes(G_MI) | in_lanes(G_DTV), aux, cum)

    lane_g = lax.broadcasted_iota(jnp.int32, (1, GROUP_W), 1)
    first_half = (lane_g % HEAD_DIM) < (HEAD_DIM // 2)

    def rotary(x):
        swapped = jnp.where(first_half, pltpu.roll(x, GROUP_W - HEAD_DIM // 2, axis=1),
                            pltpu.roll(x, HEAD_DIM // 2, axis=1))
        return x * cos_ref[...] + swapped * sin_ref[...]

    rq_ref[...] = rotary(proj_ref[:, COL_RET:COL_RET + GROUP_W])
    rk_ref[...] = keep(rotary(proj_ref[:, COL_RET + GROUP_W:COL_RET + 2 * GROUP_W]) * HEAD_DIM ** -0.5)

    raw = proj_ref[:, COL_CONV:COL_CONV + CONV_CH]
    if sample:
        raw = raw + hist_ref[...]
    e_ref[SUBLANES:SUBLANES + rows, :] = raw
    conv = raw * convw_ref[CONV_W - 1:CONV_W, :]
    for s in range(1, CONV_W):
        conv = conv + e_ref[SUBLANES - s:SUBLANES - s + rows, :] * convw_ref[CONV_W - 1 - s:CONV_W - s, :]
    conv = keep(_silu(conv + convb_ref[...]))
    if sample:
        convraw_ref[...] = raw
    else:
        convraw_ref[...] = e_ref[rows:rows + SUBLANES, :]
        e_ref[0:SUBLANES, :] = e_ref[rows:rows + SUBLANES, :]
    sx_ref[...] = conv[:, 0:GROUP_W]
    sbc_ref[...] = conv[:, GROUP_W:SSM_CONV_CH]
    ones_bd = ones_ref[...]

    def head_sumsq(x):
        return _dot_sel_r(x * x, ones_bd)

    gq = conv[:, SSM_CONV_CH:SSM_CONV_CH + GROUP_W]
    gk = conv[:, SSM_CONV_CH + GROUP_W:SSM_CONV_CH + 2 * GROUP_W]
    gq_ref[...] = gq * lax.rsqrt(head_sumsq(gq) + EPS) * HEAD_DIM ** -0.5
    gk_ref[...] = gk * lax.rsqrt(head_sumsq(gk) + EPS)
    gv_ref[...] = conv[:, SSM_CONV_CH + 2 * GROUP_W:CONV_CH]

    r_i = lax.broadcasted_iota(jnp.int32, (chunk, chunk), 0)
    c_i = lax.broadcasted_iota(jnp.int32, (chunk, chunk), 1)
    tri = r_i >= c_i
    strict = r_i > c_i
    lane_h = lax.broadcasted_iota(jnp.int32, (1, N_HEADS), 1)

    def chunk_step(i, carry):
        r0 = pl.multiple_of(i * chunk, chunk)
        rs = pl.ds(r0, chunk)
        sidx = i if sample else 0
        gc = g_ref[rs, :]
        gt = _dot_sel_nt(sel_ref[...], gc)
        col = lambda j: gc[:, j:j + 1]
        rowv = lambda j: gt[j:j + 1, :]
        last = lambda j: gc[chunk - 1:chunk, j:j + 1]

        heads = range(N_HEADS)
        s_ret = [sret_in[sidx, h] for h in heads]
        s_ssm = [sssm_in[sidx, h] for h in heads]
        s_gdn = [sgdn_in[sidx, h] for h in heads]
        s_mc = [smc_in[sidx, h] for h in heads]
        n_all = smn_in[sidx]
        m_row = smm_in[sidx]
        rq_all, rk_all = rq_ref[rs, :], rk_ref[rs, :]
        rv_all = proj_ref[rs, COL_RET + 2 * GROUP_W:COL_RET + 3 * GROUP_W]
        x_all, bc_all = sx_ref[rs, :], sbc_ref[rs, :]
        gq_all, gk_all, gv_all = gq_ref[rs, :], gk_ref[rs, :], gv_ref[rs, :]
        mq_all = proj_ref[rs, COL_ML:COL_ML + GROUP_W]
        mk_all = proj_ref[rs, COL_ML + GROUP_W:COL_ML + 2 * GROUP_W] * HEAD_DIM ** -0.5
        mv_all = proj_ref[rs, COL_ML + 2 * GROUP_W:COL_ML + 3 * GROUP_W]

        qi_all = rq_all * rinter_ref[...]
        kt_all = rk_all * rtail_ref[...]
        o_ret, n_ret = [], []
        for h in heads:
            att = _dot_nt(_head(rq_all, h), _head(rk_all, h)) * rintra_ref[h]
            o_ret.append(_dot(att, _head(rv_all, h)) + _dot(_head(qi_all, h), s_ret[h]))
            n_ret.append(s_ret[h] * ret_carry[h] + _dot_tn(_head(kt_all, h), _head(rv_all, h)))

        o_ssm, n_ssm = [], []
        for h in heads:
            grp = h // (N_HEADS // 2)
            b_g = bc_all[:, grp * HEAD_DIM:(grp + 1) * HEAD_DIM]
            c_g = bc_all[:, 2 * HEAD_DIM + grp * HEAD_DIM:2 * HEAD_DIM + (grp + 1) * HEAD_DIM]
            x_h = _head(x_all, h)
            cum_c, cum_r, cum_l = col(G_DT + h), rowv(G_DT + h), last(G_DT + h)
            lmat = jnp.exp(jnp.where(tri, cum_c - cum_r, -jnp.inf))
            scores = _dot_nt(c_g, b_g) * lmat * rowv(G_DTV + h)
            o_ssm.append(_dot(scores, x_h) + _dot_nt(c_g, s_ssm[h]) * jnp.exp(cum_c))
            w_c = jnp.exp(cum_l - cum_c) * col(G_DTV + h)
            n_ssm.append(s_ssm[h] * jnp.exp(cum_l) + _dot_tn(x_h * w_c, b_g))

        o_gdn, n_gdn = [], []
        for h in heads:
            q_h, k_h, v_h = _head(gq_all, h), _head(gk_all, h), _head(gv_all, h)
            cum_c, cum_r, cum_l = col(G_GA + h), rowv(G_GA + h), last(G_GA + h)
            beta_c = col(G_GB + h)
            gam = jnp.exp(jnp.where(tri, cum_c - cum_r, -jnp.inf))
            amat = _dot_nt(k_h, k_h) * jnp.where(strict, gam, 0.0) * beta_c
            ecum = jnp.exp(cum_c)
            tinv = _neumann_inverse(amat, chunk)
            u = _dot(tinv, v_h * beta_c)
            w = _dot(tinv, k_h * (beta_c * ecum))
            v_new = u - _dot(w, s_gdn[h])
            qk = _dot_nt(q_h, k_h) * gam
            o_gdn.append(_dot(q_h * ecum, s_gdn[h]) + _dot(qk, v_new))
            n_gdn.append(s_gdn[h] * jnp.exp(cum_l) + _dot_tn(k_h * jnp.exp(cum_l - cum_c), v_new))

        m_new_row = jnp.zeros((1, N_HEADS), F32)
        o_ml, n_mc, n_mn = [], [], []
        for h in heads:
            q_h, k_h, v_h = _head(mq_all, h), _head(mk_all, h), _head(mv_all, h)
            b_c, b_r, b_l = col(G_MF + h), rowv(G_MF + h), last(G_MF + h)
            i_c, i_r = col(G_MI + h), rowv(G_MI + h)
            m_old = m_row[:, h:h + 1]
            n_h = n_all[h:h + 1, :]
            dmat = jnp.where(tri, b_c - b_r + i_r, -jnp.inf)
            m_t = jnp.maximum(b_c + m_old, jnp.max(dmat, axis=-1, keepdims=True))
            wts = jnp.exp(dmat - m_t)
            inter = jnp.exp(b_c + m_old - m_t)
            qk = _dot_nt(q_h, k_h) * wts
            num = _dot(qk, v_h) + inter * _dot(q_h, s_mc[h])
            den = jnp.sum(qk, axis=-1, keepdims=True) + inter * jnp.sum(q_h * n_h, axis=-1, keepdims=True)
            o_ml.append(num / jnp.maximum(jnp.abs(den), jnp.exp(-m_t)))
            m_new = m_t[chunk - 1:chunk, :]
            ws = jnp.exp(b_l - b_c + i_c - m_new)
            cs = jnp.exp(b_l + m_old - m_new)
            kw = k_h * ws
            n_mc.append(s_mc[h] * cs + _dot_tn(kw, v_h))
            n_mn.append(n_h * cs + jnp.sum(kw, axis=0, keepdims=True))
            m_new_row = jnp.where(lane_h == h, m_new, m_new_row)

        o_ref[rs, :] = jnp.concatenate(o_ret + o_ssm + o_gdn + o_ml, axis=1)
        for h in heads:
            sret_ref[sidx, h] = n_ret[h]
            sssm_ref[sidx, h] = n_ssm[h]
            sgdn_ref[sidx, h] = n_gdn[h]
            smc_ref[sidx, h] = n_mc[h]
        smn_ref[sidx] = jnp.concatenate(n_mn, axis=0)
        smm_ref[sidx] = m_new_row
        return carry

    lax.fori_loop(0, nchunks, chunk_step, 0)

    def head_norm(x, g):
        return x * lax.rsqrt(head_sumsq(x) * (1.0 / HEAD_DIM) + EPS) * g

    out_ret = head_norm(o_ref[:, 0:GROUP_W], norms_ref[0:1, :]) * _silu(
        proj_ref[:, COL_RET + 3 * GROUP_W:COL_RET + 4 * GROUP_W])
    mixed_ref[:, 0:GROUP_W] = keep(out_ret).astype(BF16)
    y = o_ref[:, GROUP_W:2 * GROUP_W] + dvec_ref[...] * sx_ref[...]
    out_ssm = _rms(y * _silu(proj_ref[:, COL_SSM_Z:COL_SSM_Z + GROUP_W]), norms_ref[1:2, :])
    mixed_ref[:, GROUP_W:2 * GROUP_W] = keep(out_ssm).astype(BF16)
    out_gdn = head_norm(o_ref[:, 2 * GROUP_W:3 * GROUP_W], norms_ref[2:3, :]) * _silu(
        proj_ref[:, COL_GDN_GATE:COL_GDN_GATE + GROUP_W])
    mixed_ref[:, 2 * GROUP_W:3 * GROUP_W] = keep(out_gdn).astype(BF16)
    out_ml = _sigmoid(proj_ref[:, COL_ML + 3 * GROUP_W:COL_ML + 4 * GROUP_W]) * head_norm(
        o_ref[:, 3 * GROUP_W:4 * GROUP_W], norms_ref[3:4, :])
    mixed_ref[:, 3 * GROUP_W:4 * GROUP_W] = keep(out_ml).astype(BF16)


NEXP = 6
X_CSSM, X_DT, X_CGDN, X_BETA, X_B, X_IMB = range(NEXP)


def _split2(x):
    hi = x.astype(BF16)
    return hi, (x - hi.astype(F32)).astype(BF16)


def _dot_sel_r2(x, sel):
    hi, lo = _split2(x)
    return jnp.dot(lo, sel, preferred_element_type=F32) + jnp.dot(hi, sel, preferred_element_type=F32)


(PB_RQ, PB_RK, PB_RV, PB_RG, PB_Z, PB_SX, PB_SBC, PB_GQ, PB_GK, PB_GV, PB_GG, PB_MQ, PB_MK, PB_MV, PB_MO,
 PB_X0) = range(16)
PRE_COLS = (PB_X0 + NEXP) * GROUP_W


def _pproj_body(x_ref, g_ref, w_ref, cos_ref, sin_ref, tri_ref, expand_ref, maskbd_ref, gvec_ref, convw_ref,
                convb_ref, pre_ref, convraw_ref, e_ref, *, rows):
    @pl.when(pl.program_id(1) == 0)
    def _():
        e_ref[0:SUBLANES, :] = jnp.zeros((SUBLANES, CONV_CH), F32)

    h_in = _rms(x_ref[...], g_ref[2:3, :]).astype(BF16)
    proj = lambda a, n: jnp.dot(h_in, w_ref[:, a:a + n], preferred_element_type=F32)

    def put(b, v):
        pre_ref[:, b * GROUP_W:(b + 1) * GROUP_W] = v

    maskbd_b = maskbd_ref[...].astype(BF16)
    seg_sum = lambda x: _dot_sel_r2(x, maskbd_b)

    lane = lax.broadcasted_iota(jnp.int32, (1, LANES), 1)
    in_lanes = lambda lo: (lane >= lo) & (lane < lo + N_HEADS)
    graw = proj(COL_GATES, LANES)
    put(PB_MQ, proj(COL_ML, GROUP_W))
    put(PB_MK, proj(COL_ML + GROUP_W, GROUP_W) * HEAD_DIM ** -0.5)
    put(PB_MV, proj(COL_ML + 2 * GROUP_W, GROUP_W))
    put(PB_MO, proj(COL_ML + 3 * GROUP_W, GROUP_W))
    xb = graw + gvec_ref[0:1, :]
    sp = _softplus(xb)
    neg_a = -jnp.exp(gvec_ref[1:2, :])
    logf = -_softplus(-xb)
    pre = jnp.where(lane < 2 * N_HEADS, neg_a * sp, jnp.where(in_lanes(G_MF), logf, 0.0))
    parts = jnp.dot(tri_ref[...], jnp.concatenate(_split3(pre), axis=1), preferred_element_type=F32)
    cum = (parts[:, 2 * LANES:] + parts[:, LANES:2 * LANES]) + parts[:, :LANES]
    imb = xb - pltpu.roll(cum, LANES - (G_MF - G_MI), axis=1)
    dtv = pltpu.roll(sp, G_DTV - G_DT, axis=1)
    gates = jnp.where(in_lanes(G_GB), _sigmoid(graw),
                      jnp.where(in_lanes(G_MI), imb, jnp.where(in_lanes(G_DTV), dtv, cum)))
    gates3 = jnp.concatenate(_split3(gates), axis=1)

    lane_g = lax.broadcasted_iota(jnp.int32, (1, GROUP_W), 1)
    first_half = (lane_g % HEAD_DIM) < (HEAD_DIM // 2)

    def rotary(x):
        swapped = jnp.where(first_half, pltpu.roll(x, GROUP_W - HEAD_DIM // 2, axis=1),
                            pltpu.roll(x, HEAD_DIM // 2, axis=1))
        return x * cos_ref[...] + swapped * sin_ref[...]

    def conv_in(c):
        e_ref[SUBLANES:SUBLANES + rows, c * GROUP_W:(c + 1) * GROUP_W] = proj(COL_CONV + c * GROUP_W, GROUP_W)

    def conv_out(c):
        cs = slice(c * GROUP_W, (c + 1) * GROUP_W)
        acc = e_ref[SUBLANES:SUBLANES + rows, cs] * convw_ref[CONV_W - 1:CONV_W, cs]
        for s in range(1, CONV_W):
            acc = acc + e_ref[SUBLANES - s:SUBLANES - s + rows, cs] * convw_ref[CONV_W - 1 - s:CONV_W - s, cs]
        return _silu(acc + convb_ref[:, cs])

    def expand(j):
        put(PB_X0 + j, jnp.dot(gates3, expand_ref[:, j * GROUP_W:(j + 1) * GROUP_W], preferred_element_type=F32))

    conv_in(0)
    conv_in(1)
    expand(0)
    put(PB_SX, conv_out(0))
    conv_in(2)
    expand(1)
    put(PB_SBC, conv_out(1))
    conv_in(3)
    expand(2)
    gq = conv_out(2)
    put(PB_GQ, gq * lax.rsqrt(seg_sum(gq * gq) + EPS) * HEAD_DIM ** -0.5)
    conv_in(4)
    expand(3)
    gk = conv_out(3)
    put(PB_GK, gk * lax.rsqrt(seg_sum(gk * gk) + EPS))
    rq = proj(COL_RET, GROUP_W)
    expand(4)
    put(PB_GV, conv_out(4))
    convraw_ref[...] = e_ref[rows:rows + SUBLANES, :]
    e_ref[0:SUBLANES, :] = e_ref[rows:rows + SUBLANES, :]
    rk = proj(COL_RET + GROUP_W, GROUP_W)
    expand(5)
    put(PB_RQ, rotary(rq))
    put(PB_RV, proj(COL_RET + 2 * GROUP_W, GROUP_W))
    put(PB_RG, proj(COL_RET + 3 * GROUP_W, GROUP_W))
    put(PB_RK, rotary(rk) * HEAD_DIM ** -0.5)
    put(PB_Z, proj(COL_SSM_Z, GROUP_W))
    put(PB_GG, proj(COL_GDN_GATE, GROUP_W))


def _pproj(x, sandwich, w, cos, sin, small, l, *, nbatch, seqlen):
    rows = min(TOKEN_TILE, seqlen)
    ntile = seqlen // rows
    tri, expand, maskbd = _pproj_consts(rows, PROMPT_CHUNK)
    gvec, convw, convb = small[:3]
    row_map = lambda b, t: (b * ntile + t, 0)
    full = lambda a: pl.BlockSpec(a.shape, lambda b, t: (0, 0))
    in_specs = [pl.BlockSpec((rows, D_MODEL), row_map),
                _layer_block((N_SANDWICH, D_MODEL), (l,), ngrid=2),
                _layer_block((D_MODEL, IN_PAD), (l,), ngrid=2),
                pl.BlockSpec((rows, GROUP_W), lambda b, t: (t, 0)), pl.BlockSpec((rows, GROUP_W), lambda b, t: (t, 0)),
                full(tri), full(expand), full(maskbd)]
    in_specs += [_layer_block(a.shape[1:], (l,), ngrid=2, buffered=False) for a in (gvec, convw, convb)]
    m = x.shape[0]
    return pl.pallas_call(
        functools.partial(_pproj_body, rows=rows),
        out_shape=[jax.ShapeDtypeStruct((m, PRE_COLS), F32),
                   jax.ShapeDtypeStruct((nbatch * SUBLANES, CONV_CH), F32)],
        grid=(nbatch, ntile), in_specs=in_specs,
        out_specs=[pl.BlockSpec((rows, PRE_COLS), row_map), pl.BlockSpec((SUBLANES, CONV_CH), lambda b, t: (b, 0))],
        scratch_shapes=[pltpu.VMEM((rows + SUBLANES, CONV_CH), F32)],
        compiler_params=pltpu.CompilerParams(dimension_semantics=("parallel", "arbitrary"),
                                             vmem_limit_bytes=VMEM_LIMIT),
        name="pproj",
    )(x, sandwich, w, cos, sin, tri, expand, maskbd, gvec, convw, convb)


def _mixp_body(pre_ref, rinter_ref, rtail_ref, rintra_ref, maskbd_ref, maskg_ref, maskht_ref, dvec_ref, norms_ref,
               mixed_ref, oret_ref, ossm_ref, ogdn_ref, omc_ref, omn_ref, omm_ref,
               o_ref, sret_ref, sht_ref, sgdn_ref, smc_ref, vec_ref, *, rows, chunk, ret_carry):
    nchunks = rows // chunk
    t_id = pl.program_id(1)
    blk = lambda b, rs=slice(None): pre_ref[rs, b * GROUP_W:(b + 1) * GROUP_W]

    @pl.when(t_id == 0)
    def _():
        sret_ref[...] = jnp.zeros_like(sret_ref)
        sht_ref[...] = jnp.zeros_like(sht_ref)
        sgdn_ref[...] = jnp.zeros_like(sgdn_ref)
        smc_ref[...] = jnp.zeros_like(smc_ref)
        vec_ref[...] = jnp.zeros_like(vec_ref)

    maskbd = maskbd_ref[...]
    maskbd_b = maskbd.astype(BF16)
    maskg = maskg_ref[...]
    maskg_b = maskg.astype(BF16)

    def seg_sum(x):
        return _dot_sel_r2(x, maskbd_b)

    r_i = lax.broadcasted_iota(jnp.int32, (chunk, GROUP_W), 0)
    c_i = lax.broadcasted_iota(jnp.int32, (chunk, GROUP_W), 1) % HEAD_DIM
    trim = r_i >= c_i
    diag = r_i == c_i
    strict = r_i > c_i
    eye = diag.astype(F32)
    head_of_lane = lax.broadcasted_iota(jnp.int32, (1, GROUP_W), 1) // HEAD_DIM
    carry_row = jnp.zeros((1, GROUP_W), F32)
    for h in range(N_HEADS):
        carry_row = jnp.where(head_of_lane == h, ret_carry[h], carry_row)
    maskht = maskht_ref[...]

    def bd(x):
        xb16 = x.astype(BF16)
        return jnp.concatenate([xb16] * N_HEADS, axis=0) * maskbd_b

    def last_row(x):
        return x[chunk - 1:chunk, :]

    def row_form(x):
        return jnp.sum(jnp.where(diag, x, 0.0), axis=0, keepdims=True)

    def seg_max(x):
        out = None
        for h in range(N_HEADS):
            sel = head_of_lane == h
            m_h = jnp.max(jnp.where(sel, x, -jnp.inf), axis=-1, keepdims=True)
            out = jnp.where(sel, m_h, 0.0) if out is None else jnp.where(sel, m_h, out)
        return out

    chunks = [slice(c * chunk, (c + 1) * chunk) for c in range(nchunks)]

    g_gam, g_beta, g_ecum, g_k, g_n, g_p = [], [], [], [], [], []
    for rs in chunks:
        ce = blk(PB_X0 + X_CGDN, rs)
        gam = jnp.exp(jnp.where(trim, ce - row_form(ce), -jnp.inf))
        k_c = blk(PB_GK, rs)
        beta = blk(PB_X0 + X_BETA, rs)
        n0 = -(_dot_nt(k_c, bd(k_c)) * jnp.where(strict, gam, 0.0) * beta)
        g_gam.append(gam), g_beta.append(beta), g_ecum.append(jnp.exp(ce)), g_k.append(k_c)
        g_n.append(n0), g_p.append(eye + n0)
    for _ in range(max(int(math.ceil(math.log2(chunk))) - 1, 0)):
        for c in range(nchunks):
            g_n[c] = _dot(g_n[c], bd(g_n[c]))
        for c in range(nchunks):
            g_p[c] = g_p[c] + _dot(g_p[c], bd(g_n[c]))
    g_u, g_w = [], []
    for c, rs in enumerate(chunks):
        g_u.append(_dot(g_p[c], bd(blk(PB_GV, rs) * g_beta[c])))
        g_w.append(_dot(g_p[c], bd(g_k[c] * (g_beta[c] * g_ecum[c]))))

    s_ret, s_ht, s_gdn, s_mc = sret_ref[...], sht_ref[...], sgdn_ref[...], smc_ref[...]
    n_row, m_row = vec_ref[0:1, :], vec_ref[1:2, :]
    for c, rs in enumerate(chunks):
        rq, rk, rv = blk(PB_RQ, rs), blk(PB_RK, rs), blk(PB_RV, rs)
        ce_s = blk(PB_X0 + X_CSSM, rs)
        b_c = pre_ref[rs, PB_SBC * GROUP_W:PB_SBC * GROUP_W + 2 * HEAD_DIM]
        c_c = pre_ref[rs, PB_SBC * GROUP_W + 2 * HEAD_DIM:(PB_SBC + 1) * GROUP_W]
        xdt = blk(PB_SX, rs) * blk(PB_X0 + X_DT, rs)
        ce_g, gq = blk(PB_X0 + X_CGDN, rs), blk(PB_GQ, rs)
        mq, mk, mv = blk(PB_MQ, rs), blk(PB_MK, rs), blk(PB_MV, rs)
        b_e, imb_e = blk(PB_X0 + X_B, rs), blk(PB_X0 + X_IMB, rs)

        w_s = _dot(g_w[c], s_gdn)
        att = _dot_nt(rq, bd(rk))
        scores = _dot_nt(c_c, jnp.concatenate([b_c.astype(BF16)] * N_HEADS, axis=0) * maskg_b)
        qk_g = _dot_nt(gq, bd(g_k[c]))
        qk_m = _dot_nt(mq, bd(mk))
        o_ret = _dot(rq * rinter_ref[...], s_ret)
        o_ssm = _dot(c_c, s_ht)
        o_gdn = _dot(gq * g_ecum[c], s_gdn)
        o_ml = _dot(mq, s_mc)
        qn = seg_sum(mq * n_row)

        v_new = g_u[c] - w_s
        att = att * rintra_ref[...]
        ce_s_last, ce_g_last = last_row(ce_s), last_row(ce_g)
        scores = scores * jnp.exp(jnp.where(trim, ce_s - row_form(ce_s), -jnp.inf))
        qk_g = qk_g * g_gam[c]
        dmat = jnp.where(trim, b_e + row_form(imb_e), -jnp.inf)
        m_t = jnp.maximum(b_e + m_row, seg_max(dmat))
        inter = jnp.exp(b_e + m_row - m_t)
        qk_m = qk_m * jnp.exp(dmat - m_t)
        m_new, b_last = last_row(m_t), last_row(b_e)
        kw = mk * jnp.exp(b_last + imb_e - m_new)
        cs = jnp.exp(b_last + m_row - m_new)

        o_gdn = o_gdn + _dot(qk_g, bd(v_new))
        d_gdn = _dot_tn(g_k[c] * jnp.exp(ce_g_last - ce_g), v_new)
        o_ret = o_ret + _dot(att, bd(rv))
        o_ssm = o_ssm * jnp.exp(ce_s) + _dot(scores, bd(xdt))
        num = _dot(qk_m, bd(mv)) + inter * o_ml
        den = seg_sum(qk_m) + inter * qn
        d_ret = _dot_tn(rk * rtail_ref[...], rv)
        d_ht = _dot_tn(b_c, xdt * jnp.exp(ce_s_last - ce_s))
        d_mc = _dot_tn(kw, mv)

        s_gdn = s_gdn * jnp.exp(ce_g_last) + d_gdn * maskbd
        o_ml = num / jnp.maximum(jnp.abs(den), jnp.exp(-m_t))
        o_ref[rs, :] = jnp.concatenate([o_ret, o_ssm, o_gdn, o_ml], axis=1)
        s_ret = s_ret * carry_row + d_ret * maskbd
        s_ht = s_ht * jnp.exp(ce_s_last) + d_ht * maskht
        s_mc = s_mc * cs + d_mc * maskbd
        n_row = n_row * cs + jnp.sum(kw, axis=0, keepdims=True)
        m_row = m_new

    sret_ref[...] = s_ret
    sht_ref[...] = s_ht
    sgdn_ref[...] = s_gdn
    smc_ref[...] = s_mc
    vec_ref[0:1, :] = n_row
    vec_ref[1:2, :] = m_row

    def head_norm(x, g):
        return x * lax.rsqrt(seg_sum(x * x) * (1.0 / HEAD_DIM) + EPS) * g

    out_ret = head_norm(o_ref[:, 0:GROUP_W], norms_ref[0:1, :]) * _silu(blk(PB_RG))
    mixed_ref[:, 0:GROUP_W] = out_ret.astype(BF16)
    y = o_ref[:, GROUP_W:2 * GROUP_W] + dvec_ref[...] * blk(PB_SX)
    out_ssm = _rms(y * _silu(blk(PB_Z)), norms_ref[1:2, :])
    mixed_ref[:, GROUP_W:2 * GROUP_W] = out_ssm.astype(BF16)
    out_gdn = head_norm(o_ref[:, 2 * GROUP_W:3 * GROUP_W], norms_ref[2:3, :]) * _silu(blk(PB_GG))
    mixed_ref[:, 2 * GROUP_W:3 * GROUP_W] = out_gdn.astype(BF16)
    out_ml = _sigmoid(blk(PB_MO)) * head_norm(o_ref[:, 3 * GROUP_W:4 * GROUP_W], norms_ref[3:4, :])
    mixed_ref[:, 3 * GROUP_W:4 * GROUP_W] = out_ml.astype(BF16)

    @pl.when(t_id == pl.num_programs(1) - 1)
    def _():
        for h in range(N_HEADS):
            hs = slice(h * HEAD_DIM, (h + 1) * HEAD_DIM)
            grp = h // (N_HEADS // 2)
            oret_ref[0, h] = s_ret[hs, hs]
            ogdn_ref[0, h] = s_gdn[hs, hs]
            omc_ref[0, h] = s_mc[hs, hs]
            ossm_ref[0, h] = s_ht[grp * HEAD_DIM:(grp + 1) * HEAD_DIM, hs]
        omn_ref[0] = jnp.concatenate([n_row[:, h * HEAD_DIM:(h + 1) * HEAD_DIM] for h in range(N_HEADS)], axis=0)
        omm_ref[0] = jnp.concatenate([m_row[:, h * HEAD_DIM:h * HEAD_DIM + 1] for h in range(N_HEADS)], axis=1)


def _prompt_consts(rows, chunk):
    lg = np.log(1.0 - np.exp2(-5.0 - np.arange(N_HEADS, dtype=np.float64)))
    idx = np.arange(chunk, dtype=np.float64)
    rel = idx[:, None] - idx[None, :]
    intra = np.where(rel[None] >= 0, np.exp(np.maximum(rel[None], 0.0) * lg[:, None, None]), 0.0)
    intra = np.concatenate(list(intra), axis=1)
    lanes = lambda t: np.repeat(t.T, HEAD_DIM, axis=1)
    inter = lanes(np.exp((idx[None, :] + 1.0) * lg[:, None]))
    tail = lanes(np.exp((chunk - 1.0 - idx[None, :]) * lg[:, None]))
    carry = tuple(float(c) for c in np.exp(chunk * lg))
    g = np.arange(GROUP_W)
    maskbd = (g[:, None] // HEAD_DIM == g[None, :] // HEAD_DIM)
    n = np.arange(2 * HEAD_DIM)
    maskg = (g[:, None] // (2 * HEAD_DIM) == n[None, :] // HEAD_DIM)
    f = lambda a, dt=F32: jnp.asarray(np.asarray(a, np.float32), dt)
    return (f(inter), f(tail), f(intra), f(maskbd), f(maskg), f(maskg.T)), carry


def _pproj_consts(rows, chunk):
    r = np.arange(rows)
    tri = (r[:, None] >= r[None, :]) & (r[:, None] // chunk == r[None, :] // chunk)
    expand = np.zeros((LANES, NEXP * GROUP_W), np.float32)
    for j, lo in enumerate((G_DT, G_DTV, G_GA, G_GB, G_MF, G_MI)):
        for h in range(N_HEADS):
            expand[lo + h, j * GROUP_W + h * HEAD_DIM:j * GROUP_W + (h + 1) * HEAD_DIM] = 1.0
    expand = np.tile(expand, (3, 1))
    g = np.arange(GROUP_W)
    maskbd = (g[:, None] // HEAD_DIM == g[None, :] // HEAD_DIM)
    f = lambda a, dt: jnp.asarray(np.asarray(a, np.float32), dt)
    return f(tri, BF16), f(expand, BF16), f(maskbd, F32)


def _mix_prompt(pre, small, l, *, nbatch, seqlen):
    rows, chunk = min(PROMPT_TILE, seqlen), PROMPT_CHUNK
    ntile = seqlen // rows
    consts, carry = _prompt_consts(rows, chunk)
    dvec, norms = small[3:]
    m = pre.shape[0]
    row_map = lambda b, t: (b * ntile + t, 0)
    full = lambda a: pl.BlockSpec(a.shape, lambda b, t: (0, 0))
    in_specs = [pl.BlockSpec((rows, PRE_COLS), row_map)] + [full(a) for a in consts]
    in_specs += [_layer_block(a.shape[1:], (l,), ngrid=2, buffered=False) for a in (dvec, norms)]
    st4 = pl.BlockSpec((1, N_HEADS, HEAD_DIM, HEAD_DIM), lambda b, t: (b, 0, 0, 0))
    out_shape = [jax.ShapeDtypeStruct((m, D_MODEL), BF16)]
    out_shape += [jax.ShapeDtypeStruct((nbatch, N_HEADS, HEAD_DIM, HEAD_DIM), F32) for _ in range(4)]
    out_shape += [jax.ShapeDtypeStruct((nbatch, N_HEADS, HEAD_DIM), F32),
                  jax.ShapeDtypeStruct((nbatch, 1, N_HEADS), F32)]
    out_specs = [pl.BlockSpec((rows, D_MODEL), row_map), st4, st4, st4, st4,
                 pl.BlockSpec((1, N_HEADS, HEAD_DIM), lambda b, t: (b, 0, 0)),
                 pl.BlockSpec((1, 1, N_HEADS), lambda b, t: (b, 0, 0))]
    scratch = [pltpu.VMEM((rows, D_MODEL), F32),
               pltpu.VMEM((GROUP_W, GROUP_W), F32), pltpu.VMEM((2 * HEAD_DIM, GROUP_W), F32),
               pltpu.VMEM((GROUP_W, GROUP_W), F32), pltpu.VMEM((GROUP_W, GROUP_W), F32),
               pltpu.VMEM((SUBLANES, GROUP_W), F32)]
    body = functools.partial(_mixp_body, rows=rows, chunk=chunk, ret_carry=carry)
    outs = pl.pallas_call(
        body, out_shape=out_shape, grid=(nbatch, ntile), in_specs=in_specs, out_specs=out_specs,
        scratch_shapes=scratch,
        compiler_params=pltpu.CompilerParams(dimension_semantics=("parallel", "arbitrary"),
                                             vmem_limit_bytes=VMEM_LIMIT),
        name="mix_prompt",
    )(pre, *consts, dvec, norms)
    outs = list(outs)
    outs[2] = jnp.swapaxes(outs[2], -1, -2)
    return outs


def _ret_tables(chunk, pad):
    lg = np.log(1.0 - np.exp2(-5.0 - np.arange(N_HEADS, dtype=np.float64)))
    idx = np.arange(chunk, dtype=np.float64)
    rel = idx[:, None] - idx[None, :]
    intra = np.where(rel[None] >= 0, np.exp(np.maximum(rel[None], 0.0) * lg[:, None, None]), 0.0)
    inter = np.where(idx[None, :] >= pad, np.exp((idx[None, :] - pad + 1.0) * lg[:, None]), 0.0)
    tail = np.exp((chunk - 1.0 - idx[None, :]) * lg[:, None])
    carry = np.exp((chunk - pad) * lg)
    lanes = lambda t: np.repeat(t.T, HEAD_DIM, axis=1)
    return (jnp.asarray(lanes(inter), F32), jnp.asarray(lanes(tail), F32), jnp.asarray(intra, F32),
            tuple(float(c) for c in carry))


def _const_mats(rows, chunk):
    r = np.arange(rows)
    tri = ((r[:, None] >= r[None, :]) & (r[:, None] // chunk == r[None, :] // chunk))
    sel = np.zeros((4 * SUBLANES, LANES), np.float32)
    sel[np.arange(24), np.arange(24)] = 1.0
    g = np.arange(GROUP_W)
    ones_bd = (g[:, None] // HEAD_DIM == g[None, :] // HEAD_DIM)
    return (jnp.asarray(tri, BF16), jnp.asarray(sel, BF16), jnp.asarray(ones_bd, BF16))


def _mix_sample(proj, hist, cos, sin, small, states, l, *, nbatch):
    rows, chunk, pad = SAMPLE_GROUP * SAMPLE_PAD, SAMPLE_PAD, SAMPLE_PAD - 4
    row_map = lambda i: (i, 0)
    c2 = lambda i: (0, 0)
    rinter, rtail, rintra, carry = _ret_tables(chunk, pad)
    tri, sel, ones_bd = _const_mats(rows, chunk)
    m = proj.shape[0]
    in_specs = [pl.BlockSpec((rows, IN_PAD), row_map),
                pl.BlockSpec((None, rows, CONV_CH), lambda i: (l, i, 0)),
                pl.BlockSpec((rows, GROUP_W), c2), pl.BlockSpec((rows, GROUP_W), c2),
                pl.BlockSpec((chunk, GROUP_W), c2), pl.BlockSpec((chunk, GROUP_W), c2),
                pl.BlockSpec((N_HEADS, chunk, chunk), lambda i: (0, 0, 0)),
                pl.BlockSpec((rows, rows), c2), pl.BlockSpec((4 * SUBLANES, LANES), c2),
                pl.BlockSpec((GROUP_W, GROUP_W), c2)]
    in_specs += [_layer_block(a.shape[1:], (l,), buffered=False) for a in small]
    g = SAMPLE_GROUP
    in_specs += [pl.BlockSpec((None, g, N_HEADS, HEAD_DIM, HEAD_DIM), lambda i: (l, i, 0, 0, 0))] * 4
    in_specs += [pl.BlockSpec((None, g, N_HEADS, HEAD_DIM), lambda i: (l, i, 0, 0)),
                 pl.BlockSpec((None, g, 1, N_HEADS), lambda i: (l, i, 0, 0))]
    st4 = pl.BlockSpec((g, N_HEADS, HEAD_DIM, HEAD_DIM), lambda i: (i, 0, 0, 0))
    out_specs = [pl.BlockSpec((rows, D_MODEL), row_map), pl.BlockSpec((rows, CONV_CH), row_map),
                 st4, st4, st4, st4,
                 pl.BlockSpec((g, N_HEADS, HEAD_DIM), lambda i: (i, 0, 0)),
                 pl.BlockSpec((g, 1, N_HEADS), lambda i: (i, 0, 0))]
    out_shape = [jax.ShapeDtypeStruct((m, D_MODEL), BF16), jax.ShapeDtypeStruct((m, CONV_CH), F32)]
    out_shape += [jax.ShapeDtypeStruct((nbatch, N_HEADS, HEAD_DIM, HEAD_DIM), F32) for _ in range(4)]
    out_shape += [jax.ShapeDtypeStruct((nbatch, N_HEADS, HEAD_DIM), F32),
                  jax.ShapeDtypeStruct((nbatch, 1, N_HEADS), F32)]
    scratch = [pltpu.VMEM((rows + SUBLANES, CONV_CH), F32), pltpu.VMEM((rows, LANES), F32)]
    scratch += [pltpu.VMEM((rows, GROUP_W), F32) for _ in range(7)]
    scratch += [pltpu.VMEM((rows, D_MODEL), F32)]
    body = functools.partial(_mix_body, sample=True, rows=rows, chunk=chunk, ret_carry=carry)
    return pl.pallas_call(
        body, out_shape=out_shape, grid=(nbatch // g,), in_specs=in_specs, out_specs=out_specs,
        scratch_shapes=scratch,
        compiler_params=pltpu.CompilerParams(dimension_semantics=("parallel",), vmem_limit_bytes=VMEM_LIMIT),
        name="mix_sample",
    )(proj, hist, cos, sin, rinter, rtail, rintra, tri, sel, ones_bd, *small, *states)


S_RQ, S_RK, S_RV, S_XDT, S_B, S_C, S_GQ, S_GK, S_GV, S_MQ, S_MK, S_MV, S_GATE = (
    HEAD_DIM * i for i in range(13))
SLAB_ROWS = S_GATE + SUBLANES
GT_SSM_A, GT_GDN_A, GT_BETA, GT_LOGI, GT_LOGF, GT_GAMMA = range(6)
P_RG, P_Z, P_XS, P_GG, P_MO = (GROUP_W * i for i in range(5))
POST_ROWS = 5 * GROUP_W
C_CONVW, C_CONVB = 0, CONV_W * CONV_CH
C_BIAS = C_CONVB + CONV_CH
C_ALOG = C_BIAS + 4 * SUBLANES
C_GAMMA = C_ALOG + 4 * SUBLANES
C_ROWS = C_GAMMA + SUBLANES
N_ROWS = 5 * GROUP_W
N_TOK = 4


def _spre_body(x_ref, g_ref, w_ref, hist_ref, rot_ref, colp_ref, slab_ref, post_ref, convraw_ref):
    nb = LANES
    h_in = _rms(x_ref[...], g_ref[2:3, :]).astype(BF16)
    proj = jnp.dot(h_in, w_ref[...], preferred_element_type=F32)
    for j in range(CONV_W - 1):
        convraw_ref[j] = proj[(j + 1) * nb:(j + 2) * nb, COL_CONV:COL_CONV + CONV_CH]
    pt = [proj[t * nb:(t + 1) * nb, :].T for t in range(N_TOK)]
    src = [hist_ref[j].T for j in range(CONV_W - 1)] + [p[COL_CONV:COL_CONV + CONV_CH] for p in pt]
    bias = colp_ref[C_BIAS:C_BIAS + 4 * SUBLANES, :]
    neg_a = -jnp.exp(colp_ref[C_ALOG:C_ALOG + 4 * SUBLANES, :])
    gamma = colp_ref[C_GAMMA:C_GAMMA + SUBLANES, :]
    zero_row = jnp.zeros((1, LANES), F32)

    def rotary(x, t, k):
        swapped = jnp.concatenate(
            [x[h * HEAD_DIM + off:h * HEAD_DIM + off + HEAD_DIM // 2]
             for h in range(N_HEADS) for off in (HEAD_DIM // 2, 0)], axis=0)
        return x * rot_ref[0, t] + swapped * rot_ref[1, t]

    for t in range(N_TOK):
        p = pt[t]
        g = p[COL_GATES:COL_GATES + 4 * SUBLANES]
        xb = g + bias
        sp = _softplus(xb)
        decay = jnp.exp(neg_a * sp)
        beta = _sigmoid(g)
        logf = -_softplus(-xb)
        conv = src[t] * colp_ref[C_CONVW:C_CONVW + CONV_CH, :]
        for j in range(1, CONV_W):
            conv = conv + src[t + j] * colp_ref[C_CONVW + j * CONV_CH:C_CONVW + (j + 1) * CONV_CH, :]
        cv = _silu(conv + colp_ref[C_CONVB:C_CONVB + CONV_CH, :])
        rq = rotary(p[COL_RET:COL_RET + GROUP_W], t, 0)
        rk = rotary(p[COL_RET + GROUP_W:COL_RET + 2 * GROUP_W], t, 1) * HEAD_DIM ** -0.5
        post_ref[t, P_RG:P_RG + GROUP_W, :] = p[COL_RET + 3 * GROUP_W:COL_RET + 4 * GROUP_W]
        post_ref[t, P_Z:P_Z + GROUP_W, :] = p[COL_SSM_Z:COL_SSM_Z + GROUP_W]
        post_ref[t, P_XS:P_XS + GROUP_W, :] = cv[0:GROUP_W]
        post_ref[t, P_GG:P_GG + GROUP_W, :] = p[COL_GDN_GATE:COL_GDN_GATE + GROUP_W]
        post_ref[t, P_MO:P_MO + GROUP_W, :] = p[COL_ML + 3 * GROUP_W:COL_ML + 4 * GROUP_W]
        for h in range(N_HEADS):
            hs = slice(h * HEAD_DIM, (h + 1) * HEAD_DIM)
            grp = h // (N_HEADS // 2)
            put = lambda r, v: slab_ref.__setitem__((h, t, slice(r, r + v.shape[0]), slice(None)), v)
            l2 = lambda v: v * lax.rsqrt(jnp.sum(v * v, axis=0, keepdims=True) + EPS)
            put(S_RQ, rq[hs])
            put(S_RK, rk[hs])
            put(S_RV, p[COL_RET + 2 * GROUP_W + h * HEAD_DIM:COL_RET + 2 * GROUP_W + (h + 1) * HEAD_DIM])
            put(S_XDT, cv[hs] * sp[G_DT + h:G_DT + h + 1])
            put(S_B, cv[GROUP_W + grp * HEAD_DIM:GROUP_W + (grp + 1) * HEAD_DIM])
            put(S_C, cv[GROUP_W + 2 * HEAD_DIM + grp * HEAD_DIM:GROUP_W + 2 * HEAD_DIM + (grp + 1) * HEAD_DIM])
            put(S_GQ, l2(cv[SSM_CONV_CH + h * HEAD_DIM:SSM_CONV_CH + (h + 1) * HEAD_DIM]) * HEAD_DIM ** -0.5)
            put(S_GK, l2(cv[SSM_CONV_CH + GROUP_W + h * HEAD_DIM:SSM_CONV_CH + GROUP_W + (h + 1) * HEAD_DIM]))
            put(S_GV, cv[SSM_CONV_CH + 2 * GROUP_W + h * HEAD_DIM:SSM_CONV_CH + 2 * GROUP_W + (h + 1) * HEAD_DIM])
            put(S_MQ, p[COL_ML + h * HEAD_DIM:COL_ML + (h + 1) * HEAD_DIM])
            put(S_MK, p[COL_ML + GROUP_W + h * HEAD_DIM:COL_ML + GROUP_W + (h + 1) * HEAD_DIM] * HEAD_DIM ** -0.5)
            put(S_MV, p[COL_ML + 2 * GROUP_W + h * HEAD_DIM:COL_ML + 2 * GROUP_W + (h + 1) * HEAD_DIM])
            row = lambda a, j: a[j + h:j + h + 1]
            put(S_GATE, jnp.concatenate([row(decay, G_DT), row(decay, G_GA), row(beta, G_GB), row(xb, G_MI),
                                         row(logf, G_MF), gamma[h:h + 1], zero_row, zero_row], axis=0))


def _spre(x, sandwich, w, hist, rot, colp, l):
    m = x.shape[0]
    return pl.pallas_call(
        _spre_body,
        out_shape=[jax.ShapeDtypeStruct((N_HEADS, N_TOK, SLAB_ROWS, LANES), F32),
                   jax.ShapeDtypeStruct((N_TOK, POST_ROWS, LANES), F32),
                   jax.ShapeDtypeStruct((CONV_W - 1, LANES, CONV_CH), F32)],
        grid=(1,),
        in_specs=[pl.BlockSpec((m, D_MODEL), lambda i: (0, 0)),
                  _layer_block((N_SANDWICH, D_MODEL), (l,), buffered=False),
                  _layer_block((D_MODEL, IN_PAD), (l,), buffered=False),
                  _layer_block((CONV_W - 1, LANES, CONV_CH), (l,), buffered=False),
                  pl.BlockSpec(rot.shape, lambda i: (0, 0, 0, 0)),
                  _layer_block((C_ROWS, LANES), (l,), buffered=False)],
        out_specs=[pl.BlockSpec((N_HEADS, N_TOK, SLAB_ROWS, LANES), lambda i: (0, 0, 0, 0)),
                   pl.BlockSpec((N_TOK, POST_ROWS, LANES), lambda i: (0, 0, 0)),
                   pl.BlockSpec((CONV_W - 1, LANES, CONV_CH), lambda i: (0, 0, 0))],
        compiler_params=pltpu.CompilerParams(dimension_semantics=("arbitrary",), vmem_limit_bytes=VMEM_LIMIT),
        name="spre",
    )(x, sandwich, w, hist, rot, colp)


def _srec_body(*refs, aliased):
    ins, outs = refs[:7], refs[7 + (6 if aliased else 0):]
    j = pl.program_id(1)

    @pl.when(j == 0)
    def _():
        _srec_compute(*ins, *outs)

    if not aliased:
        @pl.when(j > 0)
        def _():
            for dst in outs[1:7]:
                dst[...] = jnp.zeros_like(dst)


def _srec_compute(slab_ref, sret_in, sssm_in, sgdn_in, smc_in, smn_in, smm_in,
                  ot_ref, sret_out, sssm_out, sgdn_out, smc_out, smn_out, smm_out, y8_ref):
    tile = lambda t, r: slab_ref[t, r:r + HEAD_DIM, :]
    gate = lambda t, j: slab_ref[t, S_GATE + j:S_GATE + j + 1, :]
    drow = lambda t, r, d: slab_ref[t, pl.ds(r + d, 1), :]
    zeros = jnp.zeros((HEAD_DIM, LANES), F32)
    sweep = lambda body, init: lax.fori_loop(0, HEAD_DIM, body, init, unroll=8)
    blk = lambda d: pl.ds(pl.multiple_of(d * HEAD_DIM, HEAD_DIM), HEAD_DIM)

    gamma = gate(0, GT_GAMMA)
    for t in range(N_TOK):
        src = sret_in if t == 0 else sret_out
        v = tile(t, S_RV)

        def body(d, acc, t=t, src=src, v=v):
            s = src[blk(d), :] * gamma + drow(t, S_RK, d) * v
            sret_out[blk(d), :] = s
            return acc + drow(t, S_RQ, d) * s
        ot_ref[t, 0:HEAD_DIM, :] = sweep(body, zeros)

    for t in range(N_TOK):
        src = sssm_in if t == 0 else sssm_out
        a, b_t, c_t = gate(t, GT_SSM_A), tile(t, S_B), tile(t, S_C)

        def body(p, carry, t=t, src=src, a=a, b_t=b_t, c_t=c_t):
            hp = src[blk(p), :] * a + drow(t, S_XDT, p) * b_t
            sssm_out[blk(p), :] = hp
            part = jnp.sum((hp * c_t).reshape(HEAD_DIM // SUBLANES, SUBLANES, LANES), axis=0)
            y8_ref[pl.ds(pl.multiple_of(p * SUBLANES, SUBLANES), SUBLANES), :] = part
            return carry
        sweep(body, 0)
        ot_ref[t, HEAD_DIM:2 * HEAD_DIM, :] = jnp.sum(y8_ref[...].reshape(HEAD_DIM, SUBLANES, LANES), axis=1)

    ks = sweep(lambda d, acc: acc + drow(0, S_GK, d) * sgdn_in[blk(d), :], zeros)
    for t in range(N_TOK):
        src = sgdn_in if t == 0 else sgdn_out
        a = gate(t, GT_GDN_A)
        u = gate(t, GT_BETA) * (tile(t, S_GV) - a * ks)
        nxt = t + 1 < N_TOK

        def body(d, acc, t=t, src=src, a=a, u=u, nxt=nxt):
            s = src[blk(d), :] * a + drow(t, S_GK, d) * u
            sgdn_out[blk(d), :] = s
            o = acc[0] + drow(t, S_GQ, d) * s
            return (o, acc[1] + drow(t + 1, S_GK, d) * s) if nxt else (o, acc[1])
        o, ks = sweep(body, (zeros, zeros))
        ot_ref[t, 2 * HEAD_DIM:3 * HEAD_DIM, :] = o

    m_row, n_tile = smm_in[...], smn_in[...]
    for t in range(N_TOK):
        src = smc_in if t == 0 else smc_out
        logi, logf = gate(t, GT_LOGI), gate(t, GT_LOGF)
        m_new = jnp.maximum(logf + m_row, logi)
        fs, ins = jnp.exp(logf + m_row - m_new), jnp.exp(logi - m_new)
        n_tile = fs * n_tile + ins * tile(t, S_MK)
        qn = jnp.sum(tile(t, S_MQ) * n_tile, axis=0, keepdims=True)
        v = tile(t, S_MV) * ins

        def body(d, acc, t=t, src=src, fs=fs, v=v):
            c = src[blk(d), :] * fs + drow(t, S_MK, d) * v
            smc_out[blk(d), :] = c
            return acc + drow(t, S_MQ, d) * c
        num = sweep(body, zeros)
        ot_ref[t, 3 * HEAD_DIM:4 * HEAD_DIM, :] = num / jnp.maximum(jnp.abs(qn), jnp.exp(-m_new))
        m_row = m_new
    smn_out[...] = n_tile
    smm_out[...] = m_row


def _srec(slab, states, prev, l):
    depth = states[0].shape[0]
    rows = HEAD_DIM * HEAD_DIM
    st_shapes = [(rows, LANES)] * 4 + [(HEAD_DIM, LANES), (1, LANES)]
    spec = lambda tail, lead: pl.BlockSpec((None, None) + tail, lambda h, j: (lead(j), h, 0, 0))
    slab_spec = lambda r: pl.BlockSpec((None, N_TOK, r, LANES), lambda h, j: (h, 0, 0, 0))
    in_specs = [slab_spec(SLAB_ROWS)] + [spec(s, lambda j: l) for s in st_shapes]
    args = [slab, *states]
    aliases = {}
    if l:
        in_specs += [pl.BlockSpec(memory_space=pl.ANY)] * 6
        args += list(prev)
        aliases = {7 + i: 1 + i for i in range(6)}
    out_shape = [jax.ShapeDtypeStruct((N_HEADS, N_TOK, GROUP_W, LANES), F32)]
    out_shape += [jax.ShapeDtypeStruct((depth, N_HEADS) + s, F32) for s in st_shapes]
    out_specs = [slab_spec(GROUP_W)] + [spec(s, lambda j: l + j) for s in st_shapes]
    return pl.pallas_call(
        functools.partial(_srec_body, aliased=bool(l)),
        out_shape=out_shape, grid=(N_HEADS, 1 if l else depth), in_specs=in_specs, out_specs=out_specs,
        input_output_aliases=aliases,
        scratch_shapes=[pltpu.VMEM((HEAD_DIM * SUBLANES, LANES), F32)],
        compiler_params=pltpu.CompilerParams(dimension_semantics=("arbitrary", "arbitrary"),
                                             vmem_limit_bytes=VMEM_LIMIT),
        name="srec",
    )(*args)


def _spost_body(x_ref, ot_ref, post_ref, cn_ref, g_ref, w_ref, o_ref):
    nb = LANES

    def head_norm(x):
        out = []
        for h in range(N_HEADS):
            b = x[h * HEAD_DIM:(h + 1) * HEAD_DIM]
            out.append(b * lax.rsqrt(jnp.mean(b * b, axis=0, keepdims=True) + EPS))
        return jnp.concatenate(out, axis=0)

    gain = lambda i: cn_ref[i * GROUP_W:(i + 1) * GROUP_W, :]
    for t in range(N_TOK):
        mixer = lambda i: jnp.concatenate(
            [ot_ref[h, t, i * HEAD_DIM:(i + 1) * HEAD_DIM, :] for h in range(N_HEADS)], axis=0)
        pin = lambda r: post_ref[t, r:r + GROUP_W, :]
        out_ret = head_norm(mixer(0)) * gain(0) * _silu(pin(P_RG))
        y = (mixer(1) + gain(4) * pin(P_XS)) * _silu(pin(P_Z))
        out_ssm = y * lax.rsqrt(jnp.mean(y * y, axis=0, keepdims=True) + EPS) * gain(1)
        out_gdn = head_norm(mixer(2)) * gain(2) * _silu(pin(P_GG))
        out_ml = _sigmoid(pin(P_MO)) * (head_norm(mixer(3)) * gain(3))
        mixed = jnp.concatenate([out_ret, out_ssm, out_gdn, out_ml], axis=0).T.astype(BF16)
        y = jnp.dot(mixed, w_ref[...], preferred_element_type=F32)
        rs = slice(t * nb, (t + 1) * nb)
        o_ref[rs, :] = x_ref[rs, :] + _rms(y, g_ref[3:4, :])


def _spost(x, ot, post, cn, sandwich, w, l):
    m = x.shape[0]
    return pl.pallas_call(
        _spost_body,
        out_shape=jax.ShapeDtypeStruct((m, D_MODEL), F32),
        grid=(1,),
        in_specs=[pl.BlockSpec((m, D_MODEL), lambda i: (0, 0)),
                  pl.BlockSpec(ot.shape, lambda i: (0, 0, 0, 0)),
                  pl.BlockSpec(post.shape, lambda i: (0, 0, 0)),
                  _layer_block((N_ROWS, LANES), (l,), buffered=False),
                  _layer_block((N_SANDWICH, D_MODEL), (l,), buffered=False),
                  _layer_block((D_MODEL, D_MODEL), (l,), buffered=False)],
        out_specs=pl.BlockSpec((m, D_MODEL), lambda i: (0, 0)),
        compiler_params=pltpu.CompilerParams(dimension_semantics=("arbitrary",), vmem_limit_bytes=VMEM_LIMIT),
        name="spost",
    )(x, ot, post, cn, sandwich, w)


def _rot_tables(pos):
    half = HEAD_DIM // 2
    inv = ROPE_BASE ** (-jnp.arange(half, dtype=F32) / half)
    ang = pos.astype(F32)[:, None] * inv[None, :]
    cos, sin = jnp.cos(ang), jnp.sin(ang)
    cos_full = jnp.tile(jnp.concatenate([cos, cos], axis=-1), (1, N_HEADS))
    sin_signed = jnp.tile(jnp.concatenate([-sin, sin], axis=-1), (1, N_HEADS))
    return cos_full, sin_signed


def _small_params(ret_norm, ssm_conv_w, ssm_conv_b, ssm_dt_bias, ssm_A_log, ssm_D, ssm_norm, gdn_conv_w,
                  gdn_dt_bias, gdn_A_log, gdn_norm, mlstm_i_bias, mlstm_f_bias, mlstm_norm):
    depth = ret_norm.shape[0]
    z = lambda n: jnp.zeros((depth, n), F32)
    pad_lanes = z(LANES - 5 * N_HEADS)
    bias = jnp.concatenate([ssm_dt_bias, gdn_dt_bias, z(N_HEADS), mlstm_i_bias, mlstm_f_bias, pad_lanes], axis=1)
    alog = jnp.concatenate([ssm_A_log, gdn_A_log, z(3 * N_HEADS), pad_lanes], axis=1)
    gvec = jnp.stack([bias, alog], axis=1)
    convw = jnp.concatenate([ssm_conv_w, gdn_conv_w], axis=2)
    convb = jnp.concatenate([ssm_conv_b, z(CONV_CH - SSM_CONV_CH)], axis=1)[:, None, :]
    dvec = jnp.repeat(ssm_D, HEAD_DIM, axis=1)[:, None, :]
    norms = jnp.stack([ret_norm, ssm_norm, gdn_norm, mlstm_norm], axis=1)
    return gvec, convw, convb, dvec, norms


def kernel(x_prompt, x_sample, state_ret, state_ssm, state_ssm_conv, state_gdn, state_gdn_conv,
           state_mlstm_C, state_mlstm_n, state_mlstm_m, w_in, w_out, norm_sandwich, ffn_w_gate, ffn_w_up,
           ffn_w_down, ret_norm, ssm_conv_w, ssm_conv_b, ssm_dt_bias, ssm_A_log, ssm_D, ssm_norm, gdn_conv_w,
           gdn_dt_bias, gdn_A_log, gdn_norm, mlstm_i_bias, mlstm_f_bias, mlstm_norm):
    depth = w_in.shape[0]
    bp, lp, _ = x_prompt.shape
    bs, ls, _ = x_sample.shape
    assert (bs, ls) == (LANES, N_TOK)
    w_in_p = _prep_win(w_in)
    w_out_b = w_out.astype(BF16)
    wg, wu, wd = ffn_w_gate, ffn_w_up, ffn_w_down.astype(BF16)
    small = _small_params(ret_norm, ssm_conv_w, ssm_conv_b, ssm_dt_bias, ssm_A_log, ssm_D, ssm_norm, gdn_conv_w,
                          gdn_dt_bias, gdn_A_log, gdn_norm, mlstm_i_bias, mlstm_f_bias, mlstm_norm)
    gvec, convw, convb, dvec, norms = small

    cos_p, sin_p = _rot_tables(jnp.arange(lp, dtype=jnp.int32))
    x = x_prompt.reshape(bp * lp, D_MODEL)
    st_p = []
    for l in range(depth):
        x = _ffn(x, norm_sandwich, wg, wu, wd, l, 0)
        pre, convraw = _pproj(x, norm_sandwich, w_in_p, cos_p, sin_p, small, l, nbatch=bp, seqlen=lp)
        outs = _mix_prompt(pre, small, l, nbatch=bp, seqlen=lp)
        x = _ffn(x, norm_sandwich, wg, wu, wd, l, 1, mixed=outs[0], w_out=w_out_b)
        st_p.append([convraw] + outs[1:])
    y_prompt = x.reshape(bp, lp, D_MODEL)
    conv_p = jnp.stack([s[0].reshape(bp, -1, CONV_CH)[:, -(CONV_W - 1):, :] for s in st_p])
    stack_p = lambda k: jnp.stack([s[k] for s in st_p])
    out_p = (stack_p(1), stack_p(2), conv_p[..., :SSM_CONV_CH], stack_p(3), conv_p[..., SSM_CONV_CH:],
             stack_p(4), stack_p(5), stack_p(6)[:, :, 0, :])

    lanes = lambda a: jnp.broadcast_to(a[..., None], a.shape + (LANES,))
    cos_s, sin_s = _rot_tables(PAST_LEN + jnp.arange(ls, dtype=jnp.int32))
    rot = lanes(jnp.stack([cos_s, sin_s]))
    lg = np.log(1.0 - np.exp2(-5.0 - np.arange(N_HEADS, dtype=np.float64)))
    gamma = jnp.asarray(np.concatenate([np.exp(lg), np.zeros(SUBLANES - N_HEADS)]), F32)
    colp = lanes(jnp.concatenate(
        [convw.reshape(depth, CONV_W * CONV_CH), convb[:, 0, :], gvec[:, 0, :4 * SUBLANES], gvec[:, 1, :4 * SUBLANES],
         jnp.broadcast_to(gamma, (depth, SUBLANES))], axis=1))
    cn = lanes(jnp.concatenate([norms.reshape(depth, 4 * GROUP_W), dvec[:, 0, :]], axis=1))
    hist = jnp.transpose(jnp.concatenate([state_ssm_conv, state_gdn_conv], axis=-1), (0, 2, 1, 3))
    mat = lambda s: jnp.transpose(s, (0, 2, 3, 4, 1)).reshape(depth, N_HEADS, HEAD_DIM * HEAD_DIM, bs)
    states_s = (mat(state_ret), mat(state_ssm), mat(state_gdn), mat(state_mlstm_C),
                jnp.transpose(state_mlstm_n, (0, 2, 3, 1)), jnp.transpose(state_mlstm_m, (0, 2, 1))[:, :, None, :])
    x = jnp.transpose(x_sample, (1, 0, 2)).reshape(ls * bs, D_MODEL)
    prev, convs = None, []
    for l in range(depth):
        x = _ffn(x, norm_sandwich, wg, wu, wd, l, 0)
        slab, post, convraw = _spre(x, norm_sandwich, w_in_p, hist, rot, colp, l)
        outs = _srec(slab, states_s, prev, l)
        prev = outs[1:]
        x = _spost(x, outs[0], post, cn, norm_sandwich, w_out_b, l)
        x = _ffn(x, norm_sandwich, wg, wu, wd, l, 1)
        convs.append(convraw)
    y_sample = jnp.transpose(x.reshape(ls, bs, D_MODEL), (1, 0, 2))
    conv_s = jnp.transpose(jnp.stack(convs), (0, 2, 1, 3))
    unmat = lambda s: jnp.transpose(s.reshape(depth, N_HEADS, HEAD_DIM, HEAD_DIM, bs), (0, 4, 1, 2, 3))
    out_s = (unmat(prev[0]), unmat(prev[1]), conv_s[..., :SSM_CONV_CH], unmat(prev[2]), conv_s[..., SSM_CONV_CH:],
             unmat(prev[3]), jnp.transpose(prev[4], (0, 3, 1, 2)), jnp.transpose(prev[5][:, :, 0, :], (0, 2, 1)))
    return (y_prompt, y_sample) + out_p + out_s
```

```python
import functools
import math

import numpy as np
import jax
import jax.numpy as jnp
from jax import lax
from jax.experimental import pallas as pl
from jax.experimental.pallas import tpu as pltpu

F32 = jnp.float32
BF16 = jnp.bfloat16

D_MODEL = 1024
N_HEADS = 4
HEAD_DIM = 64
GROUP_W = N_HEADS * HEAD_DIM
D_FF = 2816
CONV_W = 4
EPS = 1e-6
ROPE_BASE = 10000.0
PAST_LEN = 16384
PROMPT_CHUNK = 64

COL_RET = 0
COL_SSM_Z = 1024
COL_CONV = 1280
CONV_CH = 1280
SSM_CONV_CH = 512
COL_GDN_GATE = 2560
COL_ML = 2816
COL_GATES = 3840
IN_PAD = 3968
G_DT, G_GA, G_GB, G_MI, G_MF, G_DTV = 0, 4, 8, 12, 16, 20

SUBLANES = 8
LANES = 128
VMEM_LIMIT = 56 * 1024 * 1024

FF_CHUNK = 256
TOKEN_TILE = 512
PROMPT_TILE = 512


def _sigmoid(x):
    return 1.0 / (1.0 + jnp.exp(-x))


def _silu(x):
    return x * _sigmoid(x)


def _softplus(x):
    return jnp.maximum(x, 0.0) + jnp.log1p(jnp.exp(-jnp.abs(x)))


def _rms(x, g):
    return x * lax.rsqrt(jnp.mean(x * x, axis=-1, keepdims=True) + EPS) * g


def _dot(a, b):
    return jnp.dot(a.astype(BF16), b.astype(BF16), preferred_element_type=F32)


def _dot_nt(a, b):
    return lax.dot_general(a.astype(BF16), b.astype(BF16), (((1,), (1,)), ((), ())),
                           preferred_element_type=F32)


def _dot_tn(a, b):
    return lax.dot_general(a.astype(BF16), b.astype(BF16), (((0,), (0,)), ((), ())),
                           preferred_element_type=F32)


N_SANDWICH = 6


def _layer_block(tail, lead, ngrid=1, buffered=True):
    idx = tuple(lead) + (0,) * len(tail)
    imap = (lambda i: idx) if ngrid == 1 else (lambda i, j: idx)
    kw = dict(pipeline_mode=pl.Buffered(1)) if buffered else {}
    return pl.BlockSpec((None,) * len(lead) + tuple(tail), imap, **kw)


WIN_SEGMENTS = ((0, 0, 1792), (1792, 1796, 1024), (2816, 2828, 1024))
WIN_GATES = ((1792, 1796), (2820, 2828), (3852, 3860))


def _prep_win_body(w_ref, o_ref):
    l = pl.program_id(0)
    step = 2 * LANES
    for dst, src, n in WIN_SEGMENTS:
        for c in range(0, n, step):
            o_ref[:, dst + c:dst + c + step] = w_ref[src + c:src + c + step, l, :].T.astype(BF16)
    gates = jnp.concatenate([w_ref[a:b, l, :] for a, b in WIN_GATES]
                            + [jnp.zeros((LANES - 5 * N_HEADS, D_MODEL), F32)], axis=0)
    o_ref[:, COL_GATES:IN_PAD] = gates.T.astype(BF16)


def _prep_win(w_in):
    depth, _, in_dim = w_in.shape
    w_t = jnp.transpose(w_in, (2, 0, 1))
    return pl.pallas_call(
        _prep_win_body,
        out_shape=jax.ShapeDtypeStruct((depth, D_MODEL, IN_PAD), BF16),
        grid=(depth,),
        in_specs=[pl.BlockSpec((in_dim, depth, D_MODEL), lambda l: (0, 0, 0), pipeline_mode=pl.Buffered(1))],
        out_specs=pl.BlockSpec((None, D_MODEL, IN_PAD), lambda l: (l, 0, 0)),
        compiler_params=pltpu.CompilerParams(dimension_semantics=("arbitrary",), vmem_limit_bytes=VMEM_LIMIT),
        name="prep_win",
    )(w_t)


def _ffn_body(*refs, k, mixer_out):
    if mixer_out:
        x_ref, m_ref, wo_ref, g_ref, wg_ref, wu_ref, wd_ref, o_ref, a_ref = refs
        y = jnp.dot(m_ref[...], wo_ref[...], preferred_element_type=F32)
        x = x_ref[...] + _rms(y, g_ref[3:4, :])
    else:
        x_ref, g_ref, wg_ref, wu_ref, wd_ref, o_ref, a_ref = refs
        x = x_ref[...]
    h = _rms(x, g_ref[4 * k:4 * k + 1, :]).astype(BF16)
    for j in range(D_FF // FF_CHUNK):
        sl = slice(j * FF_CHUNK, (j + 1) * FF_CHUNK)
        gt = jnp.dot(h, wg_ref[:, sl].astype(BF16), preferred_element_type=F32)
        up = jnp.dot(h, wu_ref[:, sl].astype(BF16), preferred_element_type=F32)
        a_ref[:, sl] = (_silu(gt) * up).astype(BF16)
    y = jnp.dot(a_ref[...], wd_ref[...], preferred_element_type=F32)
    o_ref[...] = x + 0.5 * _rms(y, g_ref[4 * k + 1:4 * k + 2, :])


def _ffn(x, sandwich, wg, wu, wd, l, k, mixed=None, w_out=None):
    m = x.shape[0]
    tm = min(TOKEN_TILE, m)
    tile = pl.BlockSpec((tm, D_MODEL), lambda i: (i, 0))
    fused = mixed is not None
    in_specs = [tile] + ([tile, _layer_block((D_MODEL, D_MODEL), (l,))] if fused else [])
    in_specs += [_layer_block((N_SANDWICH, D_MODEL), (l,)),
                 _layer_block((D_MODEL, D_FF), (l, k)),
                 _layer_block((D_MODEL, D_FF), (l, k)),
                 _layer_block((D_FF, D_MODEL), (l, k))]
    args = (x,) + ((mixed, w_out) if fused else ()) + (sandwich, wg, wu, wd)
    return pl.pallas_call(
        functools.partial(_ffn_body, k=k, mixer_out=fused),
        out_shape=jax.ShapeDtypeStruct((m, D_MODEL), F32),
        grid=(m // tm,),
        in_specs=in_specs,
        out_specs=tile,
        scratch_shapes=[pltpu.VMEM((tm, D_FF), BF16)],
        compiler_params=pltpu.CompilerParams(dimension_semantics=("parallel",),
                                             vmem_limit_bytes=VMEM_LIMIT),
        name="ffn_out" if fused else "ffn",
    )(*args)


NEXP = 6
X_CSSM, X_DT, X_CGDN, X_BETA, X_B, X_IMB = range(NEXP)


def _split2(x):
    hi = x.astype(BF16)
    return hi, (x - hi.astype(F32)).astype(BF16)


def _dot_sel_r2(x, sel):
    hi, lo = _split2(x)
    return jnp.dot(lo, sel, preferred_element_type=F32) + jnp.dot(hi, sel, preferred_element_type=F32)


(PB_RQ, PB_RK, PB_RV, PB_RG, PB_Z, PB_SX, PB_SBC, PB_GQ, PB_GK, PB_GV, PB_GG, PB_MQ, PB_MK, PB_MV, PB_MO,
 PB_X0) = range(16)
PRE_COLS = (PB_X0 + NEXP) * GROUP_W


def _pproj_body(x_ref, g_ref, w_ref, cos_ref, sin_ref, maskbd_ref, gvec_ref, convw_ref,
                convb_ref, pre_ref, convraw_ref, e_ref, *, rows):
    @pl.when(pl.program_id(1) == 0)
    def _():
        e_ref[0:SUBLANES, :] = jnp.zeros((SUBLANES, CONV_CH), F32)

    h_in = _rms(x_ref[...], g_ref[2:3, :]).astype(BF16)
    proj = lambda a, n: jnp.dot(h_in, w_ref[:, a:a + n], preferred_element_type=F32)

    def put(b, v):
        pre_ref[:, b * GROUP_W:(b + 1) * GROUP_W] = v

    maskbd_b = maskbd_ref[...].astype(BF16)
    seg_sum = lambda x: _dot_sel_r2(x, maskbd_b)

    lane = lax.broadcasted_iota(jnp.int32, (1, LANES), 1)
    in_lanes = lambda lo: (lane >= lo) & (lane < lo + N_HEADS)
    graw = proj(COL_GATES, LANES)
    put(PB_MQ, proj(COL_ML, GROUP_W))
    put(PB_MK, proj(COL_ML + GROUP_W, GROUP_W) * HEAD_DIM ** -0.5)
    put(PB_MV, proj(COL_ML + 2 * GROUP_W, GROUP_W))
    put(PB_MO, proj(COL_ML + 3 * GROUP_W, GROUP_W))
    xb = graw + gvec_ref[0:1, :]
    sp = _softplus(xb)
    neg_a = -jnp.exp(gvec_ref[1:2, :])
    logf = -_softplus(-xb)
    pre = jnp.where(lane < 2 * N_HEADS, neg_a * sp, jnp.where(in_lanes(G_MF), logf, 0.0))
    row_in_chunk = lax.broadcasted_iota(jnp.int32, (rows, 1), 0) % PROMPT_CHUNK
    cum, step = pre, 1
    while step < PROMPT_CHUNK:
        cum = cum + jnp.where(row_in_chunk >= step, pltpu.roll(cum, step, axis=0), 0.0)
        step *= 2
    imb = xb - pltpu.roll(cum, LANES - (G_MF - G_MI), axis=1)
    dtv = pltpu.roll(sp, G_DTV - G_DT, axis=1)
    gates = jnp.where(in_lanes(G_GB), _sigmoid(graw),
                      jnp.where(in_lanes(G_MI), imb, jnp.where(in_lanes(G_DTV), dtv, cum)))

    lane_g = lax.broadcasted_iota(jnp.int32, (1, GROUP_W), 1)
    first_half = (lane_g % HEAD_DIM) < (HEAD_DIM // 2)
    head_of_lane = lane_g // HEAD_DIM

    def rotary(x):
        swapped = jnp.where(first_half, pltpu.roll(x, GROUP_W - HEAD_DIM // 2, axis=1),
                            pltpu.roll(x, HEAD_DIM // 2, axis=1))
        return x * cos_ref[...] + swapped * sin_ref[...]

    def conv_in(c):
        e_ref[SUBLANES:SUBLANES + rows, c * GROUP_W:(c + 1) * GROUP_W] = proj(COL_CONV + c * GROUP_W, GROUP_W)

    def conv_out(c):
        cs = slice(c * GROUP_W, (c + 1) * GROUP_W)
        acc = e_ref[SUBLANES:SUBLANES + rows, cs] * convw_ref[CONV_W - 1:CONV_W, cs]
        for s in range(1, CONV_W):
            acc = acc + e_ref[SUBLANES - s:SUBLANES - s + rows, cs] * convw_ref[CONV_W - 1 - s:CONV_W - s, cs]
        return _silu(acc + convb_ref[:, cs])

    def expand(j):
        lo = (G_DT, G_DTV, G_GA, G_GB, G_MF, G_MI)[j]
        out = jnp.broadcast_to(gates[:, lo:lo + 1], (rows, GROUP_W))
        for h in range(1, N_HEADS):
            out = jnp.where(head_of_lane == h, jnp.broadcast_to(gates[:, lo + h:lo + h + 1], (rows, GROUP_W)), out)
        put(PB_X0 + j, out)

    conv_in(0)
    conv_in(1)
    conv_in(2)
    expand(0)
    expand(1)
    conv_in(3)
    put(PB_RV, proj(COL_RET + 2 * GROUP_W, GROUP_W))
    put(PB_SX, conv_out(0))
    conv_in(4)
    put(PB_RG, proj(COL_RET + 3 * GROUP_W, GROUP_W))
    put(PB_SBC, conv_out(1))
    rq = proj(COL_RET, GROUP_W)
    put(PB_Z, proj(COL_SSM_Z, GROUP_W))
    gq = conv_out(2)
    put(PB_GQ, gq * lax.rsqrt(seg_sum(gq * gq) + EPS) * HEAD_DIM ** -0.5)
    rk = proj(COL_RET + GROUP_W, GROUP_W)
    put(PB_GG, proj(COL_GDN_GATE, GROUP_W))
    gk = conv_out(3)
    put(PB_GK, gk * lax.rsqrt(seg_sum(gk * gk) + EPS))
    put(PB_GV, conv_out(4))
    convraw_ref[...] = e_ref[rows:rows + SUBLANES, :]
    e_ref[0:SUBLANES, :] = e_ref[rows:rows + SUBLANES, :]
    expand(2)
    put(PB_RQ, rotary(rq))
    expand(3)
    put(PB_RK, rotary(rk) * HEAD_DIM ** -0.5)
    expand(4)
    expand(5)


def _pproj(x, sandwich, w, cos, sin, small, l, *, nbatch, seqlen):
    rows = min(TOKEN_TILE, seqlen)
    ntile = seqlen // rows
    g = np.arange(GROUP_W)
    maskbd = jnp.asarray(g[:, None] // HEAD_DIM == g[None, :] // HEAD_DIM, F32)
    gvec, convw, convb = small[:3]
    row_map = lambda b, t: (b * ntile + t, 0)
    in_specs = [pl.BlockSpec((rows, D_MODEL), row_map),
                _layer_block((N_SANDWICH, D_MODEL), (l,), ngrid=2),
                _layer_block((D_MODEL, IN_PAD), (l,), ngrid=2),
                pl.BlockSpec((rows, GROUP_W), lambda b, t: (t, 0)), pl.BlockSpec((rows, GROUP_W), lambda b, t: (t, 0)),
                pl.BlockSpec(maskbd.shape, lambda b, t: (0, 0))]
    in_specs += [_layer_block(a.shape[1:], (l,), ngrid=2, buffered=False) for a in (gvec, convw, convb)]
    m = x.shape[0]
    return pl.pallas_call(
        functools.partial(_pproj_body, rows=rows),
        out_shape=[jax.ShapeDtypeStruct((m, PRE_COLS), F32),
                   jax.ShapeDtypeStruct((nbatch * SUBLANES, CONV_CH), F32)],
        grid=(nbatch, ntile), in_specs=in_specs,
        out_specs=[pl.BlockSpec((rows, PRE_COLS), row_map), pl.BlockSpec((SUBLANES, CONV_CH), lambda b, t: (b, 0))],
        scratch_shapes=[pltpu.VMEM((rows + SUBLANES, CONV_CH), F32)],
        compiler_params=pltpu.CompilerParams(dimension_semantics=("parallel", "arbitrary"),
                                             vmem_limit_bytes=VMEM_LIMIT),
        name="pproj",
    )(x, sandwich, w, cos, sin, maskbd, gvec, convw, convb)


def _mixp_body(pre_ref, rinter_ref, rtail_ref, rintra_ref, maskbd_ref, maskg_ref, maskht_ref, dvec_ref, norms_ref,
               mixed_ref, oret_ref, ossm_ref, ogdn_ref, omc_ref, omn_ref, omm_ref,
               o_ref, sret_ref, sht_ref, sgdn_ref, smc_ref, vec_ref, *, rows, chunk, ret_carry):
    nchunks = rows // chunk
    t_id = pl.program_id(1)
    blk = lambda b, rs=slice(None): pre_ref[rs, b * GROUP_W:(b + 1) * GROUP_W]

    @pl.when(t_id == 0)
    def _():
        sret_ref[...] = jnp.zeros_like(sret_ref)
        sht_ref[...] = jnp.zeros_like(sht_ref)
        sgdn_ref[...] = jnp.zeros_like(sgdn_ref)
        smc_ref[...] = jnp.zeros_like(smc_ref)
        vec_ref[...] = jnp.zeros_like(vec_ref)

    maskbd = maskbd_ref[...]
    maskbd_b = maskbd.astype(BF16)
    maskg = maskg_ref[...]
    maskg_b = maskg.astype(BF16)

    def seg_sum(x):
        return _dot_sel_r2(x, maskbd_b)

    r_i = lax.broadcasted_iota(jnp.int32, (chunk, GROUP_W), 0)
    c_i = lax.broadcasted_iota(jnp.int32, (chunk, GROUP_W), 1) % HEAD_DIM
    trim = r_i >= c_i
    diag = r_i == c_i
    strict = r_i > c_i
    eye = diag.astype(F32)
    head_of_lane = lax.broadcasted_iota(jnp.int32, (1, GROUP_W), 1) // HEAD_DIM
    carry_row = jnp.zeros((1, GROUP_W), F32)
    for h in range(N_HEADS):
        carry_row = jnp.where(head_of_lane == h, ret_carry[h], carry_row)
    maskht = maskht_ref[...]

    def bd(x):
        xb16 = x.astype(BF16)
        return jnp.concatenate([xb16] * N_HEADS, axis=0) * maskbd_b

    def last_row(x):
        return x[chunk - 1:chunk, :]

    def row_form(x):
        return jnp.sum(jnp.where(diag, x, 0.0), axis=0, keepdims=True)

    def seg_max(x):
        out = None
        for h in range(N_HEADS):
            sel = head_of_lane == h
            m_h = jnp.max(jnp.where(sel, x, -jnp.inf), axis=-1, keepdims=True)
            out = jnp.where(sel, m_h, 0.0) if out is None else jnp.where(sel, m_h, out)
        return out

    chunks = [slice(c * chunk, (c + 1) * chunk) for c in range(nchunks)]

    g_gam, g_beta, g_ecum, g_k, g_n, g_p = [], [], [], [], [], []
    for rs in chunks:
        ce = blk(PB_X0 + X_CGDN, rs)
        gam = jnp.exp(jnp.where(trim, ce - row_form(ce), -jnp.inf))
        k_c = blk(PB_GK, rs)
        beta = blk(PB_X0 + X_BETA, rs)
        n0 = -(_dot_nt(k_c, bd(k_c)) * jnp.where(strict, gam, 0.0) * beta)
        g_gam.append(gam), g_beta.append(beta), g_ecum.append(jnp.exp(ce)), g_k.append(k_c)
        g_n.append(n0), g_p.append(eye + n0)
    for _ in range(max(int(math.ceil(math.log2(chunk))) - 1, 0)):
        for c in range(nchunks):
            g_n[c] = _dot(g_n[c], bd(g_n[c]))
        for c in range(nchunks):
            g_p[c] = g_p[c] + _dot(g_p[c], bd(g_n[c]))
    g_u, g_w = [], []
    for c, rs in enumerate(chunks):
        g_u.append(_dot(g_p[c], bd(blk(PB_GV, rs) * g_beta[c])))
        g_w.append(_dot(g_p[c], bd(g_k[c] * (g_beta[c] * g_ecum[c]))))

    s_ret, s_ht, s_gdn, s_mc = sret_ref[...], sht_ref[...], sgdn_ref[...], smc_ref[...]
    n_row, m_row = vec_ref[0:1, :], vec_ref[1:2, :]
    for c, rs in enumerate(chunks):
        rq, rk, rv = blk(PB_RQ, rs), blk(PB_RK, rs), blk(PB_RV, rs)
        ce_s = blk(PB_X0 + X_CSSM, rs)
        b_c = pre_ref[rs, PB_SBC * GROUP_W:PB_SBC * GROUP_W + 2 * HEAD_DIM]
        c_c = pre_ref[rs, PB_SBC * GROUP_W + 2 * HEAD_DIM:(PB_SBC + 1) * GROUP_W]
        xdt = blk(PB_SX, rs) * blk(PB_X0 + X_DT, rs)
        ce_g, gq = blk(PB_X0 + X_CGDN, rs), blk(PB_GQ, rs)
        mq, mk, mv = blk(PB_MQ, rs), blk(PB_MK, rs), blk(PB_MV, rs)
        b_e, imb_e = blk(PB_X0 + X_B, rs), blk(PB_X0 + X_IMB, rs)

        w_s = _dot(g_w[c], s_gdn)
        att = _dot_nt(rq, bd(rk))
        scores = _dot_nt(c_c, jnp.concatenate([b_c.astype(BF16)] * N_HEADS, axis=0) * maskg_b)
        qk_g = _dot_nt(gq, bd(g_k[c]))
        qk_m = _dot_nt(mq, bd(mk))
        o_ret = _dot(rq * rinter_ref[...], s_ret)
        o_ssm = _dot(c_c, s_ht)
        o_gdn = _dot(gq * g_ecum[c], s_gdn)
        o_ml = _dot(mq, s_mc)
        qn = seg_sum(mq * n_row)

        v_new = g_u[c] - w_s
        att = att * rintra_ref[...]
        ce_s_last, ce_g_last = last_row(ce_s), last_row(ce_g)
        scores = scores * jnp.exp(jnp.where(trim, ce_s - row_form(ce_s), -jnp.inf))
        qk_g = qk_g * g_gam[c]
        dmat = jnp.where(trim, b_e + row_form(imb_e), -jnp.inf)
        m_t = jnp.maximum(b_e + m_row, seg_max(dmat))
        inter = jnp.exp(b_e + m_row - m_t)
        qk_m = qk_m * jnp.exp(dmat - m_t)
        m_new, b_last = last_row(m_t), last_row(b_e)
        kw = mk * jnp.exp(b_last + imb_e - m_new)
        cs = jnp.exp(b_last + m_row - m_new)

        o_gdn = o_gdn + _dot(qk_g, bd(v_new))
        d_gdn = _dot_tn(g_k[c] * jnp.exp(ce_g_last - ce_g), v_new)
        o_ret = o_ret + _dot(att, bd(rv))
        o_ssm = o_ssm * jnp.exp(ce_s) + _dot(scores, bd(xdt))
        num = _dot(qk_m, bd(mv)) + inter * o_ml
        den = seg_sum(qk_m) + inter * qn
        d_ret = _dot_tn(rk * rtail_ref[...], rv)
        d_ht = _dot_tn(b_c, xdt * jnp.exp(ce_s_last - ce_s))
        d_mc = _dot_tn(kw, mv)

        s_gdn = s_gdn * jnp.exp(ce_g_last) + d_gdn * maskbd
        o_ml = num / jnp.maximum(jnp.abs(den), jnp.exp(-m_t))
        o_ref[rs, :] = jnp.concatenate([o_ret, o_ssm, o_gdn, o_ml], axis=1)
        s_ret = s_ret * carry_row + d_ret * maskbd
        s_ht = s_ht * jnp.exp(ce_s_last) + d_ht * maskht
        s_mc = s_mc * cs + d_mc * maskbd
        n_row = n_row * cs + jnp.sum(kw, axis=0, keepdims=True)
        m_row = m_new

    sret_ref[...] = s_ret
    sht_ref[...] = s_ht
    sgdn_ref[...] = s_gdn
    smc_ref[...] = s_mc
    vec_ref[0:1, :] = n_row
    vec_ref[1:2, :] = m_row

    def head_norm(x, g):
        return x * lax.rsqrt(seg_sum(x * x) * (1.0 / HEAD_DIM) + EPS) * g

    out_ret = head_norm(o_ref[:, 0:GROUP_W], norms_ref[0:1, :]) * _silu(blk(PB_RG))
    mixed_ref[:, 0:GROUP_W] = out_ret.astype(BF16)
    y = o_ref[:, GROUP_W:2 * GROUP_W] + dvec_ref[...] * blk(PB_SX)
    out_ssm = _rms(y * _silu(blk(PB_Z)), norms_ref[1:2, :])
    mixed_ref[:, GROUP_W:2 * GROUP_W] = out_ssm.astype(BF16)
    out_gdn = head_norm(o_ref[:, 2 * GROUP_W:3 * GROUP_W], norms_ref[2:3, :]) * _silu(blk(PB_GG))
    mixed_ref[:, 2 * GROUP_W:3 * GROUP_W] = out_gdn.astype(BF16)
    out_ml = _sigmoid(blk(PB_MO)) * head_norm(o_ref[:, 3 * GROUP_W:4 * GROUP_W], norms_ref[3:4, :])
    mixed_ref[:, 3 * GROUP_W:4 * GROUP_W] = out_ml.astype(BF16)

    @pl.when(t_id == pl.num_programs(1) - 1)
    def _():
        for h in range(N_HEADS):
            hs = slice(h * HEAD_DIM, (h + 1) * HEAD_DIM)
            grp = h // (N_HEADS // 2)
            oret_ref[0, h] = s_ret[hs, hs]
            ogdn_ref[0, h] = s_gdn[hs, hs]
            omc_ref[0, h] = s_mc[hs, hs]
            ossm_ref[0, h] = s_ht[grp * HEAD_DIM:(grp + 1) * HEAD_DIM, hs]
        omn_ref[0] = jnp.concatenate([n_row[:, h * HEAD_DIM:(h + 1) * HEAD_DIM] for h in range(N_HEADS)], axis=0)
        omm_ref[0] = jnp.concatenate([m_row[:, h * HEAD_DIM:h * HEAD_DIM + 1] for h in range(N_HEADS)], axis=1)


def _prompt_consts(rows, chunk):
    lg = np.log(1.0 - np.exp2(-5.0 - np.arange(N_HEADS, dtype=np.float64)))
    idx = np.arange(chunk, dtype=np.float64)
    rel = idx[:, None] - idx[None, :]
    intra = np.where(rel[None] >= 0, np.exp(np.maximum(rel[None], 0.0) * lg[:, None, None]), 0.0)
    intra = np.concatenate(list(intra), axis=1)
    lanes = lambda t: np.repeat(t.T, HEAD_DIM, axis=1)
    inter = lanes(np.exp((idx[None, :] + 1.0) * lg[:, None]))
    tail = lanes(np.exp((chunk - 1.0 - idx[None, :]) * lg[:, None]))
    carry = tuple(float(c) for c in np.exp(chunk * lg))
    g = np.arange(GROUP_W)
    maskbd = (g[:, None] // HEAD_DIM == g[None, :] // HEAD_DIM)
    n = np.arange(2 * HEAD_DIM)
    maskg = (g[:, None] // (2 * HEAD_DIM) == n[None, :] // HEAD_DIM)
    f = lambda a, dt=F32: jnp.asarray(np.asarray(a, np.float32), dt)
    return (f(inter), f(tail), f(intra), f(maskbd), f(maskg), f(maskg.T)), carry


def _mix_prompt(pre, small, l, *, nbatch, seqlen):
    rows, chunk = min(PROMPT_TILE, seqlen), PROMPT_CHUNK
    ntile = seqlen // rows
    consts, carry = _prompt_consts(rows, chunk)
    dvec, norms = small[3:]
    m = pre.shape[0]
    row_map = lambda b, t: (b * ntile + t, 0)
    full = lambda a: pl.BlockSpec(a.shape, lambda b, t: (0, 0))
    in_specs = [pl.BlockSpec((rows, PRE_COLS), row_map)] + [full(a) for a in consts]
    in_specs += [_layer_block(a.shape[1:], (l,), ngrid=2, buffered=False) for a in (dvec, norms)]
    st4 = pl.BlockSpec((1, N_HEADS, HEAD_DIM, HEAD_DIM), lambda b, t: (b, 0, 0, 0))
    out_shape = [jax.ShapeDtypeStruct((m, D_MODEL), BF16)]
    out_shape += [jax.ShapeDtypeStruct((nbatch, N_HEADS, HEAD_DIM, HEAD_DIM), F32) for _ in range(4)]
    out_shape += [jax.ShapeDtypeStruct((nbatch, N_HEADS, HEAD_DIM), F32),
                  jax.ShapeDtypeStruct((nbatch, 1, N_HEADS), F32)]
    out_specs = [pl.BlockSpec((rows, D_MODEL), row_map), st4, st4, st4, st4,
                 pl.BlockSpec((1, N_HEADS, HEAD_DIM), lambda b, t: (b, 0, 0)),
                 pl.BlockSpec((1, 1, N_HEADS), lambda b, t: (b, 0, 0))]
    scratch = [pltpu.VMEM((rows, D_MODEL), F32),
               pltpu.VMEM((GROUP_W, GROUP_W), F32), pltpu.VMEM((2 * HEAD_DIM, GROUP_W), F32),
               pltpu.VMEM((GROUP_W, GROUP_W), F32), pltpu.VMEM((GROUP_W, GROUP_W), F32),
               pltpu.VMEM((SUBLANES, GROUP_W), F32)]
    body = functools.partial(_mixp_body, rows=rows, chunk=chunk, ret_carry=carry)
    outs = pl.pallas_call(
        body, out_shape=out_shape, grid=(nbatch, ntile), in_specs=in_specs, out_specs=out_specs,
        scratch_shapes=scratch,
        compiler_params=pltpu.CompilerParams(dimension_semantics=("parallel", "arbitrary"),
                                             vmem_limit_bytes=VMEM_LIMIT),
        name="mix_prompt",
    )(pre, *consts, dvec, norms)
    outs = list(outs)
    outs[2] = jnp.swapaxes(outs[2], -1, -2)
    return outs


S_RQ, S_RK, S_RV, S_XDT, S_B, S_C, S_GQ, S_GK, S_GV, S_MQ, S_MK, S_MV, S_GATE = (
    HEAD_DIM * i for i in range(13))
SLAB_ROWS = S_GATE + SUBLANES
GT_SSM_A, GT_GDN_A, GT_BETA, GT_LOGI, GT_LOGF, GT_GAMMA = range(6)
P_RG, P_Z, P_XS, P_GG, P_MO = (GROUP_W * i for i in range(5))
POST_ROWS = 5 * GROUP_W
C_CONVW, C_CONVB = 0, CONV_W * CONV_CH
C_BIAS = C_CONVB + CONV_CH
C_ALOG = C_BIAS + 4 * SUBLANES
C_GAMMA = C_ALOG + 4 * SUBLANES
C_ROWS = C_GAMMA + SUBLANES
N_ROWS = 5 * GROUP_W
N_TOK = 4


def _spre_body(x_ref, g_ref, w_ref, hist_ref, rot_ref, colp_ref, slab_ref, post_ref, convraw_ref):
    nb = LANES
    h_in = _rms(x_ref[...], g_ref[2:3, :]).astype(BF16)
    proj = jnp.dot(h_in, w_ref[...], preferred_element_type=F32)
    for j in range(CONV_W - 1):
        convraw_ref[j] = proj[(j + 1) * nb:(j + 2) * nb, COL_CONV:COL_CONV + CONV_CH]
    pt = [proj[t * nb:(t + 1) * nb, :].T for t in range(N_TOK)]
    src = [hist_ref[j].T for j in range(CONV_W - 1)] + [p[COL_CONV:COL_CONV + CONV_CH] for p in pt]
    bias = colp_ref[C_BIAS:C_BIAS + 4 * SUBLANES, :]
    neg_a = -jnp.exp(colp_ref[C_ALOG:C_ALOG + 4 * SUBLANES, :])
    gamma = colp_ref[C_GAMMA:C_GAMMA + SUBLANES, :]
    zero_row = jnp.zeros((1, LANES), F32)

    def rotary(x, t):
        swapped = jnp.concatenate(
            [x[h * HEAD_DIM + off:h * HEAD_DIM + off + HEAD_DIM // 2]
             for h in range(N_HEADS) for off in (HEAD_DIM // 2, 0)], axis=0)
        return x * rot_ref[0, t] + swapped * rot_ref[1, t]

    for t in range(N_TOK):
        p = pt[t]
        g = p[COL_GATES:COL_GATES + 4 * SUBLANES]
        xb = g + bias
        sp = _softplus(xb)
        decay = jnp.exp(neg_a * sp)
        beta = _sigmoid(g)
        logf = -_softplus(-xb)
        conv = src[t] * colp_ref[C_CONVW:C_CONVW + CONV_CH, :]
        for j in range(1, CONV_W):
            conv = conv + src[t + j] * colp_ref[C_CONVW + j * CONV_CH:C_CONVW + (j + 1) * CONV_CH, :]
        cv = _silu(conv + colp_ref[C_CONVB:C_CONVB + CONV_CH, :])
        rq = rotary(p[COL_RET:COL_RET + GROUP_W], t)
        rk = rotary(p[COL_RET + GROUP_W:COL_RET + 2 * GROUP_W], t) * HEAD_DIM ** -0.5
        post_ref[t, P_RG:P_RG + GROUP_W, :] = p[COL_RET + 3 * GROUP_W:COL_RET + 4 * GROUP_W]
        post_ref[t, P_Z:P_Z + GROUP_W, :] = p[COL_SSM_Z:COL_SSM_Z + GROUP_W]
        post_ref[t, P_XS:P_XS + GROUP_W, :] = cv[0:GROUP_W]
        post_ref[t, P_GG:P_GG + GROUP_W, :] = p[COL_GDN_GATE:COL_GDN_GATE + GROUP_W]
        post_ref[t, P_MO:P_MO + GROUP_W, :] = p[COL_ML + 3 * GROUP_W:COL_ML + 4 * GROUP_W]
        for h in range(N_HEADS):
            hs = slice(h * HEAD_DIM, (h + 1) * HEAD_DIM)
            grp = h // (N_HEADS // 2)
            def put(r, v, h=h, t=t):
                slab_ref[h, t, r:r + v.shape[0], :] = v

            l2 = lambda v: v * lax.rsqrt(jnp.sum(v * v, axis=0, keepdims=True) + EPS)
            put(S_RQ, rq[hs])
            put(S_RK, rk[hs])
            put(S_RV, p[COL_RET + 2 * GROUP_W + h * HEAD_DIM:COL_RET + 2 * GROUP_W + (h + 1) * HEAD_DIM])
            put(S_XDT, cv[hs] * sp[G_DT + h:G_DT + h + 1])
            put(S_B, cv[GROUP_W + grp * HEAD_DIM:GROUP_W + (grp + 1) * HEAD_DIM])
            put(S_C, cv[GROUP_W + 2 * HEAD_DIM + grp * HEAD_DIM:GROUP_W + 2 * HEAD_DIM + (grp + 1) * HEAD_DIM])
            put(S_GQ, l2(cv[SSM_CONV_CH + h * HEAD_DIM:SSM_CONV_CH + (h + 1) * HEAD_DIM]) * HEAD_DIM ** -0.5)
            put(S_GK, l2(cv[SSM_CONV_CH + GROUP_W + h * HEAD_DIM:SSM_CONV_CH + GROUP_W + (h + 1) * HEAD_DIM]))
            put(S_GV, cv[SSM_CONV_CH + 2 * GROUP_W + h * HEAD_DIM:SSM_CONV_CH + 2 * GROUP_W + (h + 1) * HEAD_DIM])
            put(S_MQ, p[COL_ML + h * HEAD_DIM:COL_ML + (h + 1) * HEAD_DIM])
            put(S_MK, p[COL_ML + GROUP_W + h * HEAD_DIM:COL_ML + GROUP_W + (h + 1) * HEAD_DIM] * HEAD_DIM ** -0.5)
            put(S_MV, p[COL_ML + 2 * GROUP_W + h * HEAD_DIM:COL_ML + 2 * GROUP_W + (h + 1) * HEAD_DIM])
            row = lambda a, j: a[j + h:j + h + 1]
            put(S_GATE, jnp.concatenate([row(decay, G_DT), row(decay, G_GA), row(beta, G_GB), row(xb, G_MI),
                                         row(logf, G_MF), gamma[h:h + 1], zero_row, zero_row], axis=0))


def _spre(x, sandwich, w, hist, rot, colp, l):
    m = x.shape[0]
    return pl.pallas_call(
        _spre_body,
        out_shape=[jax.ShapeDtypeStruct((N_HEADS, N_TOK, SLAB_ROWS, LANES), F32),
                   jax.ShapeDtypeStruct((N_TOK, POST_ROWS, LANES), F32),
                   jax.ShapeDtypeStruct((CONV_W - 1, LANES, CONV_CH), F32)],
        grid=(1,),
        in_specs=[pl.BlockSpec((m, D_MODEL), lambda i: (0, 0)),
                  _layer_block((N_SANDWICH, D_MODEL), (l,), buffered=False),
                  _layer_block((D_MODEL, IN_PAD), (l,), buffered=False),
                  _layer_block((CONV_W - 1, LANES, CONV_CH), (l,), buffered=False),
                  pl.BlockSpec(rot.shape, lambda i: (0, 0, 0, 0)),
                  _layer_block((C_ROWS, LANES), (l,), buffered=False)],
        out_specs=[pl.BlockSpec((N_HEADS, N_TOK, SLAB_ROWS, LANES), lambda i: (0, 0, 0, 0)),
                   pl.BlockSpec((N_TOK, POST_ROWS, LANES), lambda i: (0, 0, 0)),
                   pl.BlockSpec((CONV_W - 1, LANES, CONV_CH), lambda i: (0, 0, 0))],
        compiler_params=pltpu.CompilerParams(dimension_semantics=("arbitrary",), vmem_limit_bytes=VMEM_LIMIT),
        name="spre",
    )(x, sandwich, w, hist, rot, colp)


def _srec_body(*refs, aliased):
    ins, outs = refs[:7], refs[7 + (6 if aliased else 0):]
    j = pl.program_id(1)

    @pl.when(j == 0)
    def _():
        _srec_compute(*ins, *outs)

    if not aliased:
        @pl.when(j > 0)
        def _():
            for dst in outs[1:7]:
                dst[...] = jnp.zeros_like(dst)


def _srec_compute(slab_ref, sret_in, sssm_in, sgdn_in, smc_in, smn_in, smm_in,
                  ot_ref, sret_out, sssm_out, sgdn_out, smc_out, smn_out, smm_out, y8_ref):
    tile = lambda t, r: slab_ref[t, r:r + HEAD_DIM, :]
    gate = lambda t, j: slab_ref[t, S_GATE + j:S_GATE + j + 1, :]
    drow = lambda t, r, d: slab_ref[t, pl.ds(r + d, 1), :]
    zeros = jnp.zeros((HEAD_DIM, LANES), F32)
    sweep = lambda body, init: lax.fori_loop(0, HEAD_DIM, body, init, unroll=8)
    blk = lambda d: pl.ds(pl.multiple_of(d * HEAD_DIM, HEAD_DIM), HEAD_DIM)

    gamma = gate(0, GT_GAMMA)
    for t in range(N_TOK):
        src = sret_in if t == 0 else sret_out
        v = tile(t, S_RV)

        def body(d, acc, t=t, src=src, v=v):
            s = src[blk(d), :] * gamma + drow(t, S_RK, d) * v
            sret_out[blk(d), :] = s
            return acc + drow(t, S_RQ, d) * s
        ot_ref[t, 0:HEAD_DIM, :] = sweep(body, zeros)

    for t in range(N_TOK):
        src = sssm_in if t == 0 else sssm_out
        a, b_t, c_t = gate(t, GT_SSM_A), tile(t, S_B), tile(t, S_C)

        def body(p, carry, t=t, src=src, a=a, b_t=b_t, c_t=c_t):
            hp = src[blk(p), :] * a + drow(t, S_XDT, p) * b_t
            sssm_out[blk(p), :] = hp
            part = jnp.sum((hp * c_t).reshape(HEAD_DIM // SUBLANES, SUBLANES, LANES), axis=0)
            y8_ref[pl.ds(pl.multiple_of(p * SUBLANES, SUBLANES), SUBLANES), :] = part
            return carry
        sweep(body, 0)
        ot_ref[t, HEAD_DIM:2 * HEAD_DIM, :] = jnp.sum(y8_ref[...].reshape(HEAD_DIM, SUBLANES, LANES), axis=1)

    ks = sweep(lambda d, acc: acc + drow(0, S_GK, d) * sgdn_in[blk(d), :], zeros)
    for t in range(N_TOK):
        src = sgdn_in if t == 0 else sgdn_out
        a = gate(t, GT_GDN_A)
        u = gate(t, GT_BETA) * (tile(t, S_GV) - a * ks)
        nxt = t + 1 < N_TOK

        def body(d, acc, t=t, src=src, a=a, u=u, nxt=nxt):
            s = src[blk(d), :] * a + drow(t, S_GK, d) * u
            sgdn_out[blk(d), :] = s
            o = acc[0] + drow(t, S_GQ, d) * s
            return (o, acc[1] + drow(t + 1, S_GK, d) * s) if nxt else (o, acc[1])
        o, ks = sweep(body, (zeros, zeros))
        ot_ref[t, 2 * HEAD_DIM:3 * HEAD_DIM, :] = o

    m_row, n_tile = smm_in[...], smn_in[...]
    for t in range(N_TOK):
        src = smc_in if t == 0 else smc_out
        logi, logf = gate(t, GT_LOGI), gate(t, GT_LOGF)
        m_new = jnp.maximum(logf + m_row, logi)
        fs, ins = jnp.exp(logf + m_row - m_new), jnp.exp(logi - m_new)
        n_tile = fs * n_tile + ins * tile(t, S_MK)
        qn = jnp.sum(tile(t, S_MQ) * n_tile, axis=0, keepdims=True)
        v = tile(t, S_MV) * ins

        def body(d, acc, t=t, src=src, fs=fs, v=v):
            c = src[blk(d), :] * fs + drow(t, S_MK, d) * v
            smc_out[blk(d), :] = c
            return acc + drow(t, S_MQ, d) * c
        num = sweep(body, zeros)
        ot_ref[t, 3 * HEAD_DIM:4 * HEAD_DIM, :] = num / jnp.maximum(jnp.abs(qn), jnp.exp(-m_new))
        m_row = m_new
    smn_out[...] = n_tile
    smm_out[...] = m_row


def _srec(slab, states, prev, l):
    depth = states[0].shape[0]
    rows = HEAD_DIM * HEAD_DIM
    st_shapes = [(rows, LANES)] * 4 + [(HEAD_DIM, LANES), (1, LANES)]
    spec = lambda tail, lead: pl.BlockSpec((None, None) + tail, lambda h, j: (lead(j), h, 0, 0))
    slab_spec = lambda r: pl.BlockSpec((None, N_TOK, r, LANES), lambda h, j: (h, 0, 0, 0))
    in_specs = [slab_spec(SLAB_ROWS)] + [spec(s, lambda j: l) for s in st_shapes]
    args = [slab, *states]
    aliases = {}
    if l:
        in_specs += [pl.BlockSpec(memory_space=pl.ANY)] * 6
        args += list(prev)
        aliases = {7 + i: 1 + i for i in range(6)}
    out_shape = [jax.ShapeDtypeStruct((N_HEADS, N_TOK, GROUP_W, LANES), F32)]
    out_shape += [jax.ShapeDtypeStruct((depth, N_HEADS) + s, F32) for s in st_shapes]
    out_specs = [slab_spec(GROUP_W)] + [spec(s, lambda j: l + j) for s in st_shapes]
    return pl.pallas_call(
        functools.partial(_srec_body, aliased=bool(l)),
        out_shape=out_shape, grid=(N_HEADS, 1 if l else depth), in_specs=in_specs, out_specs=out_specs,
        input_output_aliases=aliases,
        scratch_shapes=[pltpu.VMEM((HEAD_DIM * SUBLANES, LANES), F32)],
        compiler_params=pltpu.CompilerParams(dimension_semantics=("arbitrary", "arbitrary"),
                                             vmem_limit_bytes=VMEM_LIMIT),
        name="srec",
    )(*args)


def _spost_body(x_ref, ot_ref, post_ref, cn_ref, g_ref, w_ref, o_ref):
    nb = LANES

    def head_norm(x):
        out = []
        for h in range(N_HEADS):
            b = x[h * HEAD_DIM:(h + 1) * HEAD_DIM]
            out.append(b * lax.rsqrt(jnp.mean(b * b, axis=0, keepdims=True) + EPS))
        return jnp.concatenate(out, axis=0)

    gain = lambda i: cn_ref[i * GROUP_W:(i + 1) * GROUP_W, :]
    for t in range(N_TOK):
        mixer = lambda i: jnp.concatenate(
            [ot_ref[h, t, i * HEAD_DIM:(i + 1) * HEAD_DIM, :] for h in range(N_HEADS)], axis=0)
        pin = lambda r: post_ref[t, r:r + GROUP_W, :]
        out_ret = head_norm(mixer(0)) * gain(0) * _silu(pin(P_RG))
        y = (mixer(1) + gain(4) * pin(P_XS)) * _silu(pin(P_Z))
        out_ssm = y * lax.rsqrt(jnp.mean(y * y, axis=0, keepdims=True) + EPS) * gain(1)
        out_gdn = head_norm(mixer(2)) * gain(2) * _silu(pin(P_GG))
        out_ml = _sigmoid(pin(P_MO)) * (head_norm(mixer(3)) * gain(3))
        mixed = jnp.concatenate([out_ret, out_ssm, out_gdn, out_ml], axis=0).T.astype(BF16)
        y = jnp.dot(mixed, w_ref[...], preferred_element_type=F32)
        rs = slice(t * nb, (t + 1) * nb)
        o_ref[rs, :] = x_ref[rs, :] + _rms(y, g_ref[3:4, :])


def _spost(x, ot, post, cn, sandwich, w, l):
    m = x.shape[0]
    return pl.pallas_call(
        _spost_body,
        out_shape=jax.ShapeDtypeStruct((m, D_MODEL), F32),
        grid=(1,),
        in_specs=[pl.BlockSpec((m, D_MODEL), lambda i: (0, 0)),
                  pl.BlockSpec(ot.shape, lambda i: (0, 0, 0, 0)),
                  pl.BlockSpec(post.shape, lambda i: (0, 0, 0)),
                  _layer_block((N_ROWS, LANES), (l,), buffered=False),
                  _layer_block((N_SANDWICH, D_MODEL), (l,), buffered=False),
                  _layer_block((D_MODEL, D_MODEL), (l,), buffered=False)],
        out_specs=pl.BlockSpec((m, D_MODEL), lambda i: (0, 0)),
        compiler_params=pltpu.CompilerParams(dimension_semantics=("arbitrary",), vmem_limit_bytes=VMEM_LIMIT),
        name="spost",
    )(x, ot, post, cn, sandwich, w)


def _rot_tables(pos):
    half = HEAD_DIM // 2
    inv = ROPE_BASE ** (-jnp.arange(half, dtype=F32) / half)
    ang = pos.astype(F32)[:, None] * inv[None, :]
    cos, sin = jnp.cos(ang), jnp.sin(ang)
    cos_full = jnp.tile(jnp.concatenate([cos, cos], axis=-1), (1, N_HEADS))
    sin_signed = jnp.tile(jnp.concatenate([-sin, sin], axis=-1), (1, N_HEADS))
    return cos_full, sin_signed


def _small_params(ret_norm, ssm_conv_w, ssm_conv_b, ssm_dt_bias, ssm_A_log, ssm_D, ssm_norm, gdn_conv_w,
                  gdn_dt_bias, gdn_A_log, gdn_norm, mlstm_i_bias, mlstm_f_bias, mlstm_norm):
    depth = ret_norm.shape[0]
    z = lambda n: jnp.zeros((depth, n), F32)
    pad_lanes = z(LANES - 5 * N_HEADS)
    bias = jnp.concatenate([ssm_dt_bias, gdn_dt_bias, z(N_HEADS), mlstm_i_bias, mlstm_f_bias, pad_lanes], axis=1)
    alog = jnp.concatenate([ssm_A_log, gdn_A_log, z(3 * N_HEADS), pad_lanes], axis=1)
    gvec = jnp.stack([bias, alog], axis=1)
    convw = jnp.concatenate([ssm_conv_w, gdn_conv_w], axis=2)
    convb = jnp.concatenate([ssm_conv_b, z(CONV_CH - SSM_CONV_CH)], axis=1)[:, None, :]
    dvec = jnp.repeat(ssm_D, HEAD_DIM, axis=1)[:, None, :]
    norms = jnp.stack([ret_norm, ssm_norm, gdn_norm, mlstm_norm], axis=1)
    return gvec, convw, convb, dvec, norms


def kernel(x_prompt, x_sample, state_ret, state_ssm, state_ssm_conv, state_gdn, state_gdn_conv,
           state_mlstm_C, state_mlstm_n, state_mlstm_m, w_in, w_out, norm_sandwich, ffn_w_gate, ffn_w_up,
           ffn_w_down, ret_norm, ssm_conv_w, ssm_conv_b, ssm_dt_bias, ssm_A_log, ssm_D, ssm_norm, gdn_conv_w,
           gdn_dt_bias, gdn_A_log, gdn_norm, mlstm_i_bias, mlstm_f_bias, mlstm_norm):
    depth = w_in.shape[0]
    bp, lp, _ = x_prompt.shape
    bs, ls, _ = x_sample.shape
    assert (bs, ls) == (LANES, N_TOK)
    w_in_p = _prep_win(w_in)
    w_out_b = w_out.astype(BF16)
    wg, wu, wd = ffn_w_gate, ffn_w_up, ffn_w_down.astype(BF16)
    small = _small_params(ret_norm, ssm_conv_w, ssm_conv_b, ssm_dt_bias, ssm_A_log, ssm_D, ssm_norm, gdn_conv_w,
                          gdn_dt_bias, gdn_A_log, gdn_norm, mlstm_i_bias, mlstm_f_bias, mlstm_norm)
    gvec, convw, convb, dvec, norms = small

    cos_p, sin_p = _rot_tables(jnp.arange(lp, dtype=jnp.int32))
    x = x_prompt.reshape(bp * lp, D_MODEL)
    st_p = []
    for l in range(depth):
        x = _ffn(x, norm_sandwich, wg, wu, wd, l, 0)
        pre, convraw = _pproj(x, norm_sandwich, w_in_p, cos_p, sin_p, small, l, nbatch=bp, seqlen=lp)
        outs = _mix_prompt(pre, small, l, nbatch=bp, seqlen=lp)
        x = _ffn(x, norm_sandwich, wg, wu, wd, l, 1, mixed=outs[0], w_out=w_out_b)
        st_p.append([convraw] + outs[1:])
    y_prompt = x.reshape(bp, lp, D_MODEL)
    conv_p = jnp.stack([s[0].reshape(bp, -1, CONV_CH)[:, -(CONV_W - 1):, :] for s in st_p])
    stack_p = lambda k: jnp.stack([s[k] for s in st_p])
    out_p = (stack_p(1), stack_p(2), conv_p[..., :SSM_CONV_CH], stack_p(3), conv_p[..., SSM_CONV_CH:],
             stack_p(4), stack_p(5), stack_p(6)[:, :, 0, :])

    lanes = lambda a: jnp.broadcast_to(a[..., None], a.shape + (LANES,))
    cos_s, sin_s = _rot_tables(PAST_LEN + jnp.arange(ls, dtype=jnp.int32))
    rot = lanes(jnp.stack([cos_s, sin_s]))
    lg = np.log(1.0 - np.exp2(-5.0 - np.arange(N_HEADS, dtype=np.float64)))
    gamma = jnp.asarray(np.concatenate([np.exp(lg), np.zeros(SUBLANES - N_HEADS)]), F32)
    colp = lanes(jnp.concatenate(
        [convw.reshape(depth, CONV_W * CONV_CH), convb[:, 0, :], gvec[:, 0, :4 * SUBLANES], gvec[:, 1, :4 * SUBLANES],
         jnp.broadcast_to(gamma, (depth, SUBLANES))], axis=1))
    cn = lanes(jnp.concatenate([norms.reshape(depth, 4 * GROUP_W), dvec[:, 0, :]], axis=1))
    hist = jnp.transpose(jnp.concatenate([state_ssm_conv, state_gdn_conv], axis=-1), (0, 2, 1, 3))
    mat = lambda s: jnp.transpose(s, (0, 2, 3, 4, 1)).reshape(depth, N_HEADS, HEAD_DIM * HEAD_DIM, bs)
    states_s = (mat(state_ret), mat(state_ssm), mat(state_gdn), mat(state_mlstm_C),
                jnp.transpose(state_mlstm_n, (0, 2, 3, 1)), jnp.transpose(state_mlstm_m, (0, 2, 1))[:, :, None, :])
    x = jnp.transpose(x_sample, (1, 0, 2)).reshape(ls * bs, D_MODEL)
    prev, convs = None, []
    for l in range(depth):
        x = _ffn(x, norm_sandwich, wg, wu, wd, l, 0)
        slab, post, convraw = _spre(x, norm_sandwich, w_in_p, hist, rot, colp, l)
        outs = _srec(slab, states_s, prev, l)
        prev = outs[1:]
        x = _spost(x, outs[0], post, cn, norm_sandwich, w_out_b, l)
        x = _ffn(x, norm_sandwich, wg, wu, wd, l, 1)
        convs.append(convraw)
    y_sample = jnp.transpose(x.reshape(ls, bs, D_MODEL), (1, 0, 2))
    conv_s = jnp.transpose(jnp.stack(convs), (0, 2, 1, 3))
    unmat = lambda s: jnp.transpose(s.reshape(depth, N_HEADS, HEAD_DIM, HEAD_DIM, bs), (0, 4, 1, 2, 3))
    out_s = (unmat(prev[0]), unmat(prev[1]), conv_s[..., :SSM_CONV_CH], unmat(prev[2]), conv_s[..., SSM_CONV_CH:],
             unmat(prev[3]), jnp.transpose(prev[4], (0, 3, 1, 2)), jnp.transpose(prev[5][:, :, 0, :], (0, 2, 1)))
    return (y_prompt, y_sample) + out_p + out_s
```

```python
import functools
import math

import numpy as np
import jax
import jax.numpy as jnp
from jax import lax
from jax.experimental import pallas as pl
from jax.experimental.pallas import tpu as pltpu

F32 = jnp.float32
BF16 = jnp.bfloat16

D_MODEL = 1024
N_HEADS = 4
HEAD_DIM = 64
GROUP_W = N_HEADS * HEAD_DIM
D_FF = 2816
CONV_W = 4
EPS = 1e-6
ROPE_BASE = 10000.0
PAST_LEN = 16384
PROMPT_CHUNK = 64

COL_RET = 0
COL_SSM_Z = 1024
COL_CONV = 1280
CONV_CH = 1280
SSM_CONV_CH = 512
COL_GDN_GATE = 2560
COL_ML = 2816
COL_GATES = 3840
IN_PAD = 3968
G_DT, G_GA, G_GB, G_MI, G_MF, G_DTV = 0, 4, 8, 12, 16, 20

SUBLANES = 8
LANES = 128
VMEM_LIMIT = 56 * 1024 * 1024

FF_CHUNK = 256
FF_STREAM = 1408
TOKEN_TILE = 512
PROMPT_TILE = 512


def _sigmoid(x):
    return 1.0 / (1.0 + jnp.exp(-x))


def _silu(x):
    return x * _sigmoid(x)


def _softplus(x):
    return jnp.maximum(x, 0.0) + jnp.log1p(jnp.exp(-jnp.abs(x)))


def _rms(x, g):
    return x * lax.rsqrt(jnp.mean(x * x, axis=-1, keepdims=True) + EPS) * g


def _dot(a, b):
    return jnp.dot(a.astype(BF16), b.astype(BF16), preferred_element_type=F32)


def _dot_nt(a, b):
    return lax.dot_general(a.astype(BF16), b.astype(BF16), (((1,), (1,)), ((), ())),
                           preferred_element_type=F32)


def _dot_tn(a, b):
    return lax.dot_general(a.astype(BF16), b.astype(BF16), (((0,), (0,)), ((), ())),
                           preferred_element_type=F32)


N_SANDWICH = 6


def _layer_block(tail, lead, ngrid=1, buffered=True):
    idx = tuple(lead) + (0,) * len(tail)
    imap = (lambda i: idx) if ngrid == 1 else (lambda i, j: idx)
    kw = dict(pipeline_mode=pl.Buffered(1)) if buffered else {}
    return pl.BlockSpec((None,) * len(lead) + tuple(tail), imap, **kw)


WIN_SEGMENTS = ((0, 0, 1792), (1792, 1796, 1024), (2816, 2828, 1024))
WIN_GATES = ((1792, 1796), (2820, 2828), (3852, 3860))


def _prep_win_body(w_ref, o_ref):
    l = pl.program_id(0)
    step = 2 * LANES
    for dst, src, n in WIN_SEGMENTS:
        for c in range(0, n, step):
            o_ref[:, dst + c:dst + c + step] = w_ref[src + c:src + c + step, l, :].T.astype(BF16)
    gates = jnp.concatenate([w_ref[a:b, l, :] for a, b in WIN_GATES]
                            + [jnp.zeros((LANES - 5 * N_HEADS, D_MODEL), F32)], axis=0)
    o_ref[:, COL_GATES:IN_PAD] = gates.T.astype(BF16)


def _prep_win(w_in):
    depth, _, in_dim = w_in.shape
    w_t = jnp.transpose(w_in, (2, 0, 1))
    return pl.pallas_call(
        _prep_win_body,
        out_shape=jax.ShapeDtypeStruct((depth, D_MODEL, IN_PAD), BF16),
        grid=(depth,),
        in_specs=[pl.BlockSpec((in_dim, depth, D_MODEL), lambda l: (0, 0, 0), pipeline_mode=pl.Buffered(1))],
        out_specs=pl.BlockSpec((None, D_MODEL, IN_PAD), lambda l: (l, 0, 0)),
        compiler_params=pltpu.CompilerParams(dimension_semantics=("arbitrary",), vmem_limit_bytes=VMEM_LIMIT),
        name="prep_win",
    )(w_t)


def _ffn_body(*refs, k, mixer_out):
    if mixer_out:
        x_ref, m_ref, wo_ref, g_ref, wg_ref, wu_ref, wd_ref, o_ref, a_ref = refs
        y = jnp.dot(m_ref[...], wo_ref[...], preferred_element_type=F32)
        x = x_ref[...] + _rms(y, g_ref[3:4, :])
    else:
        x_ref, g_ref, wg_ref, wu_ref, wd_ref, o_ref, a_ref = refs
        x = x_ref[...]
    h = _rms(x, g_ref[4 * k:4 * k + 1, :]).astype(BF16)
    for j in range(D_FF // FF_CHUNK):
        sl = slice(j * FF_CHUNK, (j + 1) * FF_CHUNK)
        gt = jnp.dot(h, wg_ref[:, sl].astype(BF16), preferred_element_type=F32)
        up = jnp.dot(h, wu_ref[:, sl].astype(BF16), preferred_element_type=F32)
        a_ref[:, sl] = (_silu(gt) * up).astype(BF16)
    y = jnp.dot(a_ref[...], wd_ref[...], preferred_element_type=F32)
    o_ref[...] = x + 0.5 * _rms(y, g_ref[4 * k + 1:4 * k + 2, :])


def _ffn(x, sandwich, wg, wu, wd, l, k, mixed=None, w_out=None):
    m = x.shape[0]
    tm = min(TOKEN_TILE, m)
    tile = pl.BlockSpec((tm, D_MODEL), lambda i: (i, 0))
    fused = mixed is not None
    in_specs = [tile] + ([tile, _layer_block((D_MODEL, D_MODEL), (l,))] if fused else [])
    in_specs += [_layer_block((N_SANDWICH, D_MODEL), (l,)),
                 _layer_block((D_MODEL, D_FF), (l, k)),
                 _layer_block((D_MODEL, D_FF), (l, k)),
                 _layer_block((D_FF, D_MODEL), (l, k))]
    args = (x,) + ((mixed, w_out) if fused else ()) + (sandwich, wg, wu, wd)
    return pl.pallas_call(
        functools.partial(_ffn_body, k=k, mixer_out=fused),
        out_shape=jax.ShapeDtypeStruct((m, D_MODEL), F32),
        grid=(m // tm,),
        in_specs=in_specs,
        out_specs=tile,
        scratch_shapes=[pltpu.VMEM((tm, D_FF), BF16)],
        compiler_params=pltpu.CompilerParams(dimension_semantics=("parallel",),
                                             vmem_limit_bytes=VMEM_LIMIT),
        name="ffn_out" if fused else "ffn",
    )(*args)


def _ffn_stream_body(x_ref, g_ref, wg_ref, wu_ref, wd_ref, o_ref, h_ref, acc_ref, *, k):
    j = pl.program_id(0)

    @pl.when(j == 0)
    def _():
        h_ref[...] = _rms(x_ref[...], g_ref[4 * k:4 * k + 1, :]).astype(BF16)
        acc_ref[...] = jnp.zeros_like(acc_ref)

    h = h_ref[...]
    gt = jnp.dot(h, wg_ref[...].astype(BF16), preferred_element_type=F32)
    up = jnp.dot(h, wu_ref[...].astype(BF16), preferred_element_type=F32)
    acc_ref[...] += jnp.dot((_silu(gt) * up).astype(BF16), wd_ref[...], preferred_element_type=F32)

    @pl.when(j == pl.num_programs(0) - 1)
    def _():
        o_ref[...] = x_ref[...] + 0.5 * _rms(acc_ref[...], g_ref[4 * k + 1:4 * k + 2, :])


def _ffn_stream(x, sandwich, wg, wu, wd, l, k):
    m = x.shape[0]
    whole = pl.BlockSpec((m, D_MODEL), lambda j: (0, 0))
    return pl.pallas_call(
        functools.partial(_ffn_stream_body, k=k),
        out_shape=jax.ShapeDtypeStruct((m, D_MODEL), F32),
        grid=(D_FF // FF_STREAM,),
        in_specs=[whole,
                  _layer_block((N_SANDWICH, D_MODEL), (l,), buffered=False),
                  pl.BlockSpec((None, None, D_MODEL, FF_STREAM), lambda j: (l, k, 0, j)),
                  pl.BlockSpec((None, None, D_MODEL, FF_STREAM), lambda j: (l, k, 0, j)),
                  pl.BlockSpec((None, None, FF_STREAM, D_MODEL), lambda j: (l, k, j, 0))],
        out_specs=whole,
        scratch_shapes=[pltpu.VMEM((m, D_MODEL), BF16), pltpu.VMEM((m, D_MODEL), F32)],
        compiler_params=pltpu.CompilerParams(dimension_semantics=("arbitrary",), vmem_limit_bytes=VMEM_LIMIT),
        name="ffn_stream",
    )(x, sandwich, wg, wu, wd)


NEXP = 6
X_CSSM, X_DT, X_CGDN, X_BETA, X_B, X_IMB = range(NEXP)


def _split2(x):
    hi = x.astype(BF16)
    return hi, (x - hi.astype(F32)).astype(BF16)


def _dot_sel_r2(x, sel):
    hi, lo = _split2(x)
    return jnp.dot(lo, sel, preferred_element_type=F32) + jnp.dot(hi, sel, preferred_element_type=F32)


(PB_RQ, PB_RK, PB_RV, PB_RG, PB_Z, PB_SX, PB_SBC, PB_GQ, PB_GK, PB_GV, PB_GG, PB_MQ, PB_MK, PB_MV, PB_MO,
 PB_X0) = range(16)
PRE_COLS = (PB_X0 + NEXP) * GROUP_W


def _pproj_body(x_ref, g_ref, w_ref, cos_ref, sin_ref, maskbd_ref, gvec_ref, convw_ref,
                convb_ref, pre_ref, convraw_ref, e_ref, *, rows):
    @pl.when(pl.program_id(1) == 0)
    def _():
        e_ref[0:SUBLANES, :] = jnp.zeros((SUBLANES, CONV_CH), F32)

    h_in = _rms(x_ref[...], g_ref[2:3, :]).astype(BF16)
    proj = lambda a, n: jnp.dot(h_in, w_ref[:, a:a + n], preferred_element_type=F32)

    def put(b, v):
        pre_ref[:, b * GROUP_W:(b + 1) * GROUP_W] = v

    maskbd_b = maskbd_ref[...].astype(BF16)
    seg_sum = lambda x: _dot_sel_r2(x, maskbd_b)

    lane = lax.broadcasted_iota(jnp.int32, (1, LANES), 1)
    in_lanes = lambda lo: (lane >= lo) & (lane < lo + N_HEADS)
    graw = proj(COL_GATES, LANES)
    put(PB_MQ, proj(COL_ML, GROUP_W))
    put(PB_MK, proj(COL_ML + GROUP_W, GROUP_W) * HEAD_DIM ** -0.5)
    put(PB_MV, proj(COL_ML + 2 * GROUP_W, GROUP_W))
    put(PB_MO, proj(COL_ML + 3 * GROUP_W, GROUP_W))
    xb = graw + gvec_ref[0:1, :]
    sp = _softplus(xb)
    neg_a = -jnp.exp(gvec_ref[1:2, :])
    logf = -_softplus(-xb)
    pre = jnp.where(lane < 2 * N_HEADS, neg_a * sp, jnp.where(in_lanes(G_MF), logf, 0.0))
    row_in_chunk = lax.broadcasted_iota(jnp.int32, (rows, 1), 0) % PROMPT_CHUNK
    cum, step = pre, 1
    while step < PROMPT_CHUNK:
        cum = cum + jnp.where(row_in_chunk >= step, pltpu.roll(cum, step, axis=0), 0.0)
        step *= 2
    imb = xb - pltpu.roll(cum, LANES - (G_MF - G_MI), axis=1)
    dtv = pltpu.roll(sp, G_DTV - G_DT, axis=1)
    gates = jnp.where(in_lanes(G_GB), _sigmoid(graw),
                      jnp.where(in_lanes(G_MI), imb, jnp.where(in_lanes(G_DTV), dtv, cum)))

    lane_g = lax.broadcasted_iota(jnp.int32, (1, GROUP_W), 1)
    first_half = (lane_g % HEAD_DIM) < (HEAD_DIM // 2)
    head_of_lane = lane_g // HEAD_DIM

    def rotary(x):
        swapped = jnp.where(first_half, pltpu.roll(x, GROUP_W - HEAD_DIM // 2, axis=1),
                            pltpu.roll(x, HEAD_DIM // 2, axis=1))
        return x * cos_ref[...] + swapped * sin_ref[...]

    def conv_in(c):
        e_ref[SUBLANES:SUBLANES + rows, c * GROUP_W:(c + 1) * GROUP_W] = proj(COL_CONV + c * GROUP_W, GROUP_W)

    def conv_out(c):
        cs = slice(c * GROUP_W, (c + 1) * GROUP_W)
        acc = e_ref[SUBLANES:SUBLANES + rows, cs] * convw_ref[CONV_W - 1:CONV_W, cs]
        for s in range(1, CONV_W):
            acc = acc + e_ref[SUBLANES - s:SUBLANES - s + rows, cs] * convw_ref[CONV_W - 1 - s:CONV_W - s, cs]
        return _silu(acc + convb_ref[:, cs])

    def expand(j):
        lo = (G_DT, G_DTV, G_GA, G_GB, G_MF, G_MI)[j]
        out = jnp.broadcast_to(gates[:, lo:lo + 1], (rows, GROUP_W))
        for h in range(1, N_HEADS):
            out = jnp.where(head_of_lane == h, jnp.broadcast_to(gates[:, lo + h:lo + h + 1], (rows, GROUP_W)), out)
        put(PB_X0 + j, out)

    conv_in(0)
    conv_in(1)
    conv_in(2)
    expand(0)
    expand(1)
    conv_in(3)
    put(PB_RV, proj(COL_RET + 2 * GROUP_W, GROUP_W))
    put(PB_SX, conv_out(0))
    conv_in(4)
    put(PB_RG, proj(COL_RET + 3 * GROUP_W, GROUP_W))
    put(PB_SBC, conv_out(1))
    rq = proj(COL_RET, GROUP_W)
    put(PB_Z, proj(COL_SSM_Z, GROUP_W))
    gq = conv_out(2)
    put(PB_GQ, gq * lax.rsqrt(seg_sum(gq * gq) + EPS) * HEAD_DIM ** -0.5)
    rk = proj(COL_RET + GROUP_W, GROUP_W)
    put(PB_GG, proj(COL_GDN_GATE, GROUP_W))
    gk = conv_out(3)
    put(PB_GK, gk * lax.rsqrt(seg_sum(gk * gk) + EPS))
    put(PB_GV, conv_out(4))
    convraw_ref[...] = e_ref[rows:rows + SUBLANES, :]
    e_ref[0:SUBLANES, :] = e_ref[rows:rows + SUBLANES, :]
    expand(2)
    put(PB_RQ, rotary(rq))
    expand(3)
    put(PB_RK, rotary(rk) * HEAD_DIM ** -0.5)
    expand(4)
    expand(5)


def _pproj(x, sandwich, w, cos, sin, small, l, *, nbatch, seqlen):
    rows = min(TOKEN_TILE, seqlen)
    ntile = seqlen // rows
    g = np.arange(GROUP_W)
    maskbd = jnp.asarray(g[:, None] // HEAD_DIM == g[None, :] // HEAD_DIM, F32)
    gvec, convw, convb = small[:3]
    row_map = lambda b, t: (b * ntile + t, 0)
    in_specs = [pl.BlockSpec((rows, D_MODEL), row_map),
                _layer_block((N_SANDWICH, D_MODEL), (l,), ngrid=2),
                _layer_block((D_MODEL, IN_PAD), (l,), ngrid=2),
                pl.BlockSpec((rows, GROUP_W), lambda b, t: (t, 0)), pl.BlockSpec((rows, GROUP_W), lambda b, t: (t, 0)),
                pl.BlockSpec(maskbd.shape, lambda b, t: (0, 0))]
    in_specs += [_layer_block(a.shape[1:], (l,), ngrid=2, buffered=False) for a in (gvec, convw, convb)]
    m = x.shape[0]
    return pl.pallas_call(
        functools.partial(_pproj_body, rows=rows),
        out_shape=[jax.ShapeDtypeStruct((m, PRE_COLS), F32),
                   jax.ShapeDtypeStruct((nbatch * SUBLANES, CONV_CH), F32)],
        grid=(nbatch, ntile), in_specs=in_specs,
        out_specs=[pl.BlockSpec((rows, PRE_COLS), row_map), pl.BlockSpec((SUBLANES, CONV_CH), lambda b, t: (b, 0))],
        scratch_shapes=[pltpu.VMEM((rows + SUBLANES, CONV_CH), F32)],
        compiler_params=pltpu.CompilerParams(dimension_semantics=("parallel", "arbitrary"),
                                             vmem_limit_bytes=VMEM_LIMIT),
        name="pproj",
    )(x, sandwich, w, cos, sin, maskbd, gvec, convw, convb)


def _mixp_body(pre_ref, rinter_ref, rtail_ref, rintra_ref, maskbd_ref, maskg_ref, maskht_ref, dvec_ref, norms_ref,
               mixed_ref, oret_ref, ossm_ref, ogdn_ref, omc_ref, omn_ref, omm_ref,
               o_ref, sret_ref, sht_ref, sgdn_ref, smc_ref, vec_ref, *, rows, chunk, ret_carry):
    nchunks = rows // chunk
    t_id = pl.program_id(1)
    blk = lambda b, rs=slice(None): pre_ref[rs, b * GROUP_W:(b + 1) * GROUP_W]

    @pl.when(t_id == 0)
    def _():
        sret_ref[...] = jnp.zeros_like(sret_ref)
        sht_ref[...] = jnp.zeros_like(sht_ref)
        sgdn_ref[...] = jnp.zeros_like(sgdn_ref)
        smc_ref[...] = jnp.zeros_like(smc_ref)
        vec_ref[...] = jnp.zeros_like(vec_ref)

    maskbd = maskbd_ref[...]
    maskbd_b = maskbd.astype(BF16)
    maskg = maskg_ref[...]
    maskg_b = maskg.astype(BF16)

    def seg_sum(x):
        return _dot_sel_r2(x, maskbd_b)

    r_i = lax.broadcasted_iota(jnp.int32, (chunk, GROUP_W), 0)
    c_i = lax.broadcasted_iota(jnp.int32, (chunk, GROUP_W), 1) % HEAD_DIM
    trim = r_i >= c_i
    diag = r_i == c_i
    strict = r_i > c_i
    eye = diag.astype(F32)
    head_of_lane = lax.broadcasted_iota(jnp.int32, (1, GROUP_W), 1) // HEAD_DIM
    carry_row = jnp.zeros((1, GROUP_W), F32)
    for h in range(N_HEADS):
        carry_row = jnp.where(head_of_lane == h, ret_carry[h], carry_row)
    maskht = maskht_ref[...]

    def bd(x):
        xb16 = x.astype(BF16)
        return jnp.concatenate([xb16] * N_HEADS, axis=0) * maskbd_b

    def last_row(x):
        return x[chunk - 1:chunk, :]

    def row_form(x):
        return jnp.sum(jnp.where(diag, x, 0.0), axis=0, keepdims=True)

    def seg_max(x):
        out = None
        for h in range(N_HEADS):
            sel = head_of_lane == h
            m_h = jnp.max(jnp.where(sel, x, -jnp.inf), axis=-1, keepdims=True)
            out = jnp.where(sel, m_h, 0.0) if out is None else jnp.where(sel, m_h, out)
        return out

    chunks = [slice(c * chunk, (c + 1) * chunk) for c in range(nchunks)]

    g_gam, g_beta, g_ecum, g_k, g_n, g_p = [], [], [], [], [], []
    for rs in chunks:
        ce = blk(PB_X0 + X_CGDN, rs)
        gam = jnp.exp(jnp.where(trim, ce - row_form(ce), -jnp.inf))
        k_c = blk(PB_GK, rs)
        beta = blk(PB_X0 + X_BETA, rs)
        n0 = -(_dot_nt(k_c, bd(k_c)) * jnp.where(strict, gam, 0.0) * beta)
        g_gam.append(gam), g_beta.append(beta), g_ecum.append(jnp.exp(ce)), g_k.append(k_c)
        g_n.append(n0), g_p.append(eye + n0)
    for _ in range(max(int(math.ceil(math.log2(chunk))) - 1, 0)):
        for c in range(nchunks):
            g_n[c] = _dot(g_n[c], bd(g_n[c]))
        for c in range(nchunks):
            g_p[c] = g_p[c] + _dot(g_p[c], bd(g_n[c]))
    g_u, g_w = [], []
    for c, rs in enumerate(chunks):
        g_u.append(_dot(g_p[c], bd(blk(PB_GV, rs) * g_beta[c])))
        g_w.append(_dot(g_p[c], bd(g_k[c] * (g_beta[c] * g_ecum[c]))))

    s_ret, s_ht, s_gdn, s_mc = sret_ref[...], sht_ref[...], sgdn_ref[...], smc_ref[...]
    n_row, m_row = vec_ref[0:1, :], vec_ref[1:2, :]
    for c, rs in enumerate(chunks):
        rq, rk, rv = blk(PB_RQ, rs), blk(PB_RK, rs), blk(PB_RV, rs)
        ce_s = blk(PB_X0 + X_CSSM, rs)
        b_c = pre_ref[rs, PB_SBC * GROUP_W:PB_SBC * GROUP_W + 2 * HEAD_DIM]
        c_c = pre_ref[rs, PB_SBC * GROUP_W + 2 * HEAD_DIM:(PB_SBC + 1) * GROUP_W]
        xdt = blk(PB_SX, rs) * blk(PB_X0 + X_DT, rs)
        ce_g, gq = blk(PB_X0 + X_CGDN, rs), blk(PB_GQ, rs)
        mq, mk, mv = blk(PB_MQ, rs), blk(PB_MK, rs), blk(PB_MV, rs)
        b_e, imb_e = blk(PB_X0 + X_B, rs), blk(PB_X0 + X_IMB, rs)

        w_s = _dot(g_w[c], s_gdn)
        att = _dot_nt(rq, bd(rk))
        scores = _dot_nt(c_c, jnp.concatenate([b_c.astype(BF16)] * N_HEADS, axis=0) * maskg_b)
        qk_g = _dot_nt(gq, bd(g_k[c]))
        qk_m = _dot_nt(mq, bd(mk))
        o_ret = _dot(rq * rinter_ref[...], s_ret)
        o_ssm = _dot(c_c, s_ht)
        o_gdn = _dot(gq * g_ecum[c], s_gdn)
        o_ml = _dot(mq, s_mc)
        qn = seg_sum(mq * n_row)

        v_new = g_u[c] - w_s
        att = att * rintra_ref[...]
        ce_s_last, ce_g_last = last_row(ce_s), last_row(ce_g)
        scores = scores * jnp.exp(jnp.where(trim, ce_s - row_form(ce_s), -jnp.inf))
        qk_g = qk_g * g_gam[c]
        dmat = jnp.where(trim, b_e + row_form(imb_e), -jnp.inf)
        m_t = jnp.maximum(b_e + m_row, seg_max(dmat))
        inter = jnp.exp(b_e + m_row - m_t)
        qk_m = qk_m * jnp.exp(dmat - m_t)
        m_new, b_last = last_row(m_t), last_row(b_e)
        kw = mk * jnp.exp(b_last + imb_e - m_new)
        cs = jnp.exp(b_last + m_row - m_new)

        o_gdn = o_gdn + _dot(qk_g, bd(v_new))
        d_gdn = _dot_tn(g_k[c] * jnp.exp(ce_g_last - ce_g), v_new)
        o_ret = o_ret + _dot(att, bd(rv))
        o_ssm = o_ssm * jnp.exp(ce_s) + _dot(scores, bd(xdt))
        num = _dot(qk_m, bd(mv)) + inter * o_ml
        den = seg_sum(qk_m) + inter * qn
        d_ret = _dot_tn(rk * rtail_ref[...], rv)
        d_ht = _dot_tn(b_c, xdt * jnp.exp(ce_s_last - ce_s))
        d_mc = _dot_tn(kw, mv)

        s_gdn = s_gdn * jnp.exp(ce_g_last) + d_gdn * maskbd
        o_ml = num / jnp.maximum(jnp.abs(den), jnp.exp(-m_t))
        o_ref[rs, :] = jnp.concatenate([o_ret, o_ssm, o_gdn, o_ml], axis=1)
        s_ret = s_ret * carry_row + d_ret * maskbd
        s_ht = s_ht * jnp.exp(ce_s_last) + d_ht * maskht
        s_mc = s_mc * cs + d_mc * maskbd
        n_row = n_row * cs + jnp.sum(kw, axis=0, keepdims=True)
        m_row = m_new

    sret_ref[...] = s_ret
    sht_ref[...] = s_ht
    sgdn_ref[...] = s_gdn
    smc_ref[...] = s_mc
    vec_ref[0:1, :] = n_row
    vec_ref[1:2, :] = m_row

    def head_norm(x, g):
        return x * lax.rsqrt(seg_sum(x * x) * (1.0 / HEAD_DIM) + EPS) * g

    out_ret = head_norm(o_ref[:, 0:GROUP_W], norms_ref[0:1, :]) * _silu(blk(PB_RG))
    mixed_ref[:, 0:GROUP_W] = out_ret.astype(BF16)
    y = o_ref[:, GROUP_W:2 * GROUP_W] + dvec_ref[...] * blk(PB_SX)
    out_ssm = _rms(y * _silu(blk(PB_Z)), norms_ref[1:2, :])
    mixed_ref[:, GROUP_W:2 * GROUP_W] = out_ssm.astype(BF16)
    out_gdn = head_norm(o_ref[:, 2 * GROUP_W:3 * GROUP_W], norms_ref[2:3, :]) * _silu(blk(PB_GG))
    mixed_ref[:, 2 * GROUP_W:3 * GROUP_W] = out_gdn.astype(BF16)
    out_ml = _sigmoid(blk(PB_MO)) * head_norm(o_ref[:, 3 * GROUP_W:4 * GROUP_W], norms_ref[3:4, :])
    mixed_ref[:, 3 * GROUP_W:4 * GROUP_W] = out_ml.astype(BF16)

    @pl.when(t_id == pl.num_programs(1) - 1)
    def _():
        for h in range(N_HEADS):
            hs = slice(h * HEAD_DIM, (h + 1) * HEAD_DIM)
            grp = h // (N_HEADS // 2)
            oret_ref[0, h] = s_ret[hs, hs]
            ogdn_ref[0, h] = s_gdn[hs, hs]
            omc_ref[0, h] = s_mc[hs, hs]
            ossm_ref[0, h] = s_ht[grp * HEAD_DIM:(grp + 1) * HEAD_DIM, hs]
        omn_ref[0] = jnp.concatenate([n_row[:, h * HEAD_DIM:(h + 1) * HEAD_DIM] for h in range(N_HEADS)], axis=0)
        omm_ref[0] = jnp.concatenate([m_row[:, h * HEAD_DIM:h * HEAD_DIM + 1] for h in range(N_HEADS)], axis=1)


def _prompt_consts(rows, chunk):
    lg = np.log(1.0 - np.exp2(-5.0 - np.arange(N_HEADS, dtype=np.float64)))
    idx = np.arange(chunk, dtype=np.float64)
    rel = idx[:, None] - idx[None, :]
    intra = np.where(rel[None] >= 0, np.exp(np.maximum(rel[None], 0.0) * lg[:, None, None]), 0.0)
    intra = np.concatenate(list(intra), axis=1)
    lanes = lambda t: np.repeat(t.T, HEAD_DIM, axis=1)
    inter = lanes(np.exp((idx[None, :] + 1.0) * lg[:, None]))
    tail = lanes(np.exp((chunk - 1.0 - idx[None, :]) * lg[:, None]))
    carry = tuple(float(c) for c in np.exp(chunk * lg))
    g = np.arange(GROUP_W)
    maskbd = (g[:, None] // HEAD_DIM == g[None, :] // HEAD_DIM)
    n = np.arange(2 * HEAD_DIM)
    maskg = (g[:, None] // (2 * HEAD_DIM) == n[None, :] // HEAD_DIM)
    f = lambda a, dt=F32: jnp.asarray(np.asarray(a, np.float32), dt)
    return (f(inter), f(tail), f(intra), f(maskbd), f(maskg), f(maskg.T)), carry


def _mix_prompt(pre, small, l, *, nbatch, seqlen):
    rows, chunk = min(PROMPT_TILE, seqlen), PROMPT_CHUNK
    ntile = seqlen // rows
    consts, carry = _prompt_consts(rows, chunk)
    dvec, norms = small[3:]
    m = pre.shape[0]
    row_map = lambda b, t: (b * ntile + t, 0)
    full = lambda a: pl.BlockSpec(a.shape, lambda b, t: (0, 0))
    in_specs = [pl.BlockSpec((rows, PRE_COLS), row_map)] + [full(a) for a in consts]
    in_specs += [_layer_block(a.shape[1:], (l,), ngrid=2, buffered=False) for a in (dvec, norms)]
    st4 = pl.BlockSpec((1, N_HEADS, HEAD_DIM, HEAD_DIM), lambda b, t: (b, 0, 0, 0))
    out_shape = [jax.ShapeDtypeStruct((m, D_MODEL), BF16)]
    out_shape += [jax.ShapeDtypeStruct((nbatch, N_HEADS, HEAD_DIM, HEAD_DIM), F32) for _ in range(4)]
    out_shape += [jax.ShapeDtypeStruct((nbatch, N_HEADS, HEAD_DIM), F32),
                  jax.ShapeDtypeStruct((nbatch, 1, N_HEADS), F32)]
    out_specs = [pl.BlockSpec((rows, D_MODEL), row_map), st4, st4, st4, st4,
                 pl.BlockSpec((1, N_HEADS, HEAD_DIM), lambda b, t: (b, 0, 0)),
                 pl.BlockSpec((1, 1, N_HEADS), lambda b, t: (b, 0, 0))]
    scratch = [pltpu.VMEM((rows, D_MODEL), F32),
               pltpu.VMEM((GROUP_W, GROUP_W), F32), pltpu.VMEM((2 * HEAD_DIM, GROUP_W), F32),
               pltpu.VMEM((GROUP_W, GROUP_W), F32), pltpu.VMEM((GROUP_W, GROUP_W), F32),
               pltpu.VMEM((SUBLANES, GROUP_W), F32)]
    body = functools.partial(_mixp_body, rows=rows, chunk=chunk, ret_carry=carry)
    outs = pl.pallas_call(
        body, out_shape=out_shape, grid=(nbatch, ntile), in_specs=in_specs, out_specs=out_specs,
        scratch_shapes=scratch,
        compiler_params=pltpu.CompilerParams(dimension_semantics=("parallel", "arbitrary"),
                                             vmem_limit_bytes=VMEM_LIMIT),
        name="mix_prompt",
    )(pre, *consts, dvec, norms)
    outs = list(outs)
    outs[2] = jnp.swapaxes(outs[2], -1, -2)
    return outs


S_RQ, S_RK, S_RV, S_XDT, S_B, S_C, S_GQ, S_GK, S_GV, S_MQ, S_MK, S_MV, S_GATE = (
    HEAD_DIM * i for i in range(13))
SLAB_ROWS = S_GATE + SUBLANES
GT_SSM_A, GT_GDN_A, GT_BETA, GT_LOGI, GT_LOGF, GT_GAMMA = range(6)
P_RG, P_Z, P_XS, P_GG, P_MO = (GROUP_W * i for i in range(5))
POST_ROWS = 5 * GROUP_W
C_CONVW, C_CONVB = 0, CONV_W * CONV_CH
C_BIAS = C_CONVB + CONV_CH
C_ALOG = C_BIAS + 4 * SUBLANES
C_GAMMA = C_ALOG + 4 * SUBLANES
C_ROWS = C_GAMMA + SUBLANES
N_ROWS = 5 * GROUP_W
N_TOK = 4


def _spre_body(x_ref, g_ref, w_ref, hist_ref, rot_ref, colp_ref, slab_ref, post_ref, convraw_ref):
    nb = LANES
    h_in = _rms(x_ref[...], g_ref[2:3, :]).astype(BF16)
    proj = jnp.dot(h_in, w_ref[...], preferred_element_type=F32)
    for j in range(CONV_W - 1):
        convraw_ref[j] = proj[(j + 1) * nb:(j + 2) * nb, COL_CONV:COL_CONV + CONV_CH]
    pt = [proj[t * nb:(t + 1) * nb, :].T for t in range(N_TOK)]
    src = [hist_ref[j].T for j in range(CONV_W - 1)] + [p[COL_CONV:COL_CONV + CONV_CH] for p in pt]
    bias = colp_ref[C_BIAS:C_BIAS + 4 * SUBLANES, :]
    neg_a = -jnp.exp(colp_ref[C_ALOG:C_ALOG + 4 * SUBLANES, :])
    gamma = colp_ref[C_GAMMA:C_GAMMA + SUBLANES, :]
    zero_row = jnp.zeros((1, LANES), F32)

    def rotary(x, t):
        swapped = jnp.concatenate(
            [x[h * HEAD_DIM + off:h * HEAD_DIM + off + HEAD_DIM // 2]
             for h in range(N_HEADS) for off in (HEAD_DIM // 2, 0)], axis=0)
        return x * rot_ref[0, t] + swapped * rot_ref[1, t]

    for t in range(N_TOK):
        p = pt[t]
        g = p[COL_GATES:COL_GATES + 4 * SUBLANES]
        xb = g + bias
        sp = _softplus(xb)
        decay = jnp.exp(neg_a * sp)
        beta = _sigmoid(g)
        logf = -_softplus(-xb)
        conv = src[t] * colp_ref[C_CONVW:C_CONVW + CONV_CH, :]
        for j in range(1, CONV_W):
            conv = conv + src[t + j] * colp_ref[C_CONVW + j * CONV_CH:C_CONVW + (j + 1) * CONV_CH, :]
        cv = _silu(conv + colp_ref[C_CONVB:C_CONVB + CONV_CH, :])
        rq = rotary(p[COL_RET:COL_RET + GROUP_W], t)
        rk = rotary(p[COL_RET + GROUP_W:COL_RET + 2 * GROUP_W], t) * HEAD_DIM ** -0.5
        post_ref[t, P_RG:P_RG + GROUP_W, :] = p[COL_RET + 3 * GROUP_W:COL_RET + 4 * GROUP_W]
        post_ref[t, P_Z:P_Z + GROUP_W, :] = p[COL_SSM_Z:COL_SSM_Z + GROUP_W]
        post_ref[t, P_XS:P_XS + GROUP_W, :] = cv[0:GROUP_W]
        post_ref[t, P_GG:P_GG + GROUP_W, :] = p[COL_GDN_GATE:COL_GDN_GATE + GROUP_W]
        post_ref[t, P_MO:P_MO + GROUP_W, :] = p[COL_ML + 3 * GROUP_W:COL_ML + 4 * GROUP_W]
        for h in range(N_HEADS):
            hs = slice(h * HEAD_DIM, (h + 1) * HEAD_DIM)
            grp = h // (N_HEADS // 2)
            def put(r, v, h=h, t=t):
                slab_ref[h, t, r:r + v.shape[0], :] = v

            l2 = lambda v: v * lax.rsqrt(jnp.sum(v * v, axis=0, keepdims=True) + EPS)
            put(S_RQ, rq[hs])
            put(S_RK, rk[hs])
            put(S_RV, p[COL_RET + 2 * GROUP_W + h * HEAD_DIM:COL_RET + 2 * GROUP_W + (h + 1) * HEAD_DIM])
            put(S_XDT, cv[hs] * sp[G_DT + h:G_DT + h + 1])
            put(S_B, cv[GROUP_W + grp * HEAD_DIM:GROUP_W + (grp + 1) * HEAD_DIM])
            put(S_C, cv[GROUP_W + 2 * HEAD_DIM + grp * HEAD_DIM:GROUP_W + 2 * HEAD_DIM + (grp + 1) * HEAD_DIM])
            put(S_GQ, l2(cv[SSM_CONV_CH + h * HEAD_DIM:SSM_CONV_CH + (h + 1) * HEAD_DIM]) * HEAD_DIM ** -0.5)
            put(S_GK, l2(cv[SSM_CONV_CH + GROUP_W + h * HEAD_DIM:SSM_CONV_CH + GROUP_W + (h + 1) * HEAD_DIM]))
            put(S_GV, cv[SSM_CONV_CH + 2 * GROUP_W + h * HEAD_DIM:SSM_CONV_CH + 2 * GROUP_W + (h + 1) * HEAD_DIM])
            put(S_MQ, p[COL_ML + h * HEAD_DIM:COL_ML + (h + 1) * HEAD_DIM])
            put(S_MK, p[COL_ML + GROUP_W + h * HEAD_DIM:COL_ML + GROUP_W + (h + 1) * HEAD_DIM] * HEAD_DIM ** -0.5)
            put(S_MV, p[COL_ML + 2 * GROUP_W + h * HEAD_DIM:COL_ML + 2 * GROUP_W + (h + 1) * HEAD_DIM])
            row = lambda a, j: a[j + h:j + h + 1]
            put(S_GATE, jnp.concatenate([row(decay, G_DT), row(decay, G_GA), row(beta, G_GB), row(xb, G_MI),
                                         row(logf, G_MF), gamma[h:h + 1], zero_row, zero_row], axis=0))


def _spre(x, sandwich, w, hist, rot, colp, l):
    m = x.shape[0]
    return pl.pallas_call(
        _spre_body,
        out_shape=[jax.ShapeDtypeStruct((N_HEADS, N_TOK, SLAB_ROWS, LANES), F32),
                   jax.ShapeDtypeStruct((N_TOK, POST_ROWS, LANES), F32),
                   jax.ShapeDtypeStruct((CONV_W - 1, LANES, CONV_CH), F32)],
        grid=(1,),
        in_specs=[pl.BlockSpec((m, D_MODEL), lambda i: (0, 0)),
                  _layer_block((N_SANDWICH, D_MODEL), (l,), buffered=False),
                  _layer_block((D_MODEL, IN_PAD), (l,), buffered=False),
                  _layer_block((CONV_W - 1, LANES, CONV_CH), (l,), buffered=False),
                  pl.BlockSpec(rot.shape, lambda i: (0, 0, 0, 0)),
                  _layer_block((C_ROWS, LANES), (l,), buffered=False)],
        out_specs=[pl.BlockSpec((N_HEADS, N_TOK, SLAB_ROWS, LANES), lambda i: (0, 0, 0, 0)),
                   pl.BlockSpec((N_TOK, POST_ROWS, LANES), lambda i: (0, 0, 0)),
                   pl.BlockSpec((CONV_W - 1, LANES, CONV_CH), lambda i: (0, 0, 0))],
        compiler_params=pltpu.CompilerParams(dimension_semantics=("arbitrary",), vmem_limit_bytes=VMEM_LIMIT),
        name="spre",
    )(x, sandwich, w, hist, rot, colp)


def _srec_body(*refs, aliased, nsteps):
    ins, outs = refs[:7], refs[7 + (6 if aliased else 0):]
    slab_ref, sret_in, sssm_in, sgdn_in, smc_in, smn_in, smm_in = ins
    ot_ref, sret_out, sssm_out, sgdn_out, smc_out, smn_out, smm_out, y8_ref = outs
    j = pl.program_id(1)

    @pl.when(j == 0)
    def _():
        _srec_ret_ssm(slab_ref, sret_in, sssm_in, ot_ref, sret_out, sssm_out, y8_ref)

    @pl.when(j == nsteps - 1)
    def _():
        _srec_gdn_ml(slab_ref, sgdn_in, smc_in, smn_in, smm_in, ot_ref, sgdn_out, smc_out, smn_out, smm_out)

    if nsteps > 1:
        @pl.when(j > 0)
        def _():
            for dst in (sret_out, sssm_out):
                dst[...] = jnp.zeros_like(dst)

        @pl.when(j < nsteps - 1)
        def _():
            for dst in (sgdn_out, smc_out, smn_out, smm_out):
                dst[...] = jnp.zeros_like(dst)


def _srec_tools(slab_ref):
    tile = lambda t, r: slab_ref[t, r:r + HEAD_DIM, :]
    gate = lambda t, j: slab_ref[t, S_GATE + j:S_GATE + j + 1, :]
    drow = lambda t, r, d: slab_ref[t, pl.ds(r + d, 1), :]
    zeros = jnp.zeros((HEAD_DIM, LANES), F32)
    sweep = lambda body, init: lax.fori_loop(0, HEAD_DIM, body, init, unroll=8)
    blk = lambda d: pl.ds(pl.multiple_of(d * HEAD_DIM, HEAD_DIM), HEAD_DIM)
    return tile, gate, drow, zeros, sweep, blk


def _srec_ret_ssm(slab_ref, sret_in, sssm_in, ot_ref, sret_out, sssm_out, y8_ref):
    tile, gate, drow, zeros, sweep, blk = _srec_tools(slab_ref)

    gamma = gate(0, GT_GAMMA)
    for t in range(N_TOK):
        src = sret_in if t == 0 else sret_out
        v = tile(t, S_RV)

        def body(d, acc, t=t, src=src, v=v):
            s = src[blk(d), :] * gamma + drow(t, S_RK, d) * v
            sret_out[blk(d), :] = s
            return acc + drow(t, S_RQ, d) * s
        ot_ref[t, 0:HEAD_DIM, :] = sweep(body, zeros)

    for t in range(N_TOK):
        src = sssm_in if t == 0 else sssm_out
        a, b_t, c_t = gate(t, GT_SSM_A), tile(t, S_B), tile(t, S_C)

        def body(p, carry, t=t, src=src, a=a, b_t=b_t, c_t=c_t):
            hp = src[blk(p), :] * a + drow(t, S_XDT, p) * b_t
            sssm_out[blk(p), :] = hp
            part = jnp.sum((hp * c_t).reshape(HEAD_DIM // SUBLANES, SUBLANES, LANES), axis=0)
            y8_ref[pl.ds(pl.multiple_of(p * SUBLANES, SUBLANES), SUBLANES), :] = part
            return carry
        sweep(body, 0)
        ot_ref[t, HEAD_DIM:2 * HEAD_DIM, :] = jnp.sum(y8_ref[...].reshape(HEAD_DIM, SUBLANES, LANES), axis=1)


def _srec_gdn_ml(slab_ref, sgdn_in, smc_in, smn_in, smm_in, ot_ref, sgdn_out, smc_out, smn_out, smm_out):
    tile, gate, drow, zeros, sweep, blk = _srec_tools(slab_ref)

    ks = sweep(lambda d, acc: acc + drow(0, S_GK, d) * sgdn_in[blk(d), :], zeros)
    for t in range(N_TOK):
        src = sgdn_in if t == 0 else sgdn_out
        a = gate(t, GT_GDN_A)
        u = gate(t, GT_BETA) * (tile(t, S_GV) - a * ks)
        nxt = t + 1 < N_TOK

        def body(d, acc, t=t, src=src, a=a, u=u, nxt=nxt):
            s = src[blk(d), :] * a + drow(t, S_GK, d) * u
            sgdn_out[blk(d), :] = s
            o = acc[0] + drow(t, S_GQ, d) * s
            return (o, acc[1] + drow(t + 1, S_GK, d) * s) if nxt else (o, acc[1])
        o, ks = sweep(body, (zeros, zeros))
        ot_ref[t, 2 * HEAD_DIM:3 * HEAD_DIM, :] = o

    m_row, n_tile = smm_in[...], smn_in[...]
    for t in range(N_TOK):
        src = smc_in if t == 0 else smc_out
        logi, logf = gate(t, GT_LOGI), gate(t, GT_LOGF)
        m_new = jnp.maximum(logf + m_row, logi)
        fs, ins = jnp.exp(logf + m_row - m_new), jnp.exp(logi - m_new)
        n_tile = fs * n_tile + ins * tile(t, S_MK)
        qn = jnp.sum(tile(t, S_MQ) * n_tile, axis=0, keepdims=True)
        v = tile(t, S_MV) * ins

        def body(d, acc, t=t, src=src, fs=fs, v=v):
            c = src[blk(d), :] * fs + drow(t, S_MK, d) * v
            smc_out[blk(d), :] = c
            return acc + drow(t, S_MQ, d) * c
        num = sweep(body, zeros)
        ot_ref[t, 3 * HEAD_DIM:4 * HEAD_DIM, :] = num / jnp.maximum(jnp.abs(qn), jnp.exp(-m_new))
        m_row = m_new
    smn_out[...] = n_tile
    smm_out[...] = m_row


def _srec(slab, states, prev, l):
    depth = states[0].shape[0]
    rows = HEAD_DIM * HEAD_DIM
    st_shapes = [(rows, LANES)] * 4 + [(HEAD_DIM, LANES), (1, LANES)]
    spec = lambda tail, lead: pl.BlockSpec((None, None) + tail, lambda h, j: (lead(j), h, 0, 0))
    slab_spec = lambda r: pl.BlockSpec((None, N_TOK, r, LANES), lambda h, j: (h, 0, 0, 0))
    in_specs = [slab_spec(SLAB_ROWS)] + [spec(s, lambda j: l) for s in st_shapes]
    args = [slab, *states]
    aliases = {}
    if l:
        in_specs += [pl.BlockSpec(memory_space=pl.ANY)] * 6
        args += list(prev)
        aliases = {7 + i: 1 + i for i in range(6)}
    out_shape = [jax.ShapeDtypeStruct((N_HEADS, N_TOK, GROUP_W, LANES), F32)]
    out_shape += [jax.ShapeDtypeStruct((depth, N_HEADS) + s, F32) for s in st_shapes]
    nsteps = 1 if l else depth
    lead = [(lambda j: l + j)] * 2 + [(lambda j: l + (j + 1) % nsteps)] * 4
    out_specs = [slab_spec(GROUP_W)] + [spec(s, f) for s, f in zip(st_shapes, lead)]
    return pl.pallas_call(
        functools.partial(_srec_body, aliased=bool(l), nsteps=nsteps),
        out_shape=out_shape, grid=(N_HEADS, nsteps), in_specs=in_specs, out_specs=out_specs,
        input_output_aliases=aliases,
        scratch_shapes=[pltpu.VMEM((HEAD_DIM * SUBLANES, LANES), F32)],
        compiler_params=pltpu.CompilerParams(dimension_semantics=("arbitrary", "arbitrary"),
                                             vmem_limit_bytes=VMEM_LIMIT),
        name="srec",
    )(*args)


def _spost_body(x_ref, ot_ref, post_ref, cn_ref, g_ref, w_ref, o_ref):
    nb = LANES

    def head_norm(x):
        out = []
        for h in range(N_HEADS):
            b = x[h * HEAD_DIM:(h + 1) * HEAD_DIM]
            out.append(b * lax.rsqrt(jnp.mean(b * b, axis=0, keepdims=True) + EPS))
        return jnp.concatenate(out, axis=0)

    gain = lambda i: cn_ref[i * GROUP_W:(i + 1) * GROUP_W, :]
    for t in range(N_TOK):
        mixer = lambda i: jnp.concatenate(
            [ot_ref[h, t, i * HEAD_DIM:(i + 1) * HEAD_DIM, :] for h in range(N_HEADS)], axis=0)
        pin = lambda r: post_ref[t, r:r + GROUP_W, :]
        out_ret = head_norm(mixer(0)) * gain(0) * _silu(pin(P_RG))
        y = (mixer(1) + gain(4) * pin(P_XS)) * _silu(pin(P_Z))
        out_ssm = y * lax.rsqrt(jnp.mean(y * y, axis=0, keepdims=True) + EPS) * gain(1)
        out_gdn = head_norm(mixer(2)) * gain(2) * _silu(pin(P_GG))
        out_ml = _sigmoid(pin(P_MO)) * (head_norm(mixer(3)) * gain(3))
        mixed = jnp.concatenate([out_ret, out_ssm, out_gdn, out_ml], axis=0).T.astype(BF16)
        y = jnp.dot(mixed, w_ref[...], preferred_element_type=F32)
        rs = slice(t * nb, (t + 1) * nb)
        o_ref[rs, :] = x_ref[rs, :] + _rms(y, g_ref[3:4, :])


def _spost(x, ot, post, cn, sandwich, w, l):
    m = x.shape[0]
    return pl.pallas_call(
        _spost_body,
        out_shape=jax.ShapeDtypeStruct((m, D_MODEL), F32),
        grid=(1,),
        in_specs=[pl.BlockSpec((m, D_MODEL), lambda i: (0, 0)),
                  pl.BlockSpec(ot.shape, lambda i: (0, 0, 0, 0)),
                  pl.BlockSpec(post.shape, lambda i: (0, 0, 0)),
                  _layer_block((N_ROWS, LANES), (l,), buffered=False),
                  _layer_block((N_SANDWICH, D_MODEL), (l,), buffered=False),
                  _layer_block((D_MODEL, D_MODEL), (l,), buffered=False)],
        out_specs=pl.BlockSpec((m, D_MODEL), lambda i: (0, 0)),
        compiler_params=pltpu.CompilerParams(dimension_semantics=("arbitrary",), vmem_limit_bytes=VMEM_LIMIT),
        name="spost",
    )(x, ot, post, cn, sandwich, w)


def _rot_tables(pos):
    half = HEAD_DIM // 2
    inv = ROPE_BASE ** (-jnp.arange(half, dtype=F32) / half)
    ang = pos.astype(F32)[:, None] * inv[None, :]
    cos, sin = jnp.cos(ang), jnp.sin(ang)
    cos_full = jnp.tile(jnp.concatenate([cos, cos], axis=-1), (1, N_HEADS))
    sin_signed = jnp.tile(jnp.concatenate([-sin, sin], axis=-1), (1, N_HEADS))
    return cos_full, sin_signed


def _small_params(ret_norm, ssm_conv_w, ssm_conv_b, ssm_dt_bias, ssm_A_log, ssm_D, ssm_norm, gdn_conv_w,
                  gdn_dt_bias, gdn_A_log, gdn_norm, mlstm_i_bias, mlstm_f_bias, mlstm_norm):
    depth = ret_norm.shape[0]
    z = lambda n: jnp.zeros((depth, n), F32)
    pad_lanes = z(LANES - 5 * N_HEADS)
    bias = jnp.concatenate([ssm_dt_bias, gdn_dt_bias, z(N_HEADS), mlstm_i_bias, mlstm_f_bias, pad_lanes], axis=1)
    alog = jnp.concatenate([ssm_A_log, gdn_A_log, z(3 * N_HEADS), pad_lanes], axis=1)
    gvec = jnp.stack([bias, alog], axis=1)
    convw = jnp.concatenate([ssm_conv_w, gdn_conv_w], axis=2)
    convb = jnp.concatenate([ssm_conv_b, z(CONV_CH - SSM_CONV_CH)], axis=1)[:, None, :]
    dvec = jnp.repeat(ssm_D, HEAD_DIM, axis=1)[:, None, :]
    norms = jnp.stack([ret_norm, ssm_norm, gdn_norm, mlstm_norm], axis=1)
    return gvec, convw, convb, dvec, norms


def kernel(x_prompt, x_sample, state_ret, state_ssm, state_ssm_conv, state_gdn, state_gdn_conv,
           state_mlstm_C, state_mlstm_n, state_mlstm_m, w_in, w_out, norm_sandwich, ffn_w_gate, ffn_w_up,
           ffn_w_down, ret_norm, ssm_conv_w, ssm_conv_b, ssm_dt_bias, ssm_A_log, ssm_D, ssm_norm, gdn_conv_w,
           gdn_dt_bias, gdn_A_log, gdn_norm, mlstm_i_bias, mlstm_f_bias, mlstm_norm):
    depth = w_in.shape[0]
    bp, lp, _ = x_prompt.shape
    bs, ls, _ = x_sample.shape
    assert (bs, ls) == (LANES, N_TOK)
    w_in_p = _prep_win(w_in)
    w_out_b = w_out.astype(BF16)
    wg, wu, wd = ffn_w_gate, ffn_w_up, ffn_w_down.astype(BF16)
    small = _small_params(ret_norm, ssm_conv_w, ssm_conv_b, ssm_dt_bias, ssm_A_log, ssm_D, ssm_norm, gdn_conv_w,
                          gdn_dt_bias, gdn_A_log, gdn_norm, mlstm_i_bias, mlstm_f_bias, mlstm_norm)
    gvec, convw, convb, dvec, norms = small

    cos_p, sin_p = _rot_tables(jnp.arange(lp, dtype=jnp.int32))
    x = x_prompt.reshape(bp * lp, D_MODEL)
    st_p = []
    for l in range(depth):
        x = _ffn(x, norm_sandwich, wg, wu, wd, l, 0)
        pre, convraw = _pproj(x, norm_sandwich, w_in_p, cos_p, sin_p, small, l, nbatch=bp, seqlen=lp)
        outs = _mix_prompt(pre, small, l, nbatch=bp, seqlen=lp)
        x = _ffn(x, norm_sandwich, wg, wu, wd, l, 1, mixed=outs[0], w_out=w_out_b)
        st_p.append([convraw] + outs[1:])
    y_prompt = x.reshape(bp, lp, D_MODEL)
    conv_p = jnp.stack([s[0].reshape(bp, -1, CONV_CH)[:, -(CONV_W - 1):, :] for s in st_p])
    stack_p = lambda k: jnp.stack([s[k] for s in st_p])
    out_p = (stack_p(1), stack_p(2), conv_p[..., :SSM_CONV_CH], stack_p(3), conv_p[..., SSM_CONV_CH:],
             stack_p(4), stack_p(5), stack_p(6)[:, :, 0, :])

    lanes = lambda a: jnp.broadcast_to(a[..., None], a.shape + (LANES,))
    cos_s, sin_s = _rot_tables(PAST_LEN + jnp.arange(ls, dtype=jnp.int32))
    rot = lanes(jnp.stack([cos_s, sin_s]))
    lg = np.log(1.0 - np.exp2(-5.0 - np.arange(N_HEADS, dtype=np.float64)))
    gamma = jnp.asarray(np.concatenate([np.exp(lg), np.zeros(SUBLANES - N_HEADS)]), F32)
    colp = lanes(jnp.concatenate(
        [convw.reshape(depth, CONV_W * CONV_CH), convb[:, 0, :], gvec[:, 0, :4 * SUBLANES], gvec[:, 1, :4 * SUBLANES],
         jnp.broadcast_to(gamma, (depth, SUBLANES))], axis=1))
    cn = lanes(jnp.concatenate([norms.reshape(depth, 4 * GROUP_W), dvec[:, 0, :]], axis=1))
    hist = jnp.transpose(jnp.concatenate([state_ssm_conv, state_gdn_conv], axis=-1), (0, 2, 1, 3))
    mat = lambda s: jnp.transpose(s, (0, 2, 3, 4, 1)).reshape(depth, N_HEADS, HEAD_DIM * HEAD_DIM, bs)
    states_s = (mat(state_ret), mat(state_ssm), mat(state_gdn), mat(state_mlstm_C),
                jnp.transpose(state_mlstm_n, (0, 2, 3, 1)), jnp.transpose(state_mlstm_m, (0, 2, 1))[:, :, None, :])
    x = jnp.transpose(x_sample, (1, 0, 2)).reshape(ls * bs, D_MODEL)
    prev, convs = None, []
    for l in range(depth):
        x = _ffn_stream(x, norm_sandwich, wg, wu, wd, l, 0)
        slab, post, convraw = _spre(x, norm_sandwich, w_in_p, hist, rot, colp, l)
        outs = _srec(slab, states_s, prev, l)
        prev = outs[1:]
        x = _spost(x, outs[0], post, cn, norm_sandwich, w_out_b, l)
        x = _ffn_stream(x, norm_sandwich, wg, wu, wd, l, 1)
        convs.append(convraw)
    y_sample = jnp.transpose(x.reshape(ls, bs, D_MODEL), (1, 0, 2))
    conv_s = jnp.transpose(jnp.stack(convs), (0, 2, 1, 3))
    unmat = lambda s: jnp.transpose(s.reshape(depth, N_HEADS, HEAD_DIM, HEAD_DIM, bs), (0, 4, 1, 2, 3))
    out_s = (unmat(prev[0]), unmat(prev[1]), conv_s[..., :SSM_CONV_CH], unmat(prev[2]), conv_s[..., SSM_CONV_CH:],
             unmat(prev[3]), jnp.transpose(prev[4], (0, 3, 1, 2)), jnp.transpose(prev[5][:, :, 0, :], (0, 2, 1)))
    return (y_prompt, y_sample) + out_p + out_s
```

```python
import functools
import math

import numpy as np
import jax
import jax.numpy as jnp
from jax import lax
from jax.experimental import pallas as pl
from jax.experimental.pallas import tpu as pltpu

F32 = jnp.float32
BF16 = jnp.bfloat16

D_MODEL = 1024
N_HEADS = 4
HEAD_DIM = 64
GROUP_W = N_HEADS * HEAD_DIM
D_FF = 2816
CONV_W = 4
EPS = 1e-6
ROPE_BASE = 10000.0
PAST_LEN = 16384
PROMPT_CHUNK = 64

COL_RET = 0
COL_SSM_Z = 1024
COL_CONV = 1280
CONV_CH = 1280
SSM_CONV_CH = 512
COL_GDN_GATE = 2560
COL_ML = 2816
COL_GATES = 3840
IN_PAD = 3968
G_DT, G_GA, G_GB, G_MI, G_MF, G_DTV = 0, 4, 8, 12, 16, 20

SUBLANES = 8
LANES = 128
VMEM_LIMIT = 56 * 1024 * 1024

FF_CHUNK = 256
TOKEN_TILE = 512
PROMPT_TILE = 512


def _sigmoid(x):
    return 1.0 / (1.0 + jnp.exp(-x))


def _silu(x):
    return x * _sigmoid(x)


def _softplus(x):
    return jnp.maximum(x, 0.0) + jnp.log1p(jnp.exp(-jnp.abs(x)))


def _rms(x, g):
    return x * lax.rsqrt(jnp.mean(x * x, axis=-1, keepdims=True) + EPS) * g


def _dot(a, b):
    return jnp.dot(a.astype(BF16), b.astype(BF16), preferred_element_type=F32)


def _dot_nt(a, b):
    return lax.dot_general(a.astype(BF16), b.astype(BF16), (((1,), (1,)), ((), ())),
                           preferred_element_type=F32)


def _dot_tn(a, b):
    return lax.dot_general(a.astype(BF16), b.astype(BF16), (((0,), (0,)), ((), ())),
                           preferred_element_type=F32)


N_SANDWICH = 6


def _layer_block(tail, lead, ngrid=1, buffered=True):
    idx = tuple(lead) + (0,) * len(tail)
    imap = (lambda i: idx) if ngrid == 1 else (lambda i, j: idx)
    kw = dict(pipeline_mode=pl.Buffered(1)) if buffered else {}
    return pl.BlockSpec((None,) * len(lead) + tuple(tail), imap, **kw)


WIN_SEGMENTS = ((0, 0, 1792), (1792, 1796, 1024), (2816, 2828, 1024))
WIN_GATES = ((1792, 1796), (2820, 2828), (3852, 3860))


def _prep_win_body(w_ref, o_ref):
    l = pl.program_id(0)
    step = 2 * LANES
    for dst, src, n in WIN_SEGMENTS:
        for c in range(0, n, step):
            o_ref[:, dst + c:dst + c + step] = w_ref[src + c:src + c + step, l, :].T.astype(BF16)
    gates = jnp.concatenate([w_ref[a:b, l, :] for a, b in WIN_GATES]
                            + [jnp.zeros((LANES - 5 * N_HEADS, D_MODEL), F32)], axis=0)
    o_ref[:, COL_GATES:IN_PAD] = gates.T.astype(BF16)


def _prep_win(w_in):
    depth, _, in_dim = w_in.shape
    w_t = jnp.transpose(w_in, (2, 0, 1))
    return pl.pallas_call(
        _prep_win_body,
        out_shape=jax.ShapeDtypeStruct((depth, D_MODEL, IN_PAD), BF16),
        grid=(depth,),
        in_specs=[pl.BlockSpec((in_dim, depth, D_MODEL), lambda l: (0, 0, 0), pipeline_mode=pl.Buffered(1))],
        out_specs=pl.BlockSpec((None, D_MODEL, IN_PAD), lambda l: (l, 0, 0)),
        compiler_params=pltpu.CompilerParams(dimension_semantics=("arbitrary",), vmem_limit_bytes=VMEM_LIMIT),
        name="prep_win",
    )(w_t)


def _ffn_body(*refs, k, mixer_out):
    if mixer_out:
        x_ref, m_ref, wo_ref, g_ref, wg_ref, wu_ref, wd_ref, o_ref, a_ref = refs
        y = jnp.dot(m_ref[...], wo_ref[...], preferred_element_type=F32)
        x = x_ref[...] + _rms(y, g_ref[3:4, :])
    else:
        x_ref, g_ref, wg_ref, wu_ref, wd_ref, o_ref, a_ref = refs
        x = x_ref[...]
    h = _rms(x, g_ref[4 * k:4 * k + 1, :]).astype(BF16)
    for j in range(D_FF // FF_CHUNK):
        sl = slice(j * FF_CHUNK, (j + 1) * FF_CHUNK)
        gt = jnp.dot(h, wg_ref[:, sl].astype(BF16), preferred_element_type=F32)
        up = jnp.dot(h, wu_ref[:, sl].astype(BF16), preferred_element_type=F32)
        a_ref[:, sl] = (_silu(gt) * up).astype(BF16)
    y = jnp.dot(a_ref[...], wd_ref[...], preferred_element_type=F32)
    o_ref[...] = x + 0.5 * _rms(y, g_ref[4 * k + 1:4 * k + 2, :])


def _ffn(x, sandwich, wg, wu, wd, l, k, mixed=None, w_out=None):
    m = x.shape[0]
    tm = min(TOKEN_TILE, m)
    tile = pl.BlockSpec((tm, D_MODEL), lambda i: (i, 0))
    fused = mixed is not None
    in_specs = [tile] + ([tile, _layer_block((D_MODEL, D_MODEL), (l,))] if fused else [])
    in_specs += [_layer_block((N_SANDWICH, D_MODEL), (l,)),
                 _layer_block((D_MODEL, D_FF), (l, k)),
                 _layer_block((D_MODEL, D_FF), (l, k)),
                 _layer_block((D_FF, D_MODEL), (l, k))]
    args = (x,) + ((mixed, w_out) if fused else ()) + (sandwich, wg, wu, wd)
    return pl.pallas_call(
        functools.partial(_ffn_body, k=k, mixer_out=fused),
        out_shape=jax.ShapeDtypeStruct((m, D_MODEL), F32),
        grid=(m // tm,),
        in_specs=in_specs,
        out_specs=tile,
        scratch_shapes=[pltpu.VMEM((tm, D_FF), BF16)],
        compiler_params=pltpu.CompilerParams(dimension_semantics=("parallel",),
                                             vmem_limit_bytes=VMEM_LIMIT),
        name="ffn_out" if fused else "ffn",
    )(*args)


NEXP = 6
X_CSSM, X_DT, X_CGDN, X_BETA, X_B, X_IMB = range(NEXP)


def _split2(x):
    hi = x.astype(BF16)
    return hi, (x - hi.astype(F32)).astype(BF16)


def _dot_sel_r2(x, sel):
    hi, lo = _split2(x)
    return jnp.dot(lo, sel, preferred_element_type=F32) + jnp.dot(hi, sel, preferred_element_type=F32)


(PB_RQ, PB_RK, PB_RV, PB_RG, PB_Z, PB_SX, PB_SBC, PB_GQ, PB_GK, PB_GV, PB_GG, PB_MQ, PB_MK, PB_MV, PB_MO,
 PB_X0) = range(16)
PRE_COLS = (PB_X0 + NEXP) * GROUP_W


def _pproj_body(x_ref, g_ref, w_ref, cos_ref, sin_ref, maskbd_ref, gvec_ref, convw_ref,
                convb_ref, pre_ref, convraw_ref, e_ref, *, rows):
    @pl.when(pl.program_id(1) == 0)
    def _():
        e_ref[0:SUBLANES, :] = jnp.zeros((SUBLANES, CONV_CH), F32)

    h_in = _rms(x_ref[...], g_ref[2:3, :]).astype(BF16)
    proj = lambda a, n: jnp.dot(h_in, w_ref[:, a:a + n], preferred_element_type=F32)

    def put(b, v):
        pre_ref[:, b * GROUP_W:(b + 1) * GROUP_W] = v

    maskbd_b = maskbd_ref[...].astype(BF16)
    seg_sum = lambda x: _dot_sel_r2(x, maskbd_b)

    lane = lax.broadcasted_iota(jnp.int32, (1, LANES), 1)
    in_lanes = lambda lo: (lane >= lo) & (lane < lo + N_HEADS)
    graw = proj(COL_GATES, LANES)
    put(PB_MQ, proj(COL_ML, GROUP_W))
    put(PB_MK, proj(COL_ML + GROUP_W, GROUP_W) * HEAD_DIM ** -0.5)
    put(PB_MV, proj(COL_ML + 2 * GROUP_W, GROUP_W))
    put(PB_MO, proj(COL_ML + 3 * GROUP_W, GROUP_W))
    xb = graw + gvec_ref[0:1, :]
    sp = _softplus(xb)
    neg_a = -jnp.exp(gvec_ref[1:2, :])
    logf = -_softplus(-xb)
    pre = jnp.where(lane < 2 * N_HEADS, neg_a * sp, jnp.where(in_lanes(G_MF), logf, 0.0))
    row_in_chunk = lax.broadcasted_iota(jnp.int32, (rows, 1), 0) % PROMPT_CHUNK
    cum, step = pre, 1
    while step < PROMPT_CHUNK:
        cum = cum + jnp.where(row_in_chunk >= step, pltpu.roll(cum, step, axis=0), 0.0)
        step *= 2
    imb = xb - pltpu.roll(cum, LANES - (G_MF - G_MI), axis=1)
    dtv = pltpu.roll(sp, G_DTV - G_DT, axis=1)
    gates = jnp.where(in_lanes(G_GB), _sigmoid(graw),
                      jnp.where(in_lanes(G_MI), imb, jnp.where(in_lanes(G_DTV), dtv, cum)))

    lane_g = lax.broadcasted_iota(jnp.int32, (1, GROUP_W), 1)
    first_half = (lane_g % HEAD_DIM) < (HEAD_DIM // 2)
    head_of_lane = lane_g // HEAD_DIM

    def rotary(x):
        swapped = jnp.where(first_half, pltpu.roll(x, GROUP_W - HEAD_DIM // 2, axis=1),
                            pltpu.roll(x, HEAD_DIM // 2, axis=1))
        return x * cos_ref[...] + swapped * sin_ref[...]

    def conv_in(c):
        e_ref[SUBLANES:SUBLANES + rows, c * GROUP_W:(c + 1) * GROUP_W] = proj(COL_CONV + c * GROUP_W, GROUP_W)

    def conv_out(c):
        cs = slice(c * GROUP_W, (c + 1) * GROUP_W)
        acc = e_ref[SUBLANES:SUBLANES + rows, cs] * convw_ref[CONV_W - 1:CONV_W, cs]
        for s in range(1, CONV_W):
            acc = acc + e_ref[SUBLANES - s:SUBLANES - s + rows, cs] * convw_ref[CONV_W - 1 - s:CONV_W - s, cs]
        return _silu(acc + convb_ref[:, cs])

    def expand(j):
        lo = (G_DT, G_DTV, G_GA, G_GB, G_MF, G_MI)[j]
        out = jnp.broadcast_to(gates[:, lo:lo + 1], (rows, GROUP_W))
        for h in range(1, N_HEADS):
            out = jnp.where(head_of_lane == h, jnp.broadcast_to(gates[:, lo + h:lo + h + 1], (rows, GROUP_W)), out)
        put(PB_X0 + j, out)

    conv_in(0)
    conv_in(1)
    conv_in(2)
    expand(0)
    expand(1)
    conv_in(3)
    put(PB_RV, proj(COL_RET + 2 * GROUP_W, GROUP_W))
    put(PB_SX, conv_out(0))
    conv_in(4)
    put(PB_RG, proj(COL_RET + 3 * GROUP_W, GROUP_W))
    put(PB_SBC, conv_out(1))
    rq = proj(COL_RET, GROUP_W)
    put(PB_Z, proj(COL_SSM_Z, GROUP_W))
    gq = conv_out(2)
    put(PB_GQ, gq * lax.rsqrt(seg_sum(gq * gq) + EPS) * HEAD_DIM ** -0.5)
    rk = proj(COL_RET + GROUP_W, GROUP_W)
    put(PB_GG, proj(COL_GDN_GATE, GROUP_W))
    gk = conv_out(3)
    put(PB_GK, gk * lax.rsqrt(seg_sum(gk * gk) + EPS))
    put(PB_GV, conv_out(4))
    convraw_ref[...] = e_ref[rows:rows + SUBLANES, :]
    e_ref[0:SUBLANES, :] = e_ref[rows:rows + SUBLANES, :]
    expand(2)
    put(PB_RQ, rotary(rq))
    expand(3)
    put(PB_RK, rotary(rk) * HEAD_DIM ** -0.5)
    expand(4)
    expand(5)


def _pproj(x, sandwich, w, cos, sin, small, l, *, nbatch, seqlen):
    rows = min(TOKEN_TILE, seqlen)
    ntile = seqlen // rows
    g = np.arange(GROUP_W)
    maskbd = jnp.asarray(g[:, None] // HEAD_DIM == g[None, :] // HEAD_DIM, F32)
    gvec, convw, convb = small[:3]
    row_map = lambda b, t: (b * ntile + t, 0)
    in_specs = [pl.BlockSpec((rows, D_MODEL), row_map),
                _layer_block((N_SANDWICH, D_MODEL), (l,), ngrid=2),
                _layer_block((D_MODEL, IN_PAD), (l,), ngrid=2),
                pl.BlockSpec((rows, GROUP_W), lambda b, t: (t, 0)), pl.BlockSpec((rows, GROUP_W), lambda b, t: (t, 0)),
                pl.BlockSpec(maskbd.shape, lambda b, t: (0, 0))]
    in_specs += [_layer_block(a.shape[1:], (l,), ngrid=2, buffered=False) for a in (gvec, convw, convb)]
    m = x.shape[0]
    return pl.pallas_call(
        functools.partial(_pproj_body, rows=rows),
        out_shape=[jax.ShapeDtypeStruct((m, PRE_COLS), F32),
                   jax.ShapeDtypeStruct((nbatch * SUBLANES, CONV_CH), F32)],
        grid=(nbatch, ntile), in_specs=in_specs,
        out_specs=[pl.BlockSpec((rows, PRE_COLS), row_map), pl.BlockSpec((SUBLANES, CONV_CH), lambda b, t: (b, 0))],
        scratch_shapes=[pltpu.VMEM((rows + SUBLANES, CONV_CH), F32)],
        compiler_params=pltpu.CompilerParams(dimension_semantics=("parallel", "arbitrary"),
                                             vmem_limit_bytes=VMEM_LIMIT),
        name="pproj",
    )(x, sandwich, w, cos, sin, maskbd, gvec, convw, convb)


def _mixp_body(pre_ref, rinter_ref, rtail_ref, rintra_ref, maskbd_ref, maskg_ref, maskht_ref, dvec_ref, norms_ref,
               mixed_ref, oret_ref, ossm_ref, ogdn_ref, omc_ref, omn_ref, omm_ref,
               o_ref, sret_ref, sht_ref, sgdn_ref, smc_ref, vec_ref, *, rows, chunk, ret_carry):
    nchunks = rows // chunk
    t_id = pl.program_id(1)
    blk = lambda b, rs=slice(None): pre_ref[rs, b * GROUP_W:(b + 1) * GROUP_W]

    @pl.when(t_id == 0)
    def _():
        sret_ref[...] = jnp.zeros_like(sret_ref)
        sht_ref[...] = jnp.zeros_like(sht_ref)
        sgdn_ref[...] = jnp.zeros_like(sgdn_ref)
        smc_ref[...] = jnp.zeros_like(smc_ref)
        vec_ref[...] = jnp.zeros_like(vec_ref)

    maskbd = maskbd_ref[...]
    maskbd_b = maskbd.astype(BF16)
    maskg = maskg_ref[...]
    maskg_b = maskg.astype(BF16)

    def seg_sum(x):
        return _dot_sel_r2(x, maskbd_b)

    r_i = lax.broadcasted_iota(jnp.int32, (chunk, GROUP_W), 0)
    c_i = lax.broadcasted_iota(jnp.int32, (chunk, GROUP_W), 1) % HEAD_DIM
    trim = r_i >= c_i
    diag = r_i == c_i
    strict = r_i > c_i
    eye = diag.astype(F32)
    head_of_lane = lax.broadcasted_iota(jnp.int32, (1, GROUP_W), 1) // HEAD_DIM
    carry_row = jnp.zeros((1, GROUP_W), F32)
    for h in range(N_HEADS):
        carry_row = jnp.where(head_of_lane == h, ret_carry[h], carry_row)
    maskht = maskht_ref[...]

    def bd(x):
        xb16 = x.astype(BF16)
        return jnp.concatenate([xb16] * N_HEADS, axis=0) * maskbd_b

    def last_row(x):
        return x[chunk - 1:chunk, :]

    def row_form(x):
        return jnp.sum(jnp.where(diag, x, 0.0), axis=0, keepdims=True)

    def seg_max(x):
        out = None
        for h in range(N_HEADS):
            sel = head_of_lane == h
            m_h = jnp.max(jnp.where(sel, x, -jnp.inf), axis=-1, keepdims=True)
            out = jnp.where(sel, m_h, 0.0) if out is None else jnp.where(sel, m_h, out)
        return out

    chunks = [slice(c * chunk, (c + 1) * chunk) for c in range(nchunks)]

    g_gam, g_beta, g_ecum, g_k, g_n, g_p = [], [], [], [], [], []
    for rs in chunks:
        ce = blk(PB_X0 + X_CGDN, rs)
        gam = jnp.exp(jnp.where(trim, ce - row_form(ce), -jnp.inf))
        k_c = blk(PB_GK, rs)
        beta = blk(PB_X0 + X_BETA, rs)
        n0 = -(_dot_nt(k_c, bd(k_c)) * jnp.where(strict, gam, 0.0) * beta)
        g_gam.append(gam), g_beta.append(beta), g_ecum.append(jnp.exp(ce)), g_k.append(k_c)
        g_n.append(n0), g_p.append(eye + n0)
    for _ in range(max(int(math.ceil(math.log2(chunk))) - 1, 0)):
        for c in range(nchunks):
            g_n[c] = _dot(g_n[c], bd(g_n[c]))
        for c in range(nchunks):
            g_p[c] = g_p[c] + _dot(g_p[c], bd(g_n[c]))
    g_u, g_w = [], []
    for c, rs in enumerate(chunks):
        g_u.append(_dot(g_p[c], bd(blk(PB_GV, rs) * g_beta[c])))
        g_w.append(_dot(g_p[c], bd(g_k[c] * (g_beta[c] * g_ecum[c]))))

    s_ret, s_ht, s_gdn, s_mc = sret_ref[...], sht_ref[...], sgdn_ref[...], smc_ref[...]
    n_row, m_row = vec_ref[0:1, :], vec_ref[1:2, :]
    for c, rs in enumerate(chunks):
        rq, rk, rv = blk(PB_RQ, rs), blk(PB_RK, rs), blk(PB_RV, rs)
        ce_s = blk(PB_X0 + X_CSSM, rs)
        b_c = pre_ref[rs, PB_SBC * GROUP_W:PB_SBC * GROUP_W + 2 * HEAD_DIM]
        c_c = pre_ref[rs, PB_SBC * GROUP_W + 2 * HEAD_DIM:(PB_SBC + 1) * GROUP_W]
        xdt = blk(PB_SX, rs) * blk(PB_X0 + X_DT, rs)
        ce_g, gq = blk(PB_X0 + X_CGDN, rs), blk(PB_GQ, rs)
        mq, mk, mv = blk(PB_MQ, rs), blk(PB_MK, rs), blk(PB_MV, rs)
        b_e, imb_e = blk(PB_X0 + X_B, rs), blk(PB_X0 + X_IMB, rs)

        w_s = _dot(g_w[c], s_gdn)
        att = _dot_nt(rq, bd(rk))
        scores = _dot_nt(c_c, jnp.concatenate([b_c.astype(BF16)] * N_HEADS, axis=0) * maskg_b)
        qk_g = _dot_nt(gq, bd(g_k[c]))
        qk_m = _dot_nt(mq, bd(mk))
        o_ret = _dot(rq * rinter_ref[...], s_ret)
        o_ssm = _dot(c_c, s_ht)
        o_gdn = _dot(gq * g_ecum[c], s_gdn)
        o_ml = _dot(mq, s_mc)
        qn = seg_sum(mq * n_row)

        v_new = g_u[c] - w_s
        att = att * rintra_ref[...]
        ce_s_last, ce_g_last = last_row(ce_s), last_row(ce_g)
        scores = scores * jnp.exp(jnp.where(trim, ce_s - row_form(ce_s), -jnp.inf))
        qk_g = qk_g * g_gam[c]
        dmat = jnp.where(trim, b_e + row_form(imb_e), -jnp.inf)
        m_t = jnp.maximum(b_e + m_row, seg_max(dmat))
        inter = jnp.exp(b_e + m_row - m_t)
        qk_m = qk_m * jnp.exp(dmat - m_t)
        m_new, b_last = last_row(m_t), last_row(b_e)
        kw = mk * jnp.exp(b_last + imb_e - m_new)
        cs = jnp.exp(b_last + m_row - m_new)

        o_gdn = o_gdn + _dot(qk_g, bd(v_new))
        d_gdn = _dot_tn(g_k[c] * jnp.exp(ce_g_last - ce_g), v_new)
        o_ret = o_ret + _dot(att, bd(rv))
        o_ssm = o_ssm * jnp.exp(ce_s) + _dot(scores, bd(xdt))
        num = _dot(qk_m, bd(mv)) + inter * o_ml
        den = seg_sum(qk_m) + inter * qn
        d_ret = _dot_tn(rk * rtail_ref[...], rv)
        d_ht = _dot_tn(b_c, xdt * jnp.exp(ce_s_last - ce_s))
        d_mc = _dot_tn(kw, mv)

        s_gdn = s_gdn * jnp.exp(ce_g_last) + d_gdn * maskbd
        o_ml = num / jnp.maximum(jnp.abs(den), jnp.exp(-m_t))
        o_ref[rs, :] = jnp.concatenate([o_ret, o_ssm, o_gdn, o_ml], axis=1)
        s_ret = s_ret * carry_row + d_ret * maskbd
        s_ht = s_ht * jnp.exp(ce_s_last) + d_ht * maskht
        s_mc = s_mc * cs + d_mc * maskbd
        n_row = n_row * cs + jnp.sum(kw, axis=0, keepdims=True)
        m_row = m_new

    sret_ref[...] = s_ret
    sht_ref[...] = s_ht
    sgdn_ref[...] = s_gdn
    smc_ref[...] = s_mc
    vec_ref[0:1, :] = n_row
    vec_ref[1:2, :] = m_row

    def head_norm(x, g):
        return x * lax.rsqrt(seg_sum(x * x) * (1.0 / HEAD_DIM) + EPS) * g

    out_ret = head_norm(o_ref[:, 0:GROUP_W], norms_ref[0:1, :]) * _silu(blk(PB_RG))
    mixed_ref[:, 0:GROUP_W] = out_ret.astype(BF16)
    y = o_ref[:, GROUP_W:2 * GROUP_W] + dvec_ref[...] * blk(PB_SX)
    out_ssm = _rms(y * _silu(blk(PB_Z)), norms_ref[1:2, :])
    mixed_ref[:, GROUP_W:2 * GROUP_W] = out_ssm.astype(BF16)
    out_gdn = head_norm(o_ref[:, 2 * GROUP_W:3 * GROUP_W], norms_ref[2:3, :]) * _silu(blk(PB_GG))
    mixed_ref[:, 2 * GROUP_W:3 * GROUP_W] = out_gdn.astype(BF16)
    out_ml = _sigmoid(blk(PB_MO)) * head_norm(o_ref[:, 3 * GROUP_W:4 * GROUP_W], norms_ref[3:4, :])
    mixed_ref[:, 3 * GROUP_W:4 * GROUP_W] = out_ml.astype(BF16)

    @pl.when(t_id == pl.num_programs(1) - 1)
    def _():
        for h in range(N_HEADS):
            hs = slice(h * HEAD_DIM, (h + 1) * HEAD_DIM)
            grp = h // (N_HEADS // 2)
            oret_ref[0, h] = s_ret[hs, hs]
            ogdn_ref[0, h] = s_gdn[hs, hs]
            omc_ref[0, h] = s_mc[hs, hs]
            ossm_ref[0, h] = s_ht[grp * HEAD_DIM:(grp + 1) * HEAD_DIM, hs]
        omn_ref[0] = jnp.concatenate([n_row[:, h * HEAD_DIM:(h + 1) * HEAD_DIM] for h in range(N_HEADS)], axis=0)
        omm_ref[0] = jnp.concatenate([m_row[:, h * HEAD_DIM:h * HEAD_DIM + 1] for h in range(N_HEADS)], axis=1)


def _prompt_consts(rows, chunk):
    lg = np.log(1.0 - np.exp2(-5.0 - np.arange(N_HEADS, dtype=np.float64)))
    idx = np.arange(chunk, dtype=np.float64)
    rel = idx[:, None] - idx[None, :]
    intra = np.where(rel[None] >= 0, np.exp(np.maximum(rel[None], 0.0) * lg[:, None, None]), 0.0)
    intra = np.concatenate(list(intra), axis=1)
    lanes = lambda t: np.repeat(t.T, HEAD_DIM, axis=1)
    inter = lanes(np.exp((idx[None, :] + 1.0) * lg[:, None]))
    tail = lanes(np.exp((chunk - 1.0 - idx[None, :]) * lg[:, None]))
    carry = tuple(float(c) for c in np.exp(chunk * lg))
    g = np.arange(GROUP_W)
    maskbd = (g[:, None] // HEAD_DIM == g[None, :] // HEAD_DIM)
    n = np.arange(2 * HEAD_DIM)
    maskg = (g[:, None] // (2 * HEAD_DIM) == n[None, :] // HEAD_DIM)
    f = lambda a, dt=F32: jnp.asarray(np.asarray(a, np.float32), dt)
    return (f(inter), f(tail), f(intra), f(maskbd), f(maskg), f(maskg.T)), carry


def _mix_prompt(pre, small, l, *, nbatch, seqlen):
    rows, chunk = min(PROMPT_TILE, seqlen), PROMPT_CHUNK
    ntile = seqlen // rows
    consts, carry = _prompt_consts(rows, chunk)
    dvec, norms = small[3:]
    m = pre.shape[0]
    row_map = lambda b, t: (b * ntile + t, 0)
    full = lambda a: pl.BlockSpec(a.shape, lambda b, t: (0, 0))
    in_specs = [pl.BlockSpec((rows, PRE_COLS), row_map)] + [full(a) for a in consts]
    in_specs += [_layer_block(a.shape[1:], (l,), ngrid=2, buffered=False) for a in (dvec, norms)]
    st4 = pl.BlockSpec((1, N_HEADS, HEAD_DIM, HEAD_DIM), lambda b, t: (b, 0, 0, 0))
    out_shape = [jax.ShapeDtypeStruct((m, D_MODEL), BF16)]
    out_shape += [jax.ShapeDtypeStruct((nbatch, N_HEADS, HEAD_DIM, HEAD_DIM), F32) for _ in range(4)]
    out_shape += [jax.ShapeDtypeStruct((nbatch, N_HEADS, HEAD_DIM), F32),
                  jax.ShapeDtypeStruct((nbatch, 1, N_HEADS), F32)]
    out_specs = [pl.BlockSpec((rows, D_MODEL), row_map), st4, st4, st4, st4,
                 pl.BlockSpec((1, N_HEADS, HEAD_DIM), lambda b, t: (b, 0, 0)),
                 pl.BlockSpec((1, 1, N_HEADS), lambda b, t: (b, 0, 0))]
    scratch = [pltpu.VMEM((rows, D_MODEL), F32),
               pltpu.VMEM((GROUP_W, GROUP_W), F32), pltpu.VMEM((2 * HEAD_DIM, GROUP_W), F32),
               pltpu.VMEM((GROUP_W, GROUP_W), F32), pltpu.VMEM((GROUP_W, GROUP_W), F32),
               pltpu.VMEM((SUBLANES, GROUP_W), F32)]
    body = functools.partial(_mixp_body, rows=rows, chunk=chunk, ret_carry=carry)
    outs = pl.pallas_call(
        body, out_shape=out_shape, grid=(nbatch, ntile), in_specs=in_specs, out_specs=out_specs,
        scratch_shapes=scratch,
        compiler_params=pltpu.CompilerParams(dimension_semantics=("parallel", "arbitrary"),
                                             vmem_limit_bytes=VMEM_LIMIT),
        name="mix_prompt",
    )(pre, *consts, dvec, norms)
    outs = list(outs)
    outs[2] = jnp.swapaxes(outs[2], -1, -2)
    return outs


S_RQ, S_RK, S_RV, S_XDT, S_B, S_C, S_GQ, S_GK, S_GV, S_MQ, S_MK, S_MV, S_GATE = (
    HEAD_DIM * i for i in range(13))
SLAB_ROWS = S_GATE + SUBLANES
GT_SSM_A, GT_GDN_A, GT_BETA, GT_LOGI, GT_LOGF, GT_GAMMA = range(6)
P_RG, P_Z, P_XS, P_GG, P_MO = (GROUP_W * i for i in range(5))
POST_ROWS = 5 * GROUP_W
C_CONVW, C_CONVB = 0, CONV_W * CONV_CH
C_BIAS = C_CONVB + CONV_CH
C_ALOG = C_BIAS + 4 * SUBLANES
C_GAMMA = C_ALOG + 4 * SUBLANES
C_ROWS = C_GAMMA + SUBLANES
N_ROWS = 5 * GROUP_W
N_TOK = 4


def _spre_body(x_ref, g_ref, w_ref, hist_ref, rot_ref, colp_ref, slab_ref, post_ref, convraw_ref):
    nb = LANES
    h_in = _rms(x_ref[...], g_ref[2:3, :]).astype(BF16)
    proj = jnp.dot(h_in, w_ref[...], preferred_element_type=F32)
    for j in range(CONV_W - 1):
        convraw_ref[j] = proj[(j + 1) * nb:(j + 2) * nb, COL_CONV:COL_CONV + CONV_CH]
    pt = [proj[t * nb:(t + 1) * nb, :].T for t in range(N_TOK)]
    src = [hist_ref[j].T for j in range(CONV_W - 1)] + [p[COL_CONV:COL_CONV + CONV_CH] for p in pt]
    bias = colp_ref[C_BIAS:C_BIAS + 4 * SUBLANES, :]
    neg_a = -jnp.exp(colp_ref[C_ALOG:C_ALOG + 4 * SUBLANES, :])
    gamma = colp_ref[C_GAMMA:C_GAMMA + SUBLANES, :]
    zero_row = jnp.zeros((1, LANES), F32)

    def rotary(x, t):
        swapped = jnp.concatenate(
            [x[h * HEAD_DIM + off:h * HEAD_DIM + off + HEAD_DIM // 2]
             for h in range(N_HEADS) for off in (HEAD_DIM // 2, 0)], axis=0)
        return x * rot_ref[0, t] + swapped * rot_ref[1, t]

    for t in range(N_TOK):
        p = pt[t]
        g = p[COL_GATES:COL_GATES + 4 * SUBLANES]
        xb = g + bias
        sp = _softplus(xb)
        decay = jnp.exp(neg_a * sp)
        beta = _sigmoid(g)
        logf = -_softplus(-xb)
        conv = src[t] * colp_ref[C_CONVW:C_CONVW + CONV_CH, :]
        for j in range(1, CONV_W):
            conv = conv + src[t + j] * colp_ref[C_CONVW + j * CONV_CH:C_CONVW + (j + 1) * CONV_CH, :]
        cv = _silu(conv + colp_ref[C_CONVB:C_CONVB + CONV_CH, :])
        rq = rotary(p[COL_RET:COL_RET + GROUP_W], t)
        rk = rotary(p[COL_RET + GROUP_W:COL_RET + 2 * GROUP_W], t) * HEAD_DIM ** -0.5
        post_ref[t, P_RG:P_RG + GROUP_W, :] = p[COL_RET + 3 * GROUP_W:COL_RET + 4 * GROUP_W]
        post_ref[t, P_Z:P_Z + GROUP_W, :] = p[COL_SSM_Z:COL_SSM_Z + GROUP_W]
        post_ref[t, P_XS:P_XS + GROUP_W, :] = cv[0:GROUP_W]
        post_ref[t, P_GG:P_GG + GROUP_W, :] = p[COL_GDN_GATE:COL_GDN_GATE + GROUP_W]
        post_ref[t, P_MO:P_MO + GROUP_W, :] = p[COL_ML + 3 * GROUP_W:COL_ML + 4 * GROUP_W]
        for h in range(N_HEADS):
            hs = slice(h * HEAD_DIM, (h + 1) * HEAD_DIM)
            grp = h // (N_HEADS // 2)
            def put(r, v, h=h, t=t):
                slab_ref[h, t, r:r + v.shape[0], :] = v

            l2 = lambda v: v * lax.rsqrt(jnp.sum(v * v, axis=0, keepdims=True) + EPS)
            put(S_RQ, rq[hs])
            put(S_RK, rk[hs])
            put(S_RV, p[COL_RET + 2 * GROUP_W + h * HEAD_DIM:COL_RET + 2 * GROUP_W + (h + 1) * HEAD_DIM])
            put(S_XDT, cv[hs] * sp[G_DT + h:G_DT + h + 1])
            put(S_B, cv[GROUP_W + grp * HEAD_DIM:GROUP_W + (grp + 1) * HEAD_DIM])
            put(S_C, cv[GROUP_W + 2 * HEAD_DIM + grp * HEAD_DIM:GROUP_W + 2 * HEAD_DIM + (grp + 1) * HEAD_DIM])
            put(S_GQ, l2(cv[SSM_CONV_CH + h * HEAD_DIM:SSM_CONV_CH + (h + 1) * HEAD_DIM]) * HEAD_DIM ** -0.5)
            put(S_GK, l2(cv[SSM_CONV_CH + GROUP_W + h * HEAD_DIM:SSM_CONV_CH + GROUP_W + (h + 1) * HEAD_DIM]))
            put(S_GV, cv[SSM_CONV_CH + 2 * GROUP_W + h * HEAD_DIM:SSM_CONV_CH + 2 * GROUP_W + (h + 1) * HEAD_DIM])
            put(S_MQ, p[COL_ML + h * HEAD_DIM:COL_ML + (h + 1) * HEAD_DIM])
            put(S_MK, p[COL_ML + GROUP_W + h * HEAD_DIM:COL_ML + GROUP_W + (h + 1) * HEAD_DIM] * HEAD_DIM ** -0.5)
            put(S_MV, p[COL_ML + 2 * GROUP_W + h * HEAD_DIM:COL_ML + 2 * GROUP_W + (h + 1) * HEAD_DIM])
            row = lambda a, j: a[j + h:j + h + 1]
            put(S_GATE, jnp.concatenate([row(decay, G_DT), row(decay, G_GA), row(beta, G_GB), row(xb, G_MI),
                                         row(logf, G_MF), gamma[h:h + 1], zero_row, zero_row], axis=0))


def _spre(x, sandwich, w, hist, rot, colp, l):
    m = x.shape[0]
    return pl.pallas_call(
        _spre_body,
        out_shape=[jax.ShapeDtypeStruct((N_HEADS, N_TOK, SLAB_ROWS, LANES), F32),
                   jax.ShapeDtypeStruct((N_TOK, POST_ROWS, LANES), F32),
                   jax.ShapeDtypeStruct((CONV_W - 1, LANES, CONV_CH), F32)],
        grid=(1,),
        in_specs=[pl.BlockSpec((m, D_MODEL), lambda i: (0, 0)),
                  _layer_block((N_SANDWICH, D_MODEL), (l,), buffered=False),
                  _layer_block((D_MODEL, IN_PAD), (l,), buffered=False),
                  _layer_block((CONV_W - 1, LANES, CONV_CH), (l,), buffered=False),
                  pl.BlockSpec(rot.shape, lambda i: (0, 0, 0, 0)),
                  _layer_block((C_ROWS, LANES), (l,), buffered=False)],
        out_specs=[pl.BlockSpec((N_HEADS, N_TOK, SLAB_ROWS, LANES), lambda i: (0, 0, 0, 0)),
                   pl.BlockSpec((N_TOK, POST_ROWS, LANES), lambda i: (0, 0, 0)),
                   pl.BlockSpec((CONV_W - 1, LANES, CONV_CH), lambda i: (0, 0, 0))],
        compiler_params=pltpu.CompilerParams(dimension_semantics=("arbitrary",), vmem_limit_bytes=VMEM_LIMIT),
        name="spre",
    )(x, sandwich, w, hist, rot, colp)


def _srec_body(*refs, aliased, nsteps):
    ins, outs = refs[:7], refs[7 + (6 if aliased else 0):]
    slab_ref, sret_in, sssm_in, sgdn_in, smc_in, smn_in, smm_in = ins
    ot_ref, sret_out, sssm_out, sgdn_out, smc_out, smn_out, smm_out, y8_ref = outs
    j = pl.program_id(1)

    @pl.when(j == 0)
    def _():
        _srec_ret_ssm(slab_ref, sret_in, sssm_in, ot_ref, sret_out, sssm_out, y8_ref)

    @pl.when(j == nsteps - 1)
    def _():
        _srec_gdn_ml(slab_ref, sgdn_in, smc_in, smn_in, smm_in, ot_ref, sgdn_out, smc_out, smn_out, smm_out)

    if nsteps > 1:
        @pl.when(j > 0)
        def _():
            for dst in (sret_out, sssm_out):
                dst[...] = jnp.zeros_like(dst)

        @pl.when(j < nsteps - 1)
        def _():
            for dst in (sgdn_out, smc_out, smn_out, smm_out):
                dst[...] = jnp.zeros_like(dst)


def _srec_tools(slab_ref):
    tile = lambda t, r: slab_ref[t, r:r + HEAD_DIM, :]
    gate = lambda t, j: slab_ref[t, S_GATE + j:S_GATE + j + 1, :]
    drow = lambda t, r, d: slab_ref[t, pl.ds(r + d, 1), :]
    zeros = jnp.zeros((HEAD_DIM, LANES), F32)
    sweep = lambda body, init: lax.fori_loop(0, HEAD_DIM, body, init, unroll=8)
    blk = lambda d: pl.ds(pl.multiple_of(d * HEAD_DIM, HEAD_DIM), HEAD_DIM)
    return tile, gate, drow, zeros, sweep, blk


def _srec_ret_ssm(slab_ref, sret_in, sssm_in, ot_ref, sret_out, sssm_out, y8_ref):
    tile, gate, drow, zeros, sweep, blk = _srec_tools(slab_ref)

    gamma = gate(0, GT_GAMMA)
    for t in range(N_TOK):
        src = sret_in if t == 0 else sret_out
        v = tile(t, S_RV)

        def body(d, acc, t=t, src=src, v=v):
            s = src[blk(d), :] * gamma + drow(t, S_RK, d) * v
            sret_out[blk(d), :] = s
            return acc + drow(t, S_RQ, d) * s
        ot_ref[t, 0:HEAD_DIM, :] = sweep(body, zeros)

    for t in range(N_TOK):
        src = sssm_in if t == 0 else sssm_out
        a, b_t, c_t = gate(t, GT_SSM_A), tile(t, S_B), tile(t, S_C)

        def body(p, carry, t=t, src=src, a=a, b_t=b_t, c_t=c_t):
            hp = src[blk(p), :] * a + drow(t, S_XDT, p) * b_t
            sssm_out[blk(p), :] = hp
            part = jnp.sum((hp * c_t).reshape(HEAD_DIM // SUBLANES, SUBLANES, LANES), axis=0)
            y8_ref[pl.ds(pl.multiple_of(p * SUBLANES, SUBLANES), SUBLANES), :] = part
            return carry
        sweep(body, 0)
        ot_ref[t, HEAD_DIM:2 * HEAD_DIM, :] = jnp.sum(y8_ref[...].reshape(HEAD_DIM, SUBLANES, LANES), axis=1)


def _srec_gdn_ml(slab_ref, sgdn_in, smc_in, smn_in, smm_in, ot_ref, sgdn_out, smc_out, smn_out, smm_out):
    tile, gate, drow, zeros, sweep, blk = _srec_tools(slab_ref)

    ks = sweep(lambda d, acc: acc + drow(0, S_GK, d) * sgdn_in[blk(d), :], zeros)
    for t in range(N_TOK):
        src = sgdn_in if t == 0 else sgdn_out
        a = gate(t, GT_GDN_A)
        u = gate(t, GT_BETA) * (tile(t, S_GV) - a * ks)
        nxt = t + 1 < N_TOK

        def body(d, acc, t=t, src=src, a=a, u=u, nxt=nxt):
            s = src[blk(d), :] * a + drow(t, S_GK, d) * u
            sgdn_out[blk(d), :] = s
            o = acc[0] + drow(t, S_GQ, d) * s
            return (o, acc[1] + drow(t + 1, S_GK, d) * s) if nxt else (o, acc[1])
        o, ks = sweep(body, (zeros, zeros))
        ot_ref[t, 2 * HEAD_DIM:3 * HEAD_DIM, :] = o

    m_row, n_tile = smm_in[...], smn_in[...]
    for t in range(N_TOK):
        src = smc_in if t == 0 else smc_out
        logi, logf = gate(t, GT_LOGI), gate(t, GT_LOGF)
        m_new = jnp.maximum(logf + m_row, logi)
        fs, ins = jnp.exp(logf + m_row - m_new), jnp.exp(logi - m_new)
        n_tile = fs * n_tile + ins * tile(t, S_MK)
        qn = jnp.sum(tile(t, S_MQ) * n_tile, axis=0, keepdims=True)
        v = tile(t, S_MV) * ins

        def body(d, acc, t=t, src=src, fs=fs, v=v):
            c = src[blk(d), :] * fs + drow(t, S_MK, d) * v
            smc_out[blk(d), :] = c
            return acc + drow(t, S_MQ, d) * c
        num = sweep(body, zeros)
        ot_ref[t, 3 * HEAD_DIM:4 * HEAD_DIM, :] = num / jnp.maximum(jnp.abs(qn), jnp.exp(-m_new))
        m_row = m_new
    smn_out[...] = n_tile
    smm_out[...] = m_row


def _srec(slab, states, prev, l):
    depth = states[0].shape[0]
    rows = HEAD_DIM * HEAD_DIM
    st_shapes = [(rows, LANES)] * 4 + [(HEAD_DIM, LANES), (1, LANES)]
    spec = lambda tail, lead: pl.BlockSpec((None, None) + tail, lambda h, j: (lead(j), h, 0, 0))
    slab_spec = lambda r: pl.BlockSpec((None, N_TOK, r, LANES), lambda h, j: (h, 0, 0, 0))
    in_specs = [slab_spec(SLAB_ROWS)] + [spec(s, lambda j: l) for s in st_shapes]
    args = [slab, *states]
    aliases = {}
    if l:
        in_specs += [pl.BlockSpec(memory_space=pl.ANY)] * 6
        args += list(prev)
        aliases = {7 + i: 1 + i for i in range(6)}
    out_shape = [jax.ShapeDtypeStruct((N_HEADS, N_TOK, GROUP_W, LANES), F32)]
    out_shape += [jax.ShapeDtypeStruct((depth, N_HEADS) + s, F32) for s in st_shapes]
    nsteps = 1 if l else depth
    lead = [(lambda j: l + j)] * 2 + [(lambda j: l + (j + 1) % nsteps)] * 4
    out_specs = [slab_spec(GROUP_W)] + [spec(s, f) for s, f in zip(st_shapes, lead)]
    return pl.pallas_call(
        functools.partial(_srec_body, aliased=bool(l), nsteps=nsteps),
        out_shape=out_shape, grid=(N_HEADS, nsteps), in_specs=in_specs, out_specs=out_specs,
        input_output_aliases=aliases,
        scratch_shapes=[pltpu.VMEM((HEAD_DIM * SUBLANES, LANES), F32)],
        compiler_params=pltpu.CompilerParams(dimension_semantics=("arbitrary", "arbitrary"),
                                             vmem_limit_bytes=VMEM_LIMIT),
        name="srec",
    )(*args)


def _spost_body(x_ref, ot_ref, post_ref, cn_ref, g_ref, w_ref, o_ref):
    nb = LANES

    def head_norm(x):
        out = []
        for h in range(N_HEADS):
            b = x[h * HEAD_DIM:(h + 1) * HEAD_DIM]
            out.append(b * lax.rsqrt(jnp.mean(b * b, axis=0, keepdims=True) + EPS))
        return jnp.concatenate(out, axis=0)

    gain = lambda i: cn_ref[i * GROUP_W:(i + 1) * GROUP_W, :]
    for t in range(N_TOK):
        mixer = lambda i: jnp.concatenate(
            [ot_ref[h, t, i * HEAD_DIM:(i + 1) * HEAD_DIM, :] for h in range(N_HEADS)], axis=0)
        pin = lambda r: post_ref[t, r:r + GROUP_W, :]
        out_ret = head_norm(mixer(0)) * gain(0) * _silu(pin(P_RG))
        y = (mixer(1) + gain(4) * pin(P_XS)) * _silu(pin(P_Z))
        out_ssm = y * lax.rsqrt(jnp.mean(y * y, axis=0, keepdims=True) + EPS) * gain(1)
        out_gdn = head_norm(mixer(2)) * gain(2) * _silu(pin(P_GG))
        out_ml = _sigmoid(pin(P_MO)) * (head_norm(mixer(3)) * gain(3))
        mixed = jnp.concatenate([out_ret, out_ssm, out_gdn, out_ml], axis=0).T.astype(BF16)
        y = jnp.dot(mixed, w_ref[...], preferred_element_type=F32)
        rs = slice(t * nb, (t + 1) * nb)
        o_ref[rs, :] = x_ref[rs, :] + _rms(y, g_ref[3:4, :])


def _spost(x, ot, post, cn, sandwich, w, l):
    m = x.shape[0]
    return pl.pallas_call(
        _spost_body,
        out_shape=jax.ShapeDtypeStruct((m, D_MODEL), F32),
        grid=(1,),
        in_specs=[pl.BlockSpec((m, D_MODEL), lambda i: (0, 0)),
                  pl.BlockSpec(ot.shape, lambda i: (0, 0, 0, 0)),
                  pl.BlockSpec(post.shape, lambda i: (0, 0, 0)),
                  _layer_block((N_ROWS, LANES), (l,), buffered=False),
                  _layer_block((N_SANDWICH, D_MODEL), (l,), buffered=False),
                  _layer_block((D_MODEL, D_MODEL), (l,), buffered=False)],
        out_specs=pl.BlockSpec((m, D_MODEL), lambda i: (0, 0)),
        compiler_params=pltpu.CompilerParams(dimension_semantics=("arbitrary",), vmem_limit_bytes=VMEM_LIMIT),
        name="spost",
    )(x, ot, post, cn, sandwich, w)


def _rot_tables(pos):
    half = HEAD_DIM // 2
    inv = ROPE_BASE ** (-jnp.arange(half, dtype=F32) / half)
    ang = pos.astype(F32)[:, None] * inv[None, :]
    cos, sin = jnp.cos(ang), jnp.sin(ang)
    cos_full = jnp.tile(jnp.concatenate([cos, cos], axis=-1), (1, N_HEADS))
    sin_signed = jnp.tile(jnp.concatenate([-sin, sin], axis=-1), (1, N_HEADS))
    return cos_full, sin_signed


def _small_params(ret_norm, ssm_conv_w, ssm_conv_b, ssm_dt_bias, ssm_A_log, ssm_D, ssm_norm, gdn_conv_w,
                  gdn_dt_bias, gdn_A_log, gdn_norm, mlstm_i_bias, mlstm_f_bias, mlstm_norm):
    depth = ret_norm.shape[0]
    z = lambda n: jnp.zeros((depth, n), F32)
    pad_lanes = z(LANES - 5 * N_HEADS)
    bias = jnp.concatenate([ssm_dt_bias, gdn_dt_bias, z(N_HEADS), mlstm_i_bias, mlstm_f_bias, pad_lanes], axis=1)
    alog = jnp.concatenate([ssm_A_log, gdn_A_log, z(3 * N_HEADS), pad_lanes], axis=1)
    gvec = jnp.stack([bias, alog], axis=1)
    convw = jnp.concatenate([ssm_conv_w, gdn_conv_w], axis=2)
    convb = jnp.concatenate([ssm_conv_b, z(CONV_CH - SSM_CONV_CH)], axis=1)[:, None, :]
    dvec = jnp.repeat(ssm_D, HEAD_DIM, axis=1)[:, None, :]
    norms = jnp.stack([ret_norm, ssm_norm, gdn_norm, mlstm_norm], axis=1)
    return gvec, convw, convb, dvec, norms


def kernel(x_prompt, x_sample, state_ret, state_ssm, state_ssm_conv, state_gdn, state_gdn_conv,
           state_mlstm_C, state_mlstm_n, state_mlstm_m, w_in, w_out, norm_sandwich, ffn_w_gate, ffn_w_up,
           ffn_w_down, ret_norm, ssm_conv_w, ssm_conv_b, ssm_dt_bias, ssm_A_log, ssm_D, ssm_norm, gdn_conv_w,
           gdn_dt_bias, gdn_A_log, gdn_norm, mlstm_i_bias, mlstm_f_bias, mlstm_norm):
    depth = w_in.shape[0]
    bp, lp, _ = x_prompt.shape
    bs, ls, _ = x_sample.shape
    assert (bs, ls) == (LANES, N_TOK)
    w_in_p = _prep_win(w_in)
    w_out_b = w_out.astype(BF16)
    wg, wu, wd = ffn_w_gate, ffn_w_up, ffn_w_down.astype(BF16)
    small = _small_params(ret_norm, ssm_conv_w, ssm_conv_b, ssm_dt_bias, ssm_A_log, ssm_D, ssm_norm, gdn_conv_w,
                          gdn_dt_bias, gdn_A_log, gdn_norm, mlstm_i_bias, mlstm_f_bias, mlstm_norm)
    gvec, convw, convb, dvec, norms = small

    cos_p, sin_p = _rot_tables(jnp.arange(lp, dtype=jnp.int32))
    x = x_prompt.reshape(bp * lp, D_MODEL)
    st_p = []
    for l in range(depth):
        x = _ffn(x, norm_sandwich, wg, wu, wd, l, 0)
        pre, convraw = _pproj(x, norm_sandwich, w_in_p, cos_p, sin_p, small, l, nbatch=bp, seqlen=lp)
        outs = _mix_prompt(pre, small, l, nbatch=bp, seqlen=lp)
        x = _ffn(x, norm_sandwich, wg, wu, wd, l, 1, mixed=outs[0], w_out=w_out_b)
        st_p.append([convraw] + outs[1:])
    y_prompt = x.reshape(bp, lp, D_MODEL)
    conv_p = jnp.stack([s[0].reshape(bp, -1, CONV_CH)[:, -(CONV_W - 1):, :] for s in st_p])
    stack_p = lambda k: jnp.stack([s[k] for s in st_p])
    out_p = (stack_p(1), stack_p(2), conv_p[..., :SSM_CONV_CH], stack_p(3), conv_p[..., SSM_CONV_CH:],
             stack_p(4), stack_p(5), stack_p(6)[:, :, 0, :])

    lanes = lambda a: jnp.broadcast_to(a[..., None], a.shape + (LANES,))
    cos_s, sin_s = _rot_tables(PAST_LEN + jnp.arange(ls, dtype=jnp.int32))
    rot = lanes(jnp.stack([cos_s, sin_s]))
    lg = np.log(1.0 - np.exp2(-5.0 - np.arange(N_HEADS, dtype=np.float64)))
    gamma = jnp.asarray(np.concatenate([np.exp(lg), np.zeros(SUBLANES - N_HEADS)]), F32)
    colp = lanes(jnp.concatenate(
        [convw.reshape(depth, CONV_W * CONV_CH), convb[:, 0, :], gvec[:, 0, :4 * SUBLANES], gvec[:, 1, :4 * SUBLANES],
         jnp.broadcast_to(gamma, (depth, SUBLANES))], axis=1))
    cn = lanes(jnp.concatenate([norms.reshape(depth, 4 * GROUP_W), dvec[:, 0, :]], axis=1))
    hist = jnp.transpose(jnp.concatenate([state_ssm_conv, state_gdn_conv], axis=-1), (0, 2, 1, 3))
    mat = lambda s: jnp.transpose(s, (0, 2, 3, 4, 1)).reshape(depth, N_HEADS, HEAD_DIM * HEAD_DIM, bs)
    states_s = (mat(state_ret), mat(state_ssm), mat(state_gdn), mat(state_mlstm_C),
                jnp.transpose(state_mlstm_n, (0, 2, 3, 1)), jnp.transpose(state_mlstm_m, (0, 2, 1))[:, :, None, :])
    x = jnp.transpose(x_sample, (1, 0, 2)).reshape(ls * bs, D_MODEL)
    prev, convs = None, []
    for l in range(depth):
        x = _ffn(x, norm_sandwich, wg, wu, wd, l, 0)
        slab, post, convraw = _spre(x, norm_sandwich, w_in_p, hist, rot, colp, l)
        outs = _srec(slab, states_s, prev, l)
        prev = outs[1:]
        x = _spost(x, outs[0], post, cn, norm_sandwich, w_out_b, l)
        x = _ffn(x, norm_sandwich, wg, wu, wd, l, 1)
        convs.append(convraw)
    y_sample = jnp.transpose(x.reshape(ls, bs, D_MODEL), (1, 0, 2))
    conv_s = jnp.transpose(jnp.stack(convs), (0, 2, 1, 3))
    unmat = lambda s: jnp.transpose(s.reshape(depth, N_HEADS, HEAD_DIM, HEAD_DIM, bs), (0, 4, 1, 2, 3))
    out_s = (unmat(prev[0]), unmat(prev[1]), conv_s[..., :SSM_CONV_CH], unmat(prev[2]), conv_s[..., SSM_CONV_CH:],
             unmat(prev[3]), jnp.transpose(prev[4], (0, 3, 1, 2)), jnp.transpose(prev[5][:, :, 0, :], (0, 2, 1)))
    return (y_prompt, y_sample) + out_p + out_s
```

```python
import functools
import math

import numpy as np
import jax
import jax.numpy as jnp
from jax import lax
from jax.experimental import pallas as pl
from jax.experimental.pallas import tpu as pltpu

F32 = jnp.float32
BF16 = jnp.bfloat16

D_MODEL = 1024
N_HEADS = 4
HEAD_DIM = 64
GROUP_W = N_HEADS * HEAD_DIM
D_FF = 2816
CONV_W = 4
EPS = 1e-6
ROPE_BASE = 10000.0
PAST_LEN = 16384
PROMPT_CHUNK = 64

COL_RET = 0
COL_SSM_Z = 1024
COL_CONV = 1280
CONV_CH = 1280
SSM_CONV_CH = 512
COL_GDN_GATE = 2560
COL_ML = 2816
COL_GATES = 3840
IN_PAD = 3968
G_DT, G_GA, G_GB, G_MI, G_MF, G_DTV = 0, 4, 8, 12, 16, 20

SUBLANES = 8
LANES = 128
VMEM_LIMIT = 56 * 1024 * 1024

FF_CHUNK = 256
TOKEN_TILE = 512
PROMPT_TILE = 512


def _sigmoid(x):
    return 1.0 / (1.0 + jnp.exp(-x))


def _silu(x):
    return x * _sigmoid(x)


def _softplus(x):
    return jnp.maximum(x, 0.0) + jnp.log1p(jnp.exp(-jnp.abs(x)))


def _rms(x, g):
    return x * lax.rsqrt(jnp.mean(x * x, axis=-1, keepdims=True) + EPS) * g


def _dot(a, b):
    return jnp.dot(a.astype(BF16), b.astype(BF16), preferred_element_type=F32)


def _dot_nt(a, b):
    return lax.dot_general(a.astype(BF16), b.astype(BF16), (((1,), (1,)), ((), ())),
                           preferred_element_type=F32)


def _dot_tn(a, b):
    return lax.dot_general(a.astype(BF16), b.astype(BF16), (((0,), (0,)), ((), ())),
                           preferred_element_type=F32)


N_SANDWICH = 6


def _layer_block(tail, lead, ngrid=1, buffered=True):
    idx = tuple(lead) + (0,) * len(tail)
    imap = (lambda i: idx) if ngrid == 1 else (lambda i, j: idx)
    kw = dict(pipeline_mode=pl.Buffered(1)) if buffered else {}
    return pl.BlockSpec((None,) * len(lead) + tuple(tail), imap, **kw)


WIN_SEGMENTS = ((0, 0, 1792), (1792, 1796, 1024), (2816, 2828, 1024))
WIN_GATES = ((1792, 1796), (2820, 2828), (3852, 3860))


def _prep_win_body(w_ref, o_ref):
    l = pl.program_id(0)
    step = 2 * LANES
    for dst, src, n in WIN_SEGMENTS:
        for c in range(0, n, step):
            o_ref[:, dst + c:dst + c + step] = w_ref[src + c:src + c + step, l, :].T.astype(BF16)
    gates = jnp.concatenate([w_ref[a:b, l, :] for a, b in WIN_GATES]
                            + [jnp.zeros((LANES - 5 * N_HEADS, D_MODEL), F32)], axis=0)
    o_ref[:, COL_GATES:IN_PAD] = gates.T.astype(BF16)


def _prep_win(w_in):
    depth, _, in_dim = w_in.shape
    w_t = jnp.transpose(w_in, (2, 0, 1))
    return pl.pallas_call(
        _prep_win_body,
        out_shape=jax.ShapeDtypeStruct((depth, D_MODEL, IN_PAD), BF16),
        grid=(depth,),
        in_specs=[pl.BlockSpec((in_dim, depth, D_MODEL), lambda l: (0, 0, 0), pipeline_mode=pl.Buffered(1))],
        out_specs=pl.BlockSpec((None, D_MODEL, IN_PAD), lambda l: (l, 0, 0)),
        compiler_params=pltpu.CompilerParams(dimension_semantics=("arbitrary",), vmem_limit_bytes=VMEM_LIMIT),
        name="prep_win",
    )(w_t)


def _ffn_body(*refs, k, mixer_out, extra, split):
    it = iter(refs)
    x_ref = next(it)
    xs_ref = next(it) if extra else None
    m_ref, wo_ref = (next(it), next(it)) if mixer_out else (None, None)
    g_ref, wg_ref, wu_ref, wd_ref = next(it), next(it), next(it), next(it)
    outs = [next(it) for _ in range(2 if split else 1)]
    a_ref = next(it)
    i, last = pl.program_id(0), pl.num_programs(0) - 1
    x = x_ref[...]
    if mixer_out:
        y = jnp.dot(m_ref[...], wo_ref[...], preferred_element_type=F32)
        x = x + _rms(y, g_ref[3:4, :])
    if extra:
        x = jnp.where(i == last, xs_ref[...], x)
    h = _rms(x, g_ref[4 * k:4 * k + 1, :]).astype(BF16)
    for j in range(D_FF // FF_CHUNK):
        sl = slice(j * FF_CHUNK, (j + 1) * FF_CHUNK)
        gt = jnp.dot(h, wg_ref[:, sl].astype(BF16), preferred_element_type=F32)
        up = jnp.dot(h, wu_ref[:, sl].astype(BF16), preferred_element_type=F32)
        a_ref[:, sl] = (_silu(gt) * up).astype(BF16)
    y = jnp.dot(a_ref[...], wd_ref[...], preferred_element_type=F32)
    res = x + 0.5 * _rms(y, g_ref[4 * k + 1:4 * k + 2, :])
    if split:
        @pl.when(i < last)
        def _():
            outs[0][...] = res

        @pl.when(i == last)
        def _():
            outs[1][...] = res
    else:
        outs[0][...] = res


def _ffn(x, sandwich, wg, wu, wd, l, k, *, ntiles, mixed=None, w_out=None, extra=None, split=False):
    tm = TOKEN_TILE
    rows = lambda a: pl.BlockSpec((tm, D_MODEL), lambda i: (jnp.minimum(i, a.shape[0] // tm - 1), 0))
    fused = mixed is not None
    in_specs = [rows(x)] + ([rows(extra)] if extra is not None else [])
    in_specs += [rows(mixed), _layer_block((D_MODEL, D_MODEL), (l,))] if fused else []
    in_specs += [_layer_block((N_SANDWICH, D_MODEL), (l,)),
                 _layer_block((D_MODEL, D_FF), (l, k)),
                 _layer_block((D_MODEL, D_FF), (l, k)),
                 _layer_block((D_FF, D_MODEL), (l, k))]
    args = (x,) + ((extra,) if extra is not None else ()) + ((mixed, w_out) if fused else ())
    args += (sandwich, wg, wu, wd)
    if split:
        out_shape = [jax.ShapeDtypeStruct(((ntiles - 1) * tm, D_MODEL), F32),
                     jax.ShapeDtypeStruct((tm, D_MODEL), F32)]
        out_specs = [pl.BlockSpec((tm, D_MODEL), lambda i: (jnp.minimum(i, ntiles - 2), 0)),
                     pl.BlockSpec((tm, D_MODEL), lambda i: (0, 0))]
    else:
        out_shape = jax.ShapeDtypeStruct((ntiles * tm, D_MODEL), F32)
        out_specs = pl.BlockSpec((tm, D_MODEL), lambda i: (i, 0))
    return pl.pallas_call(
        functools.partial(_ffn_body, k=k, mixer_out=fused, extra=extra is not None, split=split),
        out_shape=out_shape,
        grid=(ntiles,),
        in_specs=in_specs,
        out_specs=out_specs,
        scratch_shapes=[pltpu.VMEM((tm, D_FF), BF16)],
        compiler_params=pltpu.CompilerParams(dimension_semantics=("arbitrary",),
                                             vmem_limit_bytes=VMEM_LIMIT),
        name="ffn_out" if fused else "ffn",
    )(*args)


NEXP = 6
X_CSSM, X_DT, X_CGDN, X_BETA, X_B, X_IMB = range(NEXP)


def _split2(x):
    hi = x.astype(BF16)
    return hi, (x - hi.astype(F32)).astype(BF16)


def _dot_sel_r2(x, sel):
    hi, lo = _split2(x)
    return jnp.dot(lo, sel, preferred_element_type=F32) + jnp.dot(hi, sel, preferred_element_type=F32)


(PB_RQ, PB_RK, PB_RV, PB_RG, PB_Z, PB_SX, PB_SBC, PB_GQ, PB_GK, PB_GV, PB_GG, PB_MQ, PB_MK, PB_MV, PB_MO,
 PB_X0) = range(16)
PRE_COLS = (PB_X0 + NEXP) * GROUP_W


def _pproj_body(x_ref, g_ref, w_ref, cos_ref, sin_ref, maskbd_ref, gvec_ref, convw_ref,
                convb_ref, pre_ref, convraw_ref, e_ref, *, rows):
    @pl.when(pl.program_id(1) == 0)
    def _():
        e_ref[0:SUBLANES, :] = jnp.zeros((SUBLANES, CONV_CH), F32)

    h_in = _rms(x_ref[...], g_ref[2:3, :]).astype(BF16)
    proj = lambda a, n: jnp.dot(h_in, w_ref[:, a:a + n], preferred_element_type=F32)

    def put(b, v):
        pre_ref[:, b * GROUP_W:(b + 1) * GROUP_W] = v

    maskbd_b = maskbd_ref[...].astype(BF16)
    seg_sum = lambda x: _dot_sel_r2(x, maskbd_b)

    lane = lax.broadcasted_iota(jnp.int32, (1, LANES), 1)
    in_lanes = lambda lo: (lane >= lo) & (lane < lo + N_HEADS)
    graw = proj(COL_GATES, LANES)
    put(PB_MQ, proj(COL_ML, GROUP_W))
    put(PB_MK, proj(COL_ML + GROUP_W, GROUP_W) * HEAD_DIM ** -0.5)
    put(PB_MV, proj(COL_ML + 2 * GROUP_W, GROUP_W))
    put(PB_MO, proj(COL_ML + 3 * GROUP_W, GROUP_W))
    xb = graw + gvec_ref[0:1, :]
    sp = _softplus(xb)
    neg_a = -jnp.exp(gvec_ref[1:2, :])
    logf = -_softplus(-xb)
    pre = jnp.where(lane < 2 * N_HEADS, neg_a * sp, jnp.where(in_lanes(G_MF), logf, 0.0))
    row_in_chunk = lax.broadcasted_iota(jnp.int32, (rows, 1), 0) % PROMPT_CHUNK
    cum, step = pre, 1
    while step < PROMPT_CHUNK:
        cum = cum + jnp.where(row_in_chunk >= step, pltpu.roll(cum, step, axis=0), 0.0)
        step *= 2
    imb = xb - pltpu.roll(cum, LANES - (G_MF - G_MI), axis=1)
    dtv = pltpu.roll(sp, G_DTV - G_DT, axis=1)
    gates = jnp.where(in_lanes(G_GB), _sigmoid(graw),
                      jnp.where(in_lanes(G_MI), imb, jnp.where(in_lanes(G_DTV), dtv, cum)))

    lane_g = lax.broadcasted_iota(jnp.int32, (1, GROUP_W), 1)
    first_half = (lane_g % HEAD_DIM) < (HEAD_DIM // 2)
    head_of_lane = lane_g // HEAD_DIM

    def rotary(x):
        swapped = jnp.where(first_half, pltpu.roll(x, GROUP_W - HEAD_DIM // 2, axis=1),
                            pltpu.roll(x, HEAD_DIM // 2, axis=1))
        return x * cos_ref[...] + swapped * sin_ref[...]

    def conv_in(c):
        e_ref[SUBLANES:SUBLANES + rows, c * GROUP_W:(c + 1) * GROUP_W] = proj(COL_CONV + c * GROUP_W, GROUP_W)

    def conv_out(c):
        cs = slice(c * GROUP_W, (c + 1) * GROUP_W)
        acc = e_ref[SUBLANES:SUBLANES + rows, cs] * convw_ref[CONV_W - 1:CONV_W, cs]
        for s in range(1, CONV_W):
            acc = acc + e_ref[SUBLANES - s:SUBLANES - s + rows, cs] * convw_ref[CONV_W - 1 - s:CONV_W - s, cs]
        return _silu(acc + convb_ref[:, cs])

    def expand(j):
        lo = (G_DT, G_DTV, G_GA, G_GB, G_MF, G_MI)[j]
        out = jnp.broadcast_to(gates[:, lo:lo + 1], (rows, GROUP_W))
        for h in range(1, N_HEADS):
            out = jnp.where(head_of_lane == h, jnp.broadcast_to(gates[:, lo + h:lo + h + 1], (rows, GROUP_W)), out)
        put(PB_X0 + j, out)

    conv_in(0)
    conv_in(1)
    conv_in(2)
    expand(0)
    expand(1)
    conv_in(3)
    put(PB_RV, proj(COL_RET + 2 * GROUP_W, GROUP_W))
    put(PB_SX, conv_out(0))
    conv_in(4)
    put(PB_RG, proj(COL_RET + 3 * GROUP_W, GROUP_W))
    put(PB_SBC, conv_out(1))
    rq = proj(COL_RET, GROUP_W)
    put(PB_Z, proj(COL_SSM_Z, GROUP_W))
    gq = conv_out(2)
    put(PB_GQ, gq * lax.rsqrt(seg_sum(gq * gq) + EPS) * HEAD_DIM ** -0.5)
    rk = proj(COL_RET + GROUP_W, GROUP_W)
    put(PB_GG, proj(COL_GDN_GATE, GROUP_W))
    gk = conv_out(3)
    put(PB_GK, gk * lax.rsqrt(seg_sum(gk * gk) + EPS))
    put(PB_GV, conv_out(4))
    convraw_ref[...] = e_ref[rows:rows + SUBLANES, :]
    e_ref[0:SUBLANES, :] = e_ref[rows:rows + SUBLANES, :]
    expand(2)
    put(PB_RQ, rotary(rq))
    expand(3)
    put(PB_RK, rotary(rk) * HEAD_DIM ** -0.5)
    expand(4)
    expand(5)


def _pproj(x, sandwich, w, cos, sin, small, l, *, nbatch, seqlen):
    rows = min(TOKEN_TILE, seqlen)
    ntile = seqlen // rows
    g = np.arange(GROUP_W)
    maskbd = jnp.asarray(g[:, None] // HEAD_DIM == g[None, :] // HEAD_DIM, F32)
    gvec, convw, convb = small[:3]
    row_map = lambda b, t: (b * ntile + t, 0)
    in_specs = [pl.BlockSpec((rows, D_MODEL), row_map),
                _layer_block((N_SANDWICH, D_MODEL), (l,), ngrid=2),
                _layer_block((D_MODEL, IN_PAD), (l,), ngrid=2),
                pl.BlockSpec((rows, GROUP_W), lambda b, t: (t, 0)), pl.BlockSpec((rows, GROUP_W), lambda b, t: (t, 0)),
                pl.BlockSpec(maskbd.shape, lambda b, t: (0, 0))]
    in_specs += [_layer_block(a.shape[1:], (l,), ngrid=2, buffered=False) for a in (gvec, convw, convb)]
    m = nbatch * seqlen
    return pl.pallas_call(
        functools.partial(_pproj_body, rows=rows),
        out_shape=[jax.ShapeDtypeStruct((m, PRE_COLS), F32),
                   jax.ShapeDtypeStruct((nbatch * SUBLANES, CONV_CH), F32)],
        grid=(nbatch, ntile), in_specs=in_specs,
        out_specs=[pl.BlockSpec((rows, PRE_COLS), row_map), pl.BlockSpec((SUBLANES, CONV_CH), lambda b, t: (b, 0))],
        scratch_shapes=[pltpu.VMEM((rows + SUBLANES, CONV_CH), F32)],
        compiler_params=pltpu.CompilerParams(dimension_semantics=("parallel", "arbitrary"),
                                             vmem_limit_bytes=VMEM_LIMIT),
        name="pproj",
    )(x, sandwich, w, cos, sin, maskbd, gvec, convw, convb)


def _mixp_body(pre_ref, rinter_ref, rtail_ref, rintra_ref, maskbd_ref, maskg_ref, maskht_ref, dvec_ref, norms_ref,
               mixed_ref, oret_ref, ossm_ref, ogdn_ref, omc_ref, omn_ref, omm_ref,
               o_ref, sret_ref, sht_ref, sgdn_ref, smc_ref, vec_ref, *, rows, chunk, ret_carry):
    nchunks = rows // chunk
    t_id = pl.program_id(1)
    blk = lambda b, rs=slice(None): pre_ref[rs, b * GROUP_W:(b + 1) * GROUP_W]

    @pl.when(t_id == 0)
    def _():
        sret_ref[...] = jnp.zeros_like(sret_ref)
        sht_ref[...] = jnp.zeros_like(sht_ref)
        sgdn_ref[...] = jnp.zeros_like(sgdn_ref)
        smc_ref[...] = jnp.zeros_like(smc_ref)
        vec_ref[...] = jnp.zeros_like(vec_ref)

    maskbd = maskbd_ref[...]
    maskbd_b = maskbd.astype(BF16)
    maskg = maskg_ref[...]
    maskg_b = maskg.astype(BF16)

    def seg_sum(x):
        return _dot_sel_r2(x, maskbd_b)

    r_i = lax.broadcasted_iota(jnp.int32, (chunk, GROUP_W), 0)
    c_i = lax.broadcasted_iota(jnp.int32, (chunk, GROUP_W), 1) % HEAD_DIM
    trim = r_i >= c_i
    diag = r_i == c_i
    strict = r_i > c_i
    eye = diag.astype(F32)
    head_of_lane = lax.broadcasted_iota(jnp.int32, (1, GROUP_W), 1) // HEAD_DIM
    carry_row = jnp.zeros((1, GROUP_W), F32)
    for h in range(N_HEADS):
        carry_row = jnp.where(head_of_lane == h, ret_carry[h], carry_row)
    maskht = maskht_ref[...]

    def bd(x):
        xb16 = x.astype(BF16)
        return jnp.concatenate([xb16] * N_HEADS, axis=0) * maskbd_b

    def last_row(x):
        return x[chunk - 1:chunk, :]

    def row_form(x):
        return jnp.sum(jnp.where(diag, x, 0.0), axis=0, keepdims=True)

    def seg_max(x):
        out = None
        for h in range(N_HEADS):
            sel = head_of_lane == h
            m_h = jnp.max(jnp.where(sel, x, -jnp.inf), axis=-1, keepdims=True)
            out = jnp.where(sel, m_h, 0.0) if out is None else jnp.where(sel, m_h, out)
        return out

    chunks = [slice(c * chunk, (c + 1) * chunk) for c in range(nchunks)]

    g_gam, g_beta, g_ecum, g_k, g_n, g_p = [], [], [], [], [], []
    for rs in chunks:
        ce = blk(PB_X0 + X_CGDN, rs)
        gam = jnp.exp(jnp.where(trim, ce - row_form(ce), -jnp.inf))
        k_c = blk(PB_GK, rs)
        beta = blk(PB_X0 + X_BETA, rs)
        n0 = -(_dot_nt(k_c, bd(k_c)) * jnp.where(strict, gam, 0.0) * beta)
        g_gam.append(gam), g_beta.append(beta), g_ecum.append(jnp.exp(ce)), g_k.append(k_c)
        g_n.append(n0), g_p.append(eye + n0)
    for _ in range(max(int(math.ceil(math.log2(chunk))) - 1, 0)):
        for c in range(nchunks):
            g_n[c] = _dot(g_n[c], bd(g_n[c]))
        for c in range(nchunks):
            g_p[c] = g_p[c] + _dot(g_p[c], bd(g_n[c]))
    g_u, g_w = [], []
    for c, rs in enumerate(chunks):
        g_u.append(_dot(g_p[c], bd(blk(PB_GV, rs) * g_beta[c])))
        g_w.append(_dot(g_p[c], bd(g_k[c] * (g_beta[c] * g_ecum[c]))))

    s_ret, s_ht, s_gdn, s_mc = sret_ref[...], sht_ref[...], sgdn_ref[...], smc_ref[...]
    n_row, m_row = vec_ref[0:1, :], vec_ref[1:2, :]
    for c, rs in enumerate(chunks):
        rq, rk, rv = blk(PB_RQ, rs), blk(PB_RK, rs), blk(PB_RV, rs)
        ce_s = blk(PB_X0 + X_CSSM, rs)
        b_c = pre_ref[rs, PB_SBC * GROUP_W:PB_SBC * GROUP_W + 2 * HEAD_DIM]
        c_c = pre_ref[rs, PB_SBC * GROUP_W + 2 * HEAD_DIM:(PB_SBC + 1) * GROUP_W]
        xdt = blk(PB_SX, rs) * blk(PB_X0 + X_DT, rs)
        ce_g, gq = blk(PB_X0 + X_CGDN, rs), blk(PB_GQ, rs)
        mq, mk, mv = blk(PB_MQ, rs), blk(PB_MK, rs), blk(PB_MV, rs)
        b_e, imb_e = blk(PB_X0 + X_B, rs), blk(PB_X0 + X_IMB, rs)

        w_s = _dot(g_w[c], s_gdn)
        att = _dot_nt(rq, bd(rk))
        scores = _dot_nt(c_c, jnp.concatenate([b_c.astype(BF16)] * N_HEADS, axis=0) * maskg_b)
        qk_g = _dot_nt(gq, bd(g_k[c]))
        qk_m = _dot_nt(mq, bd(mk))
        o_ret = _dot(rq * rinter_ref[...], s_ret)
        o_ssm = _dot(c_c, s_ht)
        o_gdn = _dot(gq * g_ecum[c], s_gdn)
        o_ml = _dot(mq, s_mc)
        qn = seg_sum(mq * n_row)

        v_new = g_u[c] - w_s
        att = att * rintra_ref[...]
        ce_s_last, ce_g_last = last_row(ce_s), last_row(ce_g)
        scores = scores * jnp.exp(jnp.where(trim, ce_s - row_form(ce_s), -jnp.inf))
        qk_g = qk_g * g_gam[c]
        dmat = jnp.where(trim, b_e + row_form(imb_e), -jnp.inf)
        m_t = jnp.maximum(b_e + m_row, seg_max(dmat))
        inter = jnp.exp(b_e + m_row - m_t)
        qk_m = qk_m * jnp.exp(dmat - m_t)
        m_new, b_last = last_row(m_t), last_row(b_e)
        kw = mk * jnp.exp(b_last + imb_e - m_new)
        cs = jnp.exp(b_last + m_row - m_new)

        o_gdn = o_gdn + _dot(qk_g, bd(v_new))
        d_gdn = _dot_tn(g_k[c] * jnp.exp(ce_g_last - ce_g), v_new)
        o_ret = o_ret + _dot(att, bd(rv))
        o_ssm = o_ssm * jnp.exp(ce_s) + _dot(scores, bd(xdt))
        num = _dot(qk_m, bd(mv)) + inter * o_ml
        den = seg_sum(qk_m) + inter * qn
        d_ret = _dot_tn(rk * rtail_ref[...], rv)
        d_ht = _dot_tn(b_c, xdt * jnp.exp(ce_s_last - ce_s))
        d_mc = _dot_tn(kw, mv)

        s_gdn = s_gdn * jnp.exp(ce_g_last) + d_gdn * maskbd
        o_ml = num / jnp.maximum(jnp.abs(den), jnp.exp(-m_t))
        o_ref[rs, :] = jnp.concatenate([o_ret, o_ssm, o_gdn, o_ml], axis=1)
        s_ret = s_ret * carry_row + d_ret * maskbd
        s_ht = s_ht * jnp.exp(ce_s_last) + d_ht * maskht
        s_mc = s_mc * cs + d_mc * maskbd
        n_row = n_row * cs + jnp.sum(kw, axis=0, keepdims=True)
        m_row = m_new

    sret_ref[...] = s_ret
    sht_ref[...] = s_ht
    sgdn_ref[...] = s_gdn
    smc_ref[...] = s_mc
    vec_ref[0:1, :] = n_row
    vec_ref[1:2, :] = m_row

    def head_norm(x, g):
        return x * lax.rsqrt(seg_sum(x * x) * (1.0 / HEAD_DIM) + EPS) * g

    out_ret = head_norm(o_ref[:, 0:GROUP_W], norms_ref[0:1, :]) * _silu(blk(PB_RG))
    mixed_ref[:, 0:GROUP_W] = out_ret.astype(BF16)
    y = o_ref[:, GROUP_W:2 * GROUP_W] + dvec_ref[...] * blk(PB_SX)
    out_ssm = _rms(y * _silu(blk(PB_Z)), norms_ref[1:2, :])
    mixed_ref[:, GROUP_W:2 * GROUP_W] = out_ssm.astype(BF16)
    out_gdn = head_norm(o_ref[:, 2 * GROUP_W:3 * GROUP_W], norms_ref[2:3, :]) * _silu(blk(PB_GG))
    mixed_ref[:, 2 * GROUP_W:3 * GROUP_W] = out_gdn.astype(BF16)
    out_ml = _sigmoid(blk(PB_MO)) * head_norm(o_ref[:, 3 * GROUP_W:4 * GROUP_W], norms_ref[3:4, :])
    mixed_ref[:, 3 * GROUP_W:4 * GROUP_W] = out_ml.astype(BF16)

    @pl.when(t_id == pl.num_programs(1) - 1)
    def _():
        for h in range(N_HEADS):
            hs = slice(h * HEAD_DIM, (h + 1) * HEAD_DIM)
            grp = h // (N_HEADS // 2)
            oret_ref[0, h] = s_ret[hs, hs]
            ogdn_ref[0, h] = s_gdn[hs, hs]
            omc_ref[0, h] = s_mc[hs, hs]
            ossm_ref[0, h] = s_ht[grp * HEAD_DIM:(grp + 1) * HEAD_DIM, hs]
        omn_ref[0] = jnp.concatenate([n_row[:, h * HEAD_DIM:(h + 1) * HEAD_DIM] for h in range(N_HEADS)], axis=0)
        omm_ref[0] = jnp.concatenate([m_row[:, h * HEAD_DIM:h * HEAD_DIM + 1] for h in range(N_HEADS)], axis=1)


def _prompt_consts(rows, chunk):
    lg = np.log(1.0 - np.exp2(-5.0 - np.arange(N_HEADS, dtype=np.float64)))
    idx = np.arange(chunk, dtype=np.float64)
    rel = idx[:, None] - idx[None, :]
    intra = np.where(rel[None] >= 0, np.exp(np.maximum(rel[None], 0.0) * lg[:, None, None]), 0.0)
    intra = np.concatenate(list(intra), axis=1)
    lanes = lambda t: np.repeat(t.T, HEAD_DIM, axis=1)
    inter = lanes(np.exp((idx[None, :] + 1.0) * lg[:, None]))
    tail = lanes(np.exp((chunk - 1.0 - idx[None, :]) * lg[:, None]))
    carry = tuple(float(c) for c in np.exp(chunk * lg))
    g = np.arange(GROUP_W)
    maskbd = (g[:, None] // HEAD_DIM == g[None, :] // HEAD_DIM)
    n = np.arange(2 * HEAD_DIM)
    maskg = (g[:, None] // (2 * HEAD_DIM) == n[None, :] // HEAD_DIM)
    f = lambda a, dt=F32: jnp.asarray(np.asarray(a, np.float32), dt)
    return (f(inter), f(tail), f(intra), f(maskbd), f(maskg), f(maskg.T)), carry


def _mix_prompt(pre, small, l, *, nbatch, seqlen):
    rows, chunk = min(PROMPT_TILE, seqlen), PROMPT_CHUNK
    ntile = seqlen // rows
    consts, carry = _prompt_consts(rows, chunk)
    dvec, norms = small[3:]
    m = pre.shape[0]
    row_map = lambda b, t: (b * ntile + t, 0)
    full = lambda a: pl.BlockSpec(a.shape, lambda b, t: (0, 0))
    in_specs = [pl.BlockSpec((rows, PRE_COLS), row_map)] + [full(a) for a in consts]
    in_specs += [_layer_block(a.shape[1:], (l,), ngrid=2, buffered=False) for a in (dvec, norms)]
    st4 = pl.BlockSpec((1, N_HEADS, HEAD_DIM, HEAD_DIM), lambda b, t: (b, 0, 0, 0))
    out_shape = [jax.ShapeDtypeStruct((m, D_MODEL), BF16)]
    out_shape += [jax.ShapeDtypeStruct((nbatch, N_HEADS, HEAD_DIM, HEAD_DIM), F32) for _ in range(4)]
    out_shape += [jax.ShapeDtypeStruct((nbatch, N_HEADS, HEAD_DIM), F32),
                  jax.ShapeDtypeStruct((nbatch, 1, N_HEADS), F32)]
    out_specs = [pl.BlockSpec((rows, D_MODEL), row_map), st4, st4, st4, st4,
                 pl.BlockSpec((1, N_HEADS, HEAD_DIM), lambda b, t: (b, 0, 0)),
                 pl.BlockSpec((1, 1, N_HEADS), lambda b, t: (b, 0, 0))]
    scratch = [pltpu.VMEM((rows, D_MODEL), F32),
               pltpu.VMEM((GROUP_W, GROUP_W), F32), pltpu.VMEM((2 * HEAD_DIM, GROUP_W), F32),
               pltpu.VMEM((GROUP_W, GROUP_W), F32), pltpu.VMEM((GROUP_W, GROUP_W), F32),
               pltpu.VMEM((SUBLANES, GROUP_W), F32)]
    body = functools.partial(_mixp_body, rows=rows, chunk=chunk, ret_carry=carry)
    outs = pl.pallas_call(
        body, out_shape=out_shape, grid=(nbatch, ntile), in_specs=in_specs, out_specs=out_specs,
        scratch_shapes=scratch,
        compiler_params=pltpu.CompilerParams(dimension_semantics=("parallel", "arbitrary"),
                                             vmem_limit_bytes=VMEM_LIMIT),
        name="mix_prompt",
    )(pre, *consts, dvec, norms)
    outs = list(outs)
    outs[2] = jnp.swapaxes(outs[2], -1, -2)
    return outs


S_RQ, S_RK, S_RV, S_XDT, S_B, S_C, S_GQ, S_GK, S_GV, S_MQ, S_MK, S_MV, S_GATE = (
    HEAD_DIM * i for i in range(13))
SLAB_ROWS = S_GATE + SUBLANES
GT_SSM_A, GT_GDN_A, GT_BETA, GT_LOGI, GT_LOGF, GT_GAMMA = range(6)
P_RG, P_Z, P_XS, P_GG, P_MO = (GROUP_W * i for i in range(5))
POST_ROWS = 5 * GROUP_W
C_CONVW, C_CONVB = 0, CONV_W * CONV_CH
C_BIAS = C_CONVB + CONV_CH
C_ALOG = C_BIAS + 4 * SUBLANES
C_GAMMA = C_ALOG + 4 * SUBLANES
C_ROWS = C_GAMMA + SUBLANES
N_ROWS = 5 * GROUP_W
N_TOK = 4


def _spre_body(x_ref, g_ref, w_ref, hist_ref, rot_ref, colp_ref, slab_ref, post_ref, convraw_ref):
    nb = LANES
    h_in = _rms(x_ref[...], g_ref[2:3, :]).astype(BF16)
    proj = jnp.dot(h_in, w_ref[...], preferred_element_type=F32)
    for j in range(CONV_W - 1):
        convraw_ref[j] = proj[(j + 1) * nb:(j + 2) * nb, COL_CONV:COL_CONV + CONV_CH]
    pt = [proj[t * nb:(t + 1) * nb, :].T for t in range(N_TOK)]
    src = [hist_ref[j].T for j in range(CONV_W - 1)] + [p[COL_CONV:COL_CONV + CONV_CH] for p in pt]
    bias = colp_ref[C_BIAS:C_BIAS + 4 * SUBLANES, :]
    neg_a = -jnp.exp(colp_ref[C_ALOG:C_ALOG + 4 * SUBLANES, :])
    gamma = colp_ref[C_GAMMA:C_GAMMA + SUBLANES, :]
    zero_row = jnp.zeros((1, LANES), F32)

    def rotary(x, t):
        swapped = jnp.concatenate(
            [x[h * HEAD_DIM + off:h * HEAD_DIM + off + HEAD_DIM // 2]
             for h in range(N_HEADS) for off in (HEAD_DIM // 2, 0)], axis=0)
        return x * rot_ref[0, t] + swapped * rot_ref[1, t]

    for t in range(N_TOK):
        p = pt[t]
        g = p[COL_GATES:COL_GATES + 4 * SUBLANES]
        xb = g + bias
        sp = _softplus(xb)
        decay = jnp.exp(neg_a * sp)
        beta = _sigmoid(g)
        logf = -_softplus(-xb)
        conv = src[t] * colp_ref[C_CONVW:C_CONVW + CONV_CH, :]
        for j in range(1, CONV_W):
            conv = conv + src[t + j] * colp_ref[C_CONVW + j * CONV_CH:C_CONVW + (j + 1) * CONV_CH, :]
        cv = _silu(conv + colp_ref[C_CONVB:C_CONVB + CONV_CH, :])
        rq = rotary(p[COL_RET:COL_RET + GROUP_W], t)
        rk = rotary(p[COL_RET + GROUP_W:COL_RET + 2 * GROUP_W], t) * HEAD_DIM ** -0.5
        post_ref[t, P_RG:P_RG + GROUP_W, :] = p[COL_RET + 3 * GROUP_W:COL_RET + 4 * GROUP_W]
        post_ref[t, P_Z:P_Z + GROUP_W, :] = p[COL_SSM_Z:COL_SSM_Z + GROUP_W]
        post_ref[t, P_XS:P_XS + GROUP_W, :] = cv[0:GROUP_W]
        post_ref[t, P_GG:P_GG + GROUP_W, :] = p[COL_GDN_GATE:COL_GDN_GATE + GROUP_W]
        post_ref[t, P_MO:P_MO + GROUP_W, :] = p[COL_ML + 3 * GROUP_W:COL_ML + 4 * GROUP_W]
        for h in range(N_HEADS):
            hs = slice(h * HEAD_DIM, (h + 1) * HEAD_DIM)
            grp = h // (N_HEADS // 2)
            def put(r, v, h=h, t=t):
                slab_ref[h, t, r:r + v.shape[0], :] = v

            l2 = lambda v: v * lax.rsqrt(jnp.sum(v * v, axis=0, keepdims=True) + EPS)
            put(S_RQ, rq[hs])
            put(S_RK, rk[hs])
            put(S_RV, p[COL_RET + 2 * GROUP_W + h * HEAD_DIM:COL_RET + 2 * GROUP_W + (h + 1) * HEAD_DIM])
            put(S_XDT, cv[hs] * sp[G_DT + h:G_DT + h + 1])
            put(S_B, cv[GROUP_W + grp * HEAD_DIM:GROUP_W + (grp + 1) * HEAD_DIM])
            put(S_C, cv[GROUP_W + 2 * HEAD_DIM + grp * HEAD_DIM:GROUP_W + 2 * HEAD_DIM + (grp + 1) * HEAD_DIM])
            put(S_GQ, l2(cv[SSM_CONV_CH + h * HEAD_DIM:SSM_CONV_CH + (h + 1) * HEAD_DIM]) * HEAD_DIM ** -0.5)
            put(S_GK, l2(cv[SSM_CONV_CH + GROUP_W + h * HEAD_DIM:SSM_CONV_CH + GROUP_W + (h + 1) * HEAD_DIM]))
            put(S_GV, cv[SSM_CONV_CH + 2 * GROUP_W + h * HEAD_DIM:SSM_CONV_CH + 2 * GROUP_W + (h + 1) * HEAD_DIM])
            put(S_MQ, p[COL_ML + h * HEAD_DIM:COL_ML + (h + 1) * HEAD_DIM])
            put(S_MK, p[COL_ML + GROUP_W + h * HEAD_DIM:COL_ML + GROUP_W + (h + 1) * HEAD_DIM] * HEAD_DIM ** -0.5)
            put(S_MV, p[COL_ML + 2 * GROUP_W + h * HEAD_DIM:COL_ML + 2 * GROUP_W + (h + 1) * HEAD_DIM])
            row = lambda a, j: a[j + h:j + h + 1]
            put(S_GATE, jnp.concatenate([row(decay, G_DT), row(decay, G_GA), row(beta, G_GB), row(xb, G_MI),
                                         row(logf, G_MF), gamma[h:h + 1], zero_row, zero_row], axis=0))


def _spre(x, sandwich, w, hist, rot, colp, l):
    m = N_TOK * LANES
    xtile = x.shape[0] // m - 1
    return pl.pallas_call(
        _spre_body,
        out_shape=[jax.ShapeDtypeStruct((N_HEADS, N_TOK, SLAB_ROWS, LANES), F32),
                   jax.ShapeDtypeStruct((N_TOK, POST_ROWS, LANES), F32),
                   jax.ShapeDtypeStruct((CONV_W - 1, LANES, CONV_CH), F32)],
        grid=(1,),
        in_specs=[pl.BlockSpec((m, D_MODEL), lambda i: (xtile, 0)),
                  _layer_block((N_SANDWICH, D_MODEL), (l,), buffered=False),
                  _layer_block((D_MODEL, IN_PAD), (l,), buffered=False),
                  _layer_block((CONV_W - 1, LANES, CONV_CH), (l,), buffered=False),
                  pl.BlockSpec(rot.shape, lambda i: (0, 0, 0, 0)),
                  _layer_block((C_ROWS, LANES), (l,), buffered=False)],
        out_specs=[pl.BlockSpec((N_HEADS, N_TOK, SLAB_ROWS, LANES), lambda i: (0, 0, 0, 0)),
                   pl.BlockSpec((N_TOK, POST_ROWS, LANES), lambda i: (0, 0, 0)),
                   pl.BlockSpec((CONV_W - 1, LANES, CONV_CH), lambda i: (0, 0, 0))],
        compiler_params=pltpu.CompilerParams(dimension_semantics=("arbitrary",), vmem_limit_bytes=VMEM_LIMIT),
        name="spre",
    )(x, sandwich, w, hist, rot, colp)


def _srec_body(*refs, aliased, nsteps):
    ins, outs = refs[:7], refs[7 + (6 if aliased else 0):]
    slab_ref, sret_in, sssm_in, sgdn_in, smc_in, smn_in, smm_in = ins
    ot_ref, sret_out, sssm_out, sgdn_out, smc_out, smn_out, smm_out, y8_ref = outs
    j = pl.program_id(1)

    @pl.when(j == 0)
    def _():
        _srec_ret_ssm(slab_ref, sret_in, sssm_in, ot_ref, sret_out, sssm_out, y8_ref)

    @pl.when(j == nsteps - 1)
    def _():
        _srec_gdn_ml(slab_ref, sgdn_in, smc_in, smn_in, smm_in, ot_ref, sgdn_out, smc_out, smn_out, smm_out)

    if nsteps > 1:
        @pl.when(j > 0)
        def _():
            for dst in (sret_out, sssm_out):
                dst[...] = jnp.zeros_like(dst)

        @pl.when(j < nsteps - 1)
        def _():
            for dst in (sgdn_out, smc_out, smn_out, smm_out):
                dst[...] = jnp.zeros_like(dst)


def _srec_tools(slab_ref):
    tile = lambda t, r: slab_ref[t, r:r + HEAD_DIM, :]
    gate = lambda t, j: slab_ref[t, S_GATE + j:S_GATE + j + 1, :]
    drow = lambda t, r, d: slab_ref[t, pl.ds(r + d, 1), :]
    zeros = jnp.zeros((HEAD_DIM, LANES), F32)
    sweep = lambda body, init: lax.fori_loop(0, HEAD_DIM, body, init, unroll=8)
    blk = lambda d: pl.ds(pl.multiple_of(d * HEAD_DIM, HEAD_DIM), HEAD_DIM)
    return tile, gate, drow, zeros, sweep, blk


def _srec_ret_ssm(slab_ref, sret_in, sssm_in, ot_ref, sret_out, sssm_out, y8_ref):
    tile, gate, drow, zeros, sweep, blk = _srec_tools(slab_ref)

    gamma = gate(0, GT_GAMMA)
    for t in range(N_TOK):
        src = sret_in if t == 0 else sret_out
        v = tile(t, S_RV)

        def body(d, acc, t=t, src=src, v=v):
            s = src[blk(d), :] * gamma + drow(t, S_RK, d) * v
            sret_out[blk(d), :] = s
            return acc + drow(t, S_RQ, d) * s
        ot_ref[t, 0:HEAD_DIM, :] = sweep(body, zeros)

    for t in range(N_TOK):
        src = sssm_in if t == 0 else sssm_out
        a, b_t, c_t = gate(t, GT_SSM_A), tile(t, S_B), tile(t, S_C)

        def body(p, carry, t=t, src=src, a=a, b_t=b_t, c_t=c_t):
            hp = src[blk(p), :] * a + drow(t, S_XDT, p) * b_t
            sssm_out[blk(p), :] = hp
            part = jnp.sum((hp * c_t).reshape(HEAD_DIM // SUBLANES, SUBLANES, LANES), axis=0)
            y8_ref[pl.ds(pl.multiple_of(p * SUBLANES, SUBLANES), SUBLANES), :] = part
            return carry
        sweep(body, 0)
        ot_ref[t, HEAD_DIM:2 * HEAD_DIM, :] = jnp.sum(y8_ref[...].reshape(HEAD_DIM, SUBLANES, LANES), axis=1)


def _srec_gdn_ml(slab_ref, sgdn_in, smc_in, smn_in, smm_in, ot_ref, sgdn_out, smc_out, smn_out, smm_out):
    tile, gate, drow, zeros, sweep, blk = _srec_tools(slab_ref)

    ks = sweep(lambda d, acc: acc + drow(0, S_GK, d) * sgdn_in[blk(d), :], zeros)
    for t in range(N_TOK):
        src = sgdn_in if t == 0 else sgdn_out
        a = gate(t, GT_GDN_A)
        u = gate(t, GT_BETA) * (tile(t, S_GV) - a * ks)
        nxt = t + 1 < N_TOK

        def body(d, acc, t=t, src=src, a=a, u=u, nxt=nxt):
            s = src[blk(d), :] * a + drow(t, S_GK, d) * u
            sgdn_out[blk(d), :] = s
            o = acc[0] + drow(t, S_GQ, d) * s
            return (o, acc[1] + drow(t + 1, S_GK, d) * s) if nxt else (o, acc[1])
        o, ks = sweep(body, (zeros, zeros))
        ot_ref[t, 2 * HEAD_DIM:3 * HEAD_DIM, :] = o

    m_row, n_tile = smm_in[...], smn_in[...]
    for t in range(N_TOK):
        src = smc_in if t == 0 else smc_out
        logi, logf = gate(t, GT_LOGI), gate(t, GT_LOGF)
        m_new = jnp.maximum(logf + m_row, logi)
        fs, ins = jnp.exp(logf + m_row - m_new), jnp.exp(logi - m_new)
        n_tile = fs * n_tile + ins * tile(t, S_MK)
        qn = jnp.sum(tile(t, S_MQ) * n_tile, axis=0, keepdims=True)
        v = tile(t, S_MV) * ins

        def body(d, acc, t=t, src=src, fs=fs, v=v):
            c = src[blk(d), :] * fs + drow(t, S_MK, d) * v
            smc_out[blk(d), :] = c
            return acc + drow(t, S_MQ, d) * c
        num = sweep(body, zeros)
        ot_ref[t, 3 * HEAD_DIM:4 * HEAD_DIM, :] = num / jnp.maximum(jnp.abs(qn), jnp.exp(-m_new))
        m_row = m_new
    smn_out[...] = n_tile
    smm_out[...] = m_row


def _srec(slab, states, prev, l):
    depth = states[0].shape[0]
    rows = HEAD_DIM * HEAD_DIM
    st_shapes = [(rows, LANES)] * 4 + [(HEAD_DIM, LANES), (1, LANES)]
    spec = lambda tail, lead: pl.BlockSpec((None, None) + tail, lambda h, j: (lead(j), h, 0, 0))
    slab_spec = lambda r: pl.BlockSpec((None, N_TOK, r, LANES), lambda h, j: (h, 0, 0, 0))
    in_specs = [slab_spec(SLAB_ROWS)] + [spec(s, lambda j: l) for s in st_shapes]
    args = [slab, *states]
    aliases = {}
    if l:
        in_specs += [pl.BlockSpec(memory_space=pl.ANY)] * 6
        args += list(prev)
        aliases = {7 + i: 1 + i for i in range(6)}
    out_shape = [jax.ShapeDtypeStruct((N_HEADS, N_TOK, GROUP_W, LANES), F32)]
    out_shape += [jax.ShapeDtypeStruct((depth, N_HEADS) + s, F32) for s in st_shapes]
    nsteps = 1 if l else depth
    lead = [(lambda j: l + j)] * 2 + [(lambda j: l + (j + 1) % nsteps)] * 4
    out_specs = [slab_spec(GROUP_W)] + [spec(s, f) for s, f in zip(st_shapes, lead)]
    return pl.pallas_call(
        functools.partial(_srec_body, aliased=bool(l), nsteps=nsteps),
        out_shape=out_shape, grid=(N_HEADS, nsteps), in_specs=in_specs, out_specs=out_specs,
        input_output_aliases=aliases,
        scratch_shapes=[pltpu.VMEM((HEAD_DIM * SUBLANES, LANES), F32)],
        compiler_params=pltpu.CompilerParams(dimension_semantics=("arbitrary", "arbitrary"),
                                             vmem_limit_bytes=VMEM_LIMIT),
        name="srec",
    )(*args)


def _spost_body(x_ref, ot_ref, post_ref, cn_ref, g_ref, w_ref, o_ref):
    nb = LANES

    def head_norm(x):
        out = []
        for h in range(N_HEADS):
            b = x[h * HEAD_DIM:(h + 1) * HEAD_DIM]
            out.append(b * lax.rsqrt(jnp.mean(b * b, axis=0, keepdims=True) + EPS))
        return jnp.concatenate(out, axis=0)

    gain = lambda i: cn_ref[i * GROUP_W:(i + 1) * GROUP_W, :]
    for t in range(N_TOK):
        mixer = lambda i: jnp.concatenate(
            [ot_ref[h, t, i * HEAD_DIM:(i + 1) * HEAD_DIM, :] for h in range(N_HEADS)], axis=0)
        pin = lambda r: post_ref[t, r:r + GROUP_W, :]
        out_ret = head_norm(mixer(0)) * gain(0) * _silu(pin(P_RG))
        y = (mixer(1) + gain(4) * pin(P_XS)) * _silu(pin(P_Z))
        out_ssm = y * lax.rsqrt(jnp.mean(y * y, axis=0, keepdims=True) + EPS) * gain(1)
        out_gdn = head_norm(mixer(2)) * gain(2) * _silu(pin(P_GG))
        out_ml = _sigmoid(pin(P_MO)) * (head_norm(mixer(3)) * gain(3))
        mixed = jnp.concatenate([out_ret, out_ssm, out_gdn, out_ml], axis=0).T.astype(BF16)
        y = jnp.dot(mixed, w_ref[...], preferred_element_type=F32)
        rs = slice(t * nb, (t + 1) * nb)
        o_ref[rs, :] = x_ref[rs, :] + _rms(y, g_ref[3:4, :])


def _spost(x, ot, post, cn, sandwich, w, l):
    m = N_TOK * LANES
    xtile = x.shape[0] // m - 1
    return pl.pallas_call(
        _spost_body,
        out_shape=jax.ShapeDtypeStruct((m, D_MODEL), F32),
        grid=(1,),
        in_specs=[pl.BlockSpec((m, D_MODEL), lambda i: (xtile, 0)),
                  pl.BlockSpec(ot.shape, lambda i: (0, 0, 0, 0)),
                  pl.BlockSpec(post.shape, lambda i: (0, 0, 0)),
                  _layer_block((N_ROWS, LANES), (l,), buffered=False),
                  _layer_block((N_SANDWICH, D_MODEL), (l,), buffered=False),
                  _layer_block((D_MODEL, D_MODEL), (l,), buffered=False)],
        out_specs=pl.BlockSpec((m, D_MODEL), lambda i: (0, 0)),
        compiler_params=pltpu.CompilerParams(dimension_semantics=("arbitrary",), vmem_limit_bytes=VMEM_LIMIT),
        name="spost",
    )(x, ot, post, cn, sandwich, w)


def _rot_tables(pos):
    half = HEAD_DIM // 2
    inv = ROPE_BASE ** (-jnp.arange(half, dtype=F32) / half)
    ang = pos.astype(F32)[:, None] * inv[None, :]
    cos, sin = jnp.cos(ang), jnp.sin(ang)
    cos_full = jnp.tile(jnp.concatenate([cos, cos], axis=-1), (1, N_HEADS))
    sin_signed = jnp.tile(jnp.concatenate([-sin, sin], axis=-1), (1, N_HEADS))
    return cos_full, sin_signed


def _small_params(ret_norm, ssm_conv_w, ssm_conv_b, ssm_dt_bias, ssm_A_log, ssm_D, ssm_norm, gdn_conv_w,
                  gdn_dt_bias, gdn_A_log, gdn_norm, mlstm_i_bias, mlstm_f_bias, mlstm_norm):
    depth = ret_norm.shape[0]
    z = lambda n: jnp.zeros((depth, n), F32)
    pad_lanes = z(LANES - 5 * N_HEADS)
    bias = jnp.concatenate([ssm_dt_bias, gdn_dt_bias, z(N_HEADS), mlstm_i_bias, mlstm_f_bias, pad_lanes], axis=1)
    alog = jnp.concatenate([ssm_A_log, gdn_A_log, z(3 * N_HEADS), pad_lanes], axis=1)
    gvec = jnp.stack([bias, alog], axis=1)
    convw = jnp.concatenate([ssm_conv_w, gdn_conv_w], axis=2)
    convb = jnp.concatenate([ssm_conv_b, z(CONV_CH - SSM_CONV_CH)], axis=1)[:, None, :]
    dvec = jnp.repeat(ssm_D, HEAD_DIM, axis=1)[:, None, :]
    norms = jnp.stack([ret_norm, ssm_norm, gdn_norm, mlstm_norm], axis=1)
    return gvec, convw, convb, dvec, norms


def kernel(x_prompt, x_sample, state_ret, state_ssm, state_ssm_conv, state_gdn, state_gdn_conv,
           state_mlstm_C, state_mlstm_n, state_mlstm_m, w_in, w_out, norm_sandwich, ffn_w_gate, ffn_w_up,
           ffn_w_down, ret_norm, ssm_conv_w, ssm_conv_b, ssm_dt_bias, ssm_A_log, ssm_D, ssm_norm, gdn_conv_w,
           gdn_dt_bias, gdn_A_log, gdn_norm, mlstm_i_bias, mlstm_f_bias, mlstm_norm):
    depth = w_in.shape[0]
    bp, lp, _ = x_prompt.shape
    bs, ls, _ = x_sample.shape
    assert (bs, ls) == (LANES, N_TOK)
    w_in_p = _prep_win(w_in)
    w_out_b = w_out.astype(BF16)
    wg, wu, wd = ffn_w_gate, ffn_w_up, ffn_w_down.astype(BF16)
    small = _small_params(ret_norm, ssm_conv_w, ssm_conv_b, ssm_dt_bias, ssm_A_log, ssm_D, ssm_norm, gdn_conv_w,
                          gdn_dt_bias, gdn_A_log, gdn_norm, mlstm_i_bias, mlstm_f_bias, mlstm_norm)
    gvec, convw, convb, dvec, norms = small

    lanes = lambda a: jnp.broadcast_to(a[..., None], a.shape + (LANES,))
    cos_s, sin_s = _rot_tables(PAST_LEN + jnp.arange(ls, dtype=jnp.int32))
    rot = lanes(jnp.stack([cos_s, sin_s]))
    lg = np.log(1.0 - np.exp2(-5.0 - np.arange(N_HEADS, dtype=np.float64)))
    gamma = jnp.asarray(np.concatenate([np.exp(lg), np.zeros(SUBLANES - N_HEADS)]), F32)
    colp = lanes(jnp.concatenate(
        [convw.reshape(depth, CONV_W * CONV_CH), convb[:, 0, :], gvec[:, 0, :4 * SUBLANES], gvec[:, 1, :4 * SUBLANES],
         jnp.broadcast_to(gamma, (depth, SUBLANES))], axis=1))
    cn = lanes(jnp.concatenate([norms.reshape(depth, 4 * GROUP_W), dvec[:, 0, :]], axis=1))
    hist = jnp.transpose(jnp.concatenate([state_ssm_conv, state_gdn_conv], axis=-1), (0, 2, 1, 3))
    mat = lambda s: jnp.transpose(s, (0, 2, 3, 4, 1)).reshape(depth, N_HEADS, HEAD_DIM * HEAD_DIM, bs)
    states_s = (mat(state_ret), mat(state_ssm), mat(state_gdn), mat(state_mlstm_C),
                jnp.transpose(state_mlstm_n, (0, 2, 3, 1)), jnp.transpose(state_mlstm_m, (0, 2, 1))[:, :, None, :])
    xs = jnp.transpose(x_sample, (1, 0, 2)).reshape(ls * bs, D_MODEL)

    assert ls * bs == TOKEN_TILE and (bp * lp) % TOKEN_TILE == 0
    ntiles = bp * lp // TOKEN_TILE + 1
    cos_p, sin_p = _rot_tables(jnp.arange(lp, dtype=jnp.int32))
    ffn = functools.partial(_ffn, sandwich=norm_sandwich, wg=wg, wu=wu, wd=wd, ntiles=ntiles)
    x = ffn(x_prompt.reshape(bp * lp, D_MODEL), l=0, k=0, extra=xs)
    st_p, prev, convs = [], None, []
    for l in range(depth):
        if l:
            x = ffn(x, l=l, k=0)
        pre, convraw = _pproj(x, norm_sandwich, w_in_p, cos_p, sin_p, small, l, nbatch=bp, seqlen=lp)
        outs = _mix_prompt(pre, small, l, nbatch=bp, seqlen=lp)
        st_p.append([convraw] + outs[1:])
        slab, post, convraw = _spre(x, norm_sandwich, w_in_p, hist, rot, colp, l)
        souts = _srec(slab, states_s, prev, l)
        prev = souts[1:]
        convs.append(convraw)
        xs = _spost(x, souts[0], post, cn, norm_sandwich, w_out_b, l)
        x = ffn(x, l=l, k=1, mixed=outs[0], w_out=w_out_b, extra=xs, split=l == depth - 1)
    y_prompt = x[0].reshape(bp, lp, D_MODEL)
    y_sample = jnp.transpose(x[1].reshape(ls, bs, D_MODEL), (1, 0, 2))
    conv_p = jnp.stack([s[0].reshape(bp, -1, CONV_CH)[:, -(CONV_W - 1):, :] for s in st_p])
    stack_p = lambda k: jnp.stack([s[k] for s in st_p])
    out_p = (stack_p(1), stack_p(2), conv_p[..., :SSM_CONV_CH], stack_p(3), conv_p[..., SSM_CONV_CH:],
             stack_p(4), stack_p(5), stack_p(6)[:, :, 0, :])
    conv_s = jnp.transpose(jnp.stack(convs), (0, 2, 1, 3))
    unmat = lambda s: jnp.transpose(s.reshape(depth, N_HEADS, HEAD_DIM, HEAD_DIM, bs), (0, 4, 1, 2, 3))
    out_s = (unmat(prev[0]), unmat(prev[1]), conv_s[..., :SSM_CONV_CH], unmat(prev[2]), conv_s[..., SSM_CONV_CH:],
             unmat(prev[3]), jnp.transpose(prev[4], (0, 3, 1, 2)), jnp.transpose(prev[5][:, :, 0, :], (0, 2, 1)))
    return (y_prompt, y_sample) + out_p + out_s
```
